```python
import math
import jax, jax.numpy as jnp
from jax import lax
import numpy as np

D_MODEL = 1024
BATCH = 4
SEQ = 8192
DEPTH = 1
DEC_BATCH = 128
DEC_SEQ = 4
PAST_LEN = 8192
PAGE_SIZE = 128

A_HEADS = 8
A_KV_HEADS = 4
A_REP = A_HEADS // A_KV_HEADS
A_HEAD_DIM = 64
A_Q = A_HEADS * A_HEAD_DIM
A_KV = A_KV_HEADS * A_HEAD_DIM
IDX_HEADS = 8
IDX_DIM = 64
TOPK_MAX = 256
Q_BLOCK = 128
DN_HEADS = 4
DN_DK = 128
DN_DV = 128
DN_QK = DN_HEADS * DN_DK
DN_V = DN_HEADS * DN_DV
DN_CONV_CH = 2 * DN_QK + DN_V
CONV_W = 4
DN_CHUNK = 64
PEER_HEADS = 8
PEER_KEYS = 128
PEER_EXPERTS = PEER_KEYS * PEER_KEYS
PEER_DKEY = 256
PEER_TOPK = 16
PEER_BLOCK = 256
PLE_DIM = 256
EPS = 1e-6
IN_SPLITS = (A_Q, A_KV, A_KV, IDX_HEADS * IDX_DIM, IDX_DIM, IDX_HEADS, DN_CONV_CH, DN_V, DN_HEADS, DN_HEADS, D_MODEL, D_MODEL)
IN_WIDTH = sum(IN_SPLITS)

kernel_name = 'hybrid_dsa_gdn_peer_step'


def rms_norm(x, g):
    xf = x.astype(jnp.float32)
    y = xf * lax.rsqrt(jnp.mean(xf * xf, axis=-1, keepdims=True) + EPS)
    return (y * g.astype(jnp.float32)).astype(x.dtype)


def l2_normalize(x):
    xf = x.astype(jnp.float32)
    return xf * lax.rsqrt(jnp.sum(xf * xf, axis=-1, keepdims=True) + EPS)


def mixer_inputs(x, g_mix, w_in, g_q, g_k, g_idx_k):
    B, T, _ = x.shape
    h = rms_norm(x, g_mix)
    z = h @ w_in
    cuts = np.cumsum(IN_SPLITS)[:-1].tolist()
    aq, ak, av, iq, ik, iw, dn_qkv, dn_z, dn_b, dn_a, gate_a, gate_b = jnp.split(z, cuts, axis=-1)
    q = rms_norm(aq.reshape(B, T, A_KV_HEADS, A_REP, A_HEAD_DIM), g_q)
    k = rms_norm(ak.reshape(B, T, A_KV_HEADS, A_HEAD_DIM), g_k)
    v = av.reshape(B, T, A_KV_HEADS, A_HEAD_DIM)
    qi = iq.reshape(B, T, IDX_HEADS, IDX_DIM)
    ki = rms_norm(ik, g_idx_k)
    wi = iw.astype(jnp.float32) * (IDX_HEADS * IDX_DIM) ** -0.5
    return q, k, v, qi, ki, wi, dn_qkv, dn_z, dn_b, dn_a, gate_a, gate_b


def select_and_attend(q, qi, wi, ki_all, q_pos, gather_kv, k_sel):
    L = ki_all.shape[1]
    s_idx = jnp.einsum('bqhd,bld->bqhl', qi.astype(jnp.float32), ki_all)
    score = jnp.einsum('bqhl,bqh->bql', jax.nn.relu(s_idx), wi)
    key_pos = jnp.arange(L, dtype=jnp.int32)
    allowed = key_pos[None, :] <= q_pos[:, None]
    score = jnp.where(allowed[None], score, -jnp.inf)
    _, sel = lax.top_k(score, k_sel)
    valid = sel <= q_pos[None, :, None]
    kg, vg = gather_kv(sel)
    s = jnp.einsum('bqgrd,bqkgd->bqgrk', q, kg).astype(jnp.float32) * A_HEAD_DIM ** -0.5
    s = jnp.where(valid[:, :, None, None, :], s, -jnp.inf)
    p = jax.nn.softmax(s, axis=-1).astype(vg.dtype)
    return jnp.einsum('bqgrk,bqkgd->bqgrd', p, vg)


def attn_prompt(q, k, v, qi, wi, ki):
    B, S = k.shape[:2]
    nb = S // Q_BLOCK
    k_sel = min(TOPK_MAX, S // 4)
    ki_all = ki.astype(jnp.float32)
    bidx = jnp.arange(B)[:, None, None]

    def gather_kv(sel):
        return k[bidx, sel], v[bidx, sel]

    def to_blocks(a):
        return jnp.swapaxes(a.reshape(B, nb, Q_BLOCK, *a.shape[2:]), 0, 1)

    pos = jnp.arange(S, dtype=jnp.int32).reshape(nb, Q_BLOCK)

    def one_block(args):
        qb, qib, wib, pb = args
        return select_and_attend(qb, qib, wib, ki_all, pb, gather_kv, k_sel)

    o = lax.map(one_block, (to_blocks(q), to_blocks(qi), to_blocks(wi), pos))
    return jnp.swapaxes(o, 0, 1).reshape(B, S, A_Q)


def attn_sample(q, k, v, qi, wi, ki, cache_k, cache_v, cache_idx_k, layer, page_table):
    Bd, T = k.shape[:2]
    n_pages = page_table.shape[1]
    past = n_pages * PAGE_SIZE
    ki_past = cache_idx_k[layer, page_table].reshape(Bd, past, IDX_DIM)
    ki_all = jnp.concatenate([ki_past.astype(jnp.float32), ki.astype(jnp.float32)], axis=1)
    k_sel = min(TOPK_MAX, (past + T) // 4)
    bidx = jnp.arange(Bd)[:, None, None]

    def gather_kv(sel):
        in_past = (sel < past)[..., None, None]
        ps = jnp.minimum(sel, past - 1)
        phys = page_table[bidx, ps // PAGE_SIZE]
        off = ps % PAGE_SIZE
        ns = jnp.clip(sel - past, 0, T - 1)
        kg = jnp.where(in_past, cache_k[layer, phys, off].astype(k.dtype), k[bidx, ns])
        vg = jnp.where(in_past, cache_v[layer, phys, off].astype(v.dtype), v[bidx, ns])
        return kg, vg

    pos = past + jnp.arange(T, dtype=jnp.int32)
    o = select_and_attend(q, qi, wi, ki_all, pos, gather_kv, k_sel)
    return o.reshape(Bd, T, A_Q)


def short_conv(x, buf, w):
    T = x.shape[1]
    xp = jnp.concatenate([buf, x], axis=1)
    y = xp[:, 0:T] * w[0]
    for j in range(1, CONV_W):
        y = y + xp[:, j:j + T] * w[j]
    return jax.nn.silu(y), xp[:, -(CONV_W - 1):]


def unit_lower_inverse(m):
    c = m.shape[-1]
    eye = jnp.eye(c, dtype=m.dtype)
    pw = -m
    r = eye + pw
    n = 2
    while n < c:
        pw = pw @ pw
        r = r @ (eye + pw)
        n *= 2
    return r


def to_chunks(a, c):
    B, Tp, H = a.shape[:3]
    a = a.reshape(B, Tp // c, c, H, *a.shape[3:])
    return jnp.moveaxis(a, (1, 3), (0, 2))


def gated_delta_rule(q, k, v, ld, beta, s0):
    B, T, H, dk = q.shape
    dv = v.shape[-1]
    c = min(DN_CHUNK, T)
    pad = (-T) % c
    if pad:
        pw = lambda a: jnp.pad(a, [(0, 0), (0, pad)] + [(0, 0)] * (a.ndim - 2))
        q, k, v, ld, beta = pw(q), pw(k), pw(v), pw(ld), pw(beta)
    q = to_chunks(q * dk ** -0.5, c)
    k = to_chunks(k, c)
    v = to_chunks(v, c)
    ld = to_chunks(ld, c)
    beta = to_chunks(beta, c)
    kb = k * beta[..., None]
    gc = jnp.cumsum(ld, axis=-1)
    tri_incl = jnp.tril(jnp.ones((c, c), bool))
    tri_strict = jnp.tril(jnp.ones((c, c), bool), -1)
    diff = gc[..., :, None] - gc[..., None, :]
    decay = jnp.where(tri_incl, jnp.exp(jnp.where(tri_incl, diff, 0.0)), 0.0)
    m = jnp.where(tri_strict, jnp.einsum('...id,...jd->...ij', kb, k) * decay, 0.0)
    t_inv = unit_lower_inverse(m)
    u = t_inv @ (v * beta[..., None])
    w = t_inv @ (kb * jnp.exp(gc)[..., None])
    a_qk = jnp.einsum('...id,...jd->...ij', q, k) * decay

    def step(s, inp):
        qc, kc, uc, wc, ac, gcc = inp
        v_new = uc - wc @ s
        o = (qc * jnp.exp(gcc)[..., None]) @ s + ac @ v_new
        glast = gcc[..., -1]
        kw = kc * jnp.exp(glast[..., None] - gcc)[..., None]
        s = s * jnp.exp(glast)[..., None, None] + jnp.einsum('bhck,bhcv->bhkv', kw, v_new)
        return s, o

    s_fin, o = lax.scan(step, s0, (q, k, u, w, a_qk, gc))
    o = jnp.moveaxis(o, (0, 2), (1, 3))
    o = o.reshape(B, -1, H, dv)[:, :T]
    return o, s_fin


def deltanet_branch(dn_qkv, dn_z, dn_b, dn_a, conv_buf, s0, conv_w, a_log, dt_bias, g_dn_out):
    B, T, _ = dn_qkv.shape
    qkv, new_buf = short_conv(dn_qkv, conv_buf.astype(dn_qkv.dtype), conv_w)
    q = l2_normalize(qkv[..., :DN_QK].reshape(B, T, DN_HEADS, DN_DK))
    k = l2_normalize(qkv[..., DN_QK:2 * DN_QK].reshape(B, T, DN_HEADS, DN_DK))
    v = qkv[..., 2 * DN_QK:].reshape(B, T, DN_HEADS, DN_DV).astype(jnp.float32)
    beta = jax.nn.sigmoid(dn_b.astype(jnp.float32))
    ld = -jnp.exp(a_log.astype(jnp.float32)) * jax.nn.softplus(dn_a.astype(jnp.float32) + dt_bias.astype(jnp.float32))
    o, s_fin = gated_delta_rule(q, k, v, ld, beta, s0.astype(jnp.float32))
    o = rms_norm(o, g_dn_out) * jax.nn.silu(dn_z.reshape(B, T, DN_HEADS, DN_DV).astype(jnp.float32))
    return o.reshape(B, T, DN_V).astype(dn_qkv.dtype), new_buf, s_fin


def peer_ffn(h, w_pq, sub_keys, peer_u, peer_v):
    shp = h.shape
    xt = h.reshape(-1, D_MODEL)
    n = xt.shape[0]
    pad = (-n) % PEER_BLOCK
    xt = jnp.pad(xt, ((0, pad), (0, 0)))

    def one_block(xb):
        qh = (xb @ w_pq).reshape(-1, PEER_HEADS, 2, PEER_DKEY // 2).astype(jnp.float32)
        s = jnp.einsum('nhcd,hckd->nhck', qh, sub_keys.astype(jnp.float32))
        sv, si = lax.top_k(s, PEER_TOPK)
        cand = sv[:, :, 0, :, None] + sv[:, :, 1, None, :]
        cv, ci = lax.top_k(cand.reshape(cand.shape[0], PEER_HEADS, -1), PEER_TOPK)
        e1 = jnp.take_along_axis(si[:, :, 0], ci // PEER_TOPK, axis=-1)
        e2 = jnp.take_along_axis(si[:, :, 1], ci % PEER_TOPK, axis=-1)
        e = e1 * PEER_KEYS + e2
        gsm = jax.nn.softmax(cv, axis=-1)
        u = peer_u[e]
        act = jax.nn.gelu(jnp.einsum('nhkd,nd->nhk', u, xb).astype(jnp.float32))
        ve = peer_v[e]
        return jnp.einsum('nhk,nhkd->nd', (gsm * act).astype(ve.dtype), ve)

    y = lax.map(one_block, xt.reshape(-1, PEER_BLOCK, D_MODEL))
    return y.reshape(-1, D_MODEL)[:n].reshape(shp).astype(h.dtype)


def block_output(x, o_a, o_b, gate_a, gate_b, p_emb, w_oa, w_ob, w_out, g_ffn, w_pq, sub_keys, peer_u, peer_v, g_ple, w_ple_gate, w_ple):
    m = jax.nn.sigmoid(gate_a) * (o_a @ w_oa) + jax.nn.sigmoid(gate_b) * (o_b @ w_ob)
    x = x + m @ w_out
    x = x + peer_ffn(rms_norm(x, g_ffn), w_pq, sub_keys, peer_u, peer_v)
    gate = jax.nn.sigmoid(rms_norm(x, g_ple) @ w_ple_gate)
    return x + gate * (p_emb.astype(x.dtype) @ w_ple)


def setup_inputs(seed: int = 0) -> dict:
    key = jax.random.key(seed)
    ks = jax.random.split(key, 40)
    f32 = jnp.float32

    def nrm(i, shape, scale):
        return jax.random.normal(ks[i], shape, f32) * scale

    def gain(i, n):
        return 1.0 + 0.05 * jax.random.normal(ks[i], (DEPTH, n), f32)

    n_pages = PAST_LEN // PAGE_SIZE
    n_used = DEC_BATCH * n_pages
    n_pool = n_used + n_used // 4
    page_table = jax.random.permutation(ks[0], n_pool)[:n_used].reshape(DEC_BATCH, n_pages).astype(jnp.int32)
    a_log = jnp.log(jax.random.uniform(ks[1], (DEPTH, DN_HEADS), f32, 1.0, 16.0))
    dt = jnp.exp(jax.random.uniform(ks[2], (DEPTH, DN_HEADS), f32, math.log(1e-3), math.log(1e-1)))
    dt_bias = dt + jnp.log(-jnp.expm1(-dt))
    return {
        'x_prompt': nrm(3, (BATCH, SEQ, D_MODEL), 1.0),
        'x_sample': nrm(4, (DEC_BATCH, DEC_SEQ, D_MODEL), 1.0),
        'cache_k': nrm(5, (DEPTH, n_pool, PAGE_SIZE, A_KV_HEADS, A_HEAD_DIM), 1.0),
        'cache_v': nrm(6, (DEPTH, n_pool, PAGE_SIZE, A_KV_HEADS, A_HEAD_DIM), 1.0),
        'cache_idx_k': nrm(7, (DEPTH, n_pool, PAGE_SIZE, IDX_DIM), 1.0),
        'state_conv': nrm(8, (DEPTH, DEC_BATCH, CONV_W - 1, DN_CONV_CH), 1.0),
        'state_delta': nrm(9, (DEPTH, DEC_BATCH, DN_HEADS, DN_DK, DN_DV), 0.5),
        'page_table': page_table,
        'p_prompt': nrm(10, (DEPTH, BATCH, SEQ, PLE_DIM), 1.0),
        'p_sample': nrm(11, (DEPTH, DEC_BATCH, DEC_SEQ, PLE_DIM), 1.0),
        'g_mix': gain(12, D_MODEL),
        'w_in': nrm(13, (DEPTH, D_MODEL, IN_WIDTH), D_MODEL ** -0.5),
        'g_q': gain(14, A_HEAD_DIM),
        'g_k': gain(15, A_HEAD_DIM),
        'g_idx_k': gain(16, IDX_DIM),
        'conv_w': nrm(17, (DEPTH, CONV_W, DN_CONV_CH), CONV_W ** -0.5),
        'a_log': a_log,
        'dt_bias': dt_bias,
        'g_dn_out': gain(18, DN_DV),
        'w_oa': nrm(19, (DEPTH, A_Q, D_MODEL), A_Q ** -0.5),
        'w_ob': nrm(20, (DEPTH, DN_V, D_MODEL), DN_V ** -0.5),
        'w_out': nrm(21, (DEPTH, D_MODEL, D_MODEL), D_MODEL ** -0.5),
        'g_ffn': gain(22, D_MODEL),
        'w_pq': nrm(23, (DEPTH, D_MODEL, PEER_HEADS * PEER_DKEY), D_MODEL ** -0.5),
        'sub_keys': nrm(24, (DEPTH, PEER_HEADS, 2, PEER_KEYS, PEER_DKEY // 2), (PEER_DKEY // 2) ** -0.5),
        'peer_u': nrm(25, (DEPTH, PEER_EXPERTS, D_MODEL), D_MODEL ** -0.5),
        'peer_v': nrm(26, (DEPTH, PEER_EXPERTS, D_MODEL), PEER_HEADS ** -0.5),
        'g_ple': gain(27, D_MODEL),
        'w_ple_gate': nrm(28, (DEPTH, D_MODEL, D_MODEL), D_MODEL ** -0.5),
        'w_ple': nrm(29, (DEPTH, PLE_DIM, D_MODEL), PLE_DIM ** -0.5),
    }


def reference(x_prompt, x_sample, cache_k, cache_v, cache_idx_k, state_conv, state_delta, page_table, p_prompt, p_sample, g_mix, w_in, g_q, g_k, g_idx_k, conv_w, a_log, dt_bias, g_dn_out, w_oa, w_ob, w_out, g_ffn, w_pq, sub_keys, peer_u, peer_v, g_ple, w_ple_gate, w_ple):
    xp = x_prompt
    xs = x_sample
    bp = xp.shape[0]
    kp_l, vp_l, ip_l, cp_l, sp_l = [], [], [], [], []
    ks_l, vs_l, is_l, cs_l, ss_l = [], [], [], [], []
    for i in range(DEPTH):
        q, k, v, qi, ki, wi, dq, dz, db, da, ga, gb = mixer_inputs(xp, g_mix[i], w_in[i], g_q[i], g_k[i], g_idx_k[i])
        o_a = attn_prompt(q, k, v, qi, wi, ki)
        buf0 = jnp.zeros((bp, CONV_W - 1, DN_CONV_CH), xp.dtype)
        s0 = jnp.zeros((bp, DN_HEADS, DN_DK, DN_DV), jnp.float32)
        o_b, buf, st = deltanet_branch(dq, dz, db, da, buf0, s0, conv_w[i], a_log[i], dt_bias[i], g_dn_out[i])
        xp = block_output(xp, o_a, o_b, ga, gb, p_prompt[i], w_oa[i], w_ob[i], w_out[i], g_ffn[i], w_pq[i], sub_keys[i], peer_u[i], peer_v[i], g_ple[i], w_ple_gate[i], w_ple[i])
        kp_l.append(k.astype(cache_k.dtype))
        vp_l.append(v.astype(cache_v.dtype))
        ip_l.append(ki.astype(cache_idx_k.dtype))
        cp_l.append(buf.astype(state_conv.dtype))
        sp_l.append(st.astype(state_delta.dtype))
        q, k, v, qi, ki, wi, dq, dz, db, da, ga, gb = mixer_inputs(xs, g_mix[i], w_in[i], g_q[i], g_k[i], g_idx_k[i])
        o_a = attn_sample(q, k, v, qi, wi, ki, cache_k, cache_v, cache_idx_k, i, page_table)
        o_b, buf, st = deltanet_branch(dq, dz, db, da, state_conv[i], state_delta[i], conv_w[i], a_log[i], dt_bias[i], g_dn_out[i])
        xs = block_output(xs, o_a, o_b, ga, gb, p_sample[i], w_oa[i], w_ob[i], w_out[i], g_ffn[i], w_pq[i], sub_keys[i], peer_u[i], peer_v[i], g_ple[i], w_ple_gate[i], w_ple[i])
        ks_l.append(k.astype(cache_k.dtype))
        vs_l.append(v.astype(cache_v.dtype))
        is_l.append(ki.astype(cache_idx_k.dtype))
        cs_l.append(buf.astype(state_conv.dtype))
        ss_l.append(st.astype(state_delta.dtype))
    return (xp, xs, jnp.stack(kp_l), jnp.stack(vp_l), jnp.stack(ip_l), jnp.stack(cp_l), jnp.stack(sp_l), jnp.stack(ks_l), jnp.stack(vs_l), jnp.stack(is_l), jnp.stack(cs_l), jnp.stack(ss_l))
```

```python
import functools
import math

import jax
import jax.numpy as jnp
import numpy as np
from jax import lax
from jax.experimental import pallas as pl
from jax.experimental.pallas import tpu as pltpu

F32 = jnp.float32
BF16 = jnp.bfloat16
I32 = jnp.int32

D_MODEL = 1024
PAGE_SIZE = 128
A_HEADS = 8
A_KV_HEADS = 4
A_REP = A_HEADS // A_KV_HEADS
A_HEAD_DIM = 64
A_Q = A_HEADS * A_HEAD_DIM
A_KV = A_KV_HEADS * A_HEAD_DIM
IDX_HEADS = 8
IDX_DIM = 64
TOPK_MAX = 256
DN_HEADS = 4
DN_DK = 128
DN_DV = 128
DN_QK = DN_HEADS * DN_DK
DN_V = DN_HEADS * DN_DV
DN_CONV_CH = 2 * DN_QK + DN_V
CONV_W = 4
DN_CHUNK = 64
PEER_HEADS = 8
PEER_KEYS = 128
PEER_DKEY = 256
PEER_TOPK = 16
PLE_DIM = 256
EPS = 1e-6
IN_SPLITS = (A_Q, A_KV, A_KV, IDX_HEADS * IDX_DIM, IDX_DIM, IDX_HEADS, DN_CONV_CH, DN_V, DN_HEADS, DN_HEADS, D_MODEL, D_MODEL)

LANES = 128
SUBLANES = 8
VMEM_LIMIT = 56 * 1024 * 1024

NEG_BIG = -1e30
INT_MIN = -(2 ** 31)

MISC_WI = IDX_DIM
MISC_DB = MISC_WI + IDX_HEADS
MISC_DA = MISC_DB + DN_HEADS

SEG_Q = 0
SEG_K = SEG_Q + A_Q
SEG_V = SEG_K + A_KV
SEG_QI = SEG_V + A_KV
SEG_MISC = SEG_QI + IDX_HEADS * IDX_DIM
SEG_DQKV = SEG_MISC + LANES
SEG_DZ = SEG_DQKV + DN_CONV_CH
SEG_GA = SEG_DZ + DN_V
SEG_GB = SEG_GA + D_MODEL
SEG_END = SEG_GB + D_MODEL


def _cparams(sem):
    return pltpu.CompilerParams(dimension_semantics=sem, vmem_limit_bytes=VMEM_LIMIT)


def _dot(a, b):
    return jnp.dot(a, b, preferred_element_type=F32)


def _dot_nt(a, b):
    return lax.dot_general(a, b, (((1,), (1,)), ((), ())), preferred_element_type=F32)


def _split(a):
    hi = a.astype(BF16)
    lo = (a - hi.astype(F32)).astype(BF16)
    return hi, lo


def _dot3(a, b, nt=False):
    d = _dot_nt if nt else _dot
    ah, al = _split(a)
    bh, bl = _split(b)
    return d(ah, bh) + (d(ah, bl) + d(al, bh))


def _lane_iota(shape):
    return lax.broadcasted_iota(I32, shape, len(shape) - 1)


def _row_iota(shape):
    return lax.broadcasted_iota(I32, shape, len(shape) - 2)


def _half_norm(blk, gain):
    lane = _lane_iota(blk.shape)
    lo = lane < A_HEAD_DIM
    sq = blk * blk
    s_lo = jnp.sum(jnp.where(lo, sq, 0.0), axis=-1, keepdims=True)
    s_hi = jnp.sum(jnp.where(lo, 0.0, sq), axis=-1, keepdims=True)
    r_lo = lax.rsqrt(s_lo * (1.0 / A_HEAD_DIM) + EPS)
    r_hi = lax.rsqrt(s_hi * (1.0 / A_HEAD_DIM) + EPS)
    return blk * jnp.where(lo, r_lo, r_hi) * gain


def _proj_kernel(x_ref, gmix_ref, w_ref, gq_ref, gk_ref, gik_ref,
                 q_ref, k_ref, kdup_ref, v_ref, vb_ref, qi_ref, misc_ref, kidup_ref,
                 dqkv_ref, dz_ref, ga_ref, gb_ref):
    x = x_ref[...]
    h = x * lax.rsqrt(jnp.mean(x * x, axis=-1, keepdims=True) + EPS) * gmix_ref[...]
    hb = h.astype(BF16)

    def seg(a, b):
        return _dot(hb, w_ref[:, a:b])

    lane = _lane_iota((x.shape[0], LANES))
    lo = lane < A_HEAD_DIM

    zq = seg(SEG_Q, SEG_K)
    for c in range(A_Q // LANES):
        blk = _half_norm(zq[:, c * LANES:(c + 1) * LANES], gq_ref[...])
        q_ref[:, c * LANES:(c + 1) * LANES] = (blk * (A_HEAD_DIM ** -0.5)).astype(BF16)

    zk = seg(SEG_K, SEG_V)
    for c in range(A_KV // LANES):
        blk = _half_norm(zk[:, c * LANES:(c + 1) * LANES], gk_ref[...])
        k_ref[:, c * LANES:(c + 1) * LANES] = blk
        rolled = pltpu.roll(blk, A_HEAD_DIM, axis=1)
        kdup_ref[:, (2 * c) * LANES:(2 * c + 1) * LANES] = jnp.where(lo, blk, rolled).astype(BF16)
        kdup_ref[:, (2 * c + 1) * LANES:(2 * c + 2) * LANES] = jnp.where(lo, rolled, blk).astype(BF16)

    zv = seg(SEG_V, SEG_QI)
    v_ref[...] = zv
    vb_ref[...] = zv.astype(BF16)

    qi_ref[...] = seg(SEG_QI, SEG_MISC).astype(BF16)

    zm = seg(SEG_MISC, SEG_DQKV)
    s_ik = jnp.sum(jnp.where(lo, zm * zm, 0.0), axis=-1, keepdims=True)
    kin = zm * lax.rsqrt(s_ik * (1.0 / IDX_DIM) + EPS) * gik_ref[...]
    wscale = (IDX_HEADS * IDX_DIM) ** -0.5
    misc_ref[...] = jnp.where(lo, kin, jnp.where(lane < MISC_DB, zm * wscale, zm))
    kin0 = jnp.where(lo, kin, 0.0)
    kidup_ref[...] = (kin0 + pltpu.roll(kin0, IDX_DIM, axis=1)).astype(BF16)

    dqkv_ref[...] = seg(SEG_DQKV, SEG_DZ)
    dz_ref[...] = seg(SEG_DZ, SEG_GA)
    ga_ref[...] = seg(SEG_GA, SEG_GB)
    gb_ref[...] = seg(SEG_GB, SEG_END)


def _permute_w_in(w_in):
    cuts = np.cumsum((0,) + IN_SPLITS)
    parts = [w_in[:, cuts[i]:cuts[i + 1]] for i in range(len(IN_SPLITS))]
    aq, ak, av, iq, ik, iw, dqkv, dz, db, da, ga, gb = parts
    pad = jnp.zeros((w_in.shape[0], LANES - IDX_DIM - IDX_HEADS - 2 * DN_HEADS), w_in.dtype)
    return jnp.concatenate([aq, ak, av, iq, ik, iw, db, da, pad, dqkv, dz, ga, gb], axis=1).astype(BF16)


def _tile2(g):
    return jnp.concatenate([g, g]).reshape(1, LANES).astype(F32)


def _proj(x2d, g_mix, w_perm, g_q, g_k, g_idx_k, tm):
    n = x2d.shape[0]
    assert n % tm == 0
    widths = [(A_Q, BF16), (A_KV, F32), (2 * A_KV, BF16), (A_KV, F32), (A_KV, BF16), (IDX_HEADS * IDX_DIM, BF16),
              (LANES, F32), (LANES, BF16), (DN_CONV_CH, F32), (DN_V, F32), (D_MODEL, F32), (D_MODEL, F32)]
    row = lambda i: (i, 0)
    fixed = lambda i: (0, 0)
    return pl.pallas_call(
        _proj_kernel,
        grid=(n // tm,),
        in_specs=[pl.BlockSpec((tm, D_MODEL), row), pl.BlockSpec((1, D_MODEL), fixed),
                  pl.BlockSpec((D_MODEL, SEG_END), fixed), pl.BlockSpec((1, LANES), fixed),
                  pl.BlockSpec((1, LANES), fixed), pl.BlockSpec((1, LANES), fixed)],
        out_specs=[pl.BlockSpec((tm, w), row) for w, _ in widths],
        out_shape=[jax.ShapeDtypeStruct((n, w), dt) for w, dt in widths],
        compiler_params=_cparams(("parallel",)),
        name="proj_in",
    )(x2d, g_mix.reshape(1, D_MODEL), w_perm, _tile2(g_q), _tile2(g_k), _tile2(g_idx_k))


def _sort_key(score):
    score = jnp.where(score == 0.0, 0.0, score)
    bits = pltpu.bitcast(score, I32)
    return jnp.where(bits < 0, bits ^ jnp.int32(0x7FFFFFFF), bits)


def _index_scores(qi, wi_tile, kblk):
    lane = _lane_iota((qi.shape[0], LANES))
    lo = lane < IDX_DIM
    sc = None
    for c in range(IDX_HEADS // 2):
        q128 = qi[:, c * LANES:(c + 1) * LANES]
        zero = jnp.zeros_like(q128)
        for half in range(2):
            qm = jnp.where(lo, q128, zero) if half == 0 else jnp.where(lo, zero, q128)
            s = _dot_nt(qm, kblk)
            hidx = MISC_WI + 2 * c + half
            term = jnp.maximum(s, 0.0) * wi_tile[:, hidx:hidx + 1]
            sc = term if sc is None else sc + term
    return sc


def _select_rows(keys_ref, thr_ref, jcut_ref, nvalid, k_sel, tq, tk, rb):
    nchunk = tk // LANES
    nbits_idx = int(math.ceil(math.log2(keys_ref.shape[0] * tk))) + 1

    for r in range(tq // rb):
        r0 = r * rb

        def count(pred):
            def body(kb, cnt):
                blk = keys_ref[kb, r0:r0 + rb, :]
                for c in range(nchunk):
                    cnt = cnt + pred(blk[:, c * LANES:(c + 1) * LANES], kb * tk + c * LANES)
                return cnt
            cnt = lax.fori_loop(0, nvalid, body, jnp.zeros((rb, LANES), I32))
            return jnp.sum(cnt, axis=1, keepdims=True)

        def bit_step(i, cur):
            cand = cur + lax.shift_left(jnp.int32(1), 31 - i)
            cand_b = jnp.broadcast_to(cand, (rb, LANES))
            c_ge = count(lambda kv, base: jnp.where(kv >= cand_b, 1, 0))
            return jnp.where(c_ge >= k_sel, cand, cur)

        thr = lax.fori_loop(0, 32, bit_step, jnp.full((rb, 1), INT_MIN, I32))
        thr_b = jnp.broadcast_to(thr, (rb, LANES))
        c_gt = count(lambda kv, base: jnp.where(kv > thr_b, 1, 0))
        c_ge = count(lambda kv, base: jnp.where(kv >= thr_b, 1, 0))
        need = k_sel - c_gt
        thr_ref[r0:r0 + rb, :] = thr_b
        jcut_ref[r0:r0 + rb, :] = jnp.full((rb, LANES), 2 ** 30, I32)

        @pl.when(jnp.max(c_ge) > k_sel)
        def _():
            lane = _lane_iota((rb, LANES))

            def jstep(i, jcur):
                candj = jcur + lax.shift_left(jnp.int32(1), nbits_idx - 1 - i)
                candj_b = jnp.broadcast_to(candj, (rb, LANES))
                f = count(lambda kv, base: jnp.where(kv == thr_b, jnp.where(lane + base < candj_b, 1, 0), 0))
                return jnp.where(f < need, candj, jcur)

            jcut = lax.fori_loop(0, nbits_idx, jstep, jnp.zeros((rb, 1), I32))
            jcut_ref[r0:r0 + rb, :] = jnp.broadcast_to(jcut, (rb, LANES))


def _idx_mask_kernel(qi_ref, wi_ref, kidup_ref, mask_ref, keys_ref, thr_ref, jcut_ref, *, k_sel, tq, tk, rb):
    i = pl.program_id(1)
    nkb = keys_ref.shape[0]
    nvalid = ((i + 1) * tq + tk - 1) // tk
    qi = qi_ref[0]
    wi_tile = wi_ref[0]
    q_pos = i * tq + _row_iota((tq, tk))
    lane = _lane_iota((tq, tk))

    def fill(kb, carry):
        kblk = kidup_ref[0, pl.ds(pl.multiple_of(kb * tk, tk), tk), :]
        sc = _index_scores(qi, wi_tile, kblk)
        sc = jnp.where(lane + kb * tk <= q_pos, sc, -jnp.inf)
        keys_ref[kb] = _sort_key(sc)
        return carry

    lax.fori_loop(0, nvalid, fill, 0)
    _select_rows(keys_ref, thr_ref, jcut_ref, nvalid, k_sel, tq, tk, rb)

    for kb in range(nkb):
        @pl.when(kb < nvalid)
        def _():
            thr = jnp.broadcast_to(thr_ref[:, 0:1], (tq, tk))
            jcut = jnp.broadcast_to(jcut_ref[:, 0:1], (tq, tk))
            key = keys_ref[kb]
            pos = lane + kb * tk
            sel = jnp.where(key > thr, 1, jnp.where(key == thr, jnp.where(pos <= jcut, 1, 0), 0))
            sel = jnp.where(pos <= q_pos, sel, 0)
            mask_ref[0, :, kb * tk:(kb + 1) * tk] = sel.astype(jnp.int8)

        @pl.when(kb >= nvalid)
        def _():
            mask_ref[0, :, kb * tk:(kb + 1) * tk] = jnp.zeros((tq, tk), jnp.int8)


def _idx_mask(qi, misc, kidup, k_sel, tq, tk):
    b, s, _ = qi.shape
    assert s % tq == 0 and s % tk == 0 and tq % 32 == 0
    rb = min(64, tq)
    kern = functools.partial(_idx_mask_kernel, k_sel=k_sel, tq=tq, tk=tk, rb=rb)
    return pl.pallas_call(
        kern,
        grid=(b, s // tq),
        in_specs=[pl.BlockSpec((1, tq, IDX_HEADS * IDX_DIM), lambda bi, i: (bi, i, 0)),
                  pl.BlockSpec((1, tq, LANES), lambda bi, i: (bi, i, 0)),
                  pl.BlockSpec((1, s, LANES), lambda bi, i: (bi, 0, 0))],
        out_specs=pl.BlockSpec((1, tq, s), lambda bi, i: (bi, i, 0)),
        out_shape=jax.ShapeDtypeStruct((b, s, s), jnp.int8),
        scratch_shapes=[pltpu.VMEM((s // tk, tq, tk), I32), pltpu.VMEM((tq, LANES), I32), pltpu.VMEM((tq, LANES), I32)],
        compiler_params=_cparams(("parallel", "parallel")),
        name="idx_mask",
    )(qi, misc, kidup)


def _attn_kernel(q_ref, kdup_ref, vb_ref, mask_ref, o_ref, m_ref, l_ref, acc_ref, *, tq, tk):
    i = pl.program_id(1)
    kb = pl.program_id(2)
    nk = pl.num_programs(2)
    last_needed = ((i + 1) * tq - 1) // tk

    @pl.when(kb == 0)
    def _():
        m_ref[...] = jnp.full(m_ref.shape, NEG_BIG, F32)
        l_ref[...] = jnp.zeros(l_ref.shape, F32)
        acc_ref[...] = jnp.zeros(acc_ref.shape, F32)

    @pl.when(kb <= last_needed)
    def _():
        bias = jnp.where(mask_ref[0].astype(I32) != 0, 0.0, NEG_BIG).astype(F32)
        bias2 = jnp.concatenate([bias, bias], axis=0)
        lane = _lane_iota((tq, LANES))
        lo = lane < A_HEAD_DIM
        for g in range(A_KV_HEADS):
            q128 = q_ref[0, :, g * LANES:(g + 1) * LANES]
            zero = jnp.zeros_like(q128)
            q2 = jnp.concatenate([jnp.where(lo, q128, zero), jnp.where(lo, zero, q128)], axis=0)
            s = _dot_nt(q2, kdup_ref[0, :, g * LANES:(g + 1) * LANES]) + bias2
            m_old = m_ref[g]
            m_new = jnp.maximum(m_old, jnp.max(s, axis=1, keepdims=True))
            alpha = jnp.exp(m_old - m_new)
            p = jnp.exp(s - m_new[:, 0:1])
            l_ref[g] = alpha * l_ref[g] + jnp.sum(p, axis=1, keepdims=True)
            v128 = vb_ref[0, :, (g // 2) * LANES:(g // 2 + 1) * LANES]
            acc_ref[g] = alpha * acc_ref[g] + _dot(p.astype(BF16), v128)
            m_ref[g] = m_new

    @pl.when(kb == nk - 1)
    def _():
        lane = _lane_iota((tq, LANES))
        lo = lane < A_HEAD_DIM
        for g in range(A_KV_HEADS):
            a = acc_ref[g] / l_ref[g]
            top, bot = a[:tq], a[tq:]
            if g % 2 == 0:
                o128 = jnp.where(lo, top, pltpu.roll(bot, A_HEAD_DIM, axis=1))
            else:
                o128 = jnp.where(lo, pltpu.roll(top, A_HEAD_DIM, axis=1), bot)
            o_ref[0, :, g * LANES:(g + 1) * LANES] = o128


def _attn(q, kdup, vb, mask, tq, tk):
    b, s, _ = q.shape
    nk = s // tk

    def kv_idx(bi, i, kb):
        return (bi, jnp.minimum(kb, ((i + 1) * tq - 1) // tk), 0)

    def mask_idx(bi, i, kb):
        return (bi, i, jnp.minimum(kb, ((i + 1) * tq - 1) // tk))

    kern = functools.partial(_attn_kernel, tq=tq, tk=tk)
    return pl.pallas_call(
        kern,
        grid=(b, s // tq, nk),
        in_specs=[pl.BlockSpec((1, tq, A_Q), lambda bi, i, kb: (bi, i, 0)),
                  pl.BlockSpec((1, tk, 2 * A_KV), kv_idx),
                  pl.BlockSpec((1, tk, A_KV), kv_idx),
                  pl.BlockSpec((1, tq, tk), mask_idx)],
        out_specs=pl.BlockSpec((1, tq, A_Q), lambda bi, i, kb: (bi, i, 0)),
        out_shape=jax.ShapeDtypeStruct((b, s, A_Q), F32),
        scratch_shapes=[pltpu.VMEM((A_KV_HEADS, 2 * tq, LANES), F32)] * 3,
        compiler_params=_cparams(("parallel", "parallel", "arbitrary")),
        name="sel_attn",
    )(q, kdup, vb, mask)


def _conv_kernel(x_ref, xprev_ref, halo0_ref, w_ref, q_ref, k_ref, v_ref, *, tt):
    i = pl.program_id(1)
    x = x_ref[0]
    halo = jnp.where(i == 0, halo0_ref[0], xprev_ref[0])
    w = w_ref[...]

    def post(y, rows):
        y = y * jax.nn.sigmoid(y)
        for h in range(DN_HEADS):
            for j, ref in enumerate((q_ref, k_ref)):
                blk = y[:, j * DN_QK + h * DN_DK: j * DN_QK + (h + 1) * DN_DK]
                blk = blk * lax.rsqrt(jnp.sum(blk * blk, axis=-1, keepdims=True) + EPS)
                ref[0, 0:rows, h * DN_DK:(h + 1) * DN_DK] = blk
        v_ref[0, 0:rows, :] = y[:, 2 * DN_QK:]

    y = x * w[CONV_W - 1:CONV_W, :]
    for j in range(1, CONV_W):
        y = y + pltpu.roll(x, j, axis=0) * w[CONV_W - 1 - j:CONV_W - j, :]
    post(y, tt)

    x8 = x[0:SUBLANES]
    row = _row_iota(x8.shape)
    y8 = x8 * w[CONV_W - 1:CONV_W, :]
    for j in range(1, CONV_W):
        xs = jnp.where(row < j, pltpu.roll(halo, j, axis=0), pltpu.roll(x8, j, axis=0))
        y8 = y8 + xs * w[CONV_W - 1 - j:CONV_W - j, :]
    post(y8, SUBLANES)


def _conv(dqkv, halo0, conv_w, tt):
    b, t, c = dqkv.shape
    assert t % tt == 0 and tt % SUBLANES == 0
    r = tt // SUBLANES
    kern = functools.partial(_conv_kernel, tt=tt)
    out = jax.ShapeDtypeStruct((b, t, DN_QK), F32)
    return pl.pallas_call(
        kern,
        grid=(b, t // tt),
        in_specs=[pl.BlockSpec((1, tt, c), lambda bi, i: (bi, i, 0)),
                  pl.BlockSpec((1, SUBLANES, c), lambda bi, i: (bi, jnp.maximum(i * r - 1, 0), 0)),
                  pl.BlockSpec((1, SUBLANES, c), lambda bi, i: (bi, 0, 0)),
                  pl.BlockSpec((CONV_W, c), lambda bi, i: (0, 0))],
        out_specs=[pl.BlockSpec((1, tt, DN_QK), lambda bi, i: (bi, i, 0))] * 3,
        out_shape=[out, out, out],
        compiler_params=_cparams(("parallel", "parallel")),
        name="dn_conv",
    )(dqkv, dqkv, halo0, conv_w)


def _softplus(x):
    return jnp.maximum(x, 0.0) + jnp.log(1.0 + jnp.exp(-jnp.abs(x)))


def _pad_rows(a, rows):
    if a.shape[0] == rows:
        return a
    return jnp.concatenate([a, jnp.zeros((rows - a.shape[0], a.shape[1]), a.dtype)], axis=0)


def _delta_kernel(q_ref, k_ref, v_ref, z_ref, misc_ref, nega_ref, dtb_ref, gout_ref, s0_ref, o_ref, st_ref, *, c, t_valid):
    n = pl.program_id(1)

    @pl.when(n == 0)
    def _():
        st_ref[...] = s0_ref[...]

    mt = misc_ref[0]
    row = _row_iota((c, LANES))
    live = row < t_valid
    beta_t = jnp.where(live, jax.nn.sigmoid(mt), 0.0)
    ld_t = jnp.where(live, nega_ref[...] * _softplus(mt + dtb_ref[...]), 0.0)
    gc_t = ld_t
    sft = 1
    while sft < c:
        gc_t = gc_t + jnp.where(row >= sft, pltpu.roll(gc_t, sft, axis=0), 0.0)
        sft *= 2
    gc_tr = _pad_rows(gc_t, LANES).T

    ri = lax.broadcasted_iota(I32, (c, c), 0)
    ci = lax.broadcasted_iota(I32, (c, c), 1)
    incl = ri >= ci
    strict = ri > ci
    eye = jnp.where(ri == ci, 1.0, 0.0).astype(F32)
    scale = DN_DK ** -0.5

    for h in range(DN_HEADS):
        sl = slice(h * DN_DK, (h + 1) * DN_DK)
        q = jnp.where(live, q_ref[0, :, sl] * scale, 0.0)
        k = jnp.where(live, k_ref[0, :, sl], 0.0)
        v = jnp.where(live, v_ref[0, :, sl], 0.0)
        beta = beta_t[:, MISC_DB + h:MISC_DB + h + 1]
        gcc = gc_t[:, MISC_DA + h:MISC_DA + h + 1]
        gcr = gc_tr[MISC_DA + h:MISC_DA + h + 1, 0:c]
        decay = jnp.where(incl, jnp.exp(jnp.where(incl, gcc - gcr, 0.0)), 0.0)
        kb = k * beta
        m = jnp.where(strict, _dot3(kb, k, nt=True) * decay, 0.0)
        pw = -m
        r = eye + pw
        nn = 2
        while nn < c:
            pw = _dot3(pw, pw)
            r = _dot3(r, eye + pw)
            nn *= 2
        egc = jnp.exp(gcc)
        u = _dot3(r, v * beta)
        w = _dot3(r, kb * egc)
        a_qk = _dot3(q, k, nt=True) * decay
        s = st_ref[0, h]
        v_new = u - _dot3(w, s)
        o = _dot3(q * egc, s) + _dot3(a_qk, v_new)
        glast = gcc[c - 1:c, :]
        kw = k * jnp.exp(glast - gcc)
        kw_t = _pad_rows(kw, LANES).T
        st_ref[0, h] = s * jnp.exp(glast) + _dot3(kw_t, _pad_rows(v_new, LANES))
        on = o * lax.rsqrt(jnp.mean(o * o, axis=-1, keepdims=True) + EPS) * gout_ref[...]
        z = z_ref[0, :, sl]
        o_ref[0, :, sl] = on * (z * jax.nn.sigmoid(z))


def _delta(qn, kn, v, dz, misc, a_log, dt_bias, g_dn_out, s0, c, t_valid):
    b, t, _ = qn.shape
    assert t % c == 0 and (t_valid == t or t == c)
    nega = jnp.zeros((1, LANES), F32).at[0, MISC_DA:MISC_DA + DN_HEADS].set(-jnp.exp(a_log.astype(F32)))
    dtb = jnp.zeros((1, LANES), F32).at[0, MISC_DA:MISC_DA + DN_HEADS].set(dt_bias.astype(F32))
    tok = lambda bi, n: (bi, n, 0)
    fixed = lambda bi, n: (0, 0)
    st = lambda bi, n: (bi, 0, 0, 0)
    kern = functools.partial(_delta_kernel, c=c, t_valid=t_valid)
    return pl.pallas_call(
        kern,
        grid=(b, t // c),
        in_specs=[pl.BlockSpec((1, c, DN_QK), tok), pl.BlockSpec((1, c, DN_QK), tok), pl.BlockSpec((1, c, DN_V), tok),
                  pl.BlockSpec((1, c, DN_V), tok), pl.BlockSpec((1, c, LANES), tok),
                  pl.BlockSpec((1, LANES), fixed), pl.BlockSpec((1, LANES), fixed), pl.BlockSpec((1, DN_DV), fixed),
                  pl.BlockSpec((1, DN_HEADS, DN_DK, DN_DV), st)],
        out_specs=[pl.BlockSpec((1, c, DN_V), tok), pl.BlockSpec((1, DN_HEADS, DN_DK, DN_DV), st)],
        out_shape=[jax.ShapeDtypeStruct((b, t, DN_V), F32), jax.ShapeDtypeStruct((b, DN_HEADS, DN_DK, DN_DV), F32)],
        compiler_params=_cparams(("parallel", "arbitrary")),
        name="delta_rule",
    )(qn, kn, v, dz, misc, nega, dtb, g_dn_out.reshape(1, DN_DV).astype(F32), s0)


def _merge_kernel(x_ref, oa_ref, ob_ref, ga_ref, gb_ref, woa_ref, wob_ref, wout_ref, gffn_ref, wpqt_ref,
                  x1_ref, h2_ref, qt_ref):
    ma = jax.nn.sigmoid(ga_ref[...]) * _dot(oa_ref[...].astype(BF16), woa_ref[...])
    mb = jax.nn.sigmoid(gb_ref[...]) * _dot(ob_ref[...].astype(BF16), wob_ref[...])
    x1 = x_ref[...] + _dot((ma + mb).astype(BF16), wout_ref[...])
    x1_ref[...] = x1
    h2 = x1 * lax.rsqrt(jnp.mean(x1 * x1, axis=-1, keepdims=True) + EPS) * gffn_ref[...]
    h2_ref[...] = h2
    qt_ref[...] = _dot_nt(wpqt_ref[...], h2.astype(BF16))


def _merge(x2d, oa, ob, ga, gb, w_oa, w_ob, w_out, g_ffn, w_pq, tm):
    n = x2d.shape[0]
    nq = w_pq.shape[1]
    row = lambda i: (i, 0)
    fixed = lambda i: (0, 0)
    return pl.pallas_call(
        _merge_kernel,
        grid=(n // tm,),
        in_specs=[pl.BlockSpec((tm, D_MODEL), row), pl.BlockSpec((tm, A_Q), row), pl.BlockSpec((tm, DN_V), row),
                  pl.BlockSpec((tm, D_MODEL), row), pl.BlockSpec((tm, D_MODEL), row),
                  pl.BlockSpec((A_Q, D_MODEL), fixed), pl.BlockSpec((DN_V, D_MODEL), fixed),
                  pl.BlockSpec((D_MODEL, D_MODEL), fixed), pl.BlockSpec((1, D_MODEL), fixed),
                  pl.BlockSpec((nq, D_MODEL), fixed)],
        out_specs=[pl.BlockSpec((tm, D_MODEL), row), pl.BlockSpec((tm, D_MODEL), row), pl.BlockSpec((nq, tm), lambda i: (0, i))],
        out_shape=[jax.ShapeDtypeStruct((n, D_MODEL), F32), jax.ShapeDtypeStruct((n, D_MODEL), F32),
                   jax.ShapeDtypeStruct((nq, n), F32)],
        compiler_params=_cparams(("parallel",)),
        name="merge_out",
    )(x2d, oa, ob, ga, gb, w_oa.astype(BF16), w_ob.astype(BF16), w_out.astype(BF16),
      g_ffn.reshape(1, D_MODEL).astype(F32), w_pq.T.astype(BF16))


def _peer_cands():
    cands = [(i, j) for i in range(PEER_TOPK) for j in range(PEER_TOPK) if (i + 1) * (j + 1) <= PEER_TOPK]
    return sorted(cands, key=lambda ij: ij[0] * PEER_TOPK + ij[1])


def _route_kernel(qt_ref, sk_ref, eid_ref, gate_ref, val_ref, idx_ref, *, tn):
    half = PEER_DKEY // 2
    key_iota = lax.broadcasted_iota(I32, (PEER_KEYS, tn), 0)

    def per_table(t, carry):
        h = t // 2
        qblk = qt_ref[pl.ds(pl.multiple_of(t * half, half), half), :]
        s = _dot3(sk_ref[t], qblk)
        for r in range(PEER_TOPK):
            m = jnp.max(s, axis=0, keepdims=True)
            am = jnp.min(jnp.where(s == m, key_iota, PEER_KEYS), axis=0, keepdims=True)
            val_ref[t % 2, r, pl.ds(h, 1), :] = m
            idx_ref[t % 2, r, pl.ds(h, 1), :] = am
            s = jnp.where(key_iota == am, -jnp.inf, s)
        return carry

    lax.fori_loop(0, 2 * PEER_HEADS, per_table, 0)

    cands = _peer_cands()
    cv = [val_ref[0, i] + val_ref[1, j] for i, j in cands]
    ce = [idx_ref[0, i] * PEER_KEYS + idx_ref[1, j] for i, j in cands]
    nc = len(cands)
    rank = []
    for a in range(nc):
        rk = jnp.zeros(cv[a].shape, I32)
        for b in range(nc):
            if b < a:
                rk = rk + jnp.where(cv[b] >= cv[a], 1, 0)
            elif b > a:
                rk = rk + jnp.where(cv[b] > cv[a], 1, 0)
        rank.append(rk)
    mx = cv[0]
    ex = [jnp.where(rank[a] < PEER_TOPK, jnp.exp(cv[a] - mx), 0.0) for a in range(nc)]
    den = ex[0]
    for a in range(1, nc):
        den = den + ex[a]
    inv = 1.0 / den
    for slot in range(PEER_TOPK):
        e = jnp.zeros(cv[0].shape, I32)
        g = jnp.zeros(cv[0].shape, F32)
        for a in range(nc):
            hit = rank[a] == slot
            e = jnp.where(hit, ce[a], e)
            g = jnp.where(hit, ex[a], g)
        eid_ref[slot] = e
        gate_ref[slot] = g * inv


def _route(qt, sub_keys, tn):
    nq, n = qt.shape
    sk = sub_keys.reshape(2 * PEER_HEADS, PEER_KEYS, PEER_DKEY // 2).astype(F32)
    kern = functools.partial(_route_kernel, tn=tn)
    blk = pl.BlockSpec((PEER_TOPK, PEER_HEADS, tn), lambda i: (0, 0, i))
    return pl.pallas_call(
        kern,
        grid=(n // tn,),
        in_specs=[pl.BlockSpec((nq, tn), lambda i: (0, i)),
                  pl.BlockSpec((2 * PEER_HEADS, PEER_KEYS, PEER_DKEY // 2), lambda i: (0, 0, 0))],
        out_specs=[blk, blk],
        out_shape=[jax.ShapeDtypeStruct((PEER_TOPK, PEER_HEADS, n), I32), jax.ShapeDtypeStruct((PEER_TOPK, PEER_HEADS, n), F32)],
        scratch_shapes=[pltpu.VMEM((2, PEER_TOPK, PEER_HEADS, tn), F32), pltpu.VMEM((2, PEER_TOPK, PEER_HEADS, tn), I32)],
        compiler_params=_cparams(("parallel",)),
        name="peer_route",
    )(qt, sk)


def _pack_table(tab):
    e, d = tab.shape
    bits = lax.bitcast_convert_type(tab.astype(BF16), jnp.uint16).astype(jnp.uint32)
    words = (bits[0::2] << 16) | bits[1::2]
    return lax.bitcast_convert_type(words, I32).reshape(e // 2, d // LANES, LANES)


def _unpack_row(word, e):
    sh = (e & 1) * 16
    return pltpu.bitcast(jnp.left_shift(word, sh) & jnp.int32(-65536), F32)


def _peer_u_kernel(eid_ref, x_ref, gate_ref, tab_ref, coef_ref, r_ref, *, tn):
    npair = eid_ref.shape[0]
    lane = _lane_iota((npair, tn))
    sub = lax.broadcasted_iota(I32, (SUBLANES, LANES), 0)

    def per_token(t, acc):
        xt = x_ref[t]

        def per_group(g, carry):
            prods = []
            for p in (0, 4, 2, 6, 1, 5, 3, 7):
                e = eid_ref[g * SUBLANES + p, t]
                prods.append(_unpack_row(tab_ref[e >> 1], e) * xt)
            step = SUBLANES // 2
            while step >= 1:
                nxt = []
                for a in range(0, len(prods), 2):
                    lo_t, hi_t = prods[a], prods[a + 1]
                    keep = (sub & step) == 0
                    nxt.append(jnp.where(keep, lo_t, pltpu.roll(hi_t, step, axis=0))
                               + jnp.where(keep, pltpu.roll(lo_t, SUBLANES - step, axis=0), hi_t))
                prods = nxt
                step //= 2
            r_ref[pl.ds(pl.multiple_of(g * SUBLANES, SUBLANES), SUBLANES), :] = prods[0]
            return carry

        lax.fori_loop(0, npair // SUBLANES, per_group, 0)
        col = jnp.sum(r_ref[...], axis=1, keepdims=True)
        return jnp.where(lane == t, col, acc)

    act = lax.fori_loop(0, tn, per_token, jnp.zeros((npair, tn), F32))
    coef_ref[...] = gate_ref[...] * jax.nn.gelu(act)


def _peer_v_kernel(eid_ref, coef_ref, tab_ref, y_ref, *, tn):
    npair = eid_ref.shape[0]
    nacc = 4

    def per_token(t, carry):
        def per_group(g, accs):
            accs = list(accs)
            for p in range(SUBLANES):
                kk = g * SUBLANES + p
                e = eid_ref[kk, t]
                accs[p % nacc] = accs[p % nacc] + coef_ref[kk, t] * _unpack_row(tab_ref[e >> 1], e)
            return tuple(accs)

        zero = jnp.zeros((SUBLANES, LANES), F32)
        accs = lax.fori_loop(0, npair // SUBLANES, per_group, (zero,) * nacc)
        y_ref[t] = (accs[0] + accs[1]) + (accs[2] + accs[3])
        return carry

    lax.fori_loop(0, tn, per_token, 0)


def _peer_experts(eid, gate, h2, tab_u, tab_v, tn):
    npair, n = eid.shape
    d = h2.shape[1]
    x3 = h2.reshape(n, d // LANES, LANES)
    tab_spec = pl.BlockSpec(tab_u.shape, lambda i: (0, 0, 0), pipeline_mode=pl.Buffered(1))
    smem_blk = lambda: pl.BlockSpec((npair, tn), lambda i: (0, i), memory_space=pltpu.SMEM)
    coef = pl.pallas_call(
        functools.partial(_peer_u_kernel, tn=tn),
        grid=(n // tn,),
        in_specs=[smem_blk(), pl.BlockSpec((tn, d // LANES, LANES), lambda i: (i, 0, 0)),
                  pl.BlockSpec((npair, tn), lambda i: (0, i)), tab_spec],
        out_specs=pl.BlockSpec((npair, tn), lambda i: (0, i)),
        out_shape=jax.ShapeDtypeStruct((npair, n), F32),
        scratch_shapes=[pltpu.VMEM((npair, LANES), F32)],
        compiler_params=_cparams(("arbitrary",)),
        name="peer_u",
    )(eid, x3, gate, tab_u)
    y3 = pl.pallas_call(
        functools.partial(_peer_v_kernel, tn=tn),
        grid=(n // tn,),
        in_specs=[smem_blk(), smem_blk(), tab_spec],
        out_specs=pl.BlockSpec((tn, d // LANES, LANES), lambda i: (i, 0, 0)),
        out_shape=jax.ShapeDtypeStruct((n, d // LANES, LANES), F32),
        compiler_params=_cparams(("arbitrary",)),
        name="peer_v",
    )(eid, coef, tab_v)
    return y3.reshape(n, d)


def _ple_kernel(x1_ref, yp_ref, p_ref, gple_ref, wg_ref, wp_ref, o_ref):
    x2 = x1_ref[...] + yp_ref[...]
    hn = x2 * lax.rsqrt(jnp.mean(x2 * x2, axis=-1, keepdims=True) + EPS) * gple_ref[...]
    gate = jax.nn.sigmoid(_dot(hn.astype(BF16), wg_ref[...]))
    o_ref[...] = x2 + gate * _dot(p_ref[...].astype(BF16), wp_ref[...])


def _ple(x1, yp, p2d, g_ple, w_ple_gate, w_ple, tm):
    n = x1.shape[0]
    row = lambda i: (i, 0)
    fixed = lambda i: (0, 0)
    return pl.pallas_call(
        _ple_kernel,
        grid=(n // tm,),
        in_specs=[pl.BlockSpec((tm, D_MODEL), row), pl.BlockSpec((tm, D_MODEL), row), pl.BlockSpec((tm, PLE_DIM), row),
                  pl.BlockSpec((1, D_MODEL), fixed), pl.BlockSpec((D_MODEL, D_MODEL), fixed),
                  pl.BlockSpec((PLE_DIM, D_MODEL), fixed)],
        out_specs=pl.BlockSpec((tm, D_MODEL), row),
        out_shape=jax.ShapeDtypeStruct((n, D_MODEL), F32),
        compiler_params=_cparams(("parallel",)),
        name="ple_out",
    )(x1, yp, p2d, g_ple.reshape(1, D_MODEL).astype(F32), w_ple_gate.astype(BF16), w_ple.astype(BF16))


def _attn_sample_jax(q, k, v, qi, wi, ki, cache_k, cache_v, cache_idx_k, page_table):
    bd, t = k.shape[:2]
    n_pages = page_table.shape[1]
    past = n_pages * PAGE_SIZE
    ki_past = cache_idx_k[page_table].reshape(bd, past, IDX_DIM)
    ki_all = jnp.concatenate([ki_past.astype(F32), ki.astype(F32)], axis=1)
    k_sel = min(TOPK_MAX, (past + t) // 4)
    bidx = jnp.arange(bd)[:, None, None]
    q_pos = past + jnp.arange(t, dtype=I32)
    length = ki_all.shape[1]
    s_idx = jnp.einsum('bqhd,bld->bqhl', qi.astype(F32), ki_all)
    score = jnp.einsum('bqhl,bqh->bql', jax.nn.relu(s_idx), wi)
    allowed = jnp.arange(length, dtype=I32)[None, :] <= q_pos[:, None]
    score = jnp.where(allowed[None], score, -jnp.inf)
    _, sel = lax.top_k(score, k_sel)
    valid = sel <= q_pos[None, :, None]
    in_past = (sel < past)[..., None, None]
    ps = jnp.minimum(sel, past - 1)
    phys = page_table[bidx, ps // PAGE_SIZE]
    off = ps % PAGE_SIZE
    ns = jnp.clip(sel - past, 0, t - 1)
    kg = jnp.where(in_past, cache_k[phys, off].astype(k.dtype), k[bidx, ns])
    vg = jnp.where(in_past, cache_v[phys, off].astype(v.dtype), v[bidx, ns])
    s = jnp.einsum('bqgrd,bqkgd->bqgrk', q, kg).astype(F32) * A_HEAD_DIM ** -0.5
    s = jnp.where(valid[:, :, None, None, :], s, -jnp.inf)
    p = jax.nn.softmax(s, axis=-1).astype(vg.dtype)
    return jnp.einsum('bqgrk,bqkgd->bqgrd', p, vg).reshape(bd, t, A_Q)


def _tile(n, pref):
    return pref if n % pref == 0 else n


def _layer(x, p_emb, conv_state, delta_state, attn_fn, prm):
    b, t, _ = x.shape
    n = b * t
    x2d = x.reshape(n, D_MODEL)
    tm = _tile(n, 256)
    (q, k, kdup, v, vb, qi, misc, kidup, dqkv, dz, ga, gb) = _proj(
        x2d, prm['g_mix'], prm['w_perm'], prm['g_q'], prm['g_k'], prm['g_idx_k'], tm)

    oa = attn_fn(q, k, kdup, v, vb, qi, misc, kidup)

    tp = -(-t // SUBLANES) * SUBLANES
    c = min(DN_CHUNK, tp)
    pad_t = lambda a: jnp.pad(a.reshape(b, t, a.shape[-1]), ((0, 0), (0, tp - t), (0, 0)))
    dqkv3 = dqkv.reshape(b, t, DN_CONV_CH)
    halo = jnp.zeros((b, SUBLANES, DN_CONV_CH), F32)
    if conv_state is not None:
        halo = halo.at[:, SUBLANES - (CONV_W - 1):].set(conv_state.astype(F32))
        hist = jnp.concatenate([conv_state.astype(F32), dqkv3], axis=1)
    else:
        hist = jnp.concatenate([jnp.zeros((b, CONV_W - 1, DN_CONV_CH), F32), dqkv3], axis=1)
    new_conv = hist[:, -(CONV_W - 1):]
    qn, kn, vv = _conv(pad_t(dqkv), halo, prm['conv_w'], _tile(tp, 512))
    s0 = jnp.zeros((b, DN_HEADS, DN_DK, DN_DV), F32) if delta_state is None else delta_state.astype(F32)
    ob, new_delta = _delta(qn, kn, vv, pad_t(dz), pad_t(misc), prm['a_log'], prm['dt_bias'], prm['g_dn_out'], s0, c, t)
    ob = ob[:, :t].reshape(n, DN_V)

    x1, h2, qt = _merge(x2d, oa, ob, ga, gb, prm['w_oa'], prm['w_ob'], prm['w_out'], prm['g_ffn'], prm['w_pq'], tm)
    eid, gate = _route(qt, prm['sub_keys'], _tile(n, LANES))
    npair = PEER_TOPK * PEER_HEADS
    yp = _peer_experts(eid.reshape(npair, n), gate.reshape(npair, n), h2, prm['tab_u'], prm['tab_v'], _tile(n, LANES))
    y = _ple(x1, yp, p_emb.reshape(n, PLE_DIM), prm['g_ple'], prm['w_ple_gate'], prm['w_ple'], tm)

    return (y.reshape(b, t, D_MODEL), k.reshape(b, t, A_KV_HEADS, A_HEAD_DIM), v.reshape(b, t, A_KV_HEADS, A_HEAD_DIM),
            misc[:, :IDX_DIM].reshape(b, t, IDX_DIM), new_conv, new_delta)


def kernel(x_prompt, x_sample, cache_k, cache_v, cache_idx_k, state_conv, state_delta, page_table, p_prompt, p_sample,
           g_mix, w_in, g_q, g_k, g_idx_k, conv_w, a_log, dt_bias, g_dn_out, w_oa, w_ob, w_out, g_ffn, w_pq, sub_keys,
           peer_u, peer_v, g_ple, w_ple_gate, w_ple):
    depth = w_in.shape[0]
    xp, xs = x_prompt, x_sample
    outs = [[] for _ in range(10)]
    for i in range(depth):
        prm = dict(g_mix=g_mix[i], w_perm=_permute_w_in(w_in[i]), g_q=g_q[i], g_k=g_k[i], g_idx_k=g_idx_k[i],
                   conv_w=conv_w[i].astype(F32), a_log=a_log[i], dt_bias=dt_bias[i], g_dn_out=g_dn_out[i],
                   w_oa=w_oa[i], w_ob=w_ob[i], w_out=w_out[i], g_ffn=g_ffn[i], w_pq=w_pq[i], sub_keys=sub_keys[i],
                   tab_u=_pack_table(peer_u[i]), tab_v=_pack_table(peer_v[i]), g_ple=g_ple[i],
                   w_ple_gate=w_ple_gate[i], w_ple=w_ple[i])

        bp, s, _ = xp.shape

        def attn_prompt(q, k, kdup, v, vb, qi, misc, kidup):
            k_sel = min(TOPK_MAX, s // 4)
            tq, tk = _tile(s, 256), _tile(s, 512)
            r3 = lambda a: a.reshape(bp, s, a.shape[-1])
            mask = _idx_mask(r3(qi), r3(misc), r3(kidup), k_sel, tq, tk)
            return _attn(r3(q), r3(kdup), r3(vb), mask, tq, tk).reshape(bp * s, A_Q)

        res = _layer(xp, p_prompt[i], None, None, attn_prompt, prm)
        xp = res[0]
        for lst, val in zip(outs[:5], res[1:]):
            lst.append(val)

        bd, t, _ = xs.shape

        def attn_sample(q, k, kdup, v, vb, qi, misc, kidup):
            qf = (q.astype(F32) * A_HEAD_DIM ** 0.5).reshape(bd, t, A_KV_HEADS, A_REP, A_HEAD_DIM)
            kf = k.reshape(bd, t, A_KV_HEADS, A_HEAD_DIM)
            vf = v.reshape(bd, t, A_KV_HEADS, A_HEAD_DIM)
            qif = qi.astype(F32).reshape(bd, t, IDX_HEADS, IDX_DIM)
            kif = misc[:, :IDX_DIM].reshape(bd, t, IDX_DIM)
            wif = misc[:, MISC_WI:MISC_WI + IDX_HEADS].reshape(bd, t, IDX_HEADS)
            return _attn_sample_jax(qf, kf, vf, qif, wif, kif, cache_k[i], cache_v[i], cache_idx_k[i], page_table).reshape(bd * t, A_Q)

        res = _layer(xs, p_sample[i], state_conv[i], state_delta[i], attn_sample, prm)
        xs = res[0]
        for lst, val in zip(outs[5:], res[1:]):
            lst.append(val)

    cast = [cache_k.dtype, cache_v.dtype, cache_idx_k.dtype, state_conv.dtype, state_delta.dtype] * 2
    stacked = [jnp.stack(lst).astype(dt) for lst, dt in zip(outs, cast)]
    return (xp, xs, *stacked)
```

```python
import functools
import math

import jax
import jax.numpy as jnp
import numpy as np
from jax import lax
from jax.experimental import pallas as pl
from jax.experimental.pallas import tpu as pltpu

F32 = jnp.float32
BF16 = jnp.bfloat16
I32 = jnp.int32

D_MODEL = 1024
PAGE_SIZE = 128
A_HEADS = 8
A_KV_HEADS = 4
A_REP = A_HEADS // A_KV_HEADS
A_HEAD_DIM = 64
A_Q = A_HEADS * A_HEAD_DIM
A_KV = A_KV_HEADS * A_HEAD_DIM
IDX_HEADS = 8
IDX_DIM = 64
TOPK_MAX = 256
DN_HEADS = 4
DN_DK = 128
DN_DV = 128
DN_QK = DN_HEADS * DN_DK
DN_V = DN_HEADS * DN_DV
DN_CONV_CH = 2 * DN_QK + DN_V
CONV_W = 4
DN_CHUNK = 64
PEER_HEADS = 8
PEER_KEYS = 128
PEER_DKEY = 256
PEER_TOPK = 16
PLE_DIM = 256
EPS = 1e-6
IN_SPLITS = (A_Q, A_KV, A_KV, IDX_HEADS * IDX_DIM, IDX_DIM, IDX_HEADS, DN_CONV_CH, DN_V, DN_HEADS, DN_HEADS, D_MODEL, D_MODEL)

LANES = 128
SUBLANES = 8
VMEM_LIMIT = 56 * 1024 * 1024

NEG_BIG = -1e30
INT_MIN = -(2 ** 31)

MISC_WI = IDX_DIM
MISC_DB = MISC_WI + IDX_HEADS
MISC_DA = MISC_DB + DN_HEADS

SEG_Q = 0
SEG_K = SEG_Q + A_Q
SEG_V = SEG_K + A_KV
SEG_QI = SEG_V + A_KV
SEG_MISC = SEG_QI + IDX_HEADS * IDX_DIM
SEG_DQKV = SEG_MISC + LANES
SEG_DZ = SEG_DQKV + DN_CONV_CH
SEG_GA = SEG_DZ + DN_V
SEG_GB = SEG_GA + D_MODEL
SEG_END = SEG_GB + D_MODEL


def _cparams(sem):
    return pltpu.CompilerParams(dimension_semantics=sem, vmem_limit_bytes=VMEM_LIMIT)


def _dot(a, b):
    return jnp.dot(a, b, preferred_element_type=F32)


def _dot_nt(a, b):
    return lax.dot_general(a, b, (((1,), (1,)), ((), ())), preferred_element_type=F32)


def _split(a):
    hi = a.astype(BF16)
    lo = (a - hi.astype(F32)).astype(BF16)
    return hi, lo


def _dot3(a, b, nt=False):
    d = _dot_nt if nt else _dot
    ah, al = _split(a)
    bh, bl = _split(b)
    return d(ah, bh) + (d(ah, bl) + d(al, bh))


def _lane_iota(shape):
    return lax.broadcasted_iota(I32, shape, len(shape) - 1)


def _row_iota(shape):
    return lax.broadcasted_iota(I32, shape, len(shape) - 2)


def _half_norm(blk, gain):
    lane = _lane_iota(blk.shape)
    lo = lane < A_HEAD_DIM
    sq = blk * blk
    s_lo = jnp.sum(jnp.where(lo, sq, 0.0), axis=-1, keepdims=True)
    s_hi = jnp.sum(jnp.where(lo, 0.0, sq), axis=-1, keepdims=True)
    r_lo = lax.rsqrt(s_lo * (1.0 / A_HEAD_DIM) + EPS)
    r_hi = lax.rsqrt(s_hi * (1.0 / A_HEAD_DIM) + EPS)
    return blk * jnp.where(lo, r_lo, r_hi) * gain


def _proj_kernel(x_ref, gmix_ref, w_ref, gq_ref, gk_ref, gik_ref,
                 q_ref, k_ref, kdup_ref, v_ref, vb_ref, qi_ref, misc_ref, kidup_ref,
                 dqkv_ref, dz_ref, ga_ref, gb_ref):
    x = x_ref[...]
    h = x * lax.rsqrt(jnp.mean(x * x, axis=-1, keepdims=True) + EPS) * gmix_ref[...]
    hb = h.astype(BF16)

    def seg(a, b):
        return _dot(hb, w_ref[:, a:b])

    lane = _lane_iota((x.shape[0], LANES))
    lo = lane < A_HEAD_DIM

    zq = seg(SEG_Q, SEG_K)
    for c in range(A_Q // LANES):
        blk = _half_norm(zq[:, c * LANES:(c + 1) * LANES], gq_ref[...])
        q_ref[:, c * LANES:(c + 1) * LANES] = (blk * (A_HEAD_DIM ** -0.5)).astype(BF16)

    zk = seg(SEG_K, SEG_V)
    for c in range(A_KV // LANES):
        blk = _half_norm(zk[:, c * LANES:(c + 1) * LANES], gk_ref[...])
        k_ref[:, c * LANES:(c + 1) * LANES] = blk
        rolled = pltpu.roll(blk, A_HEAD_DIM, axis=1)
        kdup_ref[:, (2 * c) * LANES:(2 * c + 1) * LANES] = jnp.where(lo, blk, rolled).astype(BF16)
        kdup_ref[:, (2 * c + 1) * LANES:(2 * c + 2) * LANES] = jnp.where(lo, rolled, blk).astype(BF16)

    zv = seg(SEG_V, SEG_QI)
    v_ref[...] = zv
    vb_ref[...] = zv.astype(BF16)

    qi_ref[...] = seg(SEG_QI, SEG_MISC).astype(BF16)

    zm = seg(SEG_MISC, SEG_DQKV)
    s_ik = jnp.sum(jnp.where(lo, zm * zm, 0.0), axis=-1, keepdims=True)
    kin = zm * lax.rsqrt(s_ik * (1.0 / IDX_DIM) + EPS) * gik_ref[...]
    wscale = (IDX_HEADS * IDX_DIM) ** -0.5
    misc_ref[...] = jnp.where(lo, kin, jnp.where(lane < MISC_DB, zm * wscale, zm))
    kin0 = jnp.where(lo, kin, 0.0)
    kidup_ref[...] = (kin0 + pltpu.roll(kin0, IDX_DIM, axis=1)).astype(BF16)

    dqkv_ref[...] = seg(SEG_DQKV, SEG_DZ)
    dz_ref[...] = seg(SEG_DZ, SEG_GA)
    ga_ref[...] = seg(SEG_GA, SEG_GB)
    gb_ref[...] = seg(SEG_GB, SEG_END)


def _permute_w_in(w_in):
    cuts = np.cumsum((0,) + IN_SPLITS)
    parts = [w_in[:, cuts[i]:cuts[i + 1]] for i in range(len(IN_SPLITS))]
    aq, ak, av, iq, ik, iw, dqkv, dz, db, da, ga, gb = parts
    pad = jnp.zeros((w_in.shape[0], LANES - IDX_DIM - IDX_HEADS - 2 * DN_HEADS), w_in.dtype)
    return jnp.concatenate([aq, ak, av, iq, ik, iw, db, da, pad, dqkv, dz, ga, gb], axis=1).astype(BF16)


def _tile2(g):
    return jnp.concatenate([g, g]).reshape(1, LANES).astype(F32)


def _proj(x2d, g_mix, w_perm, g_q, g_k, g_idx_k, tm):
    n = x2d.shape[0]
    assert n % tm == 0
    widths = [(A_Q, BF16), (A_KV, F32), (2 * A_KV, BF16), (A_KV, F32), (A_KV, BF16), (IDX_HEADS * IDX_DIM, BF16),
              (LANES, F32), (LANES, BF16), (DN_CONV_CH, F32), (DN_V, F32), (D_MODEL, F32), (D_MODEL, F32)]
    row = lambda i: (i, 0)
    fixed = lambda i: (0, 0)
    return pl.pallas_call(
        _proj_kernel,
        grid=(n // tm,),
        in_specs=[pl.BlockSpec((tm, D_MODEL), row), pl.BlockSpec((1, D_MODEL), fixed),
                  pl.BlockSpec((D_MODEL, SEG_END), fixed), pl.BlockSpec((1, LANES), fixed),
                  pl.BlockSpec((1, LANES), fixed), pl.BlockSpec((1, LANES), fixed)],
        out_specs=[pl.BlockSpec((tm, w), row) for w, _ in widths],
        out_shape=[jax.ShapeDtypeStruct((n, w), dt) for w, dt in widths],
        compiler_params=_cparams(("parallel",)),
        name="proj_in",
    )(x2d, g_mix.reshape(1, D_MODEL), w_perm, _tile2(g_q), _tile2(g_k), _tile2(g_idx_k))


def _sort_key(score):
    score = jnp.where(score == 0.0, 0.0, score)
    bits = pltpu.bitcast(score, I32)
    return jnp.where(bits < 0, bits ^ jnp.int32(0x7FFFFFFF), bits)


def _index_scores(qi, wi_tile, kblk):
    lane = _lane_iota((qi.shape[0], LANES))
    lo = lane < IDX_DIM
    sc = None
    for c in range(IDX_HEADS // 2):
        q128 = qi[:, c * LANES:(c + 1) * LANES]
        zero = jnp.zeros_like(q128)
        for half in range(2):
            qm = jnp.where(lo, q128, zero) if half == 0 else jnp.where(lo, zero, q128)
            s = _dot_nt(qm, kblk)
            hidx = MISC_WI + 2 * c + half
            term = jnp.maximum(s, 0.0) * wi_tile[:, hidx:hidx + 1]
            sc = term if sc is None else sc + term
    return sc


def _select_rows(keys_ref, thr_ref, jcut_ref, nvalid, k_sel, tq, tk, rb, live_rows=None):
    nchunk = tk // LANES
    nbits_idx = int(math.ceil(math.log2(keys_ref.shape[0] * tk))) + 1

    for r in range(tq // rb):
        r0 = r * rb

        def count(pred):
            def body(kb, cnt):
                blk = keys_ref[kb, r0:r0 + rb, :]
                for c in range(nchunk):
                    cnt = cnt + pred(blk[:, c * LANES:(c + 1) * LANES], kb * tk + c * LANES)
                return cnt
            cnt = lax.fori_loop(0, nvalid, body, jnp.zeros((rb, LANES), I32))
            return jnp.sum(cnt, axis=1, keepdims=True)

        def bit_step(i, cur):
            cand = cur + lax.shift_left(jnp.int32(1), 31 - i)
            cand_b = jnp.broadcast_to(cand, (rb, LANES))
            c_ge = count(lambda kv, base: jnp.where(kv >= cand_b, 1, 0))
            return jnp.where(c_ge >= k_sel, cand, cur)

        thr = lax.fori_loop(0, 32, bit_step, jnp.full((rb, 1), INT_MIN, I32))
        thr_b = jnp.broadcast_to(thr, (rb, LANES))
        c_gt = count(lambda kv, base: jnp.where(kv > thr_b, 1, 0))
        c_ge = count(lambda kv, base: jnp.where(kv >= thr_b, 1, 0))
        need = k_sel - c_gt
        thr_ref[r0:r0 + rb, :] = thr_b
        jcut_ref[r0:r0 + rb, :] = jnp.full((rb, LANES), 2 ** 30, I32)

        if live_rows is not None:
            c_ge = jnp.where(_row_iota((rb, 1)) % SUBLANES < live_rows, c_ge, 0)

        @pl.when(jnp.max(c_ge) > k_sel)
        def _():
            lane = _lane_iota((rb, LANES))

            def jstep(i, jcur):
                candj = jcur + lax.shift_left(jnp.int32(1), nbits_idx - 1 - i)
                candj_b = jnp.broadcast_to(candj, (rb, LANES))
                f = count(lambda kv, base: jnp.where(kv == thr_b, jnp.where(lane + base < candj_b, 1, 0), 0))
                return jnp.where(f < need, candj, jcur)

            jcut = lax.fori_loop(0, nbits_idx, jstep, jnp.zeros((rb, 1), I32))
            jcut_ref[r0:r0 + rb, :] = jnp.broadcast_to(jcut, (rb, LANES))


def _idx_mask_kernel(qi_ref, wi_ref, kidup_ref, mask_ref, keys_ref, thr_ref, jcut_ref, *, k_sel, tq, tk, rb):
    i = pl.program_id(1)
    nkb = keys_ref.shape[0]
    nvalid = ((i + 1) * tq + tk - 1) // tk
    qi = qi_ref[0]
    wi_tile = wi_ref[0]
    q_pos = i * tq + _row_iota((tq, tk))
    lane = _lane_iota((tq, tk))

    def fill(kb, carry):
        kblk = kidup_ref[0, pl.ds(pl.multiple_of(kb * tk, tk), tk), :]
        sc = _index_scores(qi, wi_tile, kblk)
        sc = jnp.where(lane + kb * tk <= q_pos, sc, -jnp.inf)
        keys_ref[kb] = _sort_key(sc)
        return carry

    lax.fori_loop(0, nvalid, fill, 0)
    _select_rows(keys_ref, thr_ref, jcut_ref, nvalid, k_sel, tq, tk, rb)

    for kb in range(nkb):
        @pl.when(kb < nvalid)
        def _():
            thr = jnp.broadcast_to(thr_ref[:, 0:1], (tq, tk))
            jcut = jnp.broadcast_to(jcut_ref[:, 0:1], (tq, tk))
            key = keys_ref[kb]
            pos = lane + kb * tk
            sel = jnp.where(key > thr, 1, jnp.where(key == thr, jnp.where(pos <= jcut, 1, 0), 0))
            sel = jnp.where(pos <= q_pos, sel, 0)
            mask_ref[0, :, kb * tk:(kb + 1) * tk] = sel.astype(jnp.int8)

        @pl.when(kb >= nvalid)
        def _():
            mask_ref[0, :, kb * tk:(kb + 1) * tk] = jnp.zeros((tq, tk), jnp.int8)


def _idx_mask(qi, misc, kidup, k_sel, tq, tk):
    b, s, _ = qi.shape
    assert s % tq == 0 and s % tk == 0 and tq % 32 == 0
    rb = min(64, tq)
    kern = functools.partial(_idx_mask_kernel, k_sel=k_sel, tq=tq, tk=tk, rb=rb)
    return pl.pallas_call(
        kern,
        grid=(b, s // tq),
        in_specs=[pl.BlockSpec((1, tq, IDX_HEADS * IDX_DIM), lambda bi, i: (bi, i, 0)),
                  pl.BlockSpec((1, tq, LANES), lambda bi, i: (bi, i, 0)),
                  pl.BlockSpec((1, s, LANES), lambda bi, i: (bi, 0, 0))],
        out_specs=pl.BlockSpec((1, tq, s), lambda bi, i: (bi, i, 0)),
        out_shape=jax.ShapeDtypeStruct((b, s, s), jnp.int8),
        scratch_shapes=[pltpu.VMEM((s // tk, tq, tk), I32), pltpu.VMEM((tq, LANES), I32), pltpu.VMEM((tq, LANES), I32)],
        compiler_params=_cparams(("parallel", "parallel")),
        name="idx_mask",
    )(qi, misc, kidup)


def _attn_kernel(q_ref, kdup_ref, vb_ref, mask_ref, o_ref, m_ref, l_ref, acc_ref, *, tq, tk):
    i = pl.program_id(1)
    kb = pl.program_id(2)
    nk = pl.num_programs(2)
    last_needed = ((i + 1) * tq - 1) // tk

    @pl.when(kb == 0)
    def _():
        m_ref[...] = jnp.full(m_ref.shape, NEG_BIG, F32)
        l_ref[...] = jnp.zeros(l_ref.shape, F32)
        acc_ref[...] = jnp.zeros(acc_ref.shape, F32)

    @pl.when(kb <= last_needed)
    def _():
        bias = jnp.where(mask_ref[0].astype(I32) != 0, 0.0, NEG_BIG).astype(F32)
        bias2 = jnp.concatenate([bias, bias], axis=0)
        lane = _lane_iota((tq, LANES))
        lo = lane < A_HEAD_DIM
        for g in range(A_KV_HEADS):
            q128 = q_ref[0, :, g * LANES:(g + 1) * LANES]
            zero = jnp.zeros_like(q128)
            q2 = jnp.concatenate([jnp.where(lo, q128, zero), jnp.where(lo, zero, q128)], axis=0)
            s = _dot_nt(q2, kdup_ref[0, :, g * LANES:(g + 1) * LANES]) + bias2
            m_old = m_ref[g]
            m_new = jnp.maximum(m_old, jnp.max(s, axis=1, keepdims=True))
            alpha = jnp.exp(m_old - m_new)
            p = jnp.exp(s - m_new[:, 0:1])
            l_ref[g] = alpha * l_ref[g] + jnp.sum(p, axis=1, keepdims=True)
            v128 = vb_ref[0, :, (g // 2) * LANES:(g // 2 + 1) * LANES]
            acc_ref[g] = alpha * acc_ref[g] + _dot(p.astype(BF16), v128)
            m_ref[g] = m_new

    @pl.when(kb == nk - 1)
    def _():
        lane = _lane_iota((tq, LANES))
        lo = lane < A_HEAD_DIM
        for g in range(A_KV_HEADS):
            a = acc_ref[g] / l_ref[g]
            top, bot = a[:tq], a[tq:]
            if g % 2 == 0:
                o128 = jnp.where(lo, top, pltpu.roll(bot, A_HEAD_DIM, axis=1))
            else:
                o128 = jnp.where(lo, pltpu.roll(top, A_HEAD_DIM, axis=1), bot)
            o_ref[0, :, g * LANES:(g + 1) * LANES] = o128


def _attn(q, kdup, vb, mask, tq, tk):
    b, s, _ = q.shape
    nk = s // tk

    def kv_idx(bi, i, kb):
        return (bi, jnp.minimum(kb, ((i + 1) * tq - 1) // tk), 0)

    def mask_idx(bi, i, kb):
        return (bi, i, jnp.minimum(kb, ((i + 1) * tq - 1) // tk))

    kern = functools.partial(_attn_kernel, tq=tq, tk=tk)
    return pl.pallas_call(
        kern,
        grid=(b, s // tq, nk),
        in_specs=[pl.BlockSpec((1, tq, A_Q), lambda bi, i, kb: (bi, i, 0)),
                  pl.BlockSpec((1, tk, 2 * A_KV), kv_idx),
                  pl.BlockSpec((1, tk, A_KV), kv_idx),
                  pl.BlockSpec((1, tq, tk), mask_idx)],
        out_specs=pl.BlockSpec((1, tq, A_Q), lambda bi, i, kb: (bi, i, 0)),
        out_shape=jax.ShapeDtypeStruct((b, s, A_Q), F32),
        scratch_shapes=[pltpu.VMEM((A_KV_HEADS, 2 * tq, LANES), F32)] * 3,
        compiler_params=_cparams(("parallel", "parallel", "arbitrary")),
        name="sel_attn",
    )(q, kdup, vb, mask)


def _conv_kernel(x_ref, xprev_ref, halo0_ref, w_ref, q_ref, k_ref, v_ref, *, tt):
    i = pl.program_id(1)
    x = x_ref[0]
    halo = jnp.where(i == 0, halo0_ref[0], xprev_ref[0])
    w = w_ref[...]

    def post(y, rows):
        y = y * jax.nn.sigmoid(y)
        for h in range(DN_HEADS):
            for j, ref in enumerate((q_ref, k_ref)):
                blk = y[:, j * DN_QK + h * DN_DK: j * DN_QK + (h + 1) * DN_DK]
                blk = blk * lax.rsqrt(jnp.sum(blk * blk, axis=-1, keepdims=True) + EPS)
                ref[0, 0:rows, h * DN_DK:(h + 1) * DN_DK] = blk
        v_ref[0, 0:rows, :] = y[:, 2 * DN_QK:]

    y = x * w[CONV_W - 1:CONV_W, :]
    for j in range(1, CONV_W):
        y = y + pltpu.roll(x, j, axis=0) * w[CONV_W - 1 - j:CONV_W - j, :]
    post(y, tt)

    x8 = x[0:SUBLANES]
    row = _row_iota(x8.shape)
    y8 = x8 * w[CONV_W - 1:CONV_W, :]
    for j in range(1, CONV_W):
        xs = jnp.where(row < j, pltpu.roll(halo, j, axis=0), pltpu.roll(x8, j, axis=0))
        y8 = y8 + xs * w[CONV_W - 1 - j:CONV_W - j, :]
    post(y8, SUBLANES)


def _conv(dqkv, halo0, conv_w, tt):
    b, t, c = dqkv.shape
    assert t % tt == 0 and tt % SUBLANES == 0
    r = tt // SUBLANES
    kern = functools.partial(_conv_kernel, tt=tt)
    out = jax.ShapeDtypeStruct((b, t, DN_QK), F32)
    return pl.pallas_call(
        kern,
        grid=(b, t // tt),
        in_specs=[pl.BlockSpec((1, tt, c), lambda bi, i: (bi, i, 0)),
                  pl.BlockSpec((1, SUBLANES, c), lambda bi, i: (bi, jnp.maximum(i * r - 1, 0), 0)),
                  pl.BlockSpec((1, SUBLANES, c), lambda bi, i: (bi, 0, 0)),
                  pl.BlockSpec((CONV_W, c), lambda bi, i: (0, 0))],
        out_specs=[pl.BlockSpec((1, tt, DN_QK), lambda bi, i: (bi, i, 0))] * 3,
        out_shape=[out, out, out],
        compiler_params=_cparams(("parallel", "parallel")),
        name="dn_conv",
    )(dqkv, dqkv, halo0, conv_w)


def _softplus(x):
    return jnp.maximum(x, 0.0) + jnp.log(1.0 + jnp.exp(-jnp.abs(x)))


def _pad_rows(a, rows):
    if a.shape[0] == rows:
        return a
    return jnp.concatenate([a, jnp.zeros((rows - a.shape[0], a.shape[1]), a.dtype)], axis=0)


def _delta_kernel(q_ref, k_ref, v_ref, z_ref, misc_ref, nega_ref, dtb_ref, gout_ref, s0_ref, o_ref, st_ref, *, c, t_valid):
    n = pl.program_id(1)

    @pl.when(n == 0)
    def _():
        st_ref[...] = s0_ref[...]

    mt = misc_ref[0]
    row = _row_iota((c, LANES))
    live = row < t_valid
    beta_t = jnp.where(live, jax.nn.sigmoid(mt), 0.0)
    ld_t = jnp.where(live, nega_ref[...] * _softplus(mt + dtb_ref[...]), 0.0)
    gc_t = ld_t
    sft = 1
    while sft < c:
        gc_t = gc_t + jnp.where(row >= sft, pltpu.roll(gc_t, sft, axis=0), 0.0)
        sft *= 2
    gc_tr = _pad_rows(gc_t, LANES).T

    ri = lax.broadcasted_iota(I32, (c, c), 0)
    ci = lax.broadcasted_iota(I32, (c, c), 1)
    incl = ri >= ci
    strict = ri > ci
    eye = jnp.where(ri == ci, 1.0, 0.0).astype(F32)
    scale = DN_DK ** -0.5

    for h in range(DN_HEADS):
        sl = slice(h * DN_DK, (h + 1) * DN_DK)
        q = jnp.where(live, q_ref[0, :, sl] * scale, 0.0)
        k = jnp.where(live, k_ref[0, :, sl], 0.0)
        v = jnp.where(live, v_ref[0, :, sl], 0.0)
        beta = beta_t[:, MISC_DB + h:MISC_DB + h + 1]
        gcc = gc_t[:, MISC_DA + h:MISC_DA + h + 1]
        gcr = gc_tr[MISC_DA + h:MISC_DA + h + 1, 0:c]
        decay = jnp.where(incl, jnp.exp(jnp.where(incl, gcc - gcr, 0.0)), 0.0)
        kb = k * beta
        m = jnp.where(strict, _dot3(kb, k, nt=True) * decay, 0.0)
        pw = -m
        r = eye + pw
        nn = 2
        while nn < c:
            pw = _dot3(pw, pw)
            r = _dot3(r, eye + pw)
            nn *= 2
        egc = jnp.exp(gcc)
        u = _dot3(r, v * beta)
        w = _dot3(r, kb * egc)
        a_qk = _dot3(q, k, nt=True) * decay
        s = st_ref[0, h]
        v_new = u - _dot3(w, s)
        o = _dot3(q * egc, s) + _dot3(a_qk, v_new)
        glast = gcc[c - 1:c, :]
        kw = k * jnp.exp(glast - gcc)
        kw_t = _pad_rows(kw, LANES).T
        st_ref[0, h] = s * jnp.exp(glast) + _dot3(kw_t, _pad_rows(v_new, LANES))
        on = o * lax.rsqrt(jnp.mean(o * o, axis=-1, keepdims=True) + EPS) * gout_ref[...]
        z = z_ref[0, :, sl]
        o_ref[0, :, sl] = on * (z * jax.nn.sigmoid(z))


def _delta(qn, kn, v, dz, misc, a_log, dt_bias, g_dn_out, s0, c, t_valid):
    b, t, _ = qn.shape
    assert t % c == 0 and (t_valid == t or t == c)
    nega = jnp.zeros((1, LANES), F32).at[0, MISC_DA:MISC_DA + DN_HEADS].set(-jnp.exp(a_log.astype(F32)))
    dtb = jnp.zeros((1, LANES), F32).at[0, MISC_DA:MISC_DA + DN_HEADS].set(dt_bias.astype(F32))
    tok = lambda bi, n: (bi, n, 0)
    fixed = lambda bi, n: (0, 0)
    st = lambda bi, n: (bi, 0, 0, 0)
    kern = functools.partial(_delta_kernel, c=c, t_valid=t_valid)
    return pl.pallas_call(
        kern,
        grid=(b, t // c),
        in_specs=[pl.BlockSpec((1, c, DN_QK), tok), pl.BlockSpec((1, c, DN_QK), tok), pl.BlockSpec((1, c, DN_V), tok),
                  pl.BlockSpec((1, c, DN_V), tok), pl.BlockSpec((1, c, LANES), tok),
                  pl.BlockSpec((1, LANES), fixed), pl.BlockSpec((1, LANES), fixed), pl.BlockSpec((1, DN_DV), fixed),
                  pl.BlockSpec((1, DN_HEADS, DN_DK, DN_DV), st)],
        out_specs=[pl.BlockSpec((1, c, DN_V), tok), pl.BlockSpec((1, DN_HEADS, DN_DK, DN_DV), st)],
        out_shape=[jax.ShapeDtypeStruct((b, t, DN_V), F32), jax.ShapeDtypeStruct((b, DN_HEADS, DN_DK, DN_DV), F32)],
        compiler_params=_cparams(("parallel", "arbitrary")),
        name="delta_rule",
    )(qn, kn, v, dz, misc, nega, dtb, g_dn_out.reshape(1, DN_DV).astype(F32), s0)


def _merge_kernel(x_ref, oa_ref, ob_ref, ga_ref, gb_ref, woa_ref, wob_ref, wout_ref, gffn_ref, wpqt_ref,
                  x1_ref, h2_ref, qt_ref):
    ma = jax.nn.sigmoid(ga_ref[...]) * _dot(oa_ref[...].astype(BF16), woa_ref[...])
    mb = jax.nn.sigmoid(gb_ref[...]) * _dot(ob_ref[...].astype(BF16), wob_ref[...])
    x1 = x_ref[...] + _dot((ma + mb).astype(BF16), wout_ref[...])
    x1_ref[...] = x1
    h2 = x1 * lax.rsqrt(jnp.mean(x1 * x1, axis=-1, keepdims=True) + EPS) * gffn_ref[...]
    h2_ref[...] = h2
    qt_ref[...] = _dot_nt(wpqt_ref[...], h2.astype(BF16))


def _merge(x2d, oa, ob, ga, gb, w_oa, w_ob, w_out, g_ffn, w_pq, tm):
    n = x2d.shape[0]
    nq = w_pq.shape[1]
    row = lambda i: (i, 0)
    fixed = lambda i: (0, 0)
    return pl.pallas_call(
        _merge_kernel,
        grid=(n // tm,),
        in_specs=[pl.BlockSpec((tm, D_MODEL), row), pl.BlockSpec((tm, A_Q), row), pl.BlockSpec((tm, DN_V), row),
                  pl.BlockSpec((tm, D_MODEL), row), pl.BlockSpec((tm, D_MODEL), row),
                  pl.BlockSpec((A_Q, D_MODEL), fixed), pl.BlockSpec((DN_V, D_MODEL), fixed),
                  pl.BlockSpec((D_MODEL, D_MODEL), fixed), pl.BlockSpec((1, D_MODEL), fixed),
                  pl.BlockSpec((nq, D_MODEL), fixed)],
        out_specs=[pl.BlockSpec((tm, D_MODEL), row), pl.BlockSpec((tm, D_MODEL), row), pl.BlockSpec((nq, tm), lambda i: (0, i))],
        out_shape=[jax.ShapeDtypeStruct((n, D_MODEL), F32), jax.ShapeDtypeStruct((n, D_MODEL), F32),
                   jax.ShapeDtypeStruct((nq, n), F32)],
        compiler_params=_cparams(("parallel",)),
        name="merge_out",
    )(x2d, oa, ob, ga, gb, w_oa.astype(BF16), w_ob.astype(BF16), w_out.astype(BF16),
      g_ffn.reshape(1, D_MODEL).astype(F32), w_pq.T.astype(BF16))


def _peer_cands():
    cands = [(i, j) for i in range(PEER_TOPK) for j in range(PEER_TOPK) if (i + 1) * (j + 1) <= PEER_TOPK]
    return sorted(cands, key=lambda ij: ij[0] * PEER_TOPK + ij[1])


def _route_kernel(qt_ref, sk_ref, eid_ref, gate_ref, toff_ref, val_ref, idx_ref, *, tn):
    half = PEER_DKEY // 2
    key_iota = lax.broadcasted_iota(I32, (PEER_KEYS, tn), 0)

    def per_table(t, carry):
        h = t // 2
        qblk = qt_ref[pl.ds(pl.multiple_of(t * half, half), half), :]
        s = _dot3(sk_ref[t], qblk)
        for r in range(PEER_TOPK):
            m = jnp.max(s, axis=0, keepdims=True)
            am = jnp.min(jnp.where(s == m, key_iota, PEER_KEYS), axis=0, keepdims=True)
            val_ref[t % 2, r, pl.ds(h, 1), :] = m
            idx_ref[t % 2, r, pl.ds(h, 1), :] = am
            s = jnp.where(key_iota == am, -jnp.inf, s)
        return carry

    lax.fori_loop(0, 2 * PEER_HEADS, per_table, 0)

    cands = _peer_cands()
    cv = [val_ref[0, i] + val_ref[1, j] for i, j in cands]
    ce = [idx_ref[0, i] * PEER_KEYS + idx_ref[1, j] for i, j in cands]
    nc = len(cands)
    rank = []
    for a in range(nc):
        rk = jnp.zeros(cv[a].shape, I32)
        for b in range(nc):
            if b < a:
                rk = rk + jnp.where(cv[b] >= cv[a], 1, 0)
            elif b > a:
                rk = rk + jnp.where(cv[b] > cv[a], 1, 0)
        rank.append(rk)
    mx = cv[0]
    ex = [jnp.where(rank[a] < PEER_TOPK, jnp.exp(cv[a] - mx), 0.0) for a in range(nc)]
    den = ex[0]
    for a in range(1, nc):
        den = den + ex[a]
    inv = 1.0 / den
    e_slots = []
    for slot in range(PEER_TOPK):
        e = jnp.zeros(cv[0].shape, I32)
        g = jnp.zeros(cv[0].shape, F32)
        for a in range(nc):
            hit = rank[a] == slot
            e = jnp.where(hit, ce[a], e)
            g = jnp.where(hit, ex[a], g)
        eid_ref[slot] = e
        gate_ref[slot] = g * inv
        e_slots.append(e)
    toff_ref[...] = ((jnp.concatenate(e_slots, axis=0) >> 1) * SUBLANES).T


def _route(qt, sub_keys, tn):
    nq, n = qt.shape
    sk = sub_keys.reshape(2 * PEER_HEADS, PEER_KEYS, PEER_DKEY // 2).astype(F32)
    kern = functools.partial(_route_kernel, tn=tn)
    blk = pl.BlockSpec((PEER_TOPK, PEER_HEADS, tn), lambda i: (0, 0, i))
    npair = PEER_TOPK * PEER_HEADS
    return pl.pallas_call(
        kern,
        grid=(n // tn,),
        in_specs=[pl.BlockSpec((nq, tn), lambda i: (0, i)),
                  pl.BlockSpec((2 * PEER_HEADS, PEER_KEYS, PEER_DKEY // 2), lambda i: (0, 0, 0))],
        out_specs=[blk, blk, pl.BlockSpec((tn, npair), lambda i: (i, 0))],
        out_shape=[jax.ShapeDtypeStruct((PEER_TOPK, PEER_HEADS, n), I32), jax.ShapeDtypeStruct((PEER_TOPK, PEER_HEADS, n), F32),
                   jax.ShapeDtypeStruct((n, npair), I32)],
        scratch_shapes=[pltpu.VMEM((2, PEER_TOPK, PEER_HEADS, tn), F32), pltpu.VMEM((2, PEER_TOPK, PEER_HEADS, tn), I32)],
        compiler_params=_cparams(("parallel",)),
        name="peer_route",
    )(qt, sk)


def _pack_table(tab):
    e, d = tab.shape
    assert d == SUBLANES * LANES
    bits = lax.bitcast_convert_type(tab.astype(BF16), jnp.uint16).astype(jnp.uint32)
    words = (bits[0::2] << 16) | bits[1::2]
    return lax.bitcast_convert_type(words, I32).reshape(e // 2 * SUBLANES, LANES)


HI_HALF = -65536


def _bf16_words(x):
    return pltpu.bitcast(x.astype(BF16).astype(F32), I32) & HI_HALF


def _shr16(w):
    return lax.shift_right_logical(w, jnp.full(w.shape, 16, I32))


def _packed_mul(a_words, b_words):
    return pltpu.bitcast(pltpu.bitcast(a_words, BF16) * pltpu.bitcast(b_words, BF16), I32)


def _packed_add(a_words, b_words):
    return pltpu.bitcast(pltpu.bitcast(a_words, BF16) + pltpu.bitcast(b_words, BF16), I32)


def _hi_f32(w):
    return pltpu.bitcast(w & HI_HALF, F32)


def _lo_f32(w):
    return pltpu.bitcast(jnp.left_shift(w, 16), F32)


_BITREV8 = (0, 4, 2, 6, 1, 5, 3, 7)


def _peer_u_kernel(toff_ref, x_ref, eid_ref, gate_ref, tab_ref, cpk_ref, rhi_ref, rlo_ref, *, tn):
    npair = gate_ref.shape[0]
    lane = _lane_iota((npair, tn))
    sub = lax.broadcasted_iota(I32, (SUBLANES, LANES), 0)
    keep = {step: (sub & step) == 0 for step in (4, 2, 1)}

    def per_token(t, accs):
        acc_hi, acc_lo = accs
        xw = _bf16_words(x_ref[t])
        xw = xw | _shr16(xw)
        for g in range(npair // SUBLANES):
            prods = []
            for p in _BITREV8:
                off = pl.multiple_of(toff_ref[t * npair + (g * SUBLANES + p)], SUBLANES)
                prods.append(_packed_mul(tab_ref[pl.ds(off, SUBLANES), :], xw))
            step = SUBLANES // 2
            while step >= 1:
                nxt = []
                for a in range(0, len(prods), 2):
                    lo_t, hi_t = prods[a], prods[a + 1]
                    left = jnp.where(keep[step], lo_t, pltpu.roll(hi_t, step, axis=0))
                    right = jnp.where(keep[step], pltpu.roll(lo_t, SUBLANES - step, axis=0), hi_t)
                    nxt.append(_packed_add(left, right))
                prods = nxt
                step //= 2
            rhi_ref[g * SUBLANES:(g + 1) * SUBLANES, :] = _hi_f32(prods[0])
            rlo_ref[g * SUBLANES:(g + 1) * SUBLANES, :] = _lo_f32(prods[0])
        hit = lane == t
        acc_hi = jnp.where(hit, jnp.sum(rhi_ref[...], axis=1, keepdims=True), acc_hi)
        acc_lo = jnp.where(hit, jnp.sum(rlo_ref[...], axis=1, keepdims=True), acc_lo)
        return acc_hi, acc_lo

    zero = jnp.zeros((npair, tn), F32)
    acc_hi, acc_lo = lax.fori_loop(0, tn, per_token, (zero, zero))
    even = (eid_ref[...] & 1) == 0
    coef = gate_ref[...] * jax.nn.gelu(jnp.where(even, acc_hi, acc_lo))
    cw = _bf16_words(coef)
    cpk_ref[...] = jnp.where(even, cw, _shr16(cw)).T


def _peer_v_kernel(toff_ref, cpk_ref, tab_ref, y_ref, *, tn, npair):
    nacc = 4

    nsplit = 2
    per_iter = npair // nsplit

    def per_part(i, carry):
        t = i // nsplit
        part = i % nsplit
        zero = jnp.zeros((SUBLANES, LANES), F32)
        acc_hi = [zero] * nacc
        acc_lo = [zero] * nacc
        base = i * per_iter
        for k in range(per_iter):
            off = pl.multiple_of(toff_ref[base + k], SUBLANES)
            cw = jnp.full((SUBLANES, LANES), cpk_ref[base + k], I32)
            prod = _packed_mul(tab_ref[pl.ds(off, SUBLANES), :], cw)
            acc_hi[k % nacc] = acc_hi[k % nacc] + _hi_f32(prod)
            acc_lo[k % nacc] = acc_lo[k % nacc] + _lo_f32(prod)
        tot = ((acc_hi[0] + acc_hi[1]) + (acc_hi[2] + acc_hi[3])) + ((acc_lo[0] + acc_lo[1]) + (acc_lo[2] + acc_lo[3]))

        @pl.when(part == 0)
        def _():
            y_ref[t] = tot

        @pl.when(part != 0)
        def _():
            y_ref[t] = y_ref[t] + tot

        return carry

    lax.fori_loop(0, tn * nsplit, per_part, 0)


def _peer_experts(eid, gate, toff, h2, tab_u, tab_v, tn):
    npair, n = eid.shape
    d = h2.shape[1]
    x3 = h2.reshape(n, d // LANES, LANES)
    tab_spec = pl.BlockSpec(tab_u.shape, lambda i: (0, 0), pipeline_mode=pl.Buffered(1))
    smem_blk = lambda: pl.BlockSpec((tn * npair,), lambda i: (i,), memory_space=pltpu.SMEM)
    slot_blk = lambda: pl.BlockSpec((npair, tn), lambda i: (0, i))
    cpk = pl.pallas_call(
        functools.partial(_peer_u_kernel, tn=tn),
        grid=(n // tn,),
        in_specs=[smem_blk(), pl.BlockSpec((tn, d // LANES, LANES), lambda i: (i, 0, 0)), slot_blk(), slot_blk(), tab_spec],
        out_specs=pl.BlockSpec((tn, npair), lambda i: (i, 0)),
        out_shape=jax.ShapeDtypeStruct((n, npair), I32),
        scratch_shapes=[pltpu.VMEM((npair, LANES), F32), pltpu.VMEM((npair, LANES), F32)],
        compiler_params=_cparams(("arbitrary",)),
        name="peer_u",
    )(toff.reshape(n * npair), x3, eid, gate, tab_u)
    y3 = pl.pallas_call(
        functools.partial(_peer_v_kernel, tn=tn, npair=npair),
        grid=(n // tn,),
        in_specs=[smem_blk(), smem_blk(), tab_spec],
        out_specs=pl.BlockSpec((tn, d // LANES, LANES), lambda i: (i, 0, 0)),
        out_shape=jax.ShapeDtypeStruct((n, d // LANES, LANES), F32),
        compiler_params=_cparams(("arbitrary",)),
        name="peer_v",
    )(toff.reshape(n * npair), cpk.reshape(n * npair), tab_v)
    return y3.reshape(n, d)


def _ple_kernel(x1_ref, yp_ref, p_ref, gple_ref, wg_ref, wp_ref, o_ref):
    x2 = x1_ref[...] + yp_ref[...]
    hn = x2 * lax.rsqrt(jnp.mean(x2 * x2, axis=-1, keepdims=True) + EPS) * gple_ref[...]
    gate = jax.nn.sigmoid(_dot(hn.astype(BF16), wg_ref[...]))
    o_ref[...] = x2 + gate * _dot(p_ref[...].astype(BF16), wp_ref[...])


def _ple(x1, yp, p2d, g_ple, w_ple_gate, w_ple, tm):
    n = x1.shape[0]
    row = lambda i: (i, 0)
    fixed = lambda i: (0, 0)
    return pl.pallas_call(
        _ple_kernel,
        grid=(n // tm,),
        in_specs=[pl.BlockSpec((tm, D_MODEL), row), pl.BlockSpec((tm, D_MODEL), row), pl.BlockSpec((tm, PLE_DIM), row),
                  pl.BlockSpec((1, D_MODEL), fixed), pl.BlockSpec((D_MODEL, D_MODEL), fixed),
                  pl.BlockSpec((PLE_DIM, D_MODEL), fixed)],
        out_specs=pl.BlockSpec((tm, D_MODEL), row),
        out_shape=jax.ShapeDtypeStruct((n, D_MODEL), F32),
        compiler_params=_cparams(("parallel",)),
        name="ple_out",
    )(x1, yp, p2d, g_ple.reshape(1, D_MODEL).astype(F32), w_ple_gate.astype(BF16), w_ple.astype(BF16))


PAGES_PER_STEP = 4


def _sample_scores_kernel(pt_ref, qh_ref, wrep_ref, kinew_ref, *refs, npg):
    pages = refs[:npg]
    sc_ref, scnew_ref = refs[npg:]
    j = pl.program_id(1)
    qh = qh_ref[0]
    wrep = wrep_ref[0]

    def score(keys):
        s = jnp.maximum(_dot_nt(qh, keys.astype(BF16)), 0.0) * wrep
        tot = s[0:SUBLANES]
        for h in range(1, IDX_HEADS):
            tot = tot + s[h * SUBLANES:(h + 1) * SUBLANES]
        return tot

    for p in range(npg):
        sc_ref[0, :, p * PAGE_SIZE:(p + 1) * PAGE_SIZE] = score(pages[p][0])

    @pl.when(j == 0)
    def _():
        sn = score(kinew_ref[0])
        causal = _lane_iota((SUBLANES, PAGE_SIZE)) <= _row_iota((SUBLANES, PAGE_SIZE))
        scnew_ref[0] = jnp.where(causal, sn, -jnp.inf)


def _sample_thr_kernel(sc_ref, thr_ref, jcut_ref, keys_ref, *, k_sel, tq, tk, t):
    nkb = keys_ref.shape[0]
    for kb in range(nkb):
        keys_ref[kb] = _sort_key(sc_ref[:, :, kb * tk:(kb + 1) * tk].reshape(tq, tk))
    _select_rows(keys_ref, thr_ref, jcut_ref, nkb, k_sel, tq, tk, min(64, tq), live_rows=t)


def _sample_attn_kernel(pt_ref, q_ref, sc_ref, scnew_ref, thr_ref, jcut_ref, e_ref, gm_ref, knew_ref, vnew_ref, *refs, npg, past):
    kpages = refs[:npg]
    vpages = refs[npg:2 * npg]
    o_ref, m_ref, l_ref, acc_ref = refs[2 * npg:]
    j = pl.program_id(1)

    @pl.when(j == 0)
    def _():
        m_ref[...] = jnp.full(m_ref.shape, NEG_BIG, F32)
        l_ref[...] = jnp.zeros(l_ref.shape, F32)
        acc_ref[...] = jnp.zeros(acc_ref.shape, F32)

    q = q_ref[0]
    thr = thr_ref[...]
    jcut = jcut_ref[...]
    lane = _lane_iota((SUBLANES, PAGE_SIZE))

    def page(kp, vp, sc8, pos0, causal):
        key = _sort_key(sc8)
        pos = lane + pos0
        sel = jnp.where(key > thr, 1.0, jnp.where(key == thr, jnp.where(pos <= jcut, 1.0, 0.0), 0.0))
        if causal is not None:
            sel = jnp.where(causal, sel, 0.0)
        selx = _dot(sel.astype(BF16), e_ref[...])
        ok = jnp.concatenate([selx] * A_HEADS, axis=0) * gm_ref[...]
        s = _dot_nt(q, kp.astype(BF16)) + jnp.where(ok > 0.5, 0.0, NEG_BIG)
        m_old = m_ref[...]
        m_new = jnp.maximum(m_old, jnp.max(s, axis=1, keepdims=True))
        alpha = jnp.exp(m_old - m_new)
        p = jnp.exp(s - m_new[:, 0:1])
        l_ref[...] = alpha * l_ref[...] + jnp.sum(p, axis=1, keepdims=True)
        acc_ref[...] = alpha[:, 0:A_HEAD_DIM] * acc_ref[...] + _dot(p.astype(BF16), vp.astype(BF16))
        m_ref[...] = m_new

    for p in range(npg):
        page(kpages[p][0], vpages[p][0], sc_ref[0, :, p * PAGE_SIZE:(p + 1) * PAGE_SIZE],
             (j * npg + p) * PAGE_SIZE, None)

    @pl.when(j == pl.num_programs(1) - 1)
    def _():
        page(knew_ref[0], vnew_ref[0], scnew_ref[0], past, lane <= _row_iota((SUBLANES, PAGE_SIZE)))
        o_ref[0] = acc_ref[...] / l_ref[:, 0:A_HEAD_DIM]


def _head_major(a, bd, t, nh, hd):
    a = a.reshape(bd, t, nh, hd).transpose(0, 2, 1, 3)
    a = jnp.pad(a, ((0, 0), (0, 0), (0, SUBLANES - t), (0, 0)))
    return a.reshape(bd, nh * SUBLANES, hd)


def _attn_sample(q, k, v, qi, misc, cache_k, cache_v, cache_idx_k, page_table, bd, t):
    assert t <= SUBLANES
    n_pages = page_table.shape[1]
    past = n_pages * PAGE_SIZE
    npg = PAGES_PER_STEP if n_pages % PAGES_PER_STEP == 0 else 1
    nj = n_pages // npg
    k_sel = min(TOPK_MAX, (past + t) // 4)
    n_pool = cache_k.shape[0]
    rows = A_HEADS * SUBLANES
    kvw = A_KV_HEADS * PAGE_SIZE

    qh = _head_major(qi, bd, t, IDX_HEADS, IDX_DIM)
    wi = misc[:, MISC_WI:MISC_WI + IDX_HEADS].reshape(bd, t, IDX_HEADS).transpose(0, 2, 1)
    wrep = jnp.broadcast_to(jnp.pad(wi, ((0, 0), (0, 0), (0, SUBLANES - t))).reshape(bd, rows, 1), (bd, rows, LANES))
    pad_page = lambda a, w: jnp.pad(a.reshape(bd, t, w), ((0, 0), (0, PAGE_SIZE - t), (0, 0)))
    kinew = pad_page(misc[:, :IDX_DIM], IDX_DIM)

    per_b = lambda b, j, pt: (b, 0, 0)
    page_spec = lambda shape, p: pl.BlockSpec(shape, lambda b, j, pt: (pt[b, j * npg + p], 0, 0))
    sc, scnew = pl.pallas_call(
        functools.partial(_sample_scores_kernel, npg=npg),
        grid_spec=pltpu.PrefetchScalarGridSpec(
            num_scalar_prefetch=1, grid=(bd, nj),
            in_specs=[pl.BlockSpec((1, rows, IDX_DIM), per_b), pl.BlockSpec((1, rows, LANES), per_b),
                      pl.BlockSpec((1, PAGE_SIZE, IDX_DIM), per_b)]
                     + [page_spec((1, PAGE_SIZE, IDX_DIM), p) for p in range(npg)],
            out_specs=[pl.BlockSpec((1, SUBLANES, npg * PAGE_SIZE), lambda b, j, pt: (b, 0, j)),
                       pl.BlockSpec((1, SUBLANES, PAGE_SIZE), per_b)]),
        out_shape=[jax.ShapeDtypeStruct((bd, SUBLANES, past), F32), jax.ShapeDtypeStruct((bd, SUBLANES, PAGE_SIZE), F32)],
        compiler_params=_cparams(("parallel", "arbitrary")),
        name="sample_scores",
    )(page_table, qh, wrep, kinew, *([cache_idx_k] * npg))

    length = past + PAGE_SIZE
    sc_all = jnp.concatenate([sc, scnew], axis=2)
    tb = SUBLANES if bd % SUBLANES == 0 else 1
    tq = tb * SUBLANES
    tk = 5 * LANES if length % (5 * LANES) == 0 else LANES
    thr, jcut = pl.pallas_call(
        functools.partial(_sample_thr_kernel, k_sel=k_sel, tq=tq, tk=tk, t=t),
        grid=(bd // tb,),
        in_specs=[pl.BlockSpec((tb, SUBLANES, length), lambda i: (i, 0, 0))],
        out_specs=[pl.BlockSpec((tq, LANES), lambda i: (i, 0))] * 2,
        out_shape=[jax.ShapeDtypeStruct((bd * SUBLANES, LANES), I32)] * 2,
        scratch_shapes=[pltpu.VMEM((length // tk, tq, tk), I32)],
        compiler_params=_cparams(("parallel",)),
        name="sample_thr",
    )(sc_all)

    qa = _head_major(q, bd, t, A_HEADS, A_HEAD_DIM)
    knew = pad_page(k, A_KV).reshape(bd, kvw, A_HEAD_DIM)
    vnew = pad_page(v, A_KV).reshape(bd, kvw, A_HEAD_DIM)
    col = np.arange(kvw)
    expand = jnp.asarray(col[None, :] // A_KV_HEADS == np.arange(PAGE_SIZE)[:, None], BF16)
    gmask = jnp.asarray((col[None, :] % A_KV_HEADS) == (np.arange(rows)[:, None] // SUBLANES) // A_REP, F32)
    ck = cache_k.reshape(n_pool, kvw, A_HEAD_DIM)
    cv = cache_v.reshape(n_pool, kvw, A_HEAD_DIM)
    fixed = lambda b, j, pt: (0, 0)
    o = pl.pallas_call(
        functools.partial(_sample_attn_kernel, npg=npg, past=past),
        grid_spec=pltpu.PrefetchScalarGridSpec(
            num_scalar_prefetch=1, grid=(bd, nj),
            in_specs=[pl.BlockSpec((1, rows, A_HEAD_DIM), per_b),
                      pl.BlockSpec((1, SUBLANES, npg * PAGE_SIZE), lambda b, j, pt: (b, 0, j)),
                      pl.BlockSpec((1, SUBLANES, PAGE_SIZE), per_b),
                      pl.BlockSpec((SUBLANES, LANES), lambda b, j, pt: (b, 0)),
                      pl.BlockSpec((SUBLANES, LANES), lambda b, j, pt: (b, 0)),
                      pl.BlockSpec((PAGE_SIZE, kvw), fixed), pl.BlockSpec((rows, kvw), fixed),
                      pl.BlockSpec((1, kvw, A_HEAD_DIM), per_b), pl.BlockSpec((1, kvw, A_HEAD_DIM), per_b)]
                     + [page_spec((1, kvw, A_HEAD_DIM), p) for p in range(npg)] * 2,
            out_specs=pl.BlockSpec((1, rows, A_HEAD_DIM), per_b),
            scratch_shapes=[pltpu.VMEM((rows, LANES), F32), pltpu.VMEM((rows, LANES), F32), pltpu.VMEM((rows, A_HEAD_DIM), F32)]),
        out_shape=jax.ShapeDtypeStruct((bd, rows, A_HEAD_DIM), F32),
        compiler_params=_cparams(("parallel", "arbitrary")),
        name="sample_attn",
    )(page_table, qa, sc, scnew, thr, jcut, expand, gmask, knew, vnew, *([ck] * npg), *([cv] * npg))
    o = o.reshape(bd, A_HEADS, SUBLANES, A_HEAD_DIM)[:, :, :t].transpose(0, 2, 1, 3)
    return o.reshape(bd * t, A_Q)


def _tile(n, pref):
    return pref if n % pref == 0 else n


def _layer(x, p_emb, conv_state, delta_state, attn_fn, prm):
    b, t, _ = x.shape
    n = b * t
    x2d = x.reshape(n, D_MODEL)
    tm = _tile(n, 256)
    (q, k, kdup, v, vb, qi, misc, kidup, dqkv, dz, ga, gb) = _proj(
        x2d, prm['g_mix'], prm['w_perm'], prm['g_q'], prm['g_k'], prm['g_idx_k'], tm)

    oa = attn_fn(q, k, kdup, v, vb, qi, misc, kidup)

    tp = -(-t // SUBLANES) * SUBLANES
    c = min(DN_CHUNK, tp)
    pad_t = lambda a: jnp.pad(a.reshape(b, t, a.shape[-1]), ((0, 0), (0, tp - t), (0, 0)))
    dqkv3 = dqkv.reshape(b, t, DN_CONV_CH)
    halo = jnp.zeros((b, SUBLANES, DN_CONV_CH), F32)
    if conv_state is not None:
        halo = halo.at[:, SUBLANES - (CONV_W - 1):].set(conv_state.astype(F32))
        hist = jnp.concatenate([conv_state.astype(F32), dqkv3], axis=1)
    else:
        hist = jnp.concatenate([jnp.zeros((b, CONV_W - 1, DN_CONV_CH), F32), dqkv3], axis=1)
    new_conv = hist[:, -(CONV_W - 1):]
    qn, kn, vv = _conv(pad_t(dqkv), halo, prm['conv_w'], _tile(tp, 512))
    s0 = jnp.zeros((b, DN_HEADS, DN_DK, DN_DV), F32) if delta_state is None else delta_state.astype(F32)
    ob, new_delta = _delta(qn, kn, vv, pad_t(dz), pad_t(misc), prm['a_log'], prm['dt_bias'], prm['g_dn_out'], s0, c, t)
    ob = ob[:, :t].reshape(n, DN_V)

    x1, h2, qt = _merge(x2d, oa, ob, ga, gb, prm['w_oa'], prm['w_ob'], prm['w_out'], prm['g_ffn'], prm['w_pq'], tm)
    eid, gate, toff = _route(qt, prm['sub_keys'], _tile(n, LANES))
    npair = PEER_TOPK * PEER_HEADS
    yp = _peer_experts(eid.reshape(npair, n), gate.reshape(npair, n), toff, h2, prm['tab_u'], prm['tab_v'], _tile(n, LANES))
    y = _ple(x1, yp, p_emb.reshape(n, PLE_DIM), prm['g_ple'], prm['w_ple_gate'], prm['w_ple'], tm)

    return (y.reshape(b, t, D_MODEL), k.reshape(b, t, A_KV_HEADS, A_HEAD_DIM), v.reshape(b, t, A_KV_HEADS, A_HEAD_DIM),
            misc[:, :IDX_DIM].reshape(b, t, IDX_DIM), new_conv, new_delta)


def kernel(x_prompt, x_sample, cache_k, cache_v, cache_idx_k, state_conv, state_delta, page_table, p_prompt, p_sample,
           g_mix, w_in, g_q, g_k, g_idx_k, conv_w, a_log, dt_bias, g_dn_out, w_oa, w_ob, w_out, g_ffn, w_pq, sub_keys,
           peer_u, peer_v, g_ple, w_ple_gate, w_ple):
    depth = w_in.shape[0]
    xp, xs = x_prompt, x_sample
    outs = [[] for _ in range(10)]
    for i in range(depth):
        prm = dict(g_mix=g_mix[i], w_perm=_permute_w_in(w_in[i]), g_q=g_q[i], g_k=g_k[i], g_idx_k=g_idx_k[i],
                   conv_w=conv_w[i].astype(F32), a_log=a_log[i], dt_bias=dt_bias[i], g_dn_out=g_dn_out[i],
                   w_oa=w_oa[i], w_ob=w_ob[i], w_out=w_out[i], g_ffn=g_ffn[i], w_pq=w_pq[i], sub_keys=sub_keys[i],
                   tab_u=_pack_table(peer_u[i]), tab_v=_pack_table(peer_v[i]), g_ple=g_ple[i],
                   w_ple_gate=w_ple_gate[i], w_ple=w_ple[i])

        bp, s, _ = xp.shape

        def attn_prompt(q, k, kdup, v, vb, qi, misc, kidup):
            k_sel = min(TOPK_MAX, s // 4)
            tq, tk = _tile(s, 256), _tile(s, 512)
            r3 = lambda a: a.reshape(bp, s, a.shape[-1])
            mask = _idx_mask(r3(qi), r3(misc), r3(kidup), k_sel, tq, tk)
            return _attn(r3(q), r3(kdup), r3(vb), mask, tq, tk).reshape(bp * s, A_Q)

        res = _layer(xp, p_prompt[i], None, None, attn_prompt, prm)
        xp = res[0]
        for lst, val in zip(outs[:5], res[1:]):
            lst.append(val)

        bd, t, _ = xs.shape

        def attn_sample(q, k, kdup, v, vb, qi, misc, kidup):
            return _attn_sample(q, k, v, qi, misc, cache_k[i], cache_v[i], cache_idx_k[i], page_table, bd, t)

        res = _layer(xs, p_sample[i], state_conv[i], state_delta[i], attn_sample, prm)
        xs = res[0]
        for lst, val in zip(outs[5:], res[1:]):
            lst.append(val)

    cast = [cache_k.dtype, cache_v.dtype, cache_idx_k.dtype, state_conv.dtype, state_delta.dtype] * 2
    stacked = [jnp.stack(lst).astype(dt) for lst, dt in zip(outs, cast)]
    return (xp, xs, *stacked)
```

```python
import functools
import math

import jax
import jax.numpy as jnp
import numpy as np
from jax import lax
from jax.experimental import pallas as pl
from jax.experimental.pallas import tpu as pltpu

F32 = jnp.float32
BF16 = jnp.bfloat16
I32 = jnp.int32

D_MODEL = 1024
PAGE_SIZE = 128
A_HEADS = 8
A_KV_HEADS = 4
A_REP = A_HEADS // A_KV_HEADS
A_HEAD_DIM = 64
A_Q = A_HEADS * A_HEAD_DIM
A_KV = A_KV_HEADS * A_HEAD_DIM
IDX_HEADS = 8
IDX_DIM = 64
TOPK_MAX = 256
DN_HEADS = 4
DN_DK = 128
DN_DV = 128
DN_QK = DN_HEADS * DN_DK
DN_V = DN_HEADS * DN_DV
DN_CONV_CH = 2 * DN_QK + DN_V
CONV_W = 4
DN_CHUNK = 64
PEER_HEADS = 8
PEER_KEYS = 128
PEER_DKEY = 256
PEER_TOPK = 16
PLE_DIM = 256
EPS = 1e-6
IN_SPLITS = (A_Q, A_KV, A_KV, IDX_HEADS * IDX_DIM, IDX_DIM, IDX_HEADS, DN_CONV_CH, DN_V, DN_HEADS, DN_HEADS, D_MODEL, D_MODEL)

LANES = 128
SUBLANES = 8
VMEM_LIMIT = 56 * 1024 * 1024

NEG_BIG = -1e30
INT_MIN = -(2 ** 31)

MISC_WI = IDX_DIM
MISC_DB = MISC_WI + IDX_HEADS
MISC_DA = MISC_DB + DN_HEADS

SEG_Q = 0
SEG_K = SEG_Q + A_Q
SEG_V = SEG_K + A_KV
SEG_QI = SEG_V + A_KV
SEG_MISC = SEG_QI + IDX_HEADS * IDX_DIM
SEG_DQKV = SEG_MISC + LANES
SEG_DZ = SEG_DQKV + DN_CONV_CH
SEG_GA = SEG_DZ + DN_V
SEG_GB = SEG_GA + D_MODEL
SEG_END = SEG_GB + D_MODEL


def _cparams(sem):
    return pltpu.CompilerParams(dimension_semantics=sem, vmem_limit_bytes=VMEM_LIMIT)


def _dot(a, b):
    return jnp.dot(a, b, preferred_element_type=F32)


def _dot_nt(a, b):
    return lax.dot_general(a, b, (((1,), (1,)), ((), ())), preferred_element_type=F32)


def _split(a):
    hi = a.astype(BF16)
    lo = (a - hi.astype(F32)).astype(BF16)
    return hi, lo


def _dot3(a, b, nt=False):
    d = _dot_nt if nt else _dot
    ah, al = _split(a)
    bh, bl = _split(b)
    return d(ah, bh) + (d(ah, bl) + d(al, bh))


def _lane_iota(shape):
    return lax.broadcasted_iota(I32, shape, len(shape) - 1)


def _row_iota(shape):
    return lax.broadcasted_iota(I32, shape, len(shape) - 2)


def _half_norm(blk, gain):
    lane = _lane_iota(blk.shape)
    lo = lane < A_HEAD_DIM
    sq = blk * blk
    s_lo = jnp.sum(jnp.where(lo, sq, 0.0), axis=-1, keepdims=True)
    s_hi = jnp.sum(jnp.where(lo, 0.0, sq), axis=-1, keepdims=True)
    r_lo = lax.rsqrt(s_lo * (1.0 / A_HEAD_DIM) + EPS)
    r_hi = lax.rsqrt(s_hi * (1.0 / A_HEAD_DIM) + EPS)
    return blk * jnp.where(lo, r_lo, r_hi) * gain


def _proj_kernel(x_ref, gmix_ref, w_ref, gq_ref, gk_ref, gik_ref,
                 q_ref, k_ref, kdup_ref, v_ref, vb_ref, qi_ref, misc_ref, kidup_ref,
                 dqkv_ref, dz_ref, ga_ref, gb_ref):
    x = x_ref[...]
    h = x * lax.rsqrt(jnp.mean(x * x, axis=-1, keepdims=True) + EPS) * gmix_ref[...]
    hb = h.astype(BF16)

    def seg(a, b):
        return _dot(hb, w_ref[:, a:b])

    lane = _lane_iota((x.shape[0], LANES))
    lo = lane < A_HEAD_DIM

    zq = seg(SEG_Q, SEG_K)
    for c in range(A_Q // LANES):
        blk = _half_norm(zq[:, c * LANES:(c + 1) * LANES], gq_ref[...])
        q_ref[:, c * LANES:(c + 1) * LANES] = (blk * (A_HEAD_DIM ** -0.5)).astype(BF16)

    zk = seg(SEG_K, SEG_V)
    for c in range(A_KV // LANES):
        blk = _half_norm(zk[:, c * LANES:(c + 1) * LANES], gk_ref[...])
        k_ref[:, c * LANES:(c + 1) * LANES] = blk
        rolled = pltpu.roll(blk, A_HEAD_DIM, axis=1)
        kdup_ref[:, (2 * c) * LANES:(2 * c + 1) * LANES] = jnp.where(lo, blk, rolled).astype(BF16)
        kdup_ref[:, (2 * c + 1) * LANES:(2 * c + 2) * LANES] = jnp.where(lo, rolled, blk).astype(BF16)

    zv = seg(SEG_V, SEG_QI)
    v_ref[...] = zv
    vb_ref[...] = zv.astype(BF16)

    qi_ref[...] = seg(SEG_QI, SEG_MISC).astype(BF16)

    zm = seg(SEG_MISC, SEG_DQKV)
    s_ik = jnp.sum(jnp.where(lo, zm * zm, 0.0), axis=-1, keepdims=True)
    kin = zm * lax.rsqrt(s_ik * (1.0 / IDX_DIM) + EPS) * gik_ref[...]
    wscale = (IDX_HEADS * IDX_DIM) ** -0.5
    misc_ref[...] = jnp.where(lo, kin, jnp.where(lane < MISC_DB, zm * wscale, zm))
    kin0 = jnp.where(lo, kin, 0.0)
    kidup_ref[...] = (kin0 + pltpu.roll(kin0, IDX_DIM, axis=1)).astype(BF16)

    dqkv_ref[...] = seg(SEG_DQKV, SEG_DZ)
    dz_ref[...] = seg(SEG_DZ, SEG_GA)
    ga_ref[...] = seg(SEG_GA, SEG_GB)
    gb_ref[...] = seg(SEG_GB, SEG_END)


def _permute_w_in(w_in):
    cuts = np.cumsum((0,) + IN_SPLITS)
    parts = [w_in[:, cuts[i]:cuts[i + 1]] for i in range(len(IN_SPLITS))]
    aq, ak, av, iq, ik, iw, dqkv, dz, db, da, ga, gb = parts
    pad = jnp.zeros((w_in.shape[0], LANES - IDX_DIM - IDX_HEADS - 2 * DN_HEADS), w_in.dtype)
    return jnp.concatenate([aq, ak, av, iq, ik, iw, db, da, pad, dqkv, dz, ga, gb], axis=1).astype(BF16)


def _tile2(g):
    return jnp.concatenate([g, g]).reshape(1, LANES).astype(F32)


def _proj(x2d, g_mix, w_perm, g_q, g_k, g_idx_k, tm):
    n = x2d.shape[0]
    assert n % tm == 0
    widths = [(A_Q, BF16), (A_KV, F32), (2 * A_KV, BF16), (A_KV, F32), (A_KV, BF16), (IDX_HEADS * IDX_DIM, BF16),
              (LANES, F32), (LANES, BF16), (DN_CONV_CH, F32), (DN_V, F32), (D_MODEL, F32), (D_MODEL, F32)]
    row = lambda i: (i, 0)
    fixed = lambda i: (0, 0)
    return pl.pallas_call(
        _proj_kernel,
        grid=(n // tm,),
        in_specs=[pl.BlockSpec((tm, D_MODEL), row), pl.BlockSpec((1, D_MODEL), fixed),
                  pl.BlockSpec((D_MODEL, SEG_END), fixed), pl.BlockSpec((1, LANES), fixed),
                  pl.BlockSpec((1, LANES), fixed), pl.BlockSpec((1, LANES), fixed)],
        out_specs=[pl.BlockSpec((tm, w), row) for w, _ in widths],
        out_shape=[jax.ShapeDtypeStruct((n, w), dt) for w, dt in widths],
        compiler_params=_cparams(("parallel",)),
        name="proj_in",
    )(x2d, g_mix.reshape(1, D_MODEL), w_perm, _tile2(g_q), _tile2(g_k), _tile2(g_idx_k))


def _sort_key(score):
    score = jnp.where(score == 0.0, 0.0, score)
    bits = pltpu.bitcast(score, I32)
    return jnp.where(bits < 0, bits ^ jnp.int32(0x7FFFFFFF), bits)


def _index_scores(qi, wi_tile, kblk):
    lane = _lane_iota((qi.shape[0], LANES))
    lo = lane < IDX_DIM
    sc = None
    for c in range(IDX_HEADS // 2):
        q128 = qi[:, c * LANES:(c + 1) * LANES]
        zero = jnp.zeros_like(q128)
        for half in range(2):
            qm = jnp.where(lo, q128, zero) if half == 0 else jnp.where(lo, zero, q128)
            s = _dot_nt(qm, kblk)
            hidx = MISC_WI + 2 * c + half
            term = jnp.maximum(s, 0.0) * wi_tile[:, hidx:hidx + 1]
            sc = term if sc is None else sc + term
    return sc


def _select_rows(keys_ref, thr_ref, jcut_ref, nvalid, k_sel, tq, tk, rb, live_rows=None):
    nchunk = tk // LANES
    nbits_idx = int(math.ceil(math.log2(keys_ref.shape[0] * tk))) + 1

    for r in range(tq // rb):
        r0 = r * rb

        def count(pred):
            def body(kb, cnt):
                blk = keys_ref[kb, r0:r0 + rb, :]
                for c in range(nchunk):
                    cnt = cnt + pred(blk[:, c * LANES:(c + 1) * LANES], kb * tk + c * LANES)
                return cnt
            cnt = lax.fori_loop(0, nvalid, body, jnp.zeros((rb, LANES), I32))
            return jnp.sum(cnt, axis=1, keepdims=True)

        def bit_step(i, cur):
            cand = cur + lax.shift_left(jnp.int32(1), 31 - i)
            cand_b = jnp.broadcast_to(cand, (rb, LANES))
            c_ge = count(lambda kv, base: jnp.where(kv >= cand_b, 1, 0))
            return jnp.where(c_ge >= k_sel, cand, cur)

        thr = lax.fori_loop(0, 32, bit_step, jnp.full((rb, 1), INT_MIN, I32))
        thr_b = jnp.broadcast_to(thr, (rb, LANES))
        c_gt = count(lambda kv, base: jnp.where(kv > thr_b, 1, 0))
        c_ge = count(lambda kv, base: jnp.where(kv >= thr_b, 1, 0))
        need = k_sel - c_gt
        thr_ref[r0:r0 + rb, :] = thr_b
        jcut_ref[r0:r0 + rb, :] = jnp.full((rb, LANES), 2 ** 30, I32)

        if live_rows is not None:
            c_ge = jnp.where(_row_iota((rb, 1)) % SUBLANES < live_rows, c_ge, 0)

        @pl.when(jnp.max(c_ge) > k_sel)
        def _():
            lane = _lane_iota((rb, LANES))

            def jstep(i, jcur):
                candj = jcur + lax.shift_left(jnp.int32(1), nbits_idx - 1 - i)
                candj_b = jnp.broadcast_to(candj, (rb, LANES))
                f = count(lambda kv, base: jnp.where(kv == thr_b, jnp.where(lane + base < candj_b, 1, 0), 0))
                return jnp.where(f < need, candj, jcur)

            jcut = lax.fori_loop(0, nbits_idx, jstep, jnp.zeros((rb, 1), I32))
            jcut_ref[r0:r0 + rb, :] = jnp.broadcast_to(jcut, (rb, LANES))


def _idx_mask_kernel(qi_ref, wi_ref, kidup_ref, mask_ref, keys_ref, thr_ref, jcut_ref, *, k_sel, tq, tk, rb):
    i = pl.program_id(1)
    nkb = keys_ref.shape[0]
    nvalid = ((i + 1) * tq + tk - 1) // tk
    qi = qi_ref[0]
    wi_tile = wi_ref[0]
    q_pos = i * tq + _row_iota((tq, tk))
    lane = _lane_iota((tq, tk))

    def fill(kb, carry):
        kblk = kidup_ref[0, pl.ds(pl.multiple_of(kb * tk, tk), tk), :]
        sc = _index_scores(qi, wi_tile, kblk)
        sc = jnp.where(lane + kb * tk <= q_pos, sc, -jnp.inf)
        keys_ref[kb] = _sort_key(sc)
        return carry

    lax.fori_loop(0, nvalid, fill, 0)
    _select_rows(keys_ref, thr_ref, jcut_ref, nvalid, k_sel, tq, tk, rb)

    for kb in range(nkb):
        @pl.when(kb < nvalid)
        def _():
            thr = jnp.broadcast_to(thr_ref[:, 0:1], (tq, tk))
            jcut = jnp.broadcast_to(jcut_ref[:, 0:1], (tq, tk))
            key = keys_ref[kb]
            pos = lane + kb * tk
            sel = jnp.where(key > thr, 1, jnp.where(key == thr, jnp.where(pos <= jcut, 1, 0), 0))
            sel = jnp.where(pos <= q_pos, sel, 0)
            mask_ref[0, :, kb * tk:(kb + 1) * tk] = sel.astype(jnp.int8)

        @pl.when(kb >= nvalid)
        def _():
            mask_ref[0, :, kb * tk:(kb + 1) * tk] = jnp.zeros((tq, tk), jnp.int8)


def _idx_mask(qi, misc, kidup, k_sel, tq, tk):
    b, s, _ = qi.shape
    assert s % tq == 0 and s % tk == 0 and tq % 32 == 0
    rb = min(64, tq)
    kern = functools.partial(_idx_mask_kernel, k_sel=k_sel, tq=tq, tk=tk, rb=rb)
    return pl.pallas_call(
        kern,
        grid=(b, s // tq),
        in_specs=[pl.BlockSpec((1, tq, IDX_HEADS * IDX_DIM), lambda bi, i: (bi, i, 0)),
                  pl.BlockSpec((1, tq, LANES), lambda bi, i: (bi, i, 0)),
                  pl.BlockSpec((1, s, LANES), lambda bi, i: (bi, 0, 0))],
        out_specs=pl.BlockSpec((1, tq, s), lambda bi, i: (bi, i, 0)),
        out_shape=jax.ShapeDtypeStruct((b, s, s), jnp.int8),
        scratch_shapes=[pltpu.VMEM((s // tk, tq, tk), I32), pltpu.VMEM((tq, LANES), I32), pltpu.VMEM((tq, LANES), I32)],
        compiler_params=_cparams(("parallel", "parallel")),
        name="idx_mask",
    )(qi, misc, kidup)


def _attn_kernel(q_ref, kdup_ref, vb_ref, mask_ref, o_ref, m_ref, l_ref, acc_ref, *, tq, tk):
    i = pl.program_id(1)
    kb = pl.program_id(2)
    nk = pl.num_programs(2)
    last_needed = ((i + 1) * tq - 1) // tk

    @pl.when(kb == 0)
    def _():
        m_ref[...] = jnp.full(m_ref.shape, NEG_BIG, F32)
        l_ref[...] = jnp.zeros(l_ref.shape, F32)
        acc_ref[...] = jnp.zeros(acc_ref.shape, F32)

    @pl.when(kb <= last_needed)
    def _():
        bias = jnp.where(mask_ref[0].astype(I32) != 0, 0.0, NEG_BIG).astype(F32)
        bias2 = jnp.concatenate([bias, bias], axis=0)
        lane = _lane_iota((tq, LANES))
        lo = lane < A_HEAD_DIM
        for g in range(A_KV_HEADS):
            q128 = q_ref[0, :, g * LANES:(g + 1) * LANES]
            zero = jnp.zeros_like(q128)
            q2 = jnp.concatenate([jnp.where(lo, q128, zero), jnp.where(lo, zero, q128)], axis=0)
            s = _dot_nt(q2, kdup_ref[0, :, g * LANES:(g + 1) * LANES]) + bias2
            m_old = m_ref[g]
            m_new = jnp.maximum(m_old, jnp.max(s, axis=1, keepdims=True))
            alpha = jnp.exp(m_old - m_new)
            p = jnp.exp(s - m_new[:, 0:1])
            l_ref[g] = alpha * l_ref[g] + jnp.sum(p, axis=1, keepdims=True)
            v128 = vb_ref[0, :, (g // 2) * LANES:(g // 2 + 1) * LANES]
            acc_ref[g] = alpha * acc_ref[g] + _dot(p.astype(BF16), v128)
            m_ref[g] = m_new

    @pl.when(kb == nk - 1)
    def _():
        lane = _lane_iota((tq, LANES))
        lo = lane < A_HEAD_DIM
        for g in range(A_KV_HEADS):
            a = acc_ref[g] / l_ref[g]
            top, bot = a[:tq], a[tq:]
            if g % 2 == 0:
                o128 = jnp.where(lo, top, pltpu.roll(bot, A_HEAD_DIM, axis=1))
            else:
                o128 = jnp.where(lo, pltpu.roll(top, A_HEAD_DIM, axis=1), bot)
            o_ref[0, :, g * LANES:(g + 1) * LANES] = o128


def _attn(q, kdup, vb, mask, tq, tk):
    b, s, _ = q.shape
    nk = s // tk

    def kv_idx(bi, i, kb):
        return (bi, jnp.minimum(kb, ((i + 1) * tq - 1) // tk), 0)

    def mask_idx(bi, i, kb):
        return (bi, i, jnp.minimum(kb, ((i + 1) * tq - 1) // tk))

    kern = functools.partial(_attn_kernel, tq=tq, tk=tk)
    return pl.pallas_call(
        kern,
        grid=(b, s // tq, nk),
        in_specs=[pl.BlockSpec((1, tq, A_Q), lambda bi, i, kb: (bi, i, 0)),
                  pl.BlockSpec((1, tk, 2 * A_KV), kv_idx),
                  pl.BlockSpec((1, tk, A_KV), kv_idx),
                  pl.BlockSpec((1, tq, tk), mask_idx)],
        out_specs=pl.BlockSpec((1, tq, A_Q), lambda bi, i, kb: (bi, i, 0)),
        out_shape=jax.ShapeDtypeStruct((b, s, A_Q), F32),
        scratch_shapes=[pltpu.VMEM((A_KV_HEADS, 2 * tq, LANES), F32)] * 3,
        compiler_params=_cparams(("parallel", "parallel", "arbitrary")),
        name="sel_attn",
    )(q, kdup, vb, mask)


def _conv_kernel(x_ref, xprev_ref, halo0_ref, w_ref, q_ref, k_ref, v_ref, *, tt):
    i = pl.program_id(1)
    x = x_ref[0]
    halo = jnp.where(i == 0, halo0_ref[0], xprev_ref[0])
    w = w_ref[...]

    def post(y, rows):
        y = y * jax.nn.sigmoid(y)
        for h in range(DN_HEADS):
            for j, ref in enumerate((q_ref, k_ref)):
                blk = y[:, j * DN_QK + h * DN_DK: j * DN_QK + (h + 1) * DN_DK]
                blk = blk * lax.rsqrt(jnp.sum(blk * blk, axis=-1, keepdims=True) + EPS)
                ref[0, 0:rows, h * DN_DK:(h + 1) * DN_DK] = blk
        v_ref[0, 0:rows, :] = y[:, 2 * DN_QK:]

    y = x * w[CONV_W - 1:CONV_W, :]
    for j in range(1, CONV_W):
        y = y + pltpu.roll(x, j, axis=0) * w[CONV_W - 1 - j:CONV_W - j, :]
    post(y, tt)

    x8 = x[0:SUBLANES]
    row = _row_iota(x8.shape)
    y8 = x8 * w[CONV_W - 1:CONV_W, :]
    for j in range(1, CONV_W):
        xs = jnp.where(row < j, pltpu.roll(halo, j, axis=0), pltpu.roll(x8, j, axis=0))
        y8 = y8 + xs * w[CONV_W - 1 - j:CONV_W - j, :]
    post(y8, SUBLANES)


def _conv(dqkv, halo0, conv_w, tt):
    b, t, c = dqkv.shape
    assert t % tt == 0 and tt % SUBLANES == 0
    r = tt // SUBLANES
    kern = functools.partial(_conv_kernel, tt=tt)
    out = jax.ShapeDtypeStruct((b, t, DN_QK), F32)
    return pl.pallas_call(
        kern,
        grid=(b, t // tt),
        in_specs=[pl.BlockSpec((1, tt, c), lambda bi, i: (bi, i, 0)),
                  pl.BlockSpec((1, SUBLANES, c), lambda bi, i: (bi, jnp.maximum(i * r - 1, 0), 0)),
                  pl.BlockSpec((1, SUBLANES, c), lambda bi, i: (bi, 0, 0)),
                  pl.BlockSpec((CONV_W, c), lambda bi, i: (0, 0))],
        out_specs=[pl.BlockSpec((1, tt, DN_QK), lambda bi, i: (bi, i, 0))] * 3,
        out_shape=[out, out, out],
        compiler_params=_cparams(("parallel", "parallel")),
        name="dn_conv",
    )(dqkv, dqkv, halo0, conv_w)


def _softplus(x):
    return jnp.maximum(x, 0.0) + jnp.log(1.0 + jnp.exp(-jnp.abs(x)))


def _pad_rows(a, rows):
    if a.shape[0] == rows:
        return a
    return jnp.concatenate([a, jnp.zeros((rows - a.shape[0], a.shape[1]), a.dtype)], axis=0)


def _delta_kernel(q_ref, k_ref, v_ref, z_ref, misc_ref, nega_ref, dtb_ref, gout_ref, s0_ref, o_ref, st_ref, *, c, t_valid):
    n = pl.program_id(1)

    @pl.when(n == 0)
    def _():
        st_ref[...] = s0_ref[...]

    mt = misc_ref[0]
    row = _row_iota((c, LANES))
    live = row < t_valid
    beta_t = jnp.where(live, jax.nn.sigmoid(mt), 0.0)
    ld_t = jnp.where(live, nega_ref[...] * _softplus(mt + dtb_ref[...]), 0.0)
    gc_t = ld_t
    sft = 1
    while sft < c:
        gc_t = gc_t + jnp.where(row >= sft, pltpu.roll(gc_t, sft, axis=0), 0.0)
        sft *= 2
    gc_tr = _pad_rows(gc_t, LANES).T

    ri = lax.broadcasted_iota(I32, (c, c), 0)
    ci = lax.broadcasted_iota(I32, (c, c), 1)
    incl = ri >= ci
    strict = ri > ci
    eye = jnp.where(ri == ci, 1.0, 0.0).astype(F32)
    scale = DN_DK ** -0.5

    for h in range(DN_HEADS):
        sl = slice(h * DN_DK, (h + 1) * DN_DK)
        q = jnp.where(live, q_ref[0, :, sl] * scale, 0.0)
        k = jnp.where(live, k_ref[0, :, sl], 0.0)
        v = jnp.where(live, v_ref[0, :, sl], 0.0)
        beta = beta_t[:, MISC_DB + h:MISC_DB + h + 1]
        gcc = gc_t[:, MISC_DA + h:MISC_DA + h + 1]
        gcr = gc_tr[MISC_DA + h:MISC_DA + h + 1, 0:c]
        decay = jnp.where(incl, jnp.exp(jnp.where(incl, gcc - gcr, 0.0)), 0.0)
        kb = k * beta
        m = jnp.where(strict, _dot3(kb, k, nt=True) * decay, 0.0)
        pw = -m
        r = eye + pw
        nn = 2
        while nn < c:
            pw = _dot3(pw, pw)
            r = _dot3(r, eye + pw)
            nn *= 2
        egc = jnp.exp(gcc)
        u = _dot3(r, v * beta)
        w = _dot3(r, kb * egc)
        a_qk = _dot3(q, k, nt=True) * decay
        s = st_ref[0, h]
        v_new = u - _dot3(w, s)
        o = _dot3(q * egc, s) + _dot3(a_qk, v_new)
        glast = gcc[c - 1:c, :]
        kw = k * jnp.exp(glast - gcc)
        kw_t = _pad_rows(kw, LANES).T
        st_ref[0, h] = s * jnp.exp(glast) + _dot3(kw_t, _pad_rows(v_new, LANES))
        on = o * lax.rsqrt(jnp.mean(o * o, axis=-1, keepdims=True) + EPS) * gout_ref[...]
        z = z_ref[0, :, sl]
        o_ref[0, :, sl] = on * (z * jax.nn.sigmoid(z))


def _delta(qn, kn, v, dz, misc, a_log, dt_bias, g_dn_out, s0, c, t_valid):
    b, t, _ = qn.shape
    assert t % c == 0 and (t_valid == t or t == c)
    nega = jnp.zeros((1, LANES), F32).at[0, MISC_DA:MISC_DA + DN_HEADS].set(-jnp.exp(a_log.astype(F32)))
    dtb = jnp.zeros((1, LANES), F32).at[0, MISC_DA:MISC_DA + DN_HEADS].set(dt_bias.astype(F32))
    tok = lambda bi, n: (bi, n, 0)
    fixed = lambda bi, n: (0, 0)
    st = lambda bi, n: (bi, 0, 0, 0)
    kern = functools.partial(_delta_kernel, c=c, t_valid=t_valid)
    return pl.pallas_call(
        kern,
        grid=(b, t // c),
        in_specs=[pl.BlockSpec((1, c, DN_QK), tok), pl.BlockSpec((1, c, DN_QK), tok), pl.BlockSpec((1, c, DN_V), tok),
                  pl.BlockSpec((1, c, DN_V), tok), pl.BlockSpec((1, c, LANES), tok),
                  pl.BlockSpec((1, LANES), fixed), pl.BlockSpec((1, LANES), fixed), pl.BlockSpec((1, DN_DV), fixed),
                  pl.BlockSpec((1, DN_HEADS, DN_DK, DN_DV), st)],
        out_specs=[pl.BlockSpec((1, c, DN_V), tok), pl.BlockSpec((1, DN_HEADS, DN_DK, DN_DV), st)],
        out_shape=[jax.ShapeDtypeStruct((b, t, DN_V), F32), jax.ShapeDtypeStruct((b, DN_HEADS, DN_DK, DN_DV), F32)],
        compiler_params=_cparams(("parallel", "arbitrary")),
        name="delta_rule",
    )(qn, kn, v, dz, misc, nega, dtb, g_dn_out.reshape(1, DN_DV).astype(F32), s0)


def _merge_kernel(x_ref, oa_ref, ob_ref, ga_ref, gb_ref, woa_ref, wob_ref, wout_ref, gffn_ref, wpqt_ref,
                  x1_ref, h2_ref, qt_ref):
    ma = jax.nn.sigmoid(ga_ref[...]) * _dot(oa_ref[...].astype(BF16), woa_ref[...])
    mb = jax.nn.sigmoid(gb_ref[...]) * _dot(ob_ref[...].astype(BF16), wob_ref[...])
    x1 = x_ref[...] + _dot((ma + mb).astype(BF16), wout_ref[...])
    x1_ref[...] = x1
    h2 = x1 * lax.rsqrt(jnp.mean(x1 * x1, axis=-1, keepdims=True) + EPS) * gffn_ref[...]
    h2_ref[...] = h2
    qt_ref[...] = _dot_nt(wpqt_ref[...], h2.astype(BF16))


def _merge(x2d, oa, ob, ga, gb, w_oa, w_ob, w_out, g_ffn, w_pq, tm):
    n = x2d.shape[0]
    nq = w_pq.shape[1]
    row = lambda i: (i, 0)
    fixed = lambda i: (0, 0)
    return pl.pallas_call(
        _merge_kernel,
        grid=(n // tm,),
        in_specs=[pl.BlockSpec((tm, D_MODEL), row), pl.BlockSpec((tm, A_Q), row), pl.BlockSpec((tm, DN_V), row),
                  pl.BlockSpec((tm, D_MODEL), row), pl.BlockSpec((tm, D_MODEL), row),
                  pl.BlockSpec((A_Q, D_MODEL), fixed), pl.BlockSpec((DN_V, D_MODEL), fixed),
                  pl.BlockSpec((D_MODEL, D_MODEL), fixed), pl.BlockSpec((1, D_MODEL), fixed),
                  pl.BlockSpec((nq, D_MODEL), fixed)],
        out_specs=[pl.BlockSpec((tm, D_MODEL), row), pl.BlockSpec((tm, D_MODEL), row), pl.BlockSpec((nq, tm), lambda i: (0, i))],
        out_shape=[jax.ShapeDtypeStruct((n, D_MODEL), F32), jax.ShapeDtypeStruct((n, D_MODEL), F32),
                   jax.ShapeDtypeStruct((nq, n), F32)],
        compiler_params=_cparams(("parallel",)),
        name="merge_out",
    )(x2d, oa, ob, ga, gb, w_oa.astype(BF16), w_ob.astype(BF16), w_out.astype(BF16),
      g_ffn.reshape(1, D_MODEL).astype(F32), w_pq.T.astype(BF16))


def _peer_cands():
    cands = [(i, j) for i in range(PEER_TOPK) for j in range(PEER_TOPK) if (i + 1) * (j + 1) <= PEER_TOPK]
    return sorted(cands, key=lambda ij: ij[0] * PEER_TOPK + ij[1])


def _route_kernel(qt_ref, sk_ref, eid_ref, gate_ref, toff_ref, val_ref, idx_ref, *, tn):
    half = PEER_DKEY // 2
    key_iota = lax.broadcasted_iota(I32, (PEER_KEYS, tn), 0)

    def per_table(t, carry):
        h = t // 2
        qblk = qt_ref[pl.ds(pl.multiple_of(t * half, half), half), :]
        s = _dot3(sk_ref[t], qblk)
        for r in range(PEER_TOPK):
            m = jnp.max(s, axis=0, keepdims=True)
            am = jnp.min(jnp.where(s == m, key_iota, PEER_KEYS), axis=0, keepdims=True)
            val_ref[t % 2, r, pl.ds(h, 1), :] = m
            idx_ref[t % 2, r, pl.ds(h, 1), :] = am
            s = jnp.where(key_iota == am, -jnp.inf, s)
        return carry

    lax.fori_loop(0, 2 * PEER_HEADS, per_table, 0)

    cands = _peer_cands()
    cv = [val_ref[0, i] + val_ref[1, j] for i, j in cands]
    ce = [idx_ref[0, i] * PEER_KEYS + idx_ref[1, j] for i, j in cands]
    nc = len(cands)
    rank = []
    for a in range(nc):
        rk = jnp.zeros(cv[a].shape, I32)
        for b in range(nc):
            if b < a:
                rk = rk + jnp.where(cv[b] >= cv[a], 1, 0)
            elif b > a:
                rk = rk + jnp.where(cv[b] > cv[a], 1, 0)
        rank.append(rk)
    mx = cv[0]
    ex = [jnp.where(rank[a] < PEER_TOPK, jnp.exp(cv[a] - mx), 0.0) for a in range(nc)]
    den = ex[0]
    for a in range(1, nc):
        den = den + ex[a]
    inv = 1.0 / den
    e_slots = []
    for slot in range(PEER_TOPK):
        e = jnp.zeros(cv[0].shape, I32)
        g = jnp.zeros(cv[0].shape, F32)
        for a in range(nc):
            hit = rank[a] == slot
            e = jnp.where(hit, ce[a], e)
            g = jnp.where(hit, ex[a], g)
        eid_ref[slot] = e
        gate_ref[slot] = g * inv
        e_slots.append(e)
    toff_ref[...] = ((jnp.concatenate(e_slots, axis=0) >> 1) * SUBLANES).T


def _route(qt, sub_keys, tn):
    nq, n = qt.shape
    sk = sub_keys.reshape(2 * PEER_HEADS, PEER_KEYS, PEER_DKEY // 2).astype(F32)
    kern = functools.partial(_route_kernel, tn=tn)
    blk = pl.BlockSpec((PEER_TOPK, PEER_HEADS, tn), lambda i: (0, 0, i))
    npair = PEER_TOPK * PEER_HEADS
    return pl.pallas_call(
        kern,
        grid=(n // tn,),
        in_specs=[pl.BlockSpec((nq, tn), lambda i: (0, i)),
                  pl.BlockSpec((2 * PEER_HEADS, PEER_KEYS, PEER_DKEY // 2), lambda i: (0, 0, 0))],
        out_specs=[blk, blk, pl.BlockSpec((tn, npair), lambda i: (i, 0))],
        out_shape=[jax.ShapeDtypeStruct((PEER_TOPK, PEER_HEADS, n), I32), jax.ShapeDtypeStruct((PEER_TOPK, PEER_HEADS, n), F32),
                   jax.ShapeDtypeStruct((n, npair), I32)],
        scratch_shapes=[pltpu.VMEM((2, PEER_TOPK, PEER_HEADS, tn), F32), pltpu.VMEM((2, PEER_TOPK, PEER_HEADS, tn), I32)],
        compiler_params=_cparams(("parallel",)),
        name="peer_route",
    )(qt, sk)


def _pack_table(tab):
    e, d = tab.shape
    assert d == SUBLANES * LANES
    bits = lax.bitcast_convert_type(tab.astype(BF16), jnp.uint16).astype(jnp.uint32)
    words = (bits[0::2] << 16) | bits[1::2]
    return lax.bitcast_convert_type(words, I32).reshape(e // 2 * SUBLANES, LANES)


HI_HALF = -65536


def _bf16_words(x):
    return pltpu.bitcast(x.astype(BF16).astype(F32), I32) & HI_HALF


def _shr16(w):
    return lax.shift_right_logical(w, jnp.full(w.shape, 16, I32))


def _packed_mul(a_words, b_words):
    return pltpu.bitcast(pltpu.bitcast(a_words, BF16) * pltpu.bitcast(b_words, BF16), I32)


def _packed_add(a_words, b_words):
    return pltpu.bitcast(pltpu.bitcast(a_words, BF16) + pltpu.bitcast(b_words, BF16), I32)


def _hi_f32(w):
    return pltpu.bitcast(w & HI_HALF, F32)


def _lo_f32(w):
    return pltpu.bitcast(jnp.left_shift(w, 16), F32)


_BITREV8 = (0, 4, 2, 6, 1, 5, 3, 7)


PEER_U_SUB = 16


def _peer_u_kernel(toff_ref, x_ref, eid_ref, gate_ref, tab_ref, ce_ref, co_ref, rhi_ref, rlo_ref, *, tn, sub_t):
    npair = gate_ref.shape[0]
    lane = _lane_iota((npair, tn))
    sub = lax.broadcasted_iota(I32, (SUBLANES, LANES), 0)
    keep = {step: (sub & step) == 0 for step in (4, 2, 1)}

    def per_token(tt, t0):
        t = t0 + tt
        xw = _bf16_words(x_ref[t])
        xw = xw | _shr16(xw)
        for g in range(npair // SUBLANES):
            prods = []
            for p in _BITREV8:
                off = pl.multiple_of(toff_ref[t * npair + (g * SUBLANES + p)], SUBLANES)
                prods.append(_packed_mul(tab_ref[pl.ds(off, SUBLANES), :], xw))
            step = SUBLANES // 2
            while step >= 1:
                nxt = []
                for a in range(0, len(prods), 2):
                    lo_t, hi_t = prods[a], prods[a + 1]
                    left = jnp.where(keep[step], lo_t, pltpu.roll(hi_t, step, axis=0))
                    right = jnp.where(keep[step], pltpu.roll(lo_t, SUBLANES - step, axis=0), hi_t)
                    nxt.append(_packed_add(left, right))
                prods = nxt
                step //= 2
            rhi_ref[tt, g * SUBLANES:(g + 1) * SUBLANES, :] = _hi_f32(prods[0])
            rlo_ref[tt, g * SUBLANES:(g + 1) * SUBLANES, :] = _lo_f32(prods[0])
        return t0

    def per_sub(sb, accs):
        acc_hi, acc_lo = accs
        t0 = sb * sub_t
        lax.fori_loop(0, sub_t, per_token, t0)
        for tt in range(sub_t):
            hit = lane == t0 + tt
            acc_hi = jnp.where(hit, jnp.sum(rhi_ref[tt], axis=1, keepdims=True), acc_hi)
            acc_lo = jnp.where(hit, jnp.sum(rlo_ref[tt], axis=1, keepdims=True), acc_lo)
        return acc_hi, acc_lo

    zero = jnp.zeros((npair, tn), F32)
    acc_hi, acc_lo = lax.fori_loop(0, tn // sub_t, per_sub, (zero, zero))
    even = (eid_ref[...] & 1) == 0
    coef = gate_ref[...] * jax.nn.gelu(jnp.where(even, acc_hi, acc_lo))
    ce_ref[...] = jnp.where(even, coef, 0.0)
    co_ref[...] = jnp.where(even, 0.0, coef)


def _peer_v_kernel(toff_ref, ce_ref, co_ref, rep_ref, tab_ref, y_ref, bw_ref, *, tn, npair):
    ce = ce_ref[...].astype(BF16)
    co = co_ref[...].astype(BF16)
    chunk = min(16, tn)
    for c in range(tn // chunk):
        rep = rep_ref[:, c * chunk * LANES:(c + 1) * chunk * LANES]
        be = _dot(ce, rep)
        bo = _dot(co, rep)
        words = (pltpu.bitcast(be, I32) & HI_HALF) | _shr16(pltpu.bitcast(bo, I32))
        for t in range(chunk):
            bw_ref[c * chunk + t] = words[:, t * LANES:(t + 1) * LANES]

    nacc = 4

    def per_token(t, carry):
        zero = jnp.zeros((SUBLANES, LANES), F32)
        acc_hi = [zero] * nacc
        acc_lo = [zero] * nacc
        base = t * npair
        for k in range(npair):
            off = pl.multiple_of(toff_ref[base + k], SUBLANES)
            cw = jnp.broadcast_to(bw_ref[t, k:k + 1, :], (SUBLANES, LANES))
            prod = _packed_mul(tab_ref[pl.ds(off, SUBLANES), :], cw)
            acc_hi[k % nacc] = acc_hi[k % nacc] + _hi_f32(prod)
            acc_lo[k % nacc] = acc_lo[k % nacc] + _lo_f32(prod)
        y_ref[t] = ((acc_hi[0] + acc_hi[1]) + (acc_hi[2] + acc_hi[3])) + ((acc_lo[0] + acc_lo[1]) + (acc_lo[2] + acc_lo[3]))
        return carry

    lax.fori_loop(0, tn, per_token, 0)


def _peer_experts(eid, gate, toff, h2, tab_u, tab_v, tn):
    npair, n = eid.shape
    d = h2.shape[1]
    x3 = h2.reshape(n, d // LANES, LANES)
    tab_spec = pl.BlockSpec(tab_u.shape, lambda i: (0, 0), pipeline_mode=pl.Buffered(1))
    smem_blk = lambda tt: pl.BlockSpec((tt * npair,), lambda i: (i,), memory_space=pltpu.SMEM)
    slot_blk = lambda tt: pl.BlockSpec((npair, tt), lambda i: (0, i))
    sub_t = PEER_U_SUB if tn % PEER_U_SUB == 0 else tn
    ce, co = pl.pallas_call(
        functools.partial(_peer_u_kernel, tn=tn, sub_t=sub_t),
        grid=(n // tn,),
        in_specs=[smem_blk(tn), pl.BlockSpec((tn, d // LANES, LANES), lambda i: (i, 0, 0)), slot_blk(tn), slot_blk(tn), tab_spec],
        out_specs=[slot_blk(tn), slot_blk(tn)],
        out_shape=[jax.ShapeDtypeStruct((npair, n), F32)] * 2,
        scratch_shapes=[pltpu.VMEM((sub_t, npair, LANES), F32), pltpu.VMEM((sub_t, npair, LANES), F32)],
        compiler_params=_cparams(("arbitrary",)),
        name="peer_u",
    )(toff.reshape(n * npair), x3, eid, gate, tab_u)
    tv = tn
    rep = jnp.asarray(np.arange(tv)[:, None] == np.arange(tv * LANES)[None, :] // LANES, BF16)
    y3 = pl.pallas_call(
        functools.partial(_peer_v_kernel, tn=tv, npair=npair),
        grid=(n // tv,),
        in_specs=[smem_blk(tv), slot_blk(tv), slot_blk(tv),
                  pl.BlockSpec(rep.shape, lambda i: (0, 0), pipeline_mode=pl.Buffered(1)), tab_spec],
        out_specs=pl.BlockSpec((tv, d // LANES, LANES), lambda i: (i, 0, 0)),
        out_shape=jax.ShapeDtypeStruct((n, d // LANES, LANES), F32),
        scratch_shapes=[pltpu.VMEM((tv, npair, LANES), I32)],
        compiler_params=_cparams(("arbitrary",)),
        name="peer_v",
    )(toff.reshape(n * npair), ce, co, rep, tab_v)
    return y3.reshape(n, d)


def _ple_kernel(x1_ref, yp_ref, p_ref, gple_ref, wg_ref, wp_ref, o_ref):
    x2 = x1_ref[...] + yp_ref[...]
    hn = x2 * lax.rsqrt(jnp.mean(x2 * x2, axis=-1, keepdims=True) + EPS) * gple_ref[...]
    gate = jax.nn.sigmoid(_dot(hn.astype(BF16), wg_ref[...]))
    o_ref[...] = x2 + gate * _dot(p_ref[...].astype(BF16), wp_ref[...])


def _ple(x1, yp, p2d, g_ple, w_ple_gate, w_ple, tm):
    n = x1.shape[0]
    row = lambda i: (i, 0)
    fixed = lambda i: (0, 0)
    return pl.pallas_call(
        _ple_kernel,
        grid=(n // tm,),
        in_specs=[pl.BlockSpec((tm, D_MODEL), row), pl.BlockSpec((tm, D_MODEL), row), pl.BlockSpec((tm, PLE_DIM), row),
                  pl.BlockSpec((1, D_MODEL), fixed), pl.BlockSpec((D_MODEL, D_MODEL), fixed),
                  pl.BlockSpec((PLE_DIM, D_MODEL), fixed)],
        out_specs=pl.BlockSpec((tm, D_MODEL), row),
        out_shape=jax.ShapeDtypeStruct((n, D_MODEL), F32),
        compiler_params=_cparams(("parallel",)),
        name="ple_out",
    )(x1, yp, p2d, g_ple.reshape(1, D_MODEL).astype(F32), w_ple_gate.astype(BF16), w_ple.astype(BF16))


PAGES_PER_STEP = 8


def _sample_scores_kernel(pt_ref, qh_ref, wrep_ref, kinew_ref, *refs, npg):
    pages = refs[:npg]
    sc_ref, scnew_ref = refs[npg:]
    j = pl.program_id(1)
    qh = qh_ref[0]
    wrep = wrep_ref[0]

    def score(keys_t):
        s = jnp.maximum(_dot(qh, keys_t.astype(BF16)), 0.0) * wrep
        tot = s[0:SUBLANES]
        for h in range(1, IDX_HEADS):
            tot = tot + s[h * SUBLANES:(h + 1) * SUBLANES]
        return tot

    for p in range(npg):
        sc_ref[0, :, p * PAGE_SIZE:(p + 1) * PAGE_SIZE] = score(pages[p][0])

    @pl.when(j == 0)
    def _():
        sn = score(kinew_ref[0])
        causal = _lane_iota((SUBLANES, PAGE_SIZE)) <= _row_iota((SUBLANES, PAGE_SIZE))
        scnew_ref[0] = jnp.where(causal, sn, -jnp.inf)


def _sample_thr_kernel(sc_ref, thr_ref, jcut_ref, keys_ref, *, k_sel, tq, tk, t):
    nkb = keys_ref.shape[0]
    for kb in range(nkb):
        keys_ref[kb] = _sort_key(sc_ref[:, :, kb * tk:(kb + 1) * tk].reshape(tq, tk))
    _select_rows(keys_ref, thr_ref, jcut_ref, nkb, k_sel, tq, tk, min(64, tq), live_rows=t)


def _sample_attn_kernel(pt_ref, q_ref, sc_ref, scnew_ref, thr_ref, jcut_ref, knew_ref, vnew_ref, *refs, npg, past):
    kpages = refs[:npg]
    vpages = refs[npg:2 * npg]
    o_ref, m_ref, l_ref, acc_ref = refs[2 * npg:]
    j = pl.program_id(1)

    @pl.when(j == 0)
    def _():
        m_ref[...] = jnp.full(m_ref.shape, NEG_BIG, F32)
        l_ref[...] = jnp.zeros(l_ref.shape, F32)
        acc_ref[...] = jnp.zeros(acc_ref.shape, F32)

    q = q_ref[0]
    thr = thr_ref[...]
    jcut = jcut_ref[...]
    lane = _lane_iota((SUBLANES, PAGE_SIZE))

    grp = A_REP * SUBLANES

    def pages(kts, vts, scs, pos0s, causal):
        biases = []
        for sc8, pos0 in zip(scs, pos0s):
            key = _sort_key(sc8)
            pos = lane + pos0
            sel = jnp.where(key > thr, 0.0, jnp.where(key == thr, jnp.where(pos <= jcut, 0.0, NEG_BIG), NEG_BIG))
            if causal is not None:
                sel = jnp.where(causal, sel, NEG_BIG)
            biases.append(sel)
        bias = jnp.concatenate(biases, axis=1)
        bias = jnp.concatenate([bias] * A_HEADS, axis=0)
        s = jnp.concatenate(
            [jnp.concatenate([_dot(q[g * grp:(g + 1) * grp], kt[g].astype(BF16)) for g in range(A_KV_HEADS)], axis=0)
             for kt in kts], axis=1) + bias
        m_old = m_ref[...]
        m_new = jnp.maximum(m_old, jnp.max(s, axis=1, keepdims=True))
        alpha = jnp.exp(m_old - m_new)
        p = jnp.exp(s - m_new[:, 0:1])
        l_ref[...] = alpha * l_ref[...] + jnp.sum(p, axis=1, keepdims=True)
        p = p.astype(BF16)
        pv = None
        for i, vt in enumerate(vts):
            pi = p[:, i * PAGE_SIZE:(i + 1) * PAGE_SIZE]
            part = jnp.concatenate([_dot_nt(pi[g * grp:(g + 1) * grp], vt[g].astype(BF16)) for g in range(A_KV_HEADS)], axis=0)
            pv = part if pv is None else pv + part
        acc_ref[...] = alpha[:, 0:A_HEAD_DIM] * acc_ref[...] + pv
        m_ref[...] = m_new

    pages([r.at[0] for r in kpages], [r.at[0] for r in vpages],
          [sc_ref[0, :, p * PAGE_SIZE:(p + 1) * PAGE_SIZE] for p in range(npg)],
          [(j * npg + p) * PAGE_SIZE for p in range(npg)], None)

    @pl.when(j == pl.num_programs(1) - 1)
    def _():
        pages([knew_ref.at[0]], [vnew_ref.at[0]], [scnew_ref[0]], [past], lane <= _row_iota((SUBLANES, PAGE_SIZE)))
        o_ref[0] = acc_ref[...] / l_ref[:, 0:A_HEAD_DIM]


def _head_major(a, bd, t, nh, hd):
    a = a.reshape(bd, t, nh, hd).transpose(0, 2, 1, 3)
    a = jnp.pad(a, ((0, 0), (0, 0), (0, SUBLANES - t), (0, 0)))
    return a.reshape(bd, nh * SUBLANES, hd)


def _attn_sample(q, k, v, qi, misc, cache_k, cache_v, cache_idx_k, page_table, bd, t):
    assert t <= SUBLANES
    n_pages = page_table.shape[1]
    past = n_pages * PAGE_SIZE
    npg = PAGES_PER_STEP if n_pages % PAGES_PER_STEP == 0 else 1
    nj = n_pages // npg
    k_sel = min(TOPK_MAX, (past + t) // 4)
    n_pool = cache_k.shape[0]
    rows = A_HEADS * SUBLANES

    qh = _head_major(qi, bd, t, IDX_HEADS, IDX_DIM)
    wi = misc[:, MISC_WI:MISC_WI + IDX_HEADS].reshape(bd, t, IDX_HEADS).transpose(0, 2, 1)
    wrep = jnp.broadcast_to(jnp.pad(wi, ((0, 0), (0, 0), (0, SUBLANES - t))).reshape(bd, rows, 1), (bd, rows, LANES))
    pad_page = lambda a, w: jnp.pad(a.reshape(bd, t, w), ((0, 0), (0, PAGE_SIZE - t), (0, 0)))
    kinew = pad_page(misc[:, :IDX_DIM], IDX_DIM).transpose(0, 2, 1)
    cik = cache_idx_k.transpose(0, 2, 1)

    per_b = lambda b, j, pt: (b, 0, 0)
    per_b4 = lambda b, j, pt: (b, 0, 0, 0)
    page_spec = lambda shape, p: pl.BlockSpec(shape, lambda b, j, pt: (pt[b, j * npg + p],) + (0,) * (len(shape) - 1))
    sc, scnew = pl.pallas_call(
        functools.partial(_sample_scores_kernel, npg=npg),
        grid_spec=pltpu.PrefetchScalarGridSpec(
            num_scalar_prefetch=1, grid=(bd, nj),
            in_specs=[pl.BlockSpec((1, rows, IDX_DIM), per_b), pl.BlockSpec((1, rows, LANES), per_b),
                      pl.BlockSpec((1, IDX_DIM, PAGE_SIZE), per_b)]
                     + [page_spec((1, IDX_DIM, PAGE_SIZE), p) for p in range(npg)],
            out_specs=[pl.BlockSpec((1, SUBLANES, npg * PAGE_SIZE), lambda b, j, pt: (b, 0, j)),
                       pl.BlockSpec((1, SUBLANES, PAGE_SIZE), per_b)]),
        out_shape=[jax.ShapeDtypeStruct((bd, SUBLANES, past), F32), jax.ShapeDtypeStruct((bd, SUBLANES, PAGE_SIZE), F32)],
        compiler_params=_cparams(("parallel", "arbitrary")),
        name="sample_scores",
    )(page_table, qh, wrep, kinew, *([cik] * npg))

    length = past + PAGE_SIZE
    sc_all = jnp.concatenate([sc, scnew], axis=2)
    tb = SUBLANES if bd % SUBLANES == 0 else 1
    tq = tb * SUBLANES
    tk = 5 * LANES if length % (5 * LANES) == 0 else LANES
    thr, jcut = pl.pallas_call(
        functools.partial(_sample_thr_kernel, k_sel=k_sel, tq=tq, tk=tk, t=t),
        grid=(bd // tb,),
        in_specs=[pl.BlockSpec((tb, SUBLANES, length), lambda i: (i, 0, 0))],
        out_specs=[pl.BlockSpec((tq, LANES), lambda i: (i, 0))] * 2,
        out_shape=[jax.ShapeDtypeStruct((bd * SUBLANES, LANES), I32)] * 2,
        scratch_shapes=[pltpu.VMEM((length // tk, tq, tk), I32)],
        compiler_params=_cparams(("parallel",)),
        name="sample_thr",
    )(sc_all)

    qa = _head_major(q, bd, t, A_HEADS, A_HEAD_DIM)
    to_page_t = lambda a: pad_page(a, A_KV).reshape(bd, PAGE_SIZE, A_KV_HEADS, A_HEAD_DIM).transpose(0, 2, 3, 1)
    knew, vnew = to_page_t(k), to_page_t(v)
    ck = cache_k.transpose(0, 2, 3, 1)
    cv = cache_v.transpose(0, 2, 3, 1)
    kv_blk = (1, A_KV_HEADS, A_HEAD_DIM, PAGE_SIZE)
    o = pl.pallas_call(
        functools.partial(_sample_attn_kernel, npg=npg, past=past),
        grid_spec=pltpu.PrefetchScalarGridSpec(
            num_scalar_prefetch=1, grid=(bd, nj),
            in_specs=[pl.BlockSpec((1, rows, A_HEAD_DIM), per_b),
                      pl.BlockSpec((1, SUBLANES, npg * PAGE_SIZE), lambda b, j, pt: (b, 0, j)),
                      pl.BlockSpec((1, SUBLANES, PAGE_SIZE), per_b),
                      pl.BlockSpec((SUBLANES, LANES), lambda b, j, pt: (b, 0)),
                      pl.BlockSpec((SUBLANES, LANES), lambda b, j, pt: (b, 0)),
                      pl.BlockSpec(kv_blk, per_b4), pl.BlockSpec(kv_blk, per_b4)]
                     + [page_spec(kv_blk, p) for p in range(npg)] * 2,
            out_specs=pl.BlockSpec((1, rows, A_HEAD_DIM), per_b),
            scratch_shapes=[pltpu.VMEM((rows, LANES), F32), pltpu.VMEM((rows, LANES), F32), pltpu.VMEM((rows, A_HEAD_DIM), F32)]),
        out_shape=jax.ShapeDtypeStruct((bd, rows, A_HEAD_DIM), F32),
        compiler_params=_cparams(("parallel", "arbitrary")),
        name="sample_attn",
    )(page_table, qa, sc, scnew, thr, jcut, knew, vnew, *([ck] * npg), *([cv] * npg))
    o = o.reshape(bd, A_HEADS, SUBLANES, A_HEAD_DIM)[:, :, :t].transpose(0, 2, 1, 3)
    return o.reshape(bd * t, A_Q)


def _tile(n, pref):
    return pref if n % pref == 0 else n


def _layer(x, p_emb, conv_state, delta_state, attn_fn, prm):
    b, t, _ = x.shape
    n = b * t
    x2d = x.reshape(n, D_MODEL)
    tm = _tile(n, 256)
    (q, k, kdup, v, vb, qi, misc, kidup, dqkv, dz, ga, gb) = _proj(
        x2d, prm['g_mix'], prm['w_perm'], prm['g_q'], prm['g_k'], prm['g_idx_k'], tm)

    oa = attn_fn(q, k, kdup, v, vb, qi, misc, kidup)

    tp = -(-t // SUBLANES) * SUBLANES
    c = min(DN_CHUNK, tp)
    pad_t = lambda a: jnp.pad(a.reshape(b, t, a.shape[-1]), ((0, 0), (0, tp - t), (0, 0)))
    dqkv3 = dqkv.reshape(b, t, DN_CONV_CH)
    halo = jnp.zeros((b, SUBLANES, DN_CONV_CH), F32)
    if conv_state is not None:
        halo = halo.at[:, SUBLANES - (CONV_W - 1):].set(conv_state.astype(F32))
        hist = jnp.concatenate([conv_state.astype(F32), dqkv3], axis=1)
    else:
        hist = jnp.concatenate([jnp.zeros((b, CONV_W - 1, DN_CONV_CH), F32), dqkv3], axis=1)
    new_conv = hist[:, -(CONV_W - 1):]
    qn, kn, vv = _conv(pad_t(dqkv), halo, prm['conv_w'], _tile(tp, 512))
    s0 = jnp.zeros((b, DN_HEADS, DN_DK, DN_DV), F32) if delta_state is None else delta_state.astype(F32)
    ob, new_delta = _delta(qn, kn, vv, pad_t(dz), pad_t(misc), prm['a_log'], prm['dt_bias'], prm['g_dn_out'], s0, c, t)
    ob = ob[:, :t].reshape(n, DN_V)

    x1, h2, qt = _merge(x2d, oa, ob, ga, gb, prm['w_oa'], prm['w_ob'], prm['w_out'], prm['g_ffn'], prm['w_pq'], tm)
    eid, gate, toff = _route(qt, prm['sub_keys'], _tile(n, LANES))
    npair = PEER_TOPK * PEER_HEADS
    yp = _peer_experts(eid.reshape(npair, n), gate.reshape(npair, n), toff, h2, prm['tab_u'], prm['tab_v'], _tile(n, LANES))
    y = _ple(x1, yp, p_emb.reshape(n, PLE_DIM), prm['g_ple'], prm['w_ple_gate'], prm['w_ple'], tm)

    return (y.reshape(b, t, D_MODEL), k.reshape(b, t, A_KV_HEADS, A_HEAD_DIM), v.reshape(b, t, A_KV_HEADS, A_HEAD_DIM),
            misc[:, :IDX_DIM].reshape(b, t, IDX_DIM), new_conv, new_delta)


def kernel(x_prompt, x_sample, cache_k, cache_v, cache_idx_k, state_conv, state_delta, page_table, p_prompt, p_sample,
           g_mix, w_in, g_q, g_k, g_idx_k, conv_w, a_log, dt_bias, g_dn_out, w_oa, w_ob, w_out, g_ffn, w_pq, sub_keys,
           peer_u, peer_v, g_ple, w_ple_gate, w_ple):
    depth = w_in.shape[0]
    xp, xs = x_prompt, x_sample
    outs = [[] for _ in range(10)]
    for i in range(depth):
        prm = dict(g_mix=g_mix[i], w_perm=_permute_w_in(w_in[i]), g_q=g_q[i], g_k=g_k[i], g_idx_k=g_idx_k[i],
                   conv_w=conv_w[i].astype(F32), a_log=a_log[i], dt_bias=dt_bias[i], g_dn_out=g_dn_out[i],
                   w_oa=w_oa[i], w_ob=w_ob[i], w_out=w_out[i], g_ffn=g_ffn[i], w_pq=w_pq[i], sub_keys=sub_keys[i],
                   tab_u=_pack_table(peer_u[i]), tab_v=_pack_table(peer_v[i]), g_ple=g_ple[i],
                   w_ple_gate=w_ple_gate[i], w_ple=w_ple[i])

        bp, s, _ = xp.shape

        def attn_prompt(q, k, kdup, v, vb, qi, misc, kidup):
            k_sel = min(TOPK_MAX, s // 4)
            tq, tk = _tile(s, 256), _tile(s, 512)
            r3 = lambda a: a.reshape(bp, s, a.shape[-1])
            mask = _idx_mask(r3(qi), r3(misc), r3(kidup), k_sel, tq, tk)
            return _attn(r3(q), r3(kdup), r3(vb), mask, tq, tk).reshape(bp * s, A_Q)

        res = _layer(xp, p_prompt[i], None, None, attn_prompt, prm)
        xp = res[0]
        for lst, val in zip(outs[:5], res[1:]):
            lst.append(val)

        bd, t, _ = xs.shape

        def attn_sample(q, k, kdup, v, vb, qi, misc, kidup):
            return _attn_sample(q, k, v, qi, misc, cache_k[i], cache_v[i], cache_idx_k[i], page_table, bd, t)

        res = _layer(xs, p_sample[i], state_conv[i], state_delta[i], attn_sample, prm)
        xs = res[0]
        for lst, val in zip(outs[5:], res[1:]):
            lst.append(val)

    cast = [cache_k.dtype, cache_v.dtype, cache_idx_k.dtype, state_conv.dtype, state_delta.dtype] * 2
    stacked = [jnp.stack(lst).astype(dt) for lst, dt in zip(outs, cast)]
    return (xp, xs, *stacked)
```

```python
import functools
import math

import jax
import jax.numpy as jnp
import numpy as np
from jax import lax
from jax.experimental import pallas as pl
from jax.experimental.pallas import tpu as pltpu

F32 = jnp.float32
BF16 = jnp.bfloat16
I32 = jnp.int32

D_MODEL = 1024
PAGE_SIZE = 128
A_HEADS = 8
A_KV_HEADS = 4
A_REP = A_HEADS // A_KV_HEADS
A_HEAD_DIM = 64
A_Q = A_HEADS * A_HEAD_DIM
A_KV = A_KV_HEADS * A_HEAD_DIM
IDX_HEADS = 8
IDX_DIM = 64
TOPK_MAX = 256
DN_HEADS = 4
DN_DK = 128
DN_DV = 128
DN_QK = DN_HEADS * DN_DK
DN_V = DN_HEADS * DN_DV
DN_CONV_CH = 2 * DN_QK + DN_V
CONV_W = 4
DN_CHUNK = 64
PEER_HEADS = 8
PEER_KEYS = 128
PEER_DKEY = 256
PEER_TOPK = 16
PLE_DIM = 256
EPS = 1e-6
IN_SPLITS = (A_Q, A_KV, A_KV, IDX_HEADS * IDX_DIM, IDX_DIM, IDX_HEADS, DN_CONV_CH, DN_V, DN_HEADS, DN_HEADS, D_MODEL, D_MODEL)

LANES = 128
SUBLANES = 8
VMEM_LIMIT = 56 * 1024 * 1024

NEG_BIG = -1e30
INT_MIN = -(2 ** 31)

MISC_WI = IDX_DIM
MISC_DB = MISC_WI + IDX_HEADS
MISC_DA = MISC_DB + DN_HEADS

SEG_Q = 0
SEG_K = SEG_Q + A_Q
SEG_V = SEG_K + A_KV
SEG_QI = SEG_V + A_KV
SEG_MISC = SEG_QI + IDX_HEADS * IDX_DIM
SEG_DQKV = SEG_MISC + LANES
SEG_DZ = SEG_DQKV + DN_CONV_CH
SEG_GA = SEG_DZ + DN_V
SEG_GB = SEG_GA + D_MODEL
SEG_END = SEG_GB + D_MODEL


def _cparams(sem):
    return pltpu.CompilerParams(dimension_semantics=sem, vmem_limit_bytes=VMEM_LIMIT)


def _dot(a, b):
    return jnp.dot(a, b, preferred_element_type=F32)


def _dot_nt(a, b):
    return lax.dot_general(a, b, (((1,), (1,)), ((), ())), preferred_element_type=F32)


def _split(a):
    hi = a.astype(BF16)
    lo = (a - hi.astype(F32)).astype(BF16)
    return hi, lo


def _dot3(a, b, nt=False):
    d = _dot_nt if nt else _dot
    ah, al = _split(a)
    bh, bl = _split(b)
    return d(ah, bh) + (d(ah, bl) + d(al, bh))


def _lane_iota(shape):
    return lax.broadcasted_iota(I32, shape, len(shape) - 1)


def _row_iota(shape):
    return lax.broadcasted_iota(I32, shape, len(shape) - 2)


def _half_norm(blk, gain):
    lane = _lane_iota(blk.shape)
    lo = lane < A_HEAD_DIM
    sq = blk * blk
    s_lo = jnp.sum(jnp.where(lo, sq, 0.0), axis=-1, keepdims=True)
    s_hi = jnp.sum(jnp.where(lo, 0.0, sq), axis=-1, keepdims=True)
    r_lo = lax.rsqrt(s_lo * (1.0 / A_HEAD_DIM) + EPS)
    r_hi = lax.rsqrt(s_hi * (1.0 / A_HEAD_DIM) + EPS)
    return blk * jnp.where(lo, r_lo, r_hi) * gain


def _proj_kernel(x_ref, gmix_ref, w_ref, gq_ref, gk_ref, gik_ref,
                 q_ref, k_ref, kdup_ref, v_ref, vb_ref, qi_ref, misc_ref, kidup_ref,
                 dqkv_ref, dz_ref, ga_ref, gb_ref):
    x = x_ref[...]
    h = x * lax.rsqrt(jnp.mean(x * x, axis=-1, keepdims=True) + EPS) * gmix_ref[...]
    hb = h.astype(BF16)

    def seg(a, b):
        return _dot(hb, w_ref[:, a:b])

    lane = _lane_iota((x.shape[0], LANES))
    lo = lane < A_HEAD_DIM

    zq = seg(SEG_Q, SEG_K)
    for c in range(A_Q // LANES):
        blk = _half_norm(zq[:, c * LANES:(c + 1) * LANES], gq_ref[...])
        q_ref[:, c * LANES:(c + 1) * LANES] = (blk * (A_HEAD_DIM ** -0.5)).astype(BF16)

    zk = seg(SEG_K, SEG_V)
    for c in range(A_KV // LANES):
        blk = _half_norm(zk[:, c * LANES:(c + 1) * LANES], gk_ref[...])
        k_ref[:, c * LANES:(c + 1) * LANES] = blk
        rolled = pltpu.roll(blk, A_HEAD_DIM, axis=1)
        kdup_ref[:, (2 * c) * LANES:(2 * c + 1) * LANES] = jnp.where(lo, blk, rolled).astype(BF16)
        kdup_ref[:, (2 * c + 1) * LANES:(2 * c + 2) * LANES] = jnp.where(lo, rolled, blk).astype(BF16)

    zv = seg(SEG_V, SEG_QI)
    v_ref[...] = zv
    vb_ref[...] = zv.astype(BF16)

    qi_ref[...] = seg(SEG_QI, SEG_MISC).astype(BF16)

    zm = seg(SEG_MISC, SEG_DQKV)
    s_ik = jnp.sum(jnp.where(lo, zm * zm, 0.0), axis=-1, keepdims=True)
    kin = zm * lax.rsqrt(s_ik * (1.0 / IDX_DIM) + EPS) * gik_ref[...]
    wscale = (IDX_HEADS * IDX_DIM) ** -0.5
    misc_ref[...] = jnp.where(lo, kin, jnp.where(lane < MISC_DB, zm * wscale, zm))
    kin0 = jnp.where(lo, kin, 0.0)
    kidup_ref[...] = (kin0 + pltpu.roll(kin0, IDX_DIM, axis=1)).astype(BF16)

    dqkv_ref[...] = seg(SEG_DQKV, SEG_DZ)
    dz_ref[...] = seg(SEG_DZ, SEG_GA)
    ga_ref[...] = seg(SEG_GA, SEG_GB)
    gb_ref[...] = seg(SEG_GB, SEG_END)


def _permute_w_in(w_in):
    cuts = np.cumsum((0,) + IN_SPLITS)
    parts = [w_in[:, cuts[i]:cuts[i + 1]] for i in range(len(IN_SPLITS))]
    aq, ak, av, iq, ik, iw, dqkv, dz, db, da, ga, gb = parts
    pad = jnp.zeros((w_in.shape[0], LANES - IDX_DIM - IDX_HEADS - 2 * DN_HEADS), w_in.dtype)
    return jnp.concatenate([aq, ak, av, iq, ik, iw, db, da, pad, dqkv, dz, ga, gb], axis=1).astype(BF16)


def _tile2(g):
    return jnp.concatenate([g, g]).reshape(1, LANES).astype(F32)


def _proj(x2d, g_mix, w_perm, g_q, g_k, g_idx_k, tm):
    n = x2d.shape[0]
    assert n % tm == 0
    widths = [(A_Q, BF16), (A_KV, F32), (2 * A_KV, BF16), (A_KV, F32), (A_KV, BF16), (IDX_HEADS * IDX_DIM, BF16),
              (LANES, F32), (LANES, BF16), (DN_CONV_CH, F32), (DN_V, F32), (D_MODEL, F32), (D_MODEL, F32)]
    row = lambda i: (i, 0)
    fixed = lambda i: (0, 0)
    return pl.pallas_call(
        _proj_kernel,
        grid=(n // tm,),
        in_specs=[pl.BlockSpec((tm, D_MODEL), row), pl.BlockSpec((1, D_MODEL), fixed),
                  pl.BlockSpec((D_MODEL, SEG_END), fixed), pl.BlockSpec((1, LANES), fixed),
                  pl.BlockSpec((1, LANES), fixed), pl.BlockSpec((1, LANES), fixed)],
        out_specs=[pl.BlockSpec((tm, w), row) for w, _ in widths],
        out_shape=[jax.ShapeDtypeStruct((n, w), dt) for w, dt in widths],
        compiler_params=_cparams(("parallel",)),
        name="proj_in",
    )(x2d, g_mix.reshape(1, D_MODEL), w_perm, _tile2(g_q), _tile2(g_k), _tile2(g_idx_k))


def _sort_key(score):
    score = jnp.where(score == 0.0, 0.0, score)
    bits = pltpu.bitcast(score, I32)
    return jnp.where(bits < 0, bits ^ jnp.int32(0x7FFFFFFF), bits)


def _index_scores(qi, wi_tile, kblk):
    lane = _lane_iota((qi.shape[0], LANES))
    lo = lane < IDX_DIM
    sc = None
    for c in range(IDX_HEADS // 2):
        q128 = qi[:, c * LANES:(c + 1) * LANES]
        zero = jnp.zeros_like(q128)
        for half in range(2):
            qm = jnp.where(lo, q128, zero) if half == 0 else jnp.where(lo, zero, q128)
            s = _dot_nt(qm, kblk)
            hidx = MISC_WI + 2 * c + half
            term = jnp.maximum(s, 0.0) * wi_tile[:, hidx:hidx + 1]
            sc = term if sc is None else sc + term
    return sc


def _select_rows(keys_ref, thr_ref, jcut_ref, cand_ref, nvalid, k_sel, tq, tk, rb, live_rows=None):
    nchunk = tk // LANES
    nrb = tq // rb
    nbits_idx = int(math.ceil(math.log2(keys_ref.shape[0] * tk))) + 1
    lane = _lane_iota((rb, LANES))

    def count(pred):
        def body(kb, cnts):
            out = []
            for r in range(nrb):
                blk = keys_ref[kb, r * rb:(r + 1) * rb, :]
                cand_b = cand_ref[r * rb:(r + 1) * rb, :]
                cnt = cnts[r]
                for c in range(nchunk):
                    cnt = cnt + pred(blk[:, c * LANES:(c + 1) * LANES], cand_b, kb * tk + c * LANES, r)
                out.append(cnt)
            return tuple(out)

        zeros = tuple(jnp.zeros((rb, LANES), F32) for _ in range(nrb))
        cnts = lax.fori_loop(0, nvalid, body, zeros)
        return jnp.concatenate([jnp.sum(cnt, axis=1, keepdims=True) for cnt in cnts], axis=0)

    def bit_step(i, cur):
        cand = cur + lax.shift_left(jnp.int32(1), 31 - i)
        cand_ref[...] = jnp.broadcast_to(cand, (tq, LANES))
        c_ge = count(lambda kv, cb, base, r: jnp.where(kv >= cb, 1.0, 0.0))
        return jnp.where(c_ge >= k_sel, cand, cur)

    thr = lax.fori_loop(0, 32, bit_step, jnp.full((tq, 1), INT_MIN, I32))
    thr_ref[...] = jnp.broadcast_to(thr, (tq, LANES))
    cand_ref[...] = thr_ref[...]
    c_gt = count(lambda kv, cb, base, r: jnp.where(kv > cb, 1.0, 0.0))
    c_ge = count(lambda kv, cb, base, r: jnp.where(kv >= cb, 1.0, 0.0))
    need = k_sel - c_gt
    jcut_ref[...] = jnp.full((tq, LANES), 2 ** 30, I32)

    if live_rows is not None:
        c_ge = jnp.where(_row_iota((tq, 1)) % SUBLANES < live_rows, c_ge, 0.0)

    @pl.when(jnp.max(c_ge) > k_sel)
    def _():
        def jstep(i, jcur):
            candj = jcur + lax.shift_left(jnp.int32(1), nbits_idx - 1 - i)
            cand_ref[...] = jnp.broadcast_to(candj, (tq, LANES))
            f = count(lambda kv, cb, base, r: jnp.where(kv == thr_ref[r * rb:(r + 1) * rb, :],
                                                        jnp.where(lane + base < cb, 1.0, 0.0), 0.0))
            return jnp.where(f < need, candj, jcur)

        jcut = lax.fori_loop(0, nbits_idx, jstep, jnp.zeros((tq, 1), I32))
        jcut_ref[...] = jnp.broadcast_to(jcut, (tq, LANES))


def _idx_mask_kernel(qi_ref, wi_ref, kidup_ref, mask_ref, keys_ref, thr_ref, jcut_ref, cand_ref, *, k_sel, tq, tk, rb):
    i = pl.program_id(1)
    nkb = keys_ref.shape[0]
    nvalid = ((i + 1) * tq + tk - 1) // tk
    qi = qi_ref[0]
    wi_tile = wi_ref[0]
    q_pos = i * tq + _row_iota((tq, tk))
    lane = _lane_iota((tq, tk))

    def fill(kb, carry):
        kblk = kidup_ref[0, pl.ds(pl.multiple_of(kb * tk, tk), tk), :]
        sc = _index_scores(qi, wi_tile, kblk)
        sc = jnp.where(lane + kb * tk <= q_pos, sc, -jnp.inf)
        keys_ref[kb] = _sort_key(sc)
        return carry

    lax.fori_loop(0, nvalid, fill, 0)
    _select_rows(keys_ref, thr_ref, jcut_ref, cand_ref, nvalid, k_sel, tq, tk, rb)

    for kb in range(nkb):
        @pl.when(kb < nvalid)
        def _():
            thr = jnp.broadcast_to(thr_ref[:, 0:1], (tq, tk))
            jcut = jnp.broadcast_to(jcut_ref[:, 0:1], (tq, tk))
            key = keys_ref[kb]
            pos = lane + kb * tk
            sel = jnp.where(key > thr, 1, jnp.where(key == thr, jnp.where(pos <= jcut, 1, 0), 0))
            sel = jnp.where(pos <= q_pos, sel, 0)
            mask_ref[0, :, kb * tk:(kb + 1) * tk] = sel.astype(jnp.int8)

        @pl.when(kb >= nvalid)
        def _():
            mask_ref[0, :, kb * tk:(kb + 1) * tk] = jnp.zeros((tq, tk), jnp.int8)


def _idx_mask(qi, misc, kidup, k_sel, tq, tk):
    b, s, _ = qi.shape
    assert s % tq == 0 and s % tk == 0 and tq % 32 == 0
    rb = min(64, tq)
    kern = functools.partial(_idx_mask_kernel, k_sel=k_sel, tq=tq, tk=tk, rb=rb)
    return pl.pallas_call(
        kern,
        grid=(b, s // tq),
        in_specs=[pl.BlockSpec((1, tq, IDX_HEADS * IDX_DIM), lambda bi, i: (bi, i, 0)),
                  pl.BlockSpec((1, tq, LANES), lambda bi, i: (bi, i, 0)),
                  pl.BlockSpec((1, s, LANES), lambda bi, i: (bi, 0, 0))],
        out_specs=pl.BlockSpec((1, tq, s), lambda bi, i: (bi, i, 0)),
        out_shape=jax.ShapeDtypeStruct((b, s, s), jnp.int8),
        scratch_shapes=[pltpu.VMEM((s // tk, tq, tk), I32)] + [pltpu.VMEM((tq, LANES), I32)] * 3,
        compiler_params=_cparams(("parallel", "parallel")),
        name="idx_mask",
    )(qi, misc, kidup)


def _attn_kernel(q_ref, kdup_ref, vb_ref, mask_ref, o_ref, m_ref, l_ref, acc_ref, *, tq, tk):
    i = pl.program_id(1)
    kb = pl.program_id(2)
    nk = pl.num_programs(2)
    last_needed = ((i + 1) * tq - 1) // tk

    @pl.when(kb == 0)
    def _():
        m_ref[...] = jnp.full(m_ref.shape, NEG_BIG, F32)
        l_ref[...] = jnp.zeros(l_ref.shape, F32)
        acc_ref[...] = jnp.zeros(acc_ref.shape, F32)

    @pl.when(kb <= last_needed)
    def _():
        bias = jnp.where(mask_ref[0].astype(I32) != 0, 0.0, NEG_BIG).astype(F32)
        bias2 = jnp.concatenate([bias, bias], axis=0)
        lane = _lane_iota((tq, LANES))
        lo = lane < A_HEAD_DIM
        for g in range(A_KV_HEADS):
            q128 = q_ref[0, :, g * LANES:(g + 1) * LANES]
            zero = jnp.zeros_like(q128)
            q2 = jnp.concatenate([jnp.where(lo, q128, zero), jnp.where(lo, zero, q128)], axis=0)
            s = _dot_nt(q2, kdup_ref[0, :, g * LANES:(g + 1) * LANES]) + bias2
            m_old = m_ref[g]
            m_new = jnp.maximum(m_old, jnp.max(s, axis=1, keepdims=True))
            alpha = jnp.exp(m_old - m_new)
            p = jnp.exp(s - m_new[:, 0:1])
            l_ref[g] = alpha * l_ref[g] + jnp.sum(p, axis=1, keepdims=True)
            v128 = vb_ref[0, :, (g // 2) * LANES:(g // 2 + 1) * LANES]
            acc_ref[g] = alpha * acc_ref[g] + _dot(p.astype(BF16), v128)
            m_ref[g] = m_new

    @pl.when(kb == nk - 1)
    def _():
        lane = _lane_iota((tq, LANES))
        lo = lane < A_HEAD_DIM
        for g in range(A_KV_HEADS):
            a = acc_ref[g] / l_ref[g]
            top, bot = a[:tq], a[tq:]
            if g % 2 == 0:
                o128 = jnp.where(lo, top, pltpu.roll(bot, A_HEAD_DIM, axis=1))
            else:
                o128 = jnp.where(lo, pltpu.roll(top, A_HEAD_DIM, axis=1), bot)
            o_ref[0, :, g * LANES:(g + 1) * LANES] = o128


def _attn(q, kdup, vb, mask, tq, tk):
    b, s, _ = q.shape
    nk = s // tk

    def kv_idx(bi, i, kb):
        return (bi, jnp.minimum(kb, ((i + 1) * tq - 1) // tk), 0)

    def mask_idx(bi, i, kb):
        return (bi, i, jnp.minimum(kb, ((i + 1) * tq - 1) // tk))

    kern = functools.partial(_attn_kernel, tq=tq, tk=tk)
    return pl.pallas_call(
        kern,
        grid=(b, s // tq, nk),
        in_specs=[pl.BlockSpec((1, tq, A_Q), lambda bi, i, kb: (bi, i, 0)),
                  pl.BlockSpec((1, tk, 2 * A_KV), kv_idx),
                  pl.BlockSpec((1, tk, A_KV), kv_idx),
                  pl.BlockSpec((1, tq, tk), mask_idx)],
        out_specs=pl.BlockSpec((1, tq, A_Q), lambda bi, i, kb: (bi, i, 0)),
        out_shape=jax.ShapeDtypeStruct((b, s, A_Q), F32),
        scratch_shapes=[pltpu.VMEM((A_KV_HEADS, 2 * tq, LANES), F32)] * 3,
        compiler_params=_cparams(("parallel", "parallel", "arbitrary")),
        name="sel_attn",
    )(q, kdup, vb, mask)


def _conv_kernel(x_ref, xprev_ref, halo0_ref, w_ref, q_ref, k_ref, v_ref, *, tt):
    i = pl.program_id(1)
    x = x_ref[0]
    halo = jnp.where(i == 0, halo0_ref[0], xprev_ref[0])
    w = w_ref[...]

    def post(y, rows):
        y = y * jax.nn.sigmoid(y)
        for h in range(DN_HEADS):
            for j, ref in enumerate((q_ref, k_ref)):
                blk = y[:, j * DN_QK + h * DN_DK: j * DN_QK + (h + 1) * DN_DK]
                blk = blk * lax.rsqrt(jnp.sum(blk * blk, axis=-1, keepdims=True) + EPS)
                ref[0, 0:rows, h * DN_DK:(h + 1) * DN_DK] = blk
        v_ref[0, 0:rows, :] = y[:, 2 * DN_QK:]

    y = x * w[CONV_W - 1:CONV_W, :]
    for j in range(1, CONV_W):
        y = y + pltpu.roll(x, j, axis=0) * w[CONV_W - 1 - j:CONV_W - j, :]
    post(y, tt)

    x8 = x[0:SUBLANES]
    row = _row_iota(x8.shape)
    y8 = x8 * w[CONV_W - 1:CONV_W, :]
    for j in range(1, CONV_W):
        xs = jnp.where(row < j, pltpu.roll(halo, j, axis=0), pltpu.roll(x8, j, axis=0))
        y8 = y8 + xs * w[CONV_W - 1 - j:CONV_W - j, :]
    post(y8, SUBLANES)


def _conv(dqkv, halo0, conv_w, tt):
    b, t, c = dqkv.shape
    assert t % tt == 0 and tt % SUBLANES == 0
    r = tt // SUBLANES
    kern = functools.partial(_conv_kernel, tt=tt)
    out = jax.ShapeDtypeStruct((b, t, DN_QK), F32)
    return pl.pallas_call(
        kern,
        grid=(b, t // tt),
        in_specs=[pl.BlockSpec((1, tt, c), lambda bi, i: (bi, i, 0)),
                  pl.BlockSpec((1, SUBLANES, c), lambda bi, i: (bi, jnp.maximum(i * r - 1, 0), 0)),
                  pl.BlockSpec((1, SUBLANES, c), lambda bi, i: (bi, 0, 0)),
                  pl.BlockSpec((CONV_W, c), lambda bi, i: (0, 0))],
        out_specs=[pl.BlockSpec((1, tt, DN_QK), lambda bi, i: (bi, i, 0))] * 3,
        out_shape=[out, out, out],
        compiler_params=_cparams(("parallel", "parallel")),
        name="dn_conv",
    )(dqkv, dqkv, halo0, conv_w)


def _softplus(x):
    return jnp.maximum(x, 0.0) + jnp.log(1.0 + jnp.exp(-jnp.abs(x)))


def _pad_rows(a, rows):
    if a.shape[0] == rows:
        return a
    return jnp.concatenate([a, jnp.zeros((rows - a.shape[0], a.shape[1]), a.dtype)], axis=0)


DELTA_SEQS_PER_STEP = 4


def _delta_kernel(q_ref, k_ref, v_ref, z_ref, misc_ref, nega_ref, dtb_ref, gout_ref, s0_ref, o_ref, st_ref, *, c, t_valid, bb):
    n = pl.program_id(1)

    @pl.when(n == 0)
    def _():
        st_ref[...] = s0_ref[...]

    row = _row_iota((c, LANES))
    live = row < t_valid
    ri = lax.broadcasted_iota(I32, (c, c), 0)
    ci = lax.broadcasted_iota(I32, (c, c), 1)
    incl = ri >= ci
    strict = ri > ci
    eye = jnp.where(ri == ci, 1.0, 0.0).astype(F32)
    scale = DN_DK ** -0.5

    chains = [(bi, h) for bi in range(bb) for h in range(DN_HEADS)]
    nch = range(len(chains))
    beta_t, gc_t, gc_tr = [], [], []
    for bi in range(bb):
        mt = misc_ref[bi]
        beta_t.append(jnp.where(live, jax.nn.sigmoid(mt), 0.0))
        g = jnp.where(live, nega_ref[...] * _softplus(mt + dtb_ref[...]), 0.0)
        sft = 1
        while sft < c:
            g = g + jnp.where(row >= sft, pltpu.roll(g, sft, axis=0), 0.0)
            sft *= 2
        gc_t.append(g)
        gc_tr.append(_pad_rows(g, LANES).T)

    sl = lambda h: slice(h * DN_DK, (h + 1) * DN_DK)
    q = [jnp.where(live, q_ref[bi, :, sl(h)] * scale, 0.0) for bi, h in chains]
    k = [jnp.where(live, k_ref[bi, :, sl(h)], 0.0) for bi, h in chains]
    v = [jnp.where(live, v_ref[bi, :, sl(h)], 0.0) for bi, h in chains]
    beta = [beta_t[bi][:, MISC_DB + h:MISC_DB + h + 1] for bi, h in chains]
    gcc = [gc_t[bi][:, MISC_DA + h:MISC_DA + h + 1] for bi, h in chains]
    gcr = [gc_tr[bi][MISC_DA + h:MISC_DA + h + 1, 0:c] for bi, h in chains]
    decay = [jnp.where(incl, jnp.exp(jnp.where(incl, gcc[i] - gcr[i], 0.0)), 0.0) for i in nch]
    kb = [k[i] * beta[i] for i in nch]
    pw = [-jnp.where(strict, _dot3(kb[i], k[i], nt=True) * decay[i], 0.0) for i in nch]
    r = [eye + pw[i] for i in nch]
    nn = 2
    while nn < c:
        pw = [_dot3(pw[i], pw[i]) for i in nch]
        r = [_dot3(r[i], eye + pw[i]) for i in nch]
        nn *= 2
    egc = [jnp.exp(gcc[i]) for i in nch]
    u = [_dot3(r[i], v[i] * beta[i]) for i in nch]
    w = [_dot3(r[i], kb[i] * egc[i]) for i in nch]
    a_qk = [_dot3(q[i], k[i], nt=True) * decay[i] for i in nch]
    s = [st_ref[bi, h] for bi, h in chains]
    v_new = [u[i] - _dot3(w[i], s[i]) for i in nch]
    o = [_dot3(q[i] * egc[i], s[i]) + _dot3(a_qk[i], v_new[i]) for i in nch]
    glast = [gcc[i][c - 1:c, :] for i in nch]
    kw_t = [_pad_rows(k[i] * jnp.exp(glast[i] - gcc[i]), LANES).T for i in nch]
    s_new = [s[i] * jnp.exp(glast[i]) + _dot3(kw_t[i], _pad_rows(v_new[i], LANES)) for i in nch]
    for i, (bi, h) in enumerate(chains):
        st_ref[bi, h] = s_new[i]
        on = o[i] * lax.rsqrt(jnp.mean(o[i] * o[i], axis=-1, keepdims=True) + EPS) * gout_ref[...]
        z = z_ref[bi, :, sl(h)]
        o_ref[bi, :, sl(h)] = on * (z * jax.nn.sigmoid(z))


def _delta(qn, kn, v, dz, misc, a_log, dt_bias, g_dn_out, s0, c, t_valid):
    b, t, _ = qn.shape
    assert t % c == 0 and (t_valid == t or t == c)
    nega = jnp.zeros((1, LANES), F32).at[0, MISC_DA:MISC_DA + DN_HEADS].set(-jnp.exp(a_log.astype(F32)))
    dtb = jnp.zeros((1, LANES), F32).at[0, MISC_DA:MISC_DA + DN_HEADS].set(dt_bias.astype(F32))
    tok = lambda bi, n: (bi, n, 0)
    fixed = lambda bi, n: (0, 0)
    st = lambda bi, n: (bi, 0, 0, 0)
    bb = DELTA_SEQS_PER_STEP if b % DELTA_SEQS_PER_STEP == 0 else 1
    kern = functools.partial(_delta_kernel, c=c, t_valid=t_valid, bb=bb)
    return pl.pallas_call(
        kern,
        grid=(b // bb, t // c),
        in_specs=[pl.BlockSpec((bb, c, DN_QK), tok), pl.BlockSpec((bb, c, DN_QK), tok), pl.BlockSpec((bb, c, DN_V), tok),
                  pl.BlockSpec((bb, c, DN_V), tok), pl.BlockSpec((bb, c, LANES), tok),
                  pl.BlockSpec((1, LANES), fixed), pl.BlockSpec((1, LANES), fixed), pl.BlockSpec((1, DN_DV), fixed),
                  pl.BlockSpec((bb, DN_HEADS, DN_DK, DN_DV), st)],
        out_specs=[pl.BlockSpec((bb, c, DN_V), tok), pl.BlockSpec((bb, DN_HEADS, DN_DK, DN_DV), st)],
        out_shape=[jax.ShapeDtypeStruct((b, t, DN_V), F32), jax.ShapeDtypeStruct((b, DN_HEADS, DN_DK, DN_DV), F32)],
        compiler_params=_cparams(("parallel", "arbitrary")),
        name="delta_rule",
    )(qn, kn, v, dz, misc, nega, dtb, g_dn_out.reshape(1, DN_DV).astype(F32), s0)


def _merge_kernel(x_ref, oa_ref, ob_ref, ga_ref, gb_ref, woa_ref, wob_ref, wout_ref, gffn_ref, wpqt_ref,
                  x1_ref, h2_ref, qt_ref):
    ma = jax.nn.sigmoid(ga_ref[...]) * _dot(oa_ref[...].astype(BF16), woa_ref[...])
    mb = jax.nn.sigmoid(gb_ref[...]) * _dot(ob_ref[...].astype(BF16), wob_ref[...])
    x1 = x_ref[...] + _dot((ma + mb).astype(BF16), wout_ref[...])
    x1_ref[...] = x1
    h2 = x1 * lax.rsqrt(jnp.mean(x1 * x1, axis=-1, keepdims=True) + EPS) * gffn_ref[...]
    h2_ref[...] = h2
    qt_ref[...] = _dot_nt(wpqt_ref[...], h2.astype(BF16))


def _merge(x2d, oa, ob, ga, gb, w_oa, w_ob, w_out, g_ffn, w_pq, tm):
    n = x2d.shape[0]
    nq = w_pq.shape[1]
    row = lambda i: (i, 0)
    fixed = lambda i: (0, 0)
    return pl.pallas_call(
        _merge_kernel,
        grid=(n // tm,),
        in_specs=[pl.BlockSpec((tm, D_MODEL), row), pl.BlockSpec((tm, A_Q), row), pl.BlockSpec((tm, DN_V), row),
                  pl.BlockSpec((tm, D_MODEL), row), pl.BlockSpec((tm, D_MODEL), row),
                  pl.BlockSpec((A_Q, D_MODEL), fixed), pl.BlockSpec((DN_V, D_MODEL), fixed),
                  pl.BlockSpec((D_MODEL, D_MODEL), fixed), pl.BlockSpec((1, D_MODEL), fixed),
                  pl.BlockSpec((nq, D_MODEL), fixed)],
        out_specs=[pl.BlockSpec((tm, D_MODEL), row), pl.BlockSpec((tm, D_MODEL), row), pl.BlockSpec((nq, tm), lambda i: (0, i))],
        out_shape=[jax.ShapeDtypeStruct((n, D_MODEL), F32), jax.ShapeDtypeStruct((n, D_MODEL), F32),
                   jax.ShapeDtypeStruct((nq, n), F32)],
        compiler_params=_cparams(("parallel",)),
        name="merge_out",
    )(x2d, oa, ob, ga, gb, w_oa.astype(BF16), w_ob.astype(BF16), w_out.astype(BF16),
      g_ffn.reshape(1, D_MODEL).astype(F32), w_pq.T.astype(BF16))


def _peer_cands():
    cands = [(i, j) for i in range(PEER_TOPK) for j in range(PEER_TOPK) if (i + 1) * (j + 1) <= PEER_TOPK]
    return sorted(cands, key=lambda ij: ij[0] * PEER_TOPK + ij[1])


def _route_kernel(qt_ref, sk_ref, eid_ref, gate_ref, toff_ref, val_ref, idx_ref, *, tn):
    half = PEER_DKEY // 2
    key_iota = lax.broadcasted_iota(I32, (PEER_KEYS, tn), 0)

    heads_per_iter = 4

    def per_group(hg, carry):
        tabs = [(hh, c) for hh in range(heads_per_iter) for c in range(2)]
        ss = []
        for hh, c in tabs:
            t = 2 * (hg * heads_per_iter + hh) + c
            qblk = qt_ref[pl.ds(pl.multiple_of(t * half, half), half), :]
            ss.append(_dot3(sk_ref[t], qblk))
        for r in range(PEER_TOPK):
            ms = [jnp.max(s, axis=0, keepdims=True) for s in ss]
            ams = [jnp.min(jnp.where(s == m, key_iota, PEER_KEYS), axis=0, keepdims=True) for s, m in zip(ss, ms)]
            for i, (hh, c) in enumerate(tabs):
                h = hg * heads_per_iter + hh
                val_ref[c, r, pl.ds(h, 1), :] = ms[i]
                idx_ref[c, r, pl.ds(h, 1), :] = ams[i]
            ss = [jnp.where(key_iota == am, -jnp.inf, s) for s, am in zip(ss, ams)]
        return carry

    lax.fori_loop(0, PEER_HEADS // heads_per_iter, per_group, 0)

    cands = _peer_cands()
    cv = [val_ref[0, i] + val_ref[1, j] for i, j in cands]
    ce = [idx_ref[0, i] * PEER_KEYS + idx_ref[1, j] for i, j in cands]
    nc = len(cands)
    rank = []
    for a in range(nc):
        rk = jnp.zeros(cv[a].shape, I32)
        for b in range(nc):
            if b < a:
                rk = rk + jnp.where(cv[b] >= cv[a], 1, 0)
            elif b > a:
                rk = rk + jnp.where(cv[b] > cv[a], 1, 0)
        rank.append(rk)
    mx = cv[0]
    ex = [jnp.where(rank[a] < PEER_TOPK, jnp.exp(cv[a] - mx), 0.0) for a in range(nc)]
    den = ex[0]
    for a in range(1, nc):
        den = den + ex[a]
    inv = 1.0 / den
    e_slots = []
    for slot in range(PEER_TOPK):
        e = jnp.zeros(cv[0].shape, I32)
        g = jnp.zeros(cv[0].shape, F32)
        for a in range(nc):
            hit = rank[a] == slot
            e = jnp.where(hit, ce[a], e)
            g = jnp.where(hit, ex[a], g)
        eid_ref[slot] = e
        gate_ref[slot] = g * inv
        e_slots.append(e)
    toff_ref[...] = ((jnp.concatenate(e_slots, axis=0) >> 1) * SUBLANES).T


def _route(qt, sub_keys, tn):
    nq, n = qt.shape
    sk = sub_keys.reshape(2 * PEER_HEADS, PEER_KEYS, PEER_DKEY // 2).astype(F32)
    kern = functools.partial(_route_kernel, tn=tn)
    blk = pl.BlockSpec((PEER_TOPK, PEER_HEADS, tn), lambda i: (0, 0, i))
    npair = PEER_TOPK * PEER_HEADS
    return pl.pallas_call(
        kern,
        grid=(n // tn,),
        in_specs=[pl.BlockSpec((nq, tn), lambda i: (0, i)),
                  pl.BlockSpec((2 * PEER_HEADS, PEER_KEYS, PEER_DKEY // 2), lambda i: (0, 0, 0))],
        out_specs=[blk, blk, pl.BlockSpec((tn, npair), lambda i: (i, 0))],
        out_shape=[jax.ShapeDtypeStruct((PEER_TOPK, PEER_HEADS, n), I32), jax.ShapeDtypeStruct((PEER_TOPK, PEER_HEADS, n), F32),
                   jax.ShapeDtypeStruct((n, npair), I32)],
        scratch_shapes=[pltpu.VMEM((2, PEER_TOPK, PEER_HEADS, tn), F32), pltpu.VMEM((2, PEER_TOPK, PEER_HEADS, tn), I32)],
        compiler_params=_cparams(("parallel",)),
        name="peer_route",
    )(qt, sk)


def _pack_table(tab):
    e, d = tab.shape
    assert d == SUBLANES * LANES
    bits = lax.bitcast_convert_type(tab.astype(BF16), jnp.uint16).astype(jnp.uint32)
    words = (bits[0::2] << 16) | bits[1::2]
    return lax.bitcast_convert_type(words, I32).reshape(e // 2 * SUBLANES, LANES)


HI_HALF = -65536


def _bf16_words(x):
    return pltpu.bitcast(x.astype(BF16).astype(F32), I32) & HI_HALF


def _shr16(w):
    return lax.shift_right_logical(w, jnp.full(w.shape, 16, I32))


def _packed_mul(a_words, b_words):
    return pltpu.bitcast(pltpu.bitcast(a_words, BF16) * pltpu.bitcast(b_words, BF16), I32)


def _packed_add(a_words, b_words):
    return pltpu.bitcast(pltpu.bitcast(a_words, BF16) + pltpu.bitcast(b_words, BF16), I32)


def _hi_f32(w):
    return pltpu.bitcast(w & HI_HALF, F32)


def _lo_f32(w):
    return pltpu.bitcast(jnp.left_shift(w, 16), F32)


_BITREV8 = (0, 4, 2, 6, 1, 5, 3, 7)


PEER_U_SUB = 16


def _peer_u_kernel(toff_ref, x_ref, eid_ref, gate_ref, tab_ref, ce_ref, co_ref, rhi_ref, rlo_ref, *, tn, sub_t):
    npair = gate_ref.shape[0]
    lane = _lane_iota((npair, tn))
    sub = lax.broadcasted_iota(I32, (SUBLANES, LANES), 0)
    keep = {step: (sub & step) == 0 for step in (4, 2, 1)}

    def per_token(tt, t0):
        t = t0 + tt
        xw = _bf16_words(x_ref[t])
        xw = xw | _shr16(xw)
        for g in range(npair // SUBLANES):
            prods = []
            for p in _BITREV8:
                off = pl.multiple_of(toff_ref[t * npair + (g * SUBLANES + p)], SUBLANES)
                prods.append(_packed_mul(tab_ref[pl.ds(off, SUBLANES), :], xw))
            step = SUBLANES // 2
            while step >= 1:
                nxt = []
                for a in range(0, len(prods), 2):
                    lo_t, hi_t = prods[a], prods[a + 1]
                    left = jnp.where(keep[step], lo_t, pltpu.roll(hi_t, step, axis=0))
                    right = jnp.where(keep[step], pltpu.roll(lo_t, SUBLANES - step, axis=0), hi_t)
                    nxt.append(_packed_add(left, right))
                prods = nxt
                step //= 2
            rhi_ref[tt, g * SUBLANES:(g + 1) * SUBLANES, :] = _hi_f32(prods[0])
            rlo_ref[tt, g * SUBLANES:(g + 1) * SUBLANES, :] = _lo_f32(prods[0])
        return t0

    def per_sub(sb, accs):
        acc_hi, acc_lo = accs
        t0 = sb * sub_t
        lax.fori_loop(0, sub_t, per_token, t0)
        for tt in range(sub_t):
            hit = lane == t0 + tt
            acc_hi = jnp.where(hit, jnp.sum(rhi_ref[tt], axis=1, keepdims=True), acc_hi)
            acc_lo = jnp.where(hit, jnp.sum(rlo_ref[tt], axis=1, keepdims=True), acc_lo)
        return acc_hi, acc_lo

    zero = jnp.zeros((npair, tn), F32)
    acc_hi, acc_lo = lax.fori_loop(0, tn // sub_t, per_sub, (zero, zero))
    even = (eid_ref[...] & 1) == 0
    coef = gate_ref[...] * jax.nn.gelu(jnp.where(even, acc_hi, acc_lo))
    ce_ref[...] = jnp.where(even, coef, 0.0)
    co_ref[...] = jnp.where(even, 0.0, coef)


def _peer_v_kernel(toff_ref, ce_ref, co_ref, rep_ref, tab_ref, y_ref, bw_ref, *, tn, npair):
    ce = ce_ref[...].astype(BF16)
    co = co_ref[...].astype(BF16)
    chunk = min(16, tn)
    for c in range(tn // chunk):
        rep = rep_ref[:, c * chunk * LANES:(c + 1) * chunk * LANES]
        be = _dot(ce, rep)
        bo = _dot(co, rep)
        words = (pltpu.bitcast(be, I32) & HI_HALF) | _shr16(pltpu.bitcast(bo, I32))
        for t in range(chunk):
            bw_ref[c * chunk + t] = words[:, t * LANES:(t + 1) * LANES]

    nacc = 4

    def per_token(t, carry):
        zero = jnp.zeros((SUBLANES, LANES), F32)
        acc_hi = [zero] * nacc
        acc_lo = [zero] * nacc
        base = t * npair
        for k in range(npair):
            off = pl.multiple_of(toff_ref[base + k], SUBLANES)
            cw = jnp.broadcast_to(bw_ref[t, k:k + 1, :], (SUBLANES, LANES))
            prod = _packed_mul(tab_ref[pl.ds(off, SUBLANES), :], cw)
            acc_hi[k % nacc] = acc_hi[k % nacc] + _hi_f32(prod)
            acc_lo[k % nacc] = acc_lo[k % nacc] + _lo_f32(prod)
        y_ref[t] = ((acc_hi[0] + acc_hi[1]) + (acc_hi[2] + acc_hi[3])) + ((acc_lo[0] + acc_lo[1]) + (acc_lo[2] + acc_lo[3]))
        return carry

    lax.fori_loop(0, tn, per_token, 0)


def _peer_experts(eid, gate, toff, h2, tab_u, tab_v, tn):
    npair, n = eid.shape
    d = h2.shape[1]
    x3 = h2.reshape(n, d // LANES, LANES)
    tab_spec = pl.BlockSpec(tab_u.shape, lambda i: (0, 0), pipeline_mode=pl.Buffered(1))
    smem_blk = lambda tt: pl.BlockSpec((tt * npair,), lambda i: (i,), memory_space=pltpu.SMEM)
    slot_blk = lambda tt: pl.BlockSpec((npair, tt), lambda i: (0, i))
    sub_t = PEER_U_SUB if tn % PEER_U_SUB == 0 else tn
    ce, co = pl.pallas_call(
        functools.partial(_peer_u_kernel, tn=tn, sub_t=sub_t),
        grid=(n // tn,),
        in_specs=[smem_blk(tn), pl.BlockSpec((tn, d // LANES, LANES), lambda i: (i, 0, 0)), slot_blk(tn), slot_blk(tn), tab_spec],
        out_specs=[slot_blk(tn), slot_blk(tn)],
        out_shape=[jax.ShapeDtypeStruct((npair, n), F32)] * 2,
        scratch_shapes=[pltpu.VMEM((sub_t, npair, LANES), F32), pltpu.VMEM((sub_t, npair, LANES), F32)],
        compiler_params=_cparams(("arbitrary",)),
        name="peer_u",
    )(toff.reshape(n * npair), x3, eid, gate, tab_u)
    tv = tn
    rep = jnp.asarray(np.arange(tv)[:, None] == np.arange(tv * LANES)[None, :] // LANES, BF16)
    y3 = pl.pallas_call(
        functools.partial(_peer_v_kernel, tn=tv, npair=npair),
        grid=(n // tv,),
        in_specs=[smem_blk(tv), slot_blk(tv), slot_blk(tv),
                  pl.BlockSpec(rep.shape, lambda i: (0, 0), pipeline_mode=pl.Buffered(1)), tab_spec],
        out_specs=pl.BlockSpec((tv, d // LANES, LANES), lambda i: (i, 0, 0)),
        out_shape=jax.ShapeDtypeStruct((n, d // LANES, LANES), F32),
        scratch_shapes=[pltpu.VMEM((tv, npair, LANES), I32)],
        compiler_params=_cparams(("arbitrary",)),
        name="peer_v",
    )(toff.reshape(n * npair), ce, co, rep, tab_v)
    return y3.reshape(n, d)


def _ple_kernel(x1_ref, yp_ref, p_ref, gple_ref, wg_ref, wp_ref, o_ref):
    x2 = x1_ref[...] + yp_ref[...]
    hn = x2 * lax.rsqrt(jnp.mean(x2 * x2, axis=-1, keepdims=True) + EPS) * gple_ref[...]
    gate = jax.nn.sigmoid(_dot(hn.astype(BF16), wg_ref[...]))
    o_ref[...] = x2 + gate * _dot(p_ref[...].astype(BF16), wp_ref[...])


def _ple(x1, yp, p2d, g_ple, w_ple_gate, w_ple, tm):
    n = x1.shape[0]
    row = lambda i: (i, 0)
    fixed = lambda i: (0, 0)
    return pl.pallas_call(
        _ple_kernel,
        grid=(n // tm,),
        in_specs=[pl.BlockSpec((tm, D_MODEL), row), pl.BlockSpec((tm, D_MODEL), row), pl.BlockSpec((tm, PLE_DIM), row),
                  pl.BlockSpec((1, D_MODEL), fixed), pl.BlockSpec((D_MODEL, D_MODEL), fixed),
                  pl.BlockSpec((PLE_DIM, D_MODEL), fixed)],
        out_specs=pl.BlockSpec((tm, D_MODEL), row),
        out_shape=jax.ShapeDtypeStruct((n, D_MODEL), F32),
        compiler_params=_cparams(("parallel",)),
        name="ple_out",
    )(x1, yp, p2d, g_ple.reshape(1, D_MODEL).astype(F32), w_ple_gate.astype(BF16), w_ple.astype(BF16))


PAGES_PER_STEP = 8


def _sample_scores_kernel(pt_ref, qh_ref, wrep_ref, kinew_ref, *refs, npg):
    pages = refs[:npg]
    sc_ref, scnew_ref = refs[npg:]
    j = pl.program_id(1)
    qh = qh_ref[0]
    wrep = wrep_ref[0]

    def raw(keys_t):
        return _dot(qh, keys_t.astype(BF16))

    def score(s):
        s = jnp.maximum(s, 0.0) * wrep
        tot = s[0:SUBLANES]
        for h in range(1, IDX_HEADS):
            tot = tot + s[h * SUBLANES:(h + 1) * SUBLANES]
        return tot

    dots = [raw(pages[p][0]) for p in range(npg)]
    for p in range(npg):
        sc_ref[0, :, p * PAGE_SIZE:(p + 1) * PAGE_SIZE] = score(dots[p])

    @pl.when(j == 0)
    def _():
        sn = score(raw(kinew_ref[0]))
        causal = _lane_iota((SUBLANES, PAGE_SIZE)) <= _row_iota((SUBLANES, PAGE_SIZE))
        scnew_ref[0] = jnp.where(causal, sn, -jnp.inf)


def _sample_thr_kernel(sc_ref, thr_ref, jcut_ref, keys_ref, cand_ref, *, k_sel, tq, tk, t):
    nkb = keys_ref.shape[0]
    for kb in range(nkb):
        keys_ref[kb] = _sort_key(sc_ref[:, :, kb * tk:(kb + 1) * tk].reshape(tq, tk))
    _select_rows(keys_ref, thr_ref, jcut_ref, cand_ref, nkb, k_sel, tq, tk, min(64, tq), live_rows=t)


def _sample_attn_kernel(pt_ref, q_ref, sc_ref, scnew_ref, thr_ref, jcut_ref, knew_ref, vnew_ref, *refs, npg, past):
    kpages = refs[:npg]
    vpages = refs[npg:2 * npg]
    o_ref, m_ref, l_ref, acc_ref = refs[2 * npg:]
    j = pl.program_id(1)

    @pl.when(j == 0)
    def _():
        m_ref[...] = jnp.full(m_ref.shape, NEG_BIG, F32)
        l_ref[...] = jnp.zeros(l_ref.shape, F32)
        acc_ref[...] = jnp.zeros(acc_ref.shape, F32)

    q = q_ref[0]
    thr = thr_ref[...]
    jcut = jcut_ref[...]
    lane = _lane_iota((SUBLANES, PAGE_SIZE))

    grp = A_REP * SUBLANES

    def pages(kts, vts, scs, pos0s, causal):
        biases = []
        for sc8, pos0 in zip(scs, pos0s):
            key = _sort_key(sc8)
            pos = lane + pos0
            sel = jnp.where(key > thr, 0.0, jnp.where(key == thr, jnp.where(pos <= jcut, 0.0, NEG_BIG), NEG_BIG))
            if causal is not None:
                sel = jnp.where(causal, sel, NEG_BIG)
            biases.append(sel)
        bias = jnp.concatenate(biases, axis=1)
        bias = jnp.concatenate([bias] * A_HEADS, axis=0)
        s = jnp.concatenate(
            [jnp.concatenate([_dot(q[g * grp:(g + 1) * grp], kt[g].astype(BF16)) for g in range(A_KV_HEADS)], axis=0)
             for kt in kts], axis=1) + bias
        m_old = m_ref[...]
        m_new = jnp.maximum(m_old, jnp.max(s, axis=1, keepdims=True))
        alpha = jnp.exp(m_old - m_new)
        p = jnp.exp(s - m_new[:, 0:1])
        l_ref[...] = alpha * l_ref[...] + jnp.sum(p, axis=1, keepdims=True)
        p = p.astype(BF16)
        pv = None
        for i, vt in enumerate(vts):
            pi = p[:, i * PAGE_SIZE:(i + 1) * PAGE_SIZE]
            part = jnp.concatenate([_dot_nt(pi[g * grp:(g + 1) * grp], vt[g].astype(BF16)) for g in range(A_KV_HEADS)], axis=0)
            pv = part if pv is None else pv + part
        acc_ref[...] = alpha[:, 0:A_HEAD_DIM] * acc_ref[...] + pv
        m_ref[...] = m_new

    pages([r.at[0] for r in kpages], [r.at[0] for r in vpages],
          [sc_ref[0, :, p * PAGE_SIZE:(p + 1) * PAGE_SIZE] for p in range(npg)],
          [(j * npg + p) * PAGE_SIZE for p in range(npg)], None)

    @pl.when(j == pl.num_programs(1) - 1)
    def _():
        pages([knew_ref.at[0]], [vnew_ref.at[0]], [scnew_ref[0]], [past], lane <= _row_iota((SUBLANES, PAGE_SIZE)))
        o_ref[0] = acc_ref[...] / l_ref[:, 0:A_HEAD_DIM]


def _head_major(a, bd, t, nh, hd):
    a = a.reshape(bd, t, nh, hd).transpose(0, 2, 1, 3)
    a = jnp.pad(a, ((0, 0), (0, 0), (0, SUBLANES - t), (0, 0)))
    return a.reshape(bd, nh * SUBLANES, hd)


def _attn_sample(q, k, v, qi, misc, cache_k, cache_v, cache_idx_k, page_table, bd, t):
    assert t <= SUBLANES
    n_pages = page_table.shape[1]
    past = n_pages * PAGE_SIZE
    npg = PAGES_PER_STEP if n_pages % PAGES_PER_STEP == 0 else 1
    nj = n_pages // npg
    k_sel = min(TOPK_MAX, (past + t) // 4)
    n_pool = cache_k.shape[0]
    rows = A_HEADS * SUBLANES

    qh = _head_major(qi, bd, t, IDX_HEADS, IDX_DIM)
    wi = misc[:, MISC_WI:MISC_WI + IDX_HEADS].reshape(bd, t, IDX_HEADS).transpose(0, 2, 1)
    wrep = jnp.broadcast_to(jnp.pad(wi, ((0, 0), (0, 0), (0, SUBLANES - t))).reshape(bd, rows, 1), (bd, rows, LANES))
    pad_page = lambda a, w: jnp.pad(a.reshape(bd, t, w), ((0, 0), (0, PAGE_SIZE - t), (0, 0)))
    kinew = pad_page(misc[:, :IDX_DIM], IDX_DIM).transpose(0, 2, 1)
    cik = cache_idx_k.transpose(0, 2, 1)

    per_b = lambda b, j, pt: (b, 0, 0)
    per_b4 = lambda b, j, pt: (b, 0, 0, 0)
    page_spec = lambda shape, p: pl.BlockSpec(shape, lambda b, j, pt: (pt[b, j * npg + p],) + (0,) * (len(shape) - 1))
    sc, scnew = pl.pallas_call(
        functools.partial(_sample_scores_kernel, npg=npg),
        grid_spec=pltpu.PrefetchScalarGridSpec(
            num_scalar_prefetch=1, grid=(bd, nj),
            in_specs=[pl.BlockSpec((1, rows, IDX_DIM), per_b), pl.BlockSpec((1, rows, LANES), per_b),
                      pl.BlockSpec((1, IDX_DIM, PAGE_SIZE), per_b)]
                     + [page_spec((1, IDX_DIM, PAGE_SIZE), p) for p in range(npg)],
            out_specs=[pl.BlockSpec((1, SUBLANES, npg * PAGE_SIZE), lambda b, j, pt: (b, 0, j)),
                       pl.BlockSpec((1, SUBLANES, PAGE_SIZE), per_b)]),
        out_shape=[jax.ShapeDtypeStruct((bd, SUBLANES, past), F32), jax.ShapeDtypeStruct((bd, SUBLANES, PAGE_SIZE), F32)],
        compiler_params=_cparams(("parallel", "arbitrary")),
        name="sample_scores",
    )(page_table, qh, wrep, kinew, *([cik] * npg))

    length = past + PAGE_SIZE
    sc_all = jnp.concatenate([sc, scnew], axis=2)
    tb = SUBLANES if bd % SUBLANES == 0 else 1
    tq = tb * SUBLANES
    tk = 5 * LANES if length % (5 * LANES) == 0 else LANES
    thr, jcut = pl.pallas_call(
        functools.partial(_sample_thr_kernel, k_sel=k_sel, tq=tq, tk=tk, t=t),
        grid=(bd // tb,),
        in_specs=[pl.BlockSpec((tb, SUBLANES, length), lambda i: (i, 0, 0))],
        out_specs=[pl.BlockSpec((tq, LANES), lambda i: (i, 0))] * 2,
        out_shape=[jax.ShapeDtypeStruct((bd * SUBLANES, LANES), I32)] * 2,
        scratch_shapes=[pltpu.VMEM((length // tk, tq, tk), I32), pltpu.VMEM((tq, LANES), I32)],
        compiler_params=_cparams(("parallel",)),
        name="sample_thr",
    )(sc_all)

    qa = _head_major(q, bd, t, A_HEADS, A_HEAD_DIM)
    to_page_t = lambda a: pad_page(a, A_KV).reshape(bd, PAGE_SIZE, A_KV_HEADS, A_HEAD_DIM).transpose(0, 2, 3, 1)
    knew, vnew = to_page_t(k), to_page_t(v)
    ck = cache_k.transpose(0, 2, 3, 1)
    cv = cache_v.transpose(0, 2, 3, 1)
    kv_blk = (1, A_KV_HEADS, A_HEAD_DIM, PAGE_SIZE)
    o = pl.pallas_call(
        functools.partial(_sample_attn_kernel, npg=npg, past=past),
        grid_spec=pltpu.PrefetchScalarGridSpec(
            num_scalar_prefetch=1, grid=(bd, nj),
            in_specs=[pl.BlockSpec((1, rows, A_HEAD_DIM), per_b),
                      pl.BlockSpec((1, SUBLANES, npg * PAGE_SIZE), lambda b, j, pt: (b, 0, j)),
                      pl.BlockSpec((1, SUBLANES, PAGE_SIZE), per_b),
                      pl.BlockSpec((SUBLANES, LANES), lambda b, j, pt: (b, 0)),
                      pl.BlockSpec((SUBLANES, LANES), lambda b, j, pt: (b, 0)),
                      pl.BlockSpec(kv_blk, per_b4), pl.BlockSpec(kv_blk, per_b4)]
                     + [page_spec(kv_blk, p) for p in range(npg)] * 2,
            out_specs=pl.BlockSpec((1, rows, A_HEAD_DIM), per_b),
            scratch_shapes=[pltpu.VMEM((rows, LANES), F32), pltpu.VMEM((rows, LANES), F32), pltpu.VMEM((rows, A_HEAD_DIM), F32)]),
        out_shape=jax.ShapeDtypeStruct((bd, rows, A_HEAD_DIM), F32),
        compiler_params=_cparams(("parallel", "arbitrary")),
        name="sample_attn",
    )(page_table, qa, sc, scnew, thr, jcut, knew, vnew, *([ck] * npg), *([cv] * npg))
    o = o.reshape(bd, A_HEADS, SUBLANES, A_HEAD_DIM)[:, :, :t].transpose(0, 2, 1, 3)
    return o.reshape(bd * t, A_Q)


def _tile(n, pref):
    return pref if n % pref == 0 else n


def _layer(x, p_emb, conv_state, delta_state, attn_fn, prm):
    b, t, _ = x.shape
    n = b * t
    x2d = x.reshape(n, D_MODEL)
    tm = _tile(n, 256)
    (q, k, kdup, v, vb, qi, misc, kidup, dqkv, dz, ga, gb) = _proj(
        x2d, prm['g_mix'], prm['w_perm'], prm['g_q'], prm['g_k'], prm['g_idx_k'], tm)

    oa = attn_fn(q, k, kdup, v, vb, qi, misc, kidup)

    tp = -(-t // SUBLANES) * SUBLANES
    c = min(DN_CHUNK, tp)
    pad_t = lambda a: jnp.pad(a.reshape(b, t, a.shape[-1]), ((0, 0), (0, tp - t), (0, 0)))
    dqkv3 = dqkv.reshape(b, t, DN_CONV_CH)
    halo = jnp.zeros((b, SUBLANES, DN_CONV_CH), F32)
    if conv_state is not None:
        halo = halo.at[:, SUBLANES - (CONV_W - 1):].set(conv_state.astype(F32))
        hist = jnp.concatenate([conv_state.astype(F32), dqkv3], axis=1)
    else:
        hist = jnp.concatenate([jnp.zeros((b, CONV_W - 1, DN_CONV_CH), F32), dqkv3], axis=1)
    new_conv = hist[:, -(CONV_W - 1):]
    qn, kn, vv = _conv(pad_t(dqkv), halo, prm['conv_w'], _tile(tp, 512))
    s0 = jnp.zeros((b, DN_HEADS, DN_DK, DN_DV), F32) if delta_state is None else delta_state.astype(F32)
    ob, new_delta = _delta(qn, kn, vv, pad_t(dz), pad_t(misc), prm['a_log'], prm['dt_bias'], prm['g_dn_out'], s0, c, t)
    ob = ob[:, :t].reshape(n, DN_V)

    x1, h2, qt = _merge(x2d, oa, ob, ga, gb, prm['w_oa'], prm['w_ob'], prm['w_out'], prm['g_ffn'], prm['w_pq'], tm)
    eid, gate, toff = _route(qt, prm['sub_keys'], _tile(n, LANES))
    npair = PEER_TOPK * PEER_HEADS
    yp = _peer_experts(eid.reshape(npair, n), gate.reshape(npair, n), toff, h2, prm['tab_u'], prm['tab_v'], _tile(n, LANES))
    y = _ple(x1, yp, p_emb.reshape(n, PLE_DIM), prm['g_ple'], prm['w_ple_gate'], prm['w_ple'], tm)

    return (y.reshape(b, t, D_MODEL), k.reshape(b, t, A_KV_HEADS, A_HEAD_DIM), v.reshape(b, t, A_KV_HEADS, A_HEAD_DIM),
            misc[:, :IDX_DIM].reshape(b, t, IDX_DIM), new_conv, new_delta)


def kernel(x_prompt, x_sample, cache_k, cache_v, cache_idx_k, state_conv, state_delta, page_table, p_prompt, p_sample,
           g_mix, w_in, g_q, g_k, g_idx_k, conv_w, a_log, dt_bias, g_dn_out, w_oa, w_ob, w_out, g_ffn, w_pq, sub_keys,
           peer_u, peer_v, g_ple, w_ple_gate, w_ple):
    depth = w_in.shape[0]
    xp, xs = x_prompt, x_sample
    outs = [[] for _ in range(10)]
    for i in range(depth):
        prm = dict(g_mix=g_mix[i], w_perm=_permute_w_in(w_in[i]), g_q=g_q[i], g_k=g_k[i], g_idx_k=g_idx_k[i],
                   conv_w=conv_w[i].astype(F32), a_log=a_log[i], dt_bias=dt_bias[i], g_dn_out=g_dn_out[i],
                   w_oa=w_oa[i], w_ob=w_ob[i], w_out=w_out[i], g_ffn=g_ffn[i], w_pq=w_pq[i], sub_keys=sub_keys[i],
                   tab_u=_pack_table(peer_u[i]), tab_v=_pack_table(peer_v[i]), g_ple=g_ple[i],
                   w_ple_gate=w_ple_gate[i], w_ple=w_ple[i])

        bp, s, _ = xp.shape

        def attn_prompt(q, k, kdup, v, vb, qi, misc, kidup):
            k_sel = min(TOPK_MAX, s // 4)
            tq, tk = _tile(s, 256), _tile(s, 512)
            r3 = lambda a: a.reshape(bp, s, a.shape[-1])
            mask = _idx_mask(r3(qi), r3(misc), r3(kidup), k_sel, tq, tk)
            return _attn(r3(q), r3(kdup), r3(vb), mask, tq, tk).reshape(bp * s, A_Q)

        res = _layer(xp, p_prompt[i], None, None, attn_prompt, prm)
        xp = res[0]
        for lst, val in zip(outs[:5], res[1:]):
            lst.append(val)

        bd, t, _ = xs.shape

        def attn_sample(q, k, kdup, v, vb, qi, misc, kidup):
            return _attn_sample(q, k, v, qi, misc, cache_k[i], cache_v[i], cache_idx_k[i], page_table, bd, t)

        res = _layer(xs, p_sample[i], state_conv[i], state_delta[i], attn_sample, prm)
        xs = res[0]
        for lst, val in zip(outs[5:], res[1:]):
            lst.append(val)

    cast = [cache_k.dtype, cache_v.dtype, cache_idx_k.dtype, state_conv.dtype, state_delta.dtype] * 2
    stacked = [jnp.stack(lst).astype(dt) for lst, dt in zip(outs, cast)]
    return (xp, xs, *stacked)
```

```python
import functools
import math

import jax
import jax.numpy as jnp
import numpy as np
from jax import lax
from jax.experimental import pallas as pl
from jax.experimental.pallas import tpu as pltpu

F32 = jnp.float32
BF16 = jnp.bfloat16
I32 = jnp.int32

D_MODEL = 1024
PAGE_SIZE = 128
A_HEADS = 8
A_KV_HEADS = 4
A_REP = A_HEADS // A_KV_HEADS
A_HEAD_DIM = 64
A_Q = A_HEADS * A_HEAD_DIM
A_KV = A_KV_HEADS * A_HEAD_DIM
IDX_HEADS = 8
IDX_DIM = 64
TOPK_MAX = 256
DN_HEADS = 4
DN_DK = 128
DN_DV = 128
DN_QK = DN_HEADS * DN_DK
DN_V = DN_HEADS * DN_DV
DN_CONV_CH = 2 * DN_QK + DN_V
CONV_W = 4
DN_CHUNK = 64
PEER_HEADS = 8
PEER_KEYS = 128
PEER_DKEY = 256
PEER_TOPK = 16
PEER_HALF = PEER_KEYS * PEER_KEYS // 2
PLE_DIM = 256
EPS = 1e-6
IN_SPLITS = (A_Q, A_KV, A_KV, IDX_HEADS * IDX_DIM, IDX_DIM, IDX_HEADS, DN_CONV_CH, DN_V, DN_HEADS, DN_HEADS, D_MODEL, D_MODEL)

LANES = 128
SUBLANES = 8
VMEM_LIMIT = 56 * 1024 * 1024

NEG_BIG = -1e30
Q_SCALE = A_HEAD_DIM ** -0.5 * math.log2(math.e)
INT_MIN = -(2 ** 31)

MISC_WI = IDX_DIM
MISC_DB = MISC_WI + IDX_HEADS
MISC_DA = MISC_DB + DN_HEADS

SEG_Q = 0
SEG_K = SEG_Q + A_Q
SEG_V = SEG_K + A_KV
SEG_QI = SEG_V + A_KV
SEG_MISC = SEG_QI + IDX_HEADS * IDX_DIM
SEG_DQKV = SEG_MISC + LANES
SEG_DZ = SEG_DQKV + DN_CONV_CH
SEG_GA = SEG_DZ + DN_V
SEG_GB = SEG_GA + D_MODEL
SEG_END = SEG_GB + D_MODEL


def _cparams(sem):
    return pltpu.CompilerParams(dimension_semantics=sem, vmem_limit_bytes=VMEM_LIMIT)


def _dot(a, b):
    return jnp.dot(a, b, preferred_element_type=F32)


def _dot_nt(a, b):
    return lax.dot_general(a, b, (((1,), (1,)), ((), ())), preferred_element_type=F32)


def _split(a):
    hi = a.astype(BF16)
    lo = (a - hi.astype(F32)).astype(BF16)
    return hi, lo


def _dot3(a, b, nt=False):
    d = _dot_nt if nt else _dot
    ah, al = _split(a)
    bh, bl = _split(b)
    return d(ah, bh) + (d(ah, bl) + d(al, bh))


def _lane_iota(shape):
    return lax.broadcasted_iota(I32, shape, len(shape) - 1)


def _row_iota(shape):
    return lax.broadcasted_iota(I32, shape, len(shape) - 2)


def _half_norm(blk, gain):
    lane = _lane_iota(blk.shape)
    lo = lane < A_HEAD_DIM
    sq = blk * blk
    s_lo = jnp.sum(jnp.where(lo, sq, 0.0), axis=-1, keepdims=True)
    s_hi = jnp.sum(jnp.where(lo, 0.0, sq), axis=-1, keepdims=True)
    r_lo = lax.rsqrt(s_lo * (1.0 / A_HEAD_DIM) + EPS)
    r_hi = lax.rsqrt(s_hi * (1.0 / A_HEAD_DIM) + EPS)
    return blk * jnp.where(lo, r_lo, r_hi) * gain


def _proj_kernel(x_ref, gmix_ref, w_ref, gq_ref, gk_ref, gik_ref,
                 q_ref, k_ref, kdup_ref, v_ref, vb_ref, qi_ref, misc_ref, kidup_ref,
                 dqkv_ref, dz_ref, ga_ref, gb_ref):
    x = x_ref[...]
    h = x * lax.rsqrt(jnp.mean(x * x, axis=-1, keepdims=True) + EPS) * gmix_ref[...]
    hb = h.astype(BF16)

    def seg(a, b):
        return _dot(hb, w_ref[:, a:b])

    lane = _lane_iota((x.shape[0], LANES))
    lo = lane < A_HEAD_DIM

    zq = seg(SEG_Q, SEG_K)
    for c in range(A_Q // LANES):
        blk = _half_norm(zq[:, c * LANES:(c + 1) * LANES], gq_ref[...])
        q_ref[:, c * LANES:(c + 1) * LANES] = (blk * Q_SCALE).astype(BF16)

    zk = seg(SEG_K, SEG_V)
    for c in range(A_KV // LANES):
        blk = _half_norm(zk[:, c * LANES:(c + 1) * LANES], gk_ref[...])
        k_ref[:, c * LANES:(c + 1) * LANES] = blk
        rolled = pltpu.roll(blk, A_HEAD_DIM, axis=1)
        kdup_ref[:, (2 * c) * LANES:(2 * c + 1) * LANES] = jnp.where(lo, blk, rolled).astype(BF16)
        kdup_ref[:, (2 * c + 1) * LANES:(2 * c + 2) * LANES] = jnp.where(lo, rolled, blk).astype(BF16)

    zv = seg(SEG_V, SEG_QI)
    v_ref[...] = zv
    vb_ref[...] = zv.astype(BF16)

    qi_ref[...] = seg(SEG_QI, SEG_MISC).astype(BF16)

    zm = seg(SEG_MISC, SEG_DQKV)
    s_ik = jnp.sum(jnp.where(lo, zm * zm, 0.0), axis=-1, keepdims=True)
    kin = zm * lax.rsqrt(s_ik * (1.0 / IDX_DIM) + EPS) * gik_ref[...]
    wscale = (IDX_HEADS * IDX_DIM) ** -0.5
    misc_ref[...] = jnp.where(lo, kin, jnp.where(lane < MISC_DB, zm * wscale, zm))
    kin0 = jnp.where(lo, kin, 0.0)
    kidup_ref[...] = (kin0 + pltpu.roll(kin0, IDX_DIM, axis=1)).astype(BF16)

    dqkv_ref[...] = seg(SEG_DQKV, SEG_DZ)
    dz_ref[...] = seg(SEG_DZ, SEG_GA)
    ga_ref[...] = seg(SEG_GA, SEG_GB)
    gb_ref[...] = seg(SEG_GB, SEG_END)


def _permute_w_in(w_in):
    cuts = np.cumsum((0,) + IN_SPLITS)
    parts = [w_in[:, cuts[i]:cuts[i + 1]] for i in range(len(IN_SPLITS))]
    aq, ak, av, iq, ik, iw, dqkv, dz, db, da, ga, gb = parts
    pad = jnp.zeros((w_in.shape[0], LANES - IDX_DIM - IDX_HEADS - 2 * DN_HEADS), w_in.dtype)
    return jnp.concatenate([aq, ak, av, iq, ik, iw, db, da, pad, dqkv, dz, ga, gb], axis=1).astype(BF16)


def _tile2(g):
    return jnp.concatenate([g, g]).reshape(1, LANES).astype(F32)


def _proj(x2d, g_mix, w_perm, g_q, g_k, g_idx_k, tm):
    n = x2d.shape[0]
    assert n % tm == 0
    widths = [(A_Q, BF16), (A_KV, F32), (2 * A_KV, BF16), (A_KV, F32), (A_KV, BF16), (IDX_HEADS * IDX_DIM, BF16),
              (LANES, F32), (LANES, BF16), (DN_CONV_CH, F32), (DN_V, F32), (D_MODEL, F32), (D_MODEL, F32)]
    row = lambda i: (i, 0)
    fixed = lambda i: (0, 0)
    return pl.pallas_call(
        _proj_kernel,
        grid=(n // tm,),
        in_specs=[pl.BlockSpec((tm, D_MODEL), row), pl.BlockSpec((1, D_MODEL), fixed),
                  pl.BlockSpec((D_MODEL, SEG_END), fixed), pl.BlockSpec((1, LANES), fixed),
                  pl.BlockSpec((1, LANES), fixed), pl.BlockSpec((1, LANES), fixed)],
        out_specs=[pl.BlockSpec((tm, w), row) for w, _ in widths],
        out_shape=[jax.ShapeDtypeStruct((n, w), dt) for w, dt in widths],
        compiler_params=_cparams(("parallel",)),
        name="proj_in",
    )(x2d, g_mix.reshape(1, D_MODEL), w_perm, _tile2(g_q), _tile2(g_k), _tile2(g_idx_k))


def _sort_key(score):
    score = jnp.where(score == 0.0, 0.0, score)
    bits = pltpu.bitcast(score, I32)
    return jnp.where(bits < 0, bits ^ jnp.int32(0x7FFFFFFF), bits)


def _index_scores(qi, wi_tile, kblk):
    lane = _lane_iota((qi.shape[0], LANES))
    lo = lane < IDX_DIM
    sc = None
    for c in range(IDX_HEADS // 2):
        q128 = qi[:, c * LANES:(c + 1) * LANES]
        zero = jnp.zeros_like(q128)
        for half in range(2):
            qm = jnp.where(lo, q128, zero) if half == 0 else jnp.where(lo, zero, q128)
            s = _dot_nt(qm, kblk)
            hidx = MISC_WI + 2 * c + half
            term = jnp.maximum(s, 0.0) * wi_tile[:, hidx:hidx + 1]
            sc = term if sc is None else sc + term
    return sc


def _select_rows(keys_ref, thr_ref, jcut_ref, cand_ref, nvalid, k_sel, tq, tk, rb, live_rows=None):
    nchunk = tk // LANES
    nrb = tq // rb
    nbits_idx = int(math.ceil(math.log2(keys_ref.shape[0] * tk))) + 1
    lane = _lane_iota((rb, LANES))

    def count(pred):
        def body(kb, cnts):
            out = []
            for r in range(nrb):
                blk = keys_ref[kb, r * rb:(r + 1) * rb, :]
                cand_b = cand_ref[r * rb:(r + 1) * rb, :]
                cnt = cnts[r]
                for c in range(nchunk):
                    cnt = cnt + pred(blk[:, c * LANES:(c + 1) * LANES], cand_b, kb * tk + c * LANES, r)
                out.append(cnt)
            return tuple(out)

        zeros = tuple(jnp.zeros((rb, LANES), F32) for _ in range(nrb))
        cnts = lax.fori_loop(0, nvalid, body, zeros)
        return jnp.concatenate([jnp.sum(cnt, axis=1, keepdims=True) for cnt in cnts], axis=0)

    def bit_step(i, cur):
        cand = cur + lax.shift_left(jnp.int32(1), 31 - i)
        cand_ref[...] = jnp.broadcast_to(cand, (tq, LANES))
        c_ge = count(lambda kv, cb, base, r: jnp.where(kv >= cb, 1.0, 0.0))
        return jnp.where(c_ge >= k_sel, cand, cur)

    thr = lax.fori_loop(0, 32, bit_step, jnp.full((tq, 1), INT_MIN, I32))
    thr_ref[...] = jnp.broadcast_to(thr, (tq, LANES))
    cand_ref[...] = thr_ref[...]
    c_gt = count(lambda kv, cb, base, r: jnp.where(kv > cb, 1.0, 0.0))
    c_ge = count(lambda kv, cb, base, r: jnp.where(kv >= cb, 1.0, 0.0))
    need = k_sel - c_gt
    jcut_ref[...] = jnp.full((tq, LANES), 2 ** 30, I32)

    if live_rows is not None:
        c_ge = jnp.where(_row_iota((tq, 1)) % SUBLANES < live_rows, c_ge, 0.0)

    @pl.when(jnp.max(c_ge) > k_sel)
    def _():
        def jstep(i, jcur):
            candj = jcur + lax.shift_left(jnp.int32(1), nbits_idx - 1 - i)
            cand_ref[...] = jnp.broadcast_to(candj, (tq, LANES))
            f = count(lambda kv, cb, base, r: jnp.where(kv == thr_ref[r * rb:(r + 1) * rb, :],
                                                        jnp.where(lane + base < cb, 1.0, 0.0), 0.0))
            return jnp.where(f < need, candj, jcur)

        jcut = lax.fori_loop(0, nbits_idx, jstep, jnp.zeros((tq, 1), I32))
        jcut_ref[...] = jnp.broadcast_to(jcut, (tq, LANES))


def _idx_mask_kernel(qi_ref, wi_ref, kidup_ref, mask_ref, keys_ref, thr_ref, jcut_ref, cand_ref, *, k_sel, tq, tk, rb):
    i = pl.program_id(1)
    nkb = keys_ref.shape[0]
    nvalid = ((i + 1) * tq + tk - 1) // tk
    qi = qi_ref[0]
    wi_tile = wi_ref[0]
    q_pos = i * tq + _row_iota((tq, tk))
    lane = _lane_iota((tq, tk))

    def fill(kb, carry):
        kblk = kidup_ref[0, pl.ds(pl.multiple_of(kb * tk, tk), tk), :]
        sc = _index_scores(qi, wi_tile, kblk)
        sc = jnp.where(lane + kb * tk <= q_pos, sc, -jnp.inf)
        keys_ref[kb] = _sort_key(sc)
        return carry

    lax.fori_loop(0, nvalid, fill, 0)
    _select_rows(keys_ref, thr_ref, jcut_ref, cand_ref, nvalid, k_sel, tq, tk, rb)

    for kb in range(nkb):
        @pl.when(kb < nvalid)
        def _():
            thr = jnp.broadcast_to(thr_ref[:, 0:1], (tq, tk))
            jcut = jnp.broadcast_to(jcut_ref[:, 0:1], (tq, tk))
            key = keys_ref[kb]
            pos = lane + kb * tk
            sel = jnp.where(key > thr, 1, jnp.where(key == thr, jnp.where(pos <= jcut, 1, 0), 0))
            sel = jnp.where(pos <= q_pos, sel, 0)
            mask_ref[0, :, kb * tk:(kb + 1) * tk] = sel.astype(jnp.int8)

        @pl.when(kb >= nvalid)
        def _():
            mask_ref[0, :, kb * tk:(kb + 1) * tk] = jnp.zeros((tq, tk), jnp.int8)


def _idx_mask(qi, misc, kidup, k_sel, tq, tk):
    b, s, _ = qi.shape
    assert s % tq == 0 and s % tk == 0 and tq % 32 == 0
    rb = min(64, tq)
    kern = functools.partial(_idx_mask_kernel, k_sel=k_sel, tq=tq, tk=tk, rb=rb)
    return pl.pallas_call(
        kern,
        grid=(b, s // tq),
        in_specs=[pl.BlockSpec((1, tq, IDX_HEADS * IDX_DIM), lambda bi, i: (bi, i, 0)),
                  pl.BlockSpec((1, tq, LANES), lambda bi, i: (bi, i, 0)),
                  pl.BlockSpec((1, s, LANES), lambda bi, i: (bi, 0, 0))],
        out_specs=pl.BlockSpec((1, tq, s), lambda bi, i: (bi, i, 0)),
        out_shape=jax.ShapeDtypeStruct((b, s, s), jnp.int8),
        scratch_shapes=[pltpu.VMEM((s // tk, tq, tk), I32)] + [pltpu.VMEM((tq, LANES), I32)] * 3,
        compiler_params=_cparams(("parallel", "parallel")),
        name="idx_mask",
    )(qi, misc, kidup)


def _attn_kernel(q_ref, kdup_ref, vb_ref, mask_ref, o_ref, m_ref, l_ref, acc_ref, *, tq, tk):
    i = pl.program_id(1)
    kb = pl.program_id(2)
    nk = pl.num_programs(2)
    last_needed = ((i + 1) * tq - 1) // tk

    @pl.when(kb == 0)
    def _():
        m_ref[...] = jnp.full(m_ref.shape, NEG_BIG, F32)
        l_ref[...] = jnp.zeros(l_ref.shape, F32)
        acc_ref[...] = jnp.zeros(acc_ref.shape, F32)

    @pl.when(kb <= last_needed)
    def _():
        bias = jnp.where(mask_ref[0].astype(I32) != 0, 0.0, NEG_BIG).astype(F32)
        bias2 = jnp.concatenate([bias, bias], axis=0)
        lane = _lane_iota((tq, LANES))
        lo = lane < A_HEAD_DIM

        def qk(g):
            q128 = q_ref[0, :, g * LANES:(g + 1) * LANES]
            zero = jnp.zeros_like(q128)
            q2 = jnp.concatenate([jnp.where(lo, q128, zero), jnp.where(lo, zero, q128)], axis=0)
            return _dot_nt(q2, kdup_ref[0, :, g * LANES:(g + 1) * LANES])

        s_next = qk(0)
        for g in range(A_KV_HEADS):
            s = s_next + bias2
            if g + 1 < A_KV_HEADS:
                s_next = qk(g + 1)
            m_old = m_ref[g]
            m_new = jnp.maximum(m_old, jnp.max(s, axis=1, keepdims=True))
            alpha = jnp.exp2(m_old - m_new)
            p = jnp.exp2(s - m_new[:, 0:1])
            l_ref[g] = alpha * l_ref[g] + jnp.sum(p, axis=1, keepdims=True)
            v128 = vb_ref[0, :, (g // 2) * LANES:(g // 2 + 1) * LANES]
            acc_ref[g] = alpha * acc_ref[g] + _dot(p.astype(BF16), v128)
            m_ref[g] = m_new

    @pl.when(kb == nk - 1)
    def _():
        lane = _lane_iota((tq, LANES))
        lo = lane < A_HEAD_DIM
        for g in range(A_KV_HEADS):
            a = acc_ref[g] / l_ref[g]
            top, bot = a[:tq], a[tq:]
            if g % 2 == 0:
                o128 = jnp.where(lo, top, pltpu.roll(bot, A_HEAD_DIM, axis=1))
            else:
                o128 = jnp.where(lo, pltpu.roll(top, A_HEAD_DIM, axis=1), bot)
            o_ref[0, :, g * LANES:(g + 1) * LANES] = o128


def _attn(q, kdup, vb, mask, tq, tk):
    b, s, _ = q.shape
    nk = s // tk

    def kv_idx(bi, i, kb):
        return (bi, jnp.minimum(kb, ((i + 1) * tq - 1) // tk), 0)

    def mask_idx(bi, i, kb):
        return (bi, i, jnp.minimum(kb, ((i + 1) * tq - 1) // tk))

    kern = functools.partial(_attn_kernel, tq=tq, tk=tk)
    return pl.pallas_call(
        kern,
        grid=(b, s // tq, nk),
        in_specs=[pl.BlockSpec((1, tq, A_Q), lambda bi, i, kb: (bi, i, 0)),
                  pl.BlockSpec((1, tk, 2 * A_KV), kv_idx),
                  pl.BlockSpec((1, tk, A_KV), kv_idx),
                  pl.BlockSpec((1, tq, tk), mask_idx)],
        out_specs=pl.BlockSpec((1, tq, A_Q), lambda bi, i, kb: (bi, i, 0)),
        out_shape=jax.ShapeDtypeStruct((b, s, A_Q), F32),
        scratch_shapes=[pltpu.VMEM((A_KV_HEADS, 2 * tq, LANES), F32)] * 3,
        compiler_params=_cparams(("parallel", "parallel", "arbitrary")),
        name="sel_attn",
    )(q, kdup, vb, mask)


def _conv_kernel(x_ref, xprev_ref, halo0_ref, w_ref, q_ref, k_ref, v_ref, *, tt):
    i = pl.program_id(1)
    x = x_ref[0]
    halo = jnp.where(i == 0, halo0_ref[0], xprev_ref[0])
    w = w_ref[...]

    def post(y, rows):
        y = y * jax.nn.sigmoid(y)
        for h in range(DN_HEADS):
            for j, ref in enumerate((q_ref, k_ref)):
                blk = y[:, j * DN_QK + h * DN_DK: j * DN_QK + (h + 1) * DN_DK]
                blk = blk * lax.rsqrt(jnp.sum(blk * blk, axis=-1, keepdims=True) + EPS)
                ref[0, 0:rows, h * DN_DK:(h + 1) * DN_DK] = blk
        v_ref[0, 0:rows, :] = y[:, 2 * DN_QK:]

    y = x * w[CONV_W - 1:CONV_W, :]
    for j in range(1, CONV_W):
        y = y + pltpu.roll(x, j, axis=0) * w[CONV_W - 1 - j:CONV_W - j, :]
    post(y, tt)

    x8 = x[0:SUBLANES]
    row = _row_iota(x8.shape)
    y8 = x8 * w[CONV_W - 1:CONV_W, :]
    for j in range(1, CONV_W):
        xs = jnp.where(row < j, pltpu.roll(halo, j, axis=0), pltpu.roll(x8, j, axis=0))
        y8 = y8 + xs * w[CONV_W - 1 - j:CONV_W - j, :]
    post(y8, SUBLANES)


def _conv(dqkv, halo0, conv_w, tt):
    b, t, c = dqkv.shape
    assert t % tt == 0 and tt % SUBLANES == 0
    r = tt // SUBLANES
    kern = functools.partial(_conv_kernel, tt=tt)
    out = jax.ShapeDtypeStruct((b, t, DN_QK), F32)
    return pl.pallas_call(
        kern,
        grid=(b, t // tt),
        in_specs=[pl.BlockSpec((1, tt, c), lambda bi, i: (bi, i, 0)),
                  pl.BlockSpec((1, SUBLANES, c), lambda bi, i: (bi, jnp.maximum(i * r - 1, 0), 0)),
                  pl.BlockSpec((1, SUBLANES, c), lambda bi, i: (bi, 0, 0)),
                  pl.BlockSpec((CONV_W, c), lambda bi, i: (0, 0))],
        out_specs=[pl.BlockSpec((1, tt, DN_QK), lambda bi, i: (bi, i, 0))] * 3,
        out_shape=[out, out, out],
        compiler_params=_cparams(("parallel", "parallel")),
        name="dn_conv",
    )(dqkv, dqkv, halo0, conv_w)


def _softplus(x):
    return jnp.maximum(x, 0.0) + jnp.log(1.0 + jnp.exp(-jnp.abs(x)))


def _pad_rows(a, rows):
    if a.shape[0] == rows:
        return a
    return jnp.concatenate([a, jnp.zeros((rows - a.shape[0], a.shape[1]), a.dtype)], axis=0)


DELTA_SEQS_PER_STEP = 4


def _delta_kernel(q_ref, k_ref, v_ref, z_ref, misc_ref, nega_ref, dtb_ref, gout_ref, s0_ref, o_ref, st_ref, *, c, t_valid, bb):
    n = pl.program_id(1)

    @pl.when(n == 0)
    def _():
        st_ref[...] = s0_ref[...]

    row = _row_iota((c, LANES))
    live = row < t_valid
    ri = lax.broadcasted_iota(I32, (c, c), 0)
    ci = lax.broadcasted_iota(I32, (c, c), 1)
    incl = ri >= ci
    strict = ri > ci
    eye = jnp.where(ri == ci, 1.0, 0.0).astype(F32)
    scale = DN_DK ** -0.5

    chains = [(bi, h) for bi in range(bb) for h in range(DN_HEADS)]
    nch = range(len(chains))
    beta_t, gc_t, gc_tr = [], [], []
    for bi in range(bb):
        mt = misc_ref[bi]
        beta_t.append(jnp.where(live, jax.nn.sigmoid(mt), 0.0))
        g = jnp.where(live, nega_ref[...] * _softplus(mt + dtb_ref[...]), 0.0)
        sft = 1
        while sft < c:
            g = g + jnp.where(row >= sft, pltpu.roll(g, sft, axis=0), 0.0)
            sft *= 2
        gc_t.append(g)
        gc_tr.append(_pad_rows(g, LANES).T)

    sl = lambda h: slice(h * DN_DK, (h + 1) * DN_DK)
    q = [jnp.where(live, q_ref[bi, :, sl(h)] * scale, 0.0) for bi, h in chains]
    k = [jnp.where(live, k_ref[bi, :, sl(h)], 0.0) for bi, h in chains]
    v = [jnp.where(live, v_ref[bi, :, sl(h)], 0.0) for bi, h in chains]
    beta = [beta_t[bi][:, MISC_DB + h:MISC_DB + h + 1] for bi, h in chains]
    gcc = [gc_t[bi][:, MISC_DA + h:MISC_DA + h + 1] for bi, h in chains]
    gcr = [gc_tr[bi][MISC_DA + h:MISC_DA + h + 1, 0:c] for bi, h in chains]
    decay = [jnp.where(incl, jnp.exp(jnp.where(incl, gcc[i] - gcr[i], 0.0)), 0.0) for i in nch]
    kb = [k[i] * beta[i] for i in nch]
    pw = [-jnp.where(strict, _dot3(kb[i], k[i], nt=True) * decay[i], 0.0) for i in nch]
    r = [eye + pw[i] for i in nch]
    nn = 2
    while nn < c:
        pw = [_dot3(pw[i], pw[i]) for i in nch]
        r = [_dot3(r[i], eye + pw[i]) for i in nch]
        nn *= 2
    egc = [jnp.exp(gcc[i]) for i in nch]
    u = [_dot3(r[i], v[i] * beta[i]) for i in nch]
    w = [_dot3(r[i], kb[i] * egc[i]) for i in nch]
    a_qk = [_dot3(q[i], k[i], nt=True) * decay[i] for i in nch]
    s = [st_ref[bi, h] for bi, h in chains]
    v_new = [u[i] - _dot3(w[i], s[i]) for i in nch]
    o = [_dot3(q[i] * egc[i], s[i]) + _dot3(a_qk[i], v_new[i]) for i in nch]
    glast = [gcc[i][c - 1:c, :] for i in nch]
    kw_t = [_pad_rows(k[i] * jnp.exp(glast[i] - gcc[i]), LANES).T for i in nch]
    s_new = [s[i] * jnp.exp(glast[i]) + _dot3(kw_t[i], _pad_rows(v_new[i], LANES)) for i in nch]
    for i, (bi, h) in enumerate(chains):
        st_ref[bi, h] = s_new[i]
        on = o[i] * lax.rsqrt(jnp.mean(o[i] * o[i], axis=-1, keepdims=True) + EPS) * gout_ref[...]
        z = z_ref[bi, :, sl(h)]
        o_ref[bi, :, sl(h)] = on * (z * jax.nn.sigmoid(z))


def _delta(qn, kn, v, dz, misc, a_log, dt_bias, g_dn_out, s0, c, t_valid):
    b, t, _ = qn.shape
    assert t % c == 0 and (t_valid == t or t == c)
    nega = jnp.zeros((1, LANES), F32).at[0, MISC_DA:MISC_DA + DN_HEADS].set(-jnp.exp(a_log.astype(F32)))
    dtb = jnp.zeros((1, LANES), F32).at[0, MISC_DA:MISC_DA + DN_HEADS].set(dt_bias.astype(F32))
    tok = lambda bi, n: (bi, n, 0)
    fixed = lambda bi, n: (0, 0)
    st = lambda bi, n: (bi, 0, 0, 0)
    bb = DELTA_SEQS_PER_STEP if b % DELTA_SEQS_PER_STEP == 0 else 1
    kern = functools.partial(_delta_kernel, c=c, t_valid=t_valid, bb=bb)
    return pl.pallas_call(
        kern,
        grid=(b // bb, t // c),
        in_specs=[pl.BlockSpec((bb, c, DN_QK), tok), pl.BlockSpec((bb, c, DN_QK), tok), pl.BlockSpec((bb, c, DN_V), tok),
                  pl.BlockSpec((bb, c, DN_V), tok), pl.BlockSpec((bb, c, LANES), tok),
                  pl.BlockSpec((1, LANES), fixed), pl.BlockSpec((1, LANES), fixed), pl.BlockSpec((1, DN_DV), fixed),
                  pl.BlockSpec((bb, DN_HEADS, DN_DK, DN_DV), st)],
        out_specs=[pl.BlockSpec((bb, c, DN_V), tok), pl.BlockSpec((bb, DN_HEADS, DN_DK, DN_DV), st)],
        out_shape=[jax.ShapeDtypeStruct((b, t, DN_V), F32), jax.ShapeDtypeStruct((b, DN_HEADS, DN_DK, DN_DV), F32)],
        compiler_params=_cparams(("parallel", "arbitrary")),
        name="delta_rule",
    )(qn, kn, v, dz, misc, nega, dtb, g_dn_out.reshape(1, DN_DV).astype(F32), s0)


def _merge_kernel(x_ref, oa_ref, ob_ref, ga_ref, gb_ref, woa_ref, wob_ref, wout_ref, gffn_ref, wpqt_ref,
                  x1_ref, h2_ref, qt_ref):
    ma = jax.nn.sigmoid(ga_ref[...]) * _dot(oa_ref[...].astype(BF16), woa_ref[...])
    mb = jax.nn.sigmoid(gb_ref[...]) * _dot(ob_ref[...].astype(BF16), wob_ref[...])
    x1 = x_ref[...] + _dot((ma + mb).astype(BF16), wout_ref[...])
    x1_ref[...] = x1
    h2 = x1 * lax.rsqrt(jnp.mean(x1 * x1, axis=-1, keepdims=True) + EPS) * gffn_ref[...]
    h2_ref[...] = h2
    qt_ref[...] = _dot_nt(wpqt_ref[...], h2.astype(BF16))


def _merge(x2d, oa, ob, ga, gb, w_oa, w_ob, w_out, g_ffn, w_pq, tm):
    n = x2d.shape[0]
    nq = w_pq.shape[1]
    row = lambda i: (i, 0)
    fixed = lambda i: (0, 0)
    return pl.pallas_call(
        _merge_kernel,
        grid=(n // tm,),
        in_specs=[pl.BlockSpec((tm, D_MODEL), row), pl.BlockSpec((tm, A_Q), row), pl.BlockSpec((tm, DN_V), row),
                  pl.BlockSpec((tm, D_MODEL), row), pl.BlockSpec((tm, D_MODEL), row),
                  pl.BlockSpec((A_Q, D_MODEL), fixed), pl.BlockSpec((DN_V, D_MODEL), fixed),
                  pl.BlockSpec((D_MODEL, D_MODEL), fixed), pl.BlockSpec((1, D_MODEL), fixed),
                  pl.BlockSpec((nq, D_MODEL), fixed)],
        out_specs=[pl.BlockSpec((tm, D_MODEL), row), pl.BlockSpec((tm, D_MODEL), row), pl.BlockSpec((nq, tm), lambda i: (0, i))],
        out_shape=[jax.ShapeDtypeStruct((n, D_MODEL), F32), jax.ShapeDtypeStruct((n, D_MODEL), F32),
                   jax.ShapeDtypeStruct((nq, n), F32)],
        compiler_params=_cparams(("parallel",)),
        name="merge_out",
    )(x2d, oa, ob, ga, gb, w_oa.astype(BF16), w_ob.astype(BF16), w_out.astype(BF16),
      g_ffn.reshape(1, D_MODEL).astype(F32), w_pq.T.astype(BF16))


def _peer_cands():
    cands = [(i, j) for i in range(PEER_TOPK) for j in range(PEER_TOPK) if (i + 1) * (j + 1) <= PEER_TOPK]
    return sorted(cands, key=lambda ij: ij[0] * PEER_TOPK + ij[1])


def _route_kernel(qt_ref, sk_ref, eid_ref, gate_ref, toff_ref, val_ref, idx_ref, *, tn):
    half = PEER_DKEY // 2
    key_iota = lax.broadcasted_iota(I32, (PEER_KEYS, tn), 0)

    heads_per_iter = 4

    def per_group(hg, carry):
        tabs = [(hh, c) for hh in range(heads_per_iter) for c in range(2)]
        ss = []
        for hh, c in tabs:
            t = 2 * (hg * heads_per_iter + hh) + c
            qblk = qt_ref[pl.ds(pl.multiple_of(t * half, half), half), :]
            ss.append(_dot3(sk_ref[t], qblk))
        for r in range(PEER_TOPK):
            ms = [jnp.max(s, axis=0, keepdims=True) for s in ss]
            ams = [jnp.min(jnp.where(s == m, key_iota, PEER_KEYS), axis=0, keepdims=True) for s, m in zip(ss, ms)]
            for i, (hh, c) in enumerate(tabs):
                h = hg * heads_per_iter + hh
                val_ref[c, r, pl.ds(h, 1), :] = ms[i]
                idx_ref[c, r, pl.ds(h, 1), :] = ams[i]
            ss = [jnp.where(key_iota == am, -jnp.inf, s) for s, am in zip(ss, ams)]
        return carry

    lax.fori_loop(0, PEER_HEADS // heads_per_iter, per_group, 0)

    cands = _peer_cands()
    cv = [val_ref[0, i] + val_ref[1, j] for i, j in cands]
    ce = [idx_ref[0, i] * PEER_KEYS + idx_ref[1, j] for i, j in cands]
    nc = len(cands)
    rank = []
    for a in range(nc):
        rk = jnp.zeros(cv[a].shape, I32)
        for b in range(nc):
            if b < a:
                rk = rk + jnp.where(cv[b] >= cv[a], 1, 0)
            elif b > a:
                rk = rk + jnp.where(cv[b] > cv[a], 1, 0)
        rank.append(rk)
    mx = cv[0]
    ex = [jnp.where(rank[a] < PEER_TOPK, jnp.exp(cv[a] - mx), 0.0) for a in range(nc)]
    den = ex[0]
    for a in range(1, nc):
        den = den + ex[a]
    inv = 1.0 / den
    e_slots = []
    for slot in range(PEER_TOPK):
        e = jnp.zeros(cv[0].shape, I32)
        g = jnp.zeros(cv[0].shape, F32)
        for a in range(nc):
            hit = rank[a] == slot
            e = jnp.where(hit, ce[a], e)
            g = jnp.where(hit, ex[a], g)
        eid_ref[slot] = e
        gate_ref[slot] = g * inv
        e_slots.append(e)
    toff_ref[...] = ((jnp.concatenate(e_slots, axis=0) & (PEER_HALF - 1)) * SUBLANES).T


def _route(qt, sub_keys, tn):
    nq, n = qt.shape
    sk = sub_keys.reshape(2 * PEER_HEADS, PEER_KEYS, PEER_DKEY // 2).astype(F32)
    kern = functools.partial(_route_kernel, tn=tn)
    blk = pl.BlockSpec((PEER_TOPK, PEER_HEADS, tn), lambda i: (0, 0, i))
    npair = PEER_TOPK * PEER_HEADS
    return pl.pallas_call(
        kern,
        grid=(n // tn,),
        in_specs=[pl.BlockSpec((nq, tn), lambda i: (0, i)),
                  pl.BlockSpec((2 * PEER_HEADS, PEER_KEYS, PEER_DKEY // 2), lambda i: (0, 0, 0))],
        out_specs=[blk, blk, pl.BlockSpec((tn, npair), lambda i: (i, 0))],
        out_shape=[jax.ShapeDtypeStruct((PEER_TOPK, PEER_HEADS, n), I32), jax.ShapeDtypeStruct((PEER_TOPK, PEER_HEADS, n), F32),
                   jax.ShapeDtypeStruct((n, npair), I32)],
        scratch_shapes=[pltpu.VMEM((2, PEER_TOPK, PEER_HEADS, tn), F32), pltpu.VMEM((2, PEER_TOPK, PEER_HEADS, tn), I32)],
        compiler_params=_cparams(("parallel",)),
        name="peer_route",
    )(qt, sk)


def _pack_table(tab):
    e, d = tab.shape
    assert d == SUBLANES * LANES
    bits = lax.bitcast_convert_type(tab.astype(BF16), jnp.uint16).astype(jnp.uint32)
    assert e == 2 * PEER_HALF
    words = (bits[:PEER_HALF] << 16) | bits[PEER_HALF:]
    return lax.bitcast_convert_type(words, I32).reshape(e // 2 * SUBLANES, LANES)


HI_HALF = -65536


def _bf16_words(x):
    return pltpu.bitcast(x.astype(BF16).astype(F32), I32) & HI_HALF


def _shr16(w):
    return lax.shift_right_logical(w, jnp.full(w.shape, 16, I32))


def _packed_mul(a_words, b_words):
    return pltpu.bitcast(pltpu.bitcast(a_words, BF16) * pltpu.bitcast(b_words, BF16), I32)


def _packed_add(a_words, b_words):
    return pltpu.bitcast(pltpu.bitcast(a_words, BF16) + pltpu.bitcast(b_words, BF16), I32)


def _hi_f32(w):
    return pltpu.bitcast(w & HI_HALF, F32)


def _lo_f32(w):
    return pltpu.bitcast(jnp.left_shift(w, 16), F32)


_BITREV8 = (0, 4, 2, 6, 1, 5, 3, 7)


PEER_U_SUB = 16


def _peer_u_kernel(toff_ref, x_ref, eid_ref, gate_ref, tab_ref, ce_ref, co_ref, rhi_ref, rlo_ref, *, tn, sub_t):
    npair = gate_ref.shape[0]
    lane = _lane_iota((npair, tn))
    sub = lax.broadcasted_iota(I32, (SUBLANES, LANES), 0)
    keep = {step: (sub & step) == 0 for step in (4, 2, 1)}

    def per_token(tt, t0):
        t = t0 + tt
        xw = _bf16_words(x_ref[t])
        xw = xw | _shr16(xw)
        for g in range(npair // SUBLANES):
            prods = []
            for p in _BITREV8:
                off = pl.multiple_of(toff_ref[t * npair + (g * SUBLANES + p)], SUBLANES)
                prods.append(_packed_mul(tab_ref[pl.ds(off, SUBLANES), :], xw))
            step = SUBLANES // 2
            while step >= 1:
                nxt = []
                for a in range(0, len(prods), 2):
                    lo_t, hi_t = prods[a], prods[a + 1]
                    left = jnp.where(keep[step], lo_t, pltpu.roll(hi_t, step, axis=0))
                    right = jnp.where(keep[step], pltpu.roll(lo_t, SUBLANES - step, axis=0), hi_t)
                    nxt.append(_packed_add(left, right))
                prods = nxt
                step //= 2
            rhi_ref[tt, g * SUBLANES:(g + 1) * SUBLANES, :] = _hi_f32(prods[0])
            rlo_ref[tt, g * SUBLANES:(g + 1) * SUBLANES, :] = _lo_f32(prods[0])
        return t0

    def per_sub(sb, accs):
        acc_hi, acc_lo = accs
        t0 = sb * sub_t
        lax.fori_loop(0, sub_t, per_token, t0)
        ones = jnp.ones((LANES, LANES), BF16)
        for tt in range(sub_t):
            hit = lane == t0 + tt
            acc_hi = jnp.where(hit, jnp.sum(rhi_ref[tt], axis=1, keepdims=True), acc_hi)
            acc_lo = jnp.where(hit, _dot(rlo_ref[tt].astype(BF16), ones)[:, 0:tn], acc_lo)
        return acc_hi, acc_lo

    zero = jnp.zeros((npair, tn), F32)
    acc_hi, acc_lo = lax.fori_loop(0, tn // sub_t, per_sub, (zero, zero))
    even = eid_ref[...] < PEER_HALF
    coef = gate_ref[...] * jax.nn.gelu(jnp.where(even, acc_hi, acc_lo))
    ce_ref[...] = jnp.where(even, coef, 0.0)
    co_ref[...] = jnp.where(even, 0.0, coef)


def _peer_v_kernel(toff_ref, ce_ref, co_ref, rep_ref, tab_ref, y_ref, bw_ref, *, tn, npair):
    ce = ce_ref[...].astype(BF16)
    co = co_ref[...].astype(BF16)
    chunk = min(16, tn)
    for c in range(tn // chunk):
        rep = rep_ref[:, c * chunk * LANES:(c + 1) * chunk * LANES]
        be = _dot(ce, rep)
        bo = _dot(co, rep)
        words = (pltpu.bitcast(be, I32) & HI_HALF) | _shr16(pltpu.bitcast(bo, I32))
        for t in range(chunk):
            bw_ref[c * chunk + t] = words[:, t * LANES:(t + 1) * LANES]

    nacc = 4

    def per_token(t, carry):
        zero = jnp.zeros((SUBLANES, LANES), F32)
        acc_hi = [zero] * nacc
        acc_lo = [zero] * nacc
        base = t * npair
        for k in range(npair):
            off = pl.multiple_of(toff_ref[base + k], SUBLANES)
            cw = jnp.broadcast_to(bw_ref[t, k:k + 1, :], (SUBLANES, LANES))
            prod = _packed_mul(tab_ref[pl.ds(off, SUBLANES), :], cw)
            acc_hi[k % nacc] = acc_hi[k % nacc] + _hi_f32(prod)
            acc_lo[k % nacc] = acc_lo[k % nacc] + _lo_f32(prod)
        y_ref[t] = ((acc_hi[0] + acc_hi[1]) + (acc_hi[2] + acc_hi[3])) + ((acc_lo[0] + acc_lo[1]) + (acc_lo[2] + acc_lo[3]))
        return carry

    lax.fori_loop(0, tn, per_token, 0)


def _peer_experts(eid, gate, toff, h2, tab_u, tab_v, tn):
    npair, n = eid.shape
    d = h2.shape[1]
    x3 = h2.reshape(n, d // LANES, LANES)
    tab_spec = pl.BlockSpec(tab_u.shape, lambda i: (0, 0), pipeline_mode=pl.Buffered(1))
    smem_blk = lambda tt: pl.BlockSpec((tt * npair,), lambda i: (i,), memory_space=pltpu.SMEM)
    slot_blk = lambda tt: pl.BlockSpec((npair, tt), lambda i: (0, i))
    sub_t = PEER_U_SUB if tn % PEER_U_SUB == 0 else tn
    ce, co = pl.pallas_call(
        functools.partial(_peer_u_kernel, tn=tn, sub_t=sub_t),
        grid=(n // tn,),
        in_specs=[smem_blk(tn), pl.BlockSpec((tn, d // LANES, LANES), lambda i: (i, 0, 0)), slot_blk(tn), slot_blk(tn), tab_spec],
        out_specs=[slot_blk(tn), slot_blk(tn)],
        out_shape=[jax.ShapeDtypeStruct((npair, n), F32)] * 2,
        scratch_shapes=[pltpu.VMEM((sub_t, npair, LANES), F32), pltpu.VMEM((sub_t, npair, LANES), F32)],
        compiler_params=_cparams(("arbitrary",)),
        name="peer_u",
    )(toff.reshape(n * npair), x3, eid, gate, tab_u)
    tv = tn
    rep = jnp.asarray(np.arange(tv)[:, None] == np.arange(tv * LANES)[None, :] // LANES, BF16)
    y3 = pl.pallas_call(
        functools.partial(_peer_v_kernel, tn=tv, npair=npair),
        grid=(n // tv,),
        in_specs=[smem_blk(tv), slot_blk(tv), slot_blk(tv),
                  pl.BlockSpec(rep.shape, lambda i: (0, 0), pipeline_mode=pl.Buffered(1)), tab_spec],
        out_specs=pl.BlockSpec((tv, d // LANES, LANES), lambda i: (i, 0, 0)),
        out_shape=jax.ShapeDtypeStruct((n, d // LANES, LANES), F32),
        scratch_shapes=[pltpu.VMEM((tv, npair, LANES), I32)],
        compiler_params=_cparams(("arbitrary",)),
        name="peer_v",
    )(toff.reshape(n * npair), ce, co, rep, tab_v)
    return y3.reshape(n, d)


def _ple_kernel(x1_ref, yp_ref, p_ref, gple_ref, wg_ref, wp_ref, o_ref):
    x2 = x1_ref[...] + yp_ref[...]
    hn = x2 * lax.rsqrt(jnp.mean(x2 * x2, axis=-1, keepdims=True) + EPS) * gple_ref[...]
    gate = jax.nn.sigmoid(_dot(hn.astype(BF16), wg_ref[...]))
    o_ref[...] = x2 + gate * _dot(p_ref[...].astype(BF16), wp_ref[...])


def _ple(x1, yp, p2d, g_ple, w_ple_gate, w_ple, tm):
    n = x1.shape[0]
    row = lambda i: (i, 0)
    fixed = lambda i: (0, 0)
    return pl.pallas_call(
        _ple_kernel,
        grid=(n // tm,),
        in_specs=[pl.BlockSpec((tm, D_MODEL), row), pl.BlockSpec((tm, D_MODEL), row), pl.BlockSpec((tm, PLE_DIM), row),
                  pl.BlockSpec((1, D_MODEL), fixed), pl.BlockSpec((D_MODEL, D_MODEL), fixed),
                  pl.BlockSpec((PLE_DIM, D_MODEL), fixed)],
        out_specs=pl.BlockSpec((tm, D_MODEL), row),
        out_shape=jax.ShapeDtypeStruct((n, D_MODEL), F32),
        compiler_params=_cparams(("parallel",)),
        name="ple_out",
    )(x1, yp, p2d, g_ple.reshape(1, D_MODEL).astype(F32), w_ple_gate.astype(BF16), w_ple.astype(BF16))


PAGES_PER_STEP = 8


def _sample_scores_kernel(pt_ref, qh_ref, wrep_ref, kinew_ref, *refs, npg):
    pages = refs[:npg]
    sc_ref, scnew_ref = refs[npg:]
    j = pl.program_id(1)
    qh = qh_ref[0]
    wrep = wrep_ref[0]

    def raw(keys_t):
        return _dot(qh, keys_t.astype(BF16))

    def score(s):
        s = jnp.maximum(s, 0.0) * wrep
        tot = s[0:SUBLANES]
        for h in range(1, IDX_HEADS):
            tot = tot + s[h * SUBLANES:(h + 1) * SUBLANES]
        return tot

    dots = [raw(pages[p][0]) for p in range(npg)]
    for p in range(npg):
        sc_ref[0, :, p * PAGE_SIZE:(p + 1) * PAGE_SIZE] = score(dots[p])

    @pl.when(j == 0)
    def _():
        sn = score(raw(kinew_ref[0]))
        causal = _lane_iota((SUBLANES, PAGE_SIZE)) <= _row_iota((SUBLANES, PAGE_SIZE))
        scnew_ref[0] = jnp.where(causal, sn, -jnp.inf)


def _sample_thr_kernel(sc_ref, thr_ref, jcut_ref, keys_ref, cand_ref, *, k_sel, tq, tk, t):
    nkb = keys_ref.shape[0]
    for kb in range(nkb):
        keys_ref[kb] = _sort_key(sc_ref[:, :, kb * tk:(kb + 1) * tk].reshape(tq, tk))
    _select_rows(keys_ref, thr_ref, jcut_ref, cand_ref, nkb, k_sel, tq, tk, min(64, tq), live_rows=t)


def _sample_attn_kernel(pt_ref, q_ref, sc_ref, scnew_ref, thr_ref, jcut_ref, knew_ref, vnew_ref, *refs, npg, past):
    kpages = refs[:npg]
    vpages = refs[npg:2 * npg]
    o_ref, m_ref, l_ref, acc_ref = refs[2 * npg:]
    j = pl.program_id(1)

    @pl.when(j == 0)
    def _():
        m_ref[...] = jnp.full(m_ref.shape, NEG_BIG, F32)
        l_ref[...] = jnp.zeros(l_ref.shape, F32)
        acc_ref[...] = jnp.zeros(acc_ref.shape, F32)

    q = q_ref[0]
    thr = thr_ref[...]
    jcut = jcut_ref[...]
    lane = _lane_iota((SUBLANES, PAGE_SIZE))

    grp = A_REP * SUBLANES

    def pages(kts, vts, scs, pos0s, causal):
        biases = []
        for sc8, pos0 in zip(scs, pos0s):
            key = _sort_key(sc8)
            pos = lane + pos0
            sel = jnp.where(key > thr, 0.0, jnp.where(key == thr, jnp.where(pos <= jcut, 0.0, NEG_BIG), NEG_BIG))
            if causal is not None:
                sel = jnp.where(causal, sel, NEG_BIG)
            biases.append(sel)
        bias = jnp.concatenate(biases, axis=1)
        bias = jnp.concatenate([bias] * A_HEADS, axis=0)
        s = jnp.concatenate(
            [jnp.concatenate([_dot(q[g * grp:(g + 1) * grp], kt[g].astype(BF16)) for g in range(A_KV_HEADS)], axis=0)
             for kt in kts], axis=1) + bias
        m_old = m_ref[...]
        m_new = jnp.maximum(m_old, jnp.max(s, axis=1, keepdims=True))
        alpha = jnp.exp2(m_old - m_new)
        p = jnp.exp2(s - m_new[:, 0:1])
        l_ref[...] = alpha * l_ref[...] + jnp.sum(p, axis=1, keepdims=True)
        p = p.astype(BF16)
        pv = None
        for i, vt in enumerate(vts):
            pi = p[:, i * PAGE_SIZE:(i + 1) * PAGE_SIZE]
            part = jnp.concatenate([_dot_nt(pi[g * grp:(g + 1) * grp], vt[g].astype(BF16)) for g in range(A_KV_HEADS)], axis=0)
            pv = part if pv is None else pv + part
        acc_ref[...] = alpha[:, 0:A_HEAD_DIM] * acc_ref[...] + pv
        m_ref[...] = m_new

    pages([r.at[0] for r in kpages], [r.at[0] for r in vpages],
          [sc_ref[0, :, p * PAGE_SIZE:(p + 1) * PAGE_SIZE] for p in range(npg)],
          [(j * npg + p) * PAGE_SIZE for p in range(npg)], None)

    @pl.when(j == pl.num_programs(1) - 1)
    def _():
        pages([knew_ref.at[0]], [vnew_ref.at[0]], [scnew_ref[0]], [past], lane <= _row_iota((SUBLANES, PAGE_SIZE)))
        o_ref[0] = acc_ref[...] / l_ref[:, 0:A_HEAD_DIM]


def _head_major(a, bd, t, nh, hd):
    a = a.reshape(bd, t, nh, hd).transpose(0, 2, 1, 3)
    a = jnp.pad(a, ((0, 0), (0, 0), (0, SUBLANES - t), (0, 0)))
    return a.reshape(bd, nh * SUBLANES, hd)


def _attn_sample(q, k, v, qi, misc, cache_k, cache_v, cache_idx_k, page_table, bd, t):
    assert t <= SUBLANES
    n_pages = page_table.shape[1]
    past = n_pages * PAGE_SIZE
    npg = PAGES_PER_STEP if n_pages % PAGES_PER_STEP == 0 else 1
    nj = n_pages // npg
    k_sel = min(TOPK_MAX, (past + t) // 4)
    n_pool = cache_k.shape[0]
    rows = A_HEADS * SUBLANES

    qh = _head_major(qi, bd, t, IDX_HEADS, IDX_DIM)
    wi = misc[:, MISC_WI:MISC_WI + IDX_HEADS].reshape(bd, t, IDX_HEADS).transpose(0, 2, 1)
    wrep = jnp.broadcast_to(jnp.pad(wi, ((0, 0), (0, 0), (0, SUBLANES - t))).reshape(bd, rows, 1), (bd, rows, LANES))
    pad_page = lambda a, w: jnp.pad(a.reshape(bd, t, w), ((0, 0), (0, PAGE_SIZE - t), (0, 0)))
    kinew = pad_page(misc[:, :IDX_DIM], IDX_DIM).transpose(0, 2, 1)
    cik = cache_idx_k.transpose(0, 2, 1)

    per_b = lambda b, j, pt: (b, 0, 0)
    per_b4 = lambda b, j, pt: (b, 0, 0, 0)
    page_spec = lambda shape, p: pl.BlockSpec(shape, lambda b, j, pt: (pt[b, j * npg + p],) + (0,) * (len(shape) - 1))
    sc, scnew = pl.pallas_call(
        functools.partial(_sample_scores_kernel, npg=npg),
        grid_spec=pltpu.PrefetchScalarGridSpec(
            num_scalar_prefetch=1, grid=(bd, nj),
            in_specs=[pl.BlockSpec((1, rows, IDX_DIM), per_b), pl.BlockSpec((1, rows, LANES), per_b),
                      pl.BlockSpec((1, IDX_DIM, PAGE_SIZE), per_b)]
                     + [page_spec((1, IDX_DIM, PAGE_SIZE), p) for p in range(npg)],
            out_specs=[pl.BlockSpec((1, SUBLANES, npg * PAGE_SIZE), lambda b, j, pt: (b, 0, j)),
                       pl.BlockSpec((1, SUBLANES, PAGE_SIZE), per_b)]),
        out_shape=[jax.ShapeDtypeStruct((bd, SUBLANES, past), F32), jax.ShapeDtypeStruct((bd, SUBLANES, PAGE_SIZE), F32)],
        compiler_params=_cparams(("parallel", "arbitrary")),
        name="sample_scores",
    )(page_table, qh, wrep, kinew, *([cik] * npg))

    length = past + PAGE_SIZE
    sc_all = jnp.concatenate([sc, scnew], axis=2)
    tb = SUBLANES if bd % SUBLANES == 0 else 1
    tq = tb * SUBLANES
    tk = 5 * LANES if length % (5 * LANES) == 0 else LANES
    thr, jcut = pl.pallas_call(
        functools.partial(_sample_thr_kernel, k_sel=k_sel, tq=tq, tk=tk, t=t),
        grid=(bd // tb,),
        in_specs=[pl.BlockSpec((tb, SUBLANES, length), lambda i: (i, 0, 0))],
        out_specs=[pl.BlockSpec((tq, LANES), lambda i: (i, 0))] * 2,
        out_shape=[jax.ShapeDtypeStruct((bd * SUBLANES, LANES), I32)] * 2,
        scratch_shapes=[pltpu.VMEM((length // tk, tq, tk), I32), pltpu.VMEM((tq, LANES), I32)],
        compiler_params=_cparams(("parallel",)),
        name="sample_thr",
    )(sc_all)

    qa = _head_major(q, bd, t, A_HEADS, A_HEAD_DIM)
    to_page_t = lambda a: pad_page(a, A_KV).reshape(bd, PAGE_SIZE, A_KV_HEADS, A_HEAD_DIM).transpose(0, 2, 3, 1)
    knew, vnew = to_page_t(k), to_page_t(v)
    ck = cache_k.transpose(0, 2, 3, 1)
    cv = cache_v.transpose(0, 2, 3, 1)
    kv_blk = (1, A_KV_HEADS, A_HEAD_DIM, PAGE_SIZE)
    o = pl.pallas_call(
        functools.partial(_sample_attn_kernel, npg=npg, past=past),
        grid_spec=pltpu.PrefetchScalarGridSpec(
            num_scalar_prefetch=1, grid=(bd, nj),
            in_specs=[pl.BlockSpec((1, rows, A_HEAD_DIM), per_b),
                      pl.BlockSpec((1, SUBLANES, npg * PAGE_SIZE), lambda b, j, pt: (b, 0, j)),
                      pl.BlockSpec((1, SUBLANES, PAGE_SIZE), per_b),
                      pl.BlockSpec((SUBLANES, LANES), lambda b, j, pt: (b, 0)),
                      pl.BlockSpec((SUBLANES, LANES), lambda b, j, pt: (b, 0)),
                      pl.BlockSpec(kv_blk, per_b4), pl.BlockSpec(kv_blk, per_b4)]
                     + [page_spec(kv_blk, p) for p in range(npg)] * 2,
            out_specs=pl.BlockSpec((1, rows, A_HEAD_DIM), per_b),
            scratch_shapes=[pltpu.VMEM((rows, LANES), F32), pltpu.VMEM((rows, LANES), F32), pltpu.VMEM((rows, A_HEAD_DIM), F32)]),
        out_shape=jax.ShapeDtypeStruct((bd, rows, A_HEAD_DIM), F32),
        compiler_params=_cparams(("parallel", "arbitrary")),
        name="sample_attn",
    )(page_table, qa, sc, scnew, thr, jcut, knew, vnew, *([ck] * npg), *([cv] * npg))
    o = o.reshape(bd, A_HEADS, SUBLANES, A_HEAD_DIM)[:, :, :t].transpose(0, 2, 1, 3)
    return o.reshape(bd * t, A_Q)


def _tile(n, pref):
    return pref if n % pref == 0 else n


def _layer(x, p_emb, conv_state, delta_state, attn_fn, prm):
    b, t, _ = x.shape
    n = b * t
    x2d = x.reshape(n, D_MODEL)
    tm = _tile(n, 256)
    (q, k, kdup, v, vb, qi, misc, kidup, dqkv, dz, ga, gb) = _proj(
        x2d, prm['g_mix'], prm['w_perm'], prm['g_q'], prm['g_k'], prm['g_idx_k'], tm)

    oa = attn_fn(q, k, kdup, v, vb, qi, misc, kidup)

    tp = -(-t // SUBLANES) * SUBLANES
    c = min(DN_CHUNK, tp)
    pad_t = lambda a: jnp.pad(a.reshape(b, t, a.shape[-1]), ((0, 0), (0, tp - t), (0, 0)))
    dqkv3 = dqkv.reshape(b, t, DN_CONV_CH)
    halo = jnp.zeros((b, SUBLANES, DN_CONV_CH), F32)
    if conv_state is not None:
        halo = halo.at[:, SUBLANES - (CONV_W - 1):].set(conv_state.astype(F32))
        hist = jnp.concatenate([conv_state.astype(F32), dqkv3], axis=1)
    else:
        hist = jnp.concatenate([jnp.zeros((b, CONV_W - 1, DN_CONV_CH), F32), dqkv3], axis=1)
    new_conv = hist[:, -(CONV_W - 1):]
    qn, kn, vv = _conv(pad_t(dqkv), halo, prm['conv_w'], _tile(tp, 512))
    s0 = jnp.zeros((b, DN_HEADS, DN_DK, DN_DV), F32) if delta_state is None else delta_state.astype(F32)
    ob, new_delta = _delta(qn, kn, vv, pad_t(dz), pad_t(misc), prm['a_log'], prm['dt_bias'], prm['g_dn_out'], s0, c, t)
    ob = ob[:, :t].reshape(n, DN_V)

    x1, h2, qt = _merge(x2d, oa, ob, ga, gb, prm['w_oa'], prm['w_ob'], prm['w_out'], prm['g_ffn'], prm['w_pq'], tm)
    eid, gate, toff = _route(qt, prm['sub_keys'], _tile(n, LANES))
    npair = PEER_TOPK * PEER_HEADS
    yp = _peer_experts(eid.reshape(npair, n), gate.reshape(npair, n), toff, h2, prm['tab_u'], prm['tab_v'], _tile(n, LANES))
    y = _ple(x1, yp, p_emb.reshape(n, PLE_DIM), prm['g_ple'], prm['w_ple_gate'], prm['w_ple'], tm)

    return (y.reshape(b, t, D_MODEL), k.reshape(b, t, A_KV_HEADS, A_HEAD_DIM), v.reshape(b, t, A_KV_HEADS, A_HEAD_DIM),
            misc[:, :IDX_DIM].reshape(b, t, IDX_DIM), new_conv, new_delta)


def kernel(x_prompt, x_sample, cache_k, cache_v, cache_idx_k, state_conv, state_delta, page_table, p_prompt, p_sample,
           g_mix, w_in, g_q, g_k, g_idx_k, conv_w, a_log, dt_bias, g_dn_out, w_oa, w_ob, w_out, g_ffn, w_pq, sub_keys,
           peer_u, peer_v, g_ple, w_ple_gate, w_ple):
    depth = w_in.shape[0]
    xp, xs = x_prompt, x_sample
    outs = [[] for _ in range(10)]
    for i in range(depth):
        prm = dict(g_mix=g_mix[i], w_perm=_permute_w_in(w_in[i]), g_q=g_q[i], g_k=g_k[i], g_idx_k=g_idx_k[i],
                   conv_w=conv_w[i].astype(F32), a_log=a_log[i], dt_bias=dt_bias[i], g_dn_out=g_dn_out[i],
                   w_oa=w_oa[i], w_ob=w_ob[i], w_out=w_out[i], g_ffn=g_ffn[i], w_pq=w_pq[i], sub_keys=sub_keys[i],
                   tab_u=_pack_table(peer_u[i]), tab_v=_pack_table(peer_v[i]), g_ple=g_ple[i],
                   w_ple_gate=w_ple_gate[i], w_ple=w_ple[i])

        bp, s, _ = xp.shape

        def attn_prompt(q, k, kdup, v, vb, qi, misc, kidup):
            k_sel = min(TOPK_MAX, s // 4)
            tq, tk = _tile(s, 256), _tile(s, 512)
            r3 = lambda a: a.reshape(bp, s, a.shape[-1])
            mask = _idx_mask(r3(qi), r3(misc), r3(kidup), k_sel, tq, tk)
            return _attn(r3(q), r3(kdup), r3(vb), mask, tq, tk).reshape(bp * s, A_Q)

        res = _layer(xp, p_prompt[i], None, None, attn_prompt, prm)
        xp = res[0]
        for lst, val in zip(outs[:5], res[1:]):
            lst.append(val)

        bd, t, _ = xs.shape

        def attn_sample(q, k, kdup, v, vb, qi, misc, kidup):
            return _attn_sample(q, k, v, qi, misc, cache_k[i], cache_v[i], cache_idx_k[i], page_table, bd, t)

        res = _layer(xs, p_sample[i], state_conv[i], state_delta[i], attn_sample, prm)
        xs = res[0]
        for lst, val in zip(outs[5:], res[1:]):
            lst.append(val)

    cast = [cache_k.dtype, cache_v.dtype, cache_idx_k.dtype, state_conv.dtype, state_delta.dtype] * 2
    stacked = [jnp.stack(lst).astype(dt) for lst, dt in zip(outs, cast)]
    return (xp, xs, *stacked)
```

```python
import functools
import math

import jax
import jax.numpy as jnp
import numpy as np
from jax import lax
from jax.experimental import pallas as pl
from jax.experimental.pallas import tpu as pltpu

F32 = jnp.float32
BF16 = jnp.bfloat16
I32 = jnp.int32

D_MODEL = 1024
PAGE_SIZE = 128
A_HEADS = 8
A_KV_HEADS = 4
A_REP = A_HEADS // A_KV_HEADS
A_HEAD_DIM = 64
A_Q = A_HEADS * A_HEAD_DIM
A_KV = A_KV_HEADS * A_HEAD_DIM
IDX_HEADS = 8
IDX_DIM = 64
TOPK_MAX = 256
DN_HEADS = 4
DN_DK = 128
DN_DV = 128
DN_QK = DN_HEADS * DN_DK
DN_V = DN_HEADS * DN_DV
DN_CONV_CH = 2 * DN_QK + DN_V
CONV_W = 4
DN_CHUNK = 64
PEER_HEADS = 8
PEER_KEYS = 128
PEER_DKEY = 256
PEER_TOPK = 16
PEER_HALF = PEER_KEYS * PEER_KEYS // 2
PLE_DIM = 256
EPS = 1e-6
IN_SPLITS = (A_Q, A_KV, A_KV, IDX_HEADS * IDX_DIM, IDX_DIM, IDX_HEADS, DN_CONV_CH, DN_V, DN_HEADS, DN_HEADS, D_MODEL, D_MODEL)

LANES = 128
SUBLANES = 8
VMEM_LIMIT = 56 * 1024 * 1024

ROW_TILE = 256
ATTN_TQ = 256
ATTN_TK = 512
CONV_TT = 512
TOKEN_TILE = LANES
SEARCH_ROWS = 64
NO_TIE_CUT = 2 ** 30
NEG_INF_HI_KEY = (0xFF80 ^ 0x7FFF) - 2 ** 16
MIN_NORMAL = 2.0 ** -126
NEG_BIG = -1e30
Q_SCALE = A_HEAD_DIM ** -0.5 * math.log2(math.e)
INT_MIN = -(2 ** 31)

MISC_WI = IDX_DIM
MISC_DB = MISC_WI + IDX_HEADS
MISC_DA = MISC_DB + DN_HEADS

SEG_Q = 0
SEG_K = SEG_Q + A_Q
SEG_V = SEG_K + A_KV
SEG_QI = SEG_V + A_KV
SEG_MISC = SEG_QI + IDX_HEADS * IDX_DIM
SEG_DQKV = SEG_MISC + LANES
SEG_DZ = SEG_DQKV + DN_CONV_CH
SEG_GA = SEG_DZ + DN_V
SEG_GB = SEG_GA + D_MODEL
SEG_END = SEG_GB + D_MODEL


def _cparams(sem):
    return pltpu.CompilerParams(dimension_semantics=sem, vmem_limit_bytes=VMEM_LIMIT)


def _dot(a, b):
    return jnp.dot(a, b, preferred_element_type=F32)


def _dot_nt(a, b):
    return lax.dot_general(a, b, (((1,), (1,)), ((), ())), preferred_element_type=F32)


def _split(a):
    hi = a.astype(BF16)
    lo = (a - hi.astype(F32)).astype(BF16)
    return hi, lo


def _dot3(a, b, nt=False):
    d = _dot_nt if nt else _dot
    ah, al = _split(a)
    bh, bl = _split(b)
    return d(ah, bh) + (d(ah, bl) + d(al, bh))


def _lane_iota(shape):
    return lax.broadcasted_iota(I32, shape, len(shape) - 1)


def _row_iota(shape):
    return lax.broadcasted_iota(I32, shape, len(shape) - 2)


def _half_norm(blk, gain):
    lane = _lane_iota(blk.shape)
    lo = lane < A_HEAD_DIM
    sq = blk * blk
    s_lo = jnp.sum(jnp.where(lo, sq, 0.0), axis=-1, keepdims=True)
    s_hi = jnp.sum(jnp.where(lo, 0.0, sq), axis=-1, keepdims=True)
    r_lo = lax.rsqrt(s_lo * (1.0 / A_HEAD_DIM) + EPS)
    r_hi = lax.rsqrt(s_hi * (1.0 / A_HEAD_DIM) + EPS)
    return blk * jnp.where(lo, r_lo, r_hi) * gain


def _proj_kernel(x_ref, gmix_ref, w_ref, gq_ref, gk_ref, gik_ref,
                 q_ref, k_ref, kdup_ref, v_ref, vb_ref, qi_ref, misc_ref, kidup_ref,
                 dqkv_ref, dz_ref, ga_ref, gb_ref):
    x = x_ref[...]
    h = x * lax.rsqrt(jnp.mean(x * x, axis=-1, keepdims=True) + EPS) * gmix_ref[...]
    hb = h.astype(BF16)

    def seg(a, b):
        return _dot(hb, w_ref[:, a:b])

    lane = _lane_iota((x.shape[0], LANES))
    lo = lane < A_HEAD_DIM

    zq = seg(SEG_Q, SEG_K)
    for c in range(A_Q // LANES):
        blk = _half_norm(zq[:, c * LANES:(c + 1) * LANES], gq_ref[...])
        q_ref[:, c * LANES:(c + 1) * LANES] = (blk * Q_SCALE).astype(BF16)

    zk = seg(SEG_K, SEG_V)
    for c in range(A_KV // LANES):
        blk = _half_norm(zk[:, c * LANES:(c + 1) * LANES], gk_ref[...])
        k_ref[:, c * LANES:(c + 1) * LANES] = blk
        rolled = pltpu.roll(blk, A_HEAD_DIM, axis=1)
        kdup_ref[:, (2 * c) * LANES:(2 * c + 1) * LANES] = jnp.where(lo, blk, rolled).astype(BF16)
        kdup_ref[:, (2 * c + 1) * LANES:(2 * c + 2) * LANES] = jnp.where(lo, rolled, blk).astype(BF16)

    zv = seg(SEG_V, SEG_QI)
    v_ref[...] = zv
    vb_ref[...] = zv.astype(BF16)

    qi_ref[...] = seg(SEG_QI, SEG_MISC).astype(BF16)

    zm = seg(SEG_MISC, SEG_DQKV)
    s_ik = jnp.sum(jnp.where(lo, zm * zm, 0.0), axis=-1, keepdims=True)
    kin = zm * lax.rsqrt(s_ik * (1.0 / IDX_DIM) + EPS) * gik_ref[...]
    wscale = (IDX_HEADS * IDX_DIM) ** -0.5
    misc_ref[...] = jnp.where(lo, kin, jnp.where(lane < MISC_DB, zm * wscale, zm))
    kin0 = jnp.where(lo, kin, 0.0)
    kidup_ref[...] = (kin0 + pltpu.roll(kin0, IDX_DIM, axis=1)).astype(BF16)

    dqkv_ref[...] = seg(SEG_DQKV, SEG_DZ)
    dz_ref[...] = seg(SEG_DZ, SEG_GA)
    ga_ref[...] = seg(SEG_GA, SEG_GB)
    gb_ref[...] = seg(SEG_GB, SEG_END)


def _permute_w_in(w_in):
    cuts = np.cumsum((0,) + IN_SPLITS)
    parts = [w_in[:, cuts[i]:cuts[i + 1]] for i in range(len(IN_SPLITS))]
    aq, ak, av, iq, ik, iw, dqkv, dz, db, da, ga, gb = parts
    pad = jnp.zeros((w_in.shape[0], LANES - IDX_DIM - IDX_HEADS - 2 * DN_HEADS), w_in.dtype)
    return jnp.concatenate([aq, ak, av, iq, ik, iw, db, da, pad, dqkv, dz, ga, gb], axis=1).astype(BF16)


def _tile2(g):
    return jnp.concatenate([g, g]).reshape(1, LANES).astype(F32)


def _proj(x2d, g_mix, w_perm, g_q, g_k, g_idx_k, tm):
    n = x2d.shape[0]
    assert n % tm == 0
    widths = [(A_Q, BF16), (A_KV, F32), (2 * A_KV, BF16), (A_KV, F32), (A_KV, BF16), (IDX_HEADS * IDX_DIM, BF16),
              (LANES, F32), (LANES, BF16), (DN_CONV_CH, F32), (DN_V, F32), (D_MODEL, F32), (D_MODEL, F32)]
    row = lambda i: (i, 0)
    fixed = lambda i: (0, 0)
    return pl.pallas_call(
        _proj_kernel,
        grid=(n // tm,),
        in_specs=[pl.BlockSpec((tm, D_MODEL), row), pl.BlockSpec((1, D_MODEL), fixed),
                  pl.BlockSpec((D_MODEL, SEG_END), fixed), pl.BlockSpec((1, LANES), fixed),
                  pl.BlockSpec((1, LANES), fixed), pl.BlockSpec((1, LANES), fixed)],
        out_specs=[pl.BlockSpec((tm, w), row) for w, _ in widths],
        out_shape=[jax.ShapeDtypeStruct((n, w), dt) for w, dt in widths],
        compiler_params=_cparams(("parallel",)),
        name="proj_in",
    )(x2d, g_mix.reshape(1, D_MODEL), w_perm, _tile2(g_q), _tile2(g_k), _tile2(g_idx_k))


def _sort_key(score):
    score = jnp.where(score == 0.0, 0.0, score)
    bits = pltpu.bitcast(score, I32)
    return jnp.where(bits < 0, bits ^ jnp.int32(0x7FFFFFFF), bits)


def _index_scores(qi, wi_tile, kblk):
    lane = _lane_iota((qi.shape[0], LANES))
    lo = lane < IDX_DIM
    sc = None
    for c in range(IDX_HEADS // 2):
        q128 = qi[:, c * LANES:(c + 1) * LANES]
        zero = jnp.zeros_like(q128)
        for half in range(2):
            qm = jnp.where(lo, q128, zero) if half == 0 else jnp.where(lo, zero, q128)
            s = _dot_nt(qm, kblk)
            hidx = MISC_WI + 2 * c + half
            term = jnp.maximum(s, 0.0) * wi_tile[:, hidx:hidx + 1]
            sc = term if sc is None else sc + term
    return sc


def _select_rows(keys_ref, thr_ref, jcut_ref, cand_ref, nvalid, k_sel, tq, tk, rb, live_rows=None, khi_ref=None,
                 candh_ref=None):
    nchunk = tk // LANES
    nrb = tq // rb
    nbits_idx = int(math.ceil(math.log2(keys_ref.shape[0] * tk))) + 1
    assert keys_ref.shape[0] * nchunk <= 256
    lane = _lane_iota((rb, LANES))
    ones = jnp.ones((LANES, LANES), BF16)
    rows = lambda r: slice(r * rb, (r + 1) * rb)

    def count(pred, src_ref=keys_ref, cnd_ref=cand_ref, dtype=F32):
        def body(kb, cnts):
            out = []
            for r in range(nrb):
                blk = src_ref[kb, rows(r), :]
                cand_b = cnd_ref[rows(r), :]
                cnt = cnts[r]
                for c in range(nchunk):
                    cnt = cnt + pred(blk[:, c * LANES:(c + 1) * LANES], cand_b, kb * tk + c * LANES, r)
                out.append(cnt)
            return tuple(out)

        zeros = tuple(jnp.zeros((rb, LANES), dtype) for _ in range(nrb))
        cnts = lax.fori_loop(0, nvalid, body, zeros)
        return [_dot(cnt.astype(BF16), ones) for cnt in cnts]

    def search_bits(nbits, count_ge, to_cand):
        def step(i, carry):
            cand = thr_ref[...] + lax.shift_left(jnp.int32(1), nbits - 1 - i)
            to_cand(cand)
            tots = count_ge()
            for r in range(nrb):
                thr_ref[rows(r), :] = jnp.where(tots[r] >= k_sel, cand[rows(r)], thr_ref[rows(r), :])
            return carry
        lax.fori_loop(0, nbits, step, 0)

    def store_cand(cand):
        cand_ref[...] = cand

    ge_i32 = lambda: count(lambda kv, cb, base, r: jnp.where(kv >= cb, 1.0, 0.0))
    if khi_ref is None:
        thr_ref[...] = jnp.full((tq, LANES), INT_MIN, I32)
        search_bits(32, ge_i32, store_cand)
    else:
        half = 16

        def store_cand_hi(cand):
            bits = jnp.left_shift(jnp.where(cand < 0, cand ^ jnp.int32(0x7FFF), cand), half)
            val = pltpu.bitcast(bits, F32)
            val = jnp.where(cand < NEG_INF_HI_KEY, -jnp.inf, val)
            val = jnp.where(cand > 0, jnp.maximum(val, MIN_NORMAL), val)
            candh_ref[...] = val.astype(BF16)

        one, zero = jnp.ones((), BF16), jnp.zeros((), BF16)
        ge_bf16 = lambda: count(lambda kv, cb, base, r: jnp.where(kv >= cb, one, zero), khi_ref, candh_ref, BF16)
        thr_ref[...] = jnp.full((tq, LANES), -(2 ** (half - 1)), I32)
        search_bits(half, ge_bf16, store_cand_hi)
        thr_ref[...] = jnp.left_shift(thr_ref[...], half)
        search_bits(half, ge_i32, store_cand)

    cand_ref[...] = thr_ref[...]
    c_gt = count(lambda kv, cb, base, r: jnp.where(kv > cb, 1.0, 0.0))
    c_ge = count(lambda kv, cb, base, r: jnp.where(kv >= cb, 1.0, 0.0))
    need = [k_sel - c for c in c_gt]
    jcut_ref[...] = jnp.full((tq, LANES), NO_TIE_CUT, I32)

    if live_rows is not None:
        c_ge = [jnp.where(_row_iota((rb, LANES)) % SUBLANES < live_rows, c, 0.0) for c in c_ge]
    most = c_ge[0]
    for c in c_ge[1:]:
        most = jnp.maximum(most, c)

    @pl.when(jnp.max(most) > k_sel)
    def _():
        jcut_ref[...] = jnp.zeros((tq, LANES), I32)

        def jstep(i, carry):
            candj = jcut_ref[...] + lax.shift_left(jnp.int32(1), nbits_idx - 1 - i)
            cand_ref[...] = candj
            f = count(lambda kv, cb, base, r: jnp.where(kv == thr_ref[rows(r), :], jnp.where(lane + base < cb, 1.0, 0.0), 0.0))
            for r in range(nrb):
                jcut_ref[rows(r), :] = jnp.where(f[r] < need[r], candj[rows(r)], jcut_ref[rows(r), :])
            return carry

        lax.fori_loop(0, nbits_idx, jstep, 0)


def _idx_mask_kernel(qi_ref, wi_ref, kidup_ref, mask_ref, keys_ref, thr_ref, jcut_ref, cand_ref, khi_ref, candh_ref, *,
                     k_sel, tq, tk, rb):
    i = pl.program_id(1)
    nkb = keys_ref.shape[0]
    nvalid = ((i + 1) * tq + tk - 1) // tk
    qi = qi_ref[0]
    wi_tile = wi_ref[0]
    q_pos = i * tq + _row_iota((tq, tk))
    lane = _lane_iota((tq, tk))

    def fill(kb, carry):
        kblk = kidup_ref[0, pl.ds(pl.multiple_of(kb * tk, tk), tk), :]
        sc = _index_scores(qi, wi_tile, kblk)
        sc = jnp.where(lane + kb * tk <= q_pos, sc, -jnp.inf)
        sc = jnp.where(sc == 0.0, 0.0, sc)
        keys_ref[kb] = _sort_key(sc)
        khi_ref[kb] = pltpu.bitcast(pltpu.bitcast(sc, I32) & HI_HALF, F32).astype(BF16)
        return carry

    lax.fori_loop(0, nvalid, fill, 0)
    _select_rows(keys_ref, thr_ref, jcut_ref, cand_ref, nvalid, k_sel, tq, tk, rb, khi_ref=khi_ref, candh_ref=candh_ref)

    for kb in range(nkb):
        @pl.when(kb < nvalid)
        def _():
            thr = jnp.broadcast_to(thr_ref[:, 0:1], (tq, tk))
            jcut = jnp.broadcast_to(jcut_ref[:, 0:1], (tq, tk))
            key = keys_ref[kb]
            pos = lane + kb * tk
            sel = jnp.where(key > thr, 1, jnp.where(key == thr, jnp.where(pos <= jcut, 1, 0), 0))
            sel = jnp.where(pos <= q_pos, sel, 0)
            mask_ref[0, :, kb * tk:(kb + 1) * tk] = sel.astype(jnp.int8)

        @pl.when(kb >= nvalid)
        def _():
            mask_ref[0, :, kb * tk:(kb + 1) * tk] = jnp.zeros((tq, tk), jnp.int8)


def _idx_mask(qi, misc, kidup, k_sel, tq, tk):
    b, s, _ = qi.shape
    assert s % tq == 0 and s % tk == 0 and tq % 32 == 0
    rb = min(SEARCH_ROWS, tq)
    kern = functools.partial(_idx_mask_kernel, k_sel=k_sel, tq=tq, tk=tk, rb=rb)
    return pl.pallas_call(
        kern,
        grid=(b, s // tq),
        in_specs=[pl.BlockSpec((1, tq, IDX_HEADS * IDX_DIM), lambda bi, i: (bi, i, 0)),
                  pl.BlockSpec((1, tq, LANES), lambda bi, i: (bi, i, 0)),
                  pl.BlockSpec((1, s, LANES), lambda bi, i: (bi, 0, 0))],
        out_specs=pl.BlockSpec((1, tq, s), lambda bi, i: (bi, i, 0)),
        out_shape=jax.ShapeDtypeStruct((b, s, s), jnp.int8),
        scratch_shapes=[pltpu.VMEM((s // tk, tq, tk), I32)] + [pltpu.VMEM((tq, LANES), I32)] * 3
                       + [pltpu.VMEM((s // tk, tq, tk), BF16), pltpu.VMEM((tq, LANES), BF16)],
        compiler_params=_cparams(("parallel", "parallel")),
        name="idx_mask",
    )(qi, misc, kidup)


def _attn_kernel(q_ref, kdup_ref, vb_ref, mask_ref, o_ref, m_ref, l_ref, acc_ref, *, tq, tk):
    i = pl.program_id(1)
    kb = pl.program_id(2)
    nk = pl.num_programs(2)
    last_needed = ((i + 1) * tq - 1) // tk

    @pl.when(kb == 0)
    def _():
        m_ref[...] = jnp.full(m_ref.shape, NEG_BIG, F32)
        l_ref[...] = jnp.zeros(l_ref.shape, F32)
        acc_ref[...] = jnp.zeros(acc_ref.shape, F32)

    @pl.when(kb <= last_needed)
    def _():
        bias = jnp.where(mask_ref[0].astype(I32) != 0, 0.0, NEG_BIG).astype(F32)
        bias2 = jnp.concatenate([bias, bias], axis=0)
        lane = _lane_iota((tq, LANES))
        lo = lane < A_HEAD_DIM

        def qk(g):
            q128 = q_ref[0, :, g * LANES:(g + 1) * LANES]
            zero = jnp.zeros_like(q128)
            q2 = jnp.concatenate([jnp.where(lo, q128, zero), jnp.where(lo, zero, q128)], axis=0)
            return _dot_nt(q2, kdup_ref[0, :, g * LANES:(g + 1) * LANES])

        s_next = qk(0)
        for g in range(A_KV_HEADS):
            s = s_next + bias2
            if g + 1 < A_KV_HEADS:
                s_next = qk(g + 1)
            m_old = m_ref[g]
            m_new = jnp.maximum(m_old, jnp.max(s, axis=1, keepdims=True))
            alpha = jnp.exp2(m_old - m_new)
            p = jnp.exp2(s - m_new[:, 0:1])
            l_ref[g] = alpha * l_ref[g] + jnp.sum(p, axis=1, keepdims=True)
            v128 = vb_ref[0, :, (g // 2) * LANES:(g // 2 + 1) * LANES]
            acc_ref[g] = alpha * acc_ref[g] + _dot(p.astype(BF16), v128)
            m_ref[g] = m_new

    @pl.when(kb == nk - 1)
    def _():
        lane = _lane_iota((tq, LANES))
        lo = lane < A_HEAD_DIM
        for g in range(A_KV_HEADS):
            a = acc_ref[g] / l_ref[g]
            top, bot = a[:tq], a[tq:]
            if g % 2 == 0:
                o128 = jnp.where(lo, top, pltpu.roll(bot, A_HEAD_DIM, axis=1))
            else:
                o128 = jnp.where(lo, pltpu.roll(top, A_HEAD_DIM, axis=1), bot)
            o_ref[0, :, g * LANES:(g + 1) * LANES] = o128


def _attn(q, kdup, vb, mask, tq, tk):
    b, s, _ = q.shape
    nk = s // tk

    def kv_idx(bi, i, kb):
        return (bi, jnp.minimum(kb, ((i + 1) * tq - 1) // tk), 0)

    def mask_idx(bi, i, kb):
        return (bi, i, jnp.minimum(kb, ((i + 1) * tq - 1) // tk))

    kern = functools.partial(_attn_kernel, tq=tq, tk=tk)
    return pl.pallas_call(
        kern,
        grid=(b, s // tq, nk),
        in_specs=[pl.BlockSpec((1, tq, A_Q), lambda bi, i, kb: (bi, i, 0)),
                  pl.BlockSpec((1, tk, 2 * A_KV), kv_idx),
                  pl.BlockSpec((1, tk, A_KV), kv_idx),
                  pl.BlockSpec((1, tq, tk), mask_idx)],
        out_specs=pl.BlockSpec((1, tq, A_Q), lambda bi, i, kb: (bi, i, 0)),
        out_shape=jax.ShapeDtypeStruct((b, s, A_Q), F32),
        scratch_shapes=[pltpu.VMEM((A_KV_HEADS, 2 * tq, LANES), F32)] * 3,
        compiler_params=_cparams(("parallel", "parallel", "arbitrary")),
        name="sel_attn",
    )(q, kdup, vb, mask)


def _conv_kernel(x_ref, xprev_ref, halo0_ref, w_ref, q_ref, k_ref, v_ref, *, tt):
    i = pl.program_id(1)
    x = x_ref[0]
    halo = jnp.where(i == 0, halo0_ref[0], xprev_ref[0])
    w = w_ref[...]

    def post(y, rows):
        y = y * jax.nn.sigmoid(y)
        for h in range(DN_HEADS):
            for j, ref in enumerate((q_ref, k_ref)):
                blk = y[:, j * DN_QK + h * DN_DK: j * DN_QK + (h + 1) * DN_DK]
                blk = blk * lax.rsqrt(jnp.sum(blk * blk, axis=-1, keepdims=True) + EPS)
                ref[0, 0:rows, h * DN_DK:(h + 1) * DN_DK] = blk
        v_ref[0, 0:rows, :] = y[:, 2 * DN_QK:]

    y = x * w[CONV_W - 1:CONV_W, :]
    for j in range(1, CONV_W):
        y = y + pltpu.roll(x, j, axis=0) * w[CONV_W - 1 - j:CONV_W - j, :]
    post(y, tt)

    x8 = x[0:SUBLANES]
    row = _row_iota(x8.shape)
    y8 = x8 * w[CONV_W - 1:CONV_W, :]
    for j in range(1, CONV_W):
        xs = jnp.where(row < j, pltpu.roll(halo, j, axis=0), pltpu.roll(x8, j, axis=0))
        y8 = y8 + xs * w[CONV_W - 1 - j:CONV_W - j, :]
    post(y8, SUBLANES)


def _conv(dqkv, halo0, conv_w, tt):
    b, t, c = dqkv.shape
    assert t % tt == 0 and tt % SUBLANES == 0
    r = tt // SUBLANES
    kern = functools.partial(_conv_kernel, tt=tt)
    out = jax.ShapeDtypeStruct((b, t, DN_QK), F32)
    return pl.pallas_call(
        kern,
        grid=(b, t // tt),
        in_specs=[pl.BlockSpec((1, tt, c), lambda bi, i: (bi, i, 0)),
                  pl.BlockSpec((1, SUBLANES, c), lambda bi, i: (bi, jnp.maximum(i * r - 1, 0), 0)),
                  pl.BlockSpec((1, SUBLANES, c), lambda bi, i: (bi, 0, 0)),
                  pl.BlockSpec((CONV_W, c), lambda bi, i: (0, 0))],
        out_specs=[pl.BlockSpec((1, tt, DN_QK), lambda bi, i: (bi, i, 0))] * 3,
        out_shape=[out, out, out],
        compiler_params=_cparams(("parallel", "parallel")),
        name="dn_conv",
    )(dqkv, dqkv, halo0, conv_w)


def _softplus(x):
    return jnp.maximum(x, 0.0) + jnp.log(1.0 + jnp.exp(-jnp.abs(x)))


def _pad_rows(a, rows):
    if a.shape[0] == rows:
        return a
    return jnp.concatenate([a, jnp.zeros((rows - a.shape[0], a.shape[1]), a.dtype)], axis=0)


DELTA_SEQS_PER_STEP = 4


def _delta_kernel(q_ref, k_ref, v_ref, z_ref, misc_ref, nega_ref, dtb_ref, gout_ref, s0_ref, o_ref, st_ref, *, c, t_valid, bb):
    n = pl.program_id(1)

    @pl.when(n == 0)
    def _():
        st_ref[...] = s0_ref[...]

    row = _row_iota((c, LANES))
    live = row < t_valid
    ri = lax.broadcasted_iota(I32, (c, c), 0)
    ci = lax.broadcasted_iota(I32, (c, c), 1)
    incl = ri >= ci
    strict = ri > ci
    eye = jnp.where(ri == ci, 1.0, 0.0).astype(F32)
    scale = DN_DK ** -0.5

    chains = [(bi, h) for bi in range(bb) for h in range(DN_HEADS)]
    nch = range(len(chains))
    beta_t, gc_t, gc_tr = [], [], []
    for bi in range(bb):
        mt = misc_ref[bi]
        beta_t.append(jnp.where(live, jax.nn.sigmoid(mt), 0.0))
        g = jnp.where(live, nega_ref[...] * _softplus(mt + dtb_ref[...]), 0.0)
        sft = 1
        while sft < c:
            g = g + jnp.where(row >= sft, pltpu.roll(g, sft, axis=0), 0.0)
            sft *= 2
        gc_t.append(g)
        gc_tr.append(_pad_rows(g, LANES).T)

    sl = lambda h: slice(h * DN_DK, (h + 1) * DN_DK)
    q = [jnp.where(live, q_ref[bi, :, sl(h)] * scale, 0.0) for bi, h in chains]
    k = [jnp.where(live, k_ref[bi, :, sl(h)], 0.0) for bi, h in chains]
    v = [jnp.where(live, v_ref[bi, :, sl(h)], 0.0) for bi, h in chains]
    beta = [beta_t[bi][:, MISC_DB + h:MISC_DB + h + 1] for bi, h in chains]
    gcc = [gc_t[bi][:, MISC_DA + h:MISC_DA + h + 1] for bi, h in chains]
    gcr = [gc_tr[bi][MISC_DA + h:MISC_DA + h + 1, 0:c] for bi, h in chains]
    decay = [jnp.where(incl, jnp.exp(jnp.where(incl, gcc[i] - gcr[i], 0.0)), 0.0) for i in nch]
    kb = [k[i] * beta[i] for i in nch]
    pw = [-jnp.where(strict, _dot3(kb[i], k[i], nt=True) * decay[i], 0.0) for i in nch]
    r = [eye + pw[i] for i in nch]
    nn = 2
    while nn < c:
        pw = [_dot3(pw[i], pw[i]) for i in nch]
        r = [_dot3(r[i], eye + pw[i]) for i in nch]
        nn *= 2
    egc = [jnp.exp(gcc[i]) for i in nch]
    u = [_dot3(r[i], v[i] * beta[i]) for i in nch]
    w = [_dot3(r[i], kb[i] * egc[i]) for i in nch]
    a_qk = [_dot3(q[i], k[i], nt=True) * decay[i] for i in nch]
    s = [st_ref[bi, h] for bi, h in chains]
    v_new = [u[i] - _dot3(w[i], s[i]) for i in nch]
    o = [_dot3(q[i] * egc[i], s[i]) + _dot3(a_qk[i], v_new[i]) for i in nch]
    glast = [gcc[i][c - 1:c, :] for i in nch]
    kw_t = [_pad_rows(k[i] * jnp.exp(glast[i] - gcc[i]), LANES).T for i in nch]
    s_new = [s[i] * jnp.exp(glast[i]) + _dot3(kw_t[i], _pad_rows(v_new[i], LANES)) for i in nch]
    for i, (bi, h) in enumerate(chains):
        st_ref[bi, h] = s_new[i]
        on = o[i] * lax.rsqrt(jnp.mean(o[i] * o[i], axis=-1, keepdims=True) + EPS) * gout_ref[...]
        z = z_ref[bi, :, sl(h)]
        o_ref[bi, :, sl(h)] = on * (z * jax.nn.sigmoid(z))


def _delta(qn, kn, v, dz, misc, a_log, dt_bias, g_dn_out, s0, c, t_valid):
    b, t, _ = qn.shape
    assert t % c == 0 and (t_valid == t or t == c)
    nega = jnp.zeros((1, LANES), F32).at[0, MISC_DA:MISC_DA + DN_HEADS].set(-jnp.exp(a_log.astype(F32)))
    dtb = jnp.zeros((1, LANES), F32).at[0, MISC_DA:MISC_DA + DN_HEADS].set(dt_bias.astype(F32))
    tok = lambda bi, n: (bi, n, 0)
    fixed = lambda bi, n: (0, 0)
    st = lambda bi, n: (bi, 0, 0, 0)
    bb = DELTA_SEQS_PER_STEP if b % DELTA_SEQS_PER_STEP == 0 else 1
    kern = functools.partial(_delta_kernel, c=c, t_valid=t_valid, bb=bb)
    return pl.pallas_call(
        kern,
        grid=(b // bb, t // c),
        in_specs=[pl.BlockSpec((bb, c, DN_QK), tok), pl.BlockSpec((bb, c, DN_QK), tok), pl.BlockSpec((bb, c, DN_V), tok),
                  pl.BlockSpec((bb, c, DN_V), tok), pl.BlockSpec((bb, c, LANES), tok),
                  pl.BlockSpec((1, LANES), fixed), pl.BlockSpec((1, LANES), fixed), pl.BlockSpec((1, DN_DV), fixed),
                  pl.BlockSpec((bb, DN_HEADS, DN_DK, DN_DV), st)],
        out_specs=[pl.BlockSpec((bb, c, DN_V), tok), pl.BlockSpec((bb, DN_HEADS, DN_DK, DN_DV), st)],
        out_shape=[jax.ShapeDtypeStruct((b, t, DN_V), F32), jax.ShapeDtypeStruct((b, DN_HEADS, DN_DK, DN_DV), F32)],
        compiler_params=_cparams(("parallel", "arbitrary")),
        name="delta_rule",
    )(qn, kn, v, dz, misc, nega, dtb, g_dn_out.reshape(1, DN_DV).astype(F32), s0)


def _merge_kernel(x_ref, oa_ref, ob_ref, ga_ref, gb_ref, woa_ref, wob_ref, wout_ref, gffn_ref, wpqt_ref,
                  x1_ref, h2_ref, qt_ref):
    ma = jax.nn.sigmoid(ga_ref[...]) * _dot(oa_ref[...].astype(BF16), woa_ref[...])
    mb = jax.nn.sigmoid(gb_ref[...]) * _dot(ob_ref[...].astype(BF16), wob_ref[...])
    x1 = x_ref[...] + _dot((ma + mb).astype(BF16), wout_ref[...])
    x1_ref[...] = x1
    h2 = x1 * lax.rsqrt(jnp.mean(x1 * x1, axis=-1, keepdims=True) + EPS) * gffn_ref[...]
    h2_ref[...] = h2
    qt_ref[...] = _dot_nt(wpqt_ref[...], h2.astype(BF16))


def _merge(x2d, oa, ob, ga, gb, w_oa, w_ob, w_out, g_ffn, w_pq, tm):
    n = x2d.shape[0]
    nq = w_pq.shape[1]
    row = lambda i: (i, 0)
    fixed = lambda i: (0, 0)
    return pl.pallas_call(
        _merge_kernel,
        grid=(n // tm,),
        in_specs=[pl.BlockSpec((tm, D_MODEL), row), pl.BlockSpec((tm, A_Q), row), pl.BlockSpec((tm, DN_V), row),
                  pl.BlockSpec((tm, D_MODEL), row), pl.BlockSpec((tm, D_MODEL), row),
                  pl.BlockSpec((A_Q, D_MODEL), fixed), pl.BlockSpec((DN_V, D_MODEL), fixed),
                  pl.BlockSpec((D_MODEL, D_MODEL), fixed), pl.BlockSpec((1, D_MODEL), fixed),
                  pl.BlockSpec((nq, D_MODEL), fixed)],
        out_specs=[pl.BlockSpec((tm, D_MODEL), row), pl.BlockSpec((tm, D_MODEL), row), pl.BlockSpec((nq, tm), lambda i: (0, i))],
        out_shape=[jax.ShapeDtypeStruct((n, D_MODEL), F32), jax.ShapeDtypeStruct((n, D_MODEL), F32),
                   jax.ShapeDtypeStruct((nq, n), F32)],
        compiler_params=_cparams(("parallel",)),
        name="merge_out",
    )(x2d, oa, ob, ga, gb, w_oa.astype(BF16), w_ob.astype(BF16), w_out.astype(BF16),
      g_ffn.reshape(1, D_MODEL).astype(F32), w_pq.T.astype(BF16))


def _peer_cands():
    cands = [(i, j) for i in range(PEER_TOPK) for j in range(PEER_TOPK) if (i + 1) * (j + 1) <= PEER_TOPK]
    return sorted(cands, key=lambda ij: ij[0] * PEER_TOPK + ij[1])


def _route_kernel(qt_ref, sk_ref, eid_ref, gate_ref, toff_ref, val_ref, idx_ref, *, tn):
    half = PEER_DKEY // 2
    key_iota = lax.broadcasted_iota(I32, (PEER_KEYS, tn), 0)

    heads_per_iter = 4

    def per_group(hg, carry):
        tabs = [(hh, c) for hh in range(heads_per_iter) for c in range(2)]
        ss = []
        for hh, c in tabs:
            t = 2 * (hg * heads_per_iter + hh) + c
            qblk = qt_ref[pl.ds(pl.multiple_of(t * half, half), half), :]
            ss.append(_dot3(sk_ref[t], qblk))
        for r in range(PEER_TOPK):
            ms = [jnp.max(s, axis=0, keepdims=True) for s in ss]
            ams = [jnp.min(jnp.where(s == m, key_iota, PEER_KEYS), axis=0, keepdims=True) for s, m in zip(ss, ms)]
            for i, (hh, c) in enumerate(tabs):
                h = hg * heads_per_iter + hh
                val_ref[c, r, pl.ds(h, 1), :] = ms[i]
                idx_ref[c, r, pl.ds(h, 1), :] = ams[i]
            ss = [jnp.where(key_iota == am, -jnp.inf, s) for s, am in zip(ss, ams)]
        return carry

    lax.fori_loop(0, PEER_HEADS // heads_per_iter, per_group, 0)

    cands = _peer_cands()
    cv = [val_ref[0, i] + val_ref[1, j] for i, j in cands]
    ce = [idx_ref[0, i] * PEER_KEYS + idx_ref[1, j] for i, j in cands]
    nc = len(cands)
    rank = []
    for a in range(nc):
        rk = jnp.zeros(cv[a].shape, I32)
        for b in range(nc):
            if b < a:
                rk = rk + jnp.where(cv[b] >= cv[a], 1, 0)
            elif b > a:
                rk = rk + jnp.where(cv[b] > cv[a], 1, 0)
        rank.append(rk)
    mx = cv[0]
    ex = [jnp.where(rank[a] < PEER_TOPK, jnp.exp(cv[a] - mx), 0.0) for a in range(nc)]
    den = ex[0]
    for a in range(1, nc):
        den = den + ex[a]
    inv = 1.0 / den
    e_slots = []
    for slot in range(PEER_TOPK):
        e = jnp.zeros(cv[0].shape, I32)
        g = jnp.zeros(cv[0].shape, F32)
        for a in range(nc):
            hit = rank[a] == slot
            e = jnp.where(hit, ce[a], e)
            g = jnp.where(hit, ex[a], g)
        eid_ref[slot] = e
        gate_ref[slot] = g * inv
        e_slots.append(e)
    toff_ref[...] = ((jnp.concatenate(e_slots, axis=0) & (PEER_HALF - 1)) * SUBLANES).T


def _route(qt, sub_keys, tn):
    nq, n = qt.shape
    sk = sub_keys.reshape(2 * PEER_HEADS, PEER_KEYS, PEER_DKEY // 2).astype(F32)
    kern = functools.partial(_route_kernel, tn=tn)
    blk = pl.BlockSpec((PEER_TOPK, PEER_HEADS, tn), lambda i: (0, 0, i))
    npair = PEER_TOPK * PEER_HEADS
    return pl.pallas_call(
        kern,
        grid=(n // tn,),
        in_specs=[pl.BlockSpec((nq, tn), lambda i: (0, i)),
                  pl.BlockSpec((2 * PEER_HEADS, PEER_KEYS, PEER_DKEY // 2), lambda i: (0, 0, 0))],
        out_specs=[blk, blk, pl.BlockSpec((tn, npair), lambda i: (i, 0))],
        out_shape=[jax.ShapeDtypeStruct((PEER_TOPK, PEER_HEADS, n), I32), jax.ShapeDtypeStruct((PEER_TOPK, PEER_HEADS, n), F32),
                   jax.ShapeDtypeStruct((n, npair), I32)],
        scratch_shapes=[pltpu.VMEM((2, PEER_TOPK, PEER_HEADS, tn), F32), pltpu.VMEM((2, PEER_TOPK, PEER_HEADS, tn), I32)],
        compiler_params=_cparams(("parallel",)),
        name="peer_route",
    )(qt, sk)


def _pack_table(tab):
    e, d = tab.shape
    assert d == SUBLANES * LANES
    bits = lax.bitcast_convert_type(tab.astype(BF16), jnp.uint16).astype(jnp.uint32)
    assert e == 2 * PEER_HALF
    words = (bits[:PEER_HALF] << 16) | bits[PEER_HALF:]
    return lax.bitcast_convert_type(words, I32).reshape(e // 2 * SUBLANES, LANES)


HI_HALF = -65536


def _bf16_words(x):
    return pltpu.bitcast(x.astype(BF16).astype(F32), I32) & HI_HALF


def _shr16(w):
    return lax.shift_right_logical(w, jnp.full(w.shape, 16, I32))


def _packed_mul(a_words, b_words):
    return pltpu.bitcast(pltpu.bitcast(a_words, BF16) * pltpu.bitcast(b_words, BF16), I32)


def _packed_add(a_words, b_words):
    return pltpu.bitcast(pltpu.bitcast(a_words, BF16) + pltpu.bitcast(b_words, BF16), I32)


def _hi_f32(w):
    return pltpu.bitcast(w & HI_HALF, F32)


def _lo_f32(w):
    return pltpu.bitcast(jnp.left_shift(w, 16), F32)


_BITREV8 = (0, 4, 2, 6, 1, 5, 3, 7)


PEER_U_SUB = 16


def _peer_u_kernel(toff_ref, x_ref, eid_ref, gate_ref, tab_ref, ce_ref, co_ref, rhi_ref, rlo_ref, *, tn, sub_t):
    npair = gate_ref.shape[0]
    lane = _lane_iota((npair, tn))
    sub = lax.broadcasted_iota(I32, (SUBLANES, LANES), 0)
    keep = {step: (sub & step) == 0 for step in (4, 2, 1)}

    def per_token(tt, t0):
        t = t0 + tt
        xw = _bf16_words(x_ref[t])
        xw = xw | _shr16(xw)
        for g in range(npair // SUBLANES):
            prods = []
            for p in _BITREV8:
                off = pl.multiple_of(toff_ref[t * npair + (g * SUBLANES + p)], SUBLANES)
                prods.append(_packed_mul(tab_ref[pl.ds(off, SUBLANES), :], xw))
            step = SUBLANES // 2
            while step >= 1:
                nxt = []
                for a in range(0, len(prods), 2):
                    lo_t, hi_t = prods[a], prods[a + 1]
                    left = jnp.where(keep[step], lo_t, pltpu.roll(hi_t, step, axis=0))
                    right = jnp.where(keep[step], pltpu.roll(lo_t, SUBLANES - step, axis=0), hi_t)
                    nxt.append(_packed_add(left, right))
                prods = nxt
                step //= 2
            rhi_ref[tt, g * SUBLANES:(g + 1) * SUBLANES, :] = _hi_f32(prods[0])
            rlo_ref[tt, g * SUBLANES:(g + 1) * SUBLANES, :] = _lo_f32(prods[0])
        return t0

    def per_sub(sb, accs):
        acc_hi, acc_lo = accs
        t0 = sb * sub_t
        lax.fori_loop(0, sub_t, per_token, t0)
        ones = jnp.ones((LANES, LANES), BF16)
        for tt in range(sub_t):
            hit = lane == t0 + tt
            acc_hi = jnp.where(hit, jnp.sum(rhi_ref[tt], axis=1, keepdims=True), acc_hi)
            acc_lo = jnp.where(hit, _dot(rlo_ref[tt].astype(BF16), ones)[:, 0:tn], acc_lo)
        return acc_hi, acc_lo

    zero = jnp.zeros((npair, tn), F32)
    acc_hi, acc_lo = lax.fori_loop(0, tn // sub_t, per_sub, (zero, zero))
    even = eid_ref[...] < PEER_HALF
    coef = gate_ref[...] * jax.nn.gelu(jnp.where(even, acc_hi, acc_lo))
    ce_ref[...] = jnp.where(even, coef, 0.0)
    co_ref[...] = jnp.where(even, 0.0, coef)


def _peer_v_kernel(toff_ref, ce_ref, co_ref, rep_ref, tab_ref, y_ref, bw_ref, *, tn, npair):
    ce = ce_ref[...].astype(BF16)
    co = co_ref[...].astype(BF16)
    chunk = min(16, tn)
    for c in range(tn // chunk):
        rep = rep_ref[:, c * chunk * LANES:(c + 1) * chunk * LANES]
        be = _dot(ce, rep)
        bo = _dot(co, rep)
        words = (pltpu.bitcast(be, I32) & HI_HALF) | _shr16(pltpu.bitcast(bo, I32))
        for t in range(chunk):
            bw_ref[c * chunk + t] = words[:, t * LANES:(t + 1) * LANES]

    nacc = 4

    def per_token(t, carry):
        zero = jnp.zeros((SUBLANES, LANES), F32)
        acc_hi = [zero] * nacc
        acc_lo = [zero] * nacc
        base = t * npair
        for k in range(npair):
            off = pl.multiple_of(toff_ref[base + k], SUBLANES)
            cw = jnp.broadcast_to(bw_ref[t, k:k + 1, :], (SUBLANES, LANES))
            prod = _packed_mul(tab_ref[pl.ds(off, SUBLANES), :], cw)
            acc_hi[k % nacc] = acc_hi[k % nacc] + _hi_f32(prod)
            acc_lo[k % nacc] = acc_lo[k % nacc] + _lo_f32(prod)
        y_ref[t] = ((acc_hi[0] + acc_hi[1]) + (acc_hi[2] + acc_hi[3])) + ((acc_lo[0] + acc_lo[1]) + (acc_lo[2] + acc_lo[3]))
        return carry

    lax.fori_loop(0, tn, per_token, 0)


def _peer_experts(eid, gate, toff, h2, tab_u, tab_v, tn):
    npair, n = eid.shape
    d = h2.shape[1]
    x3 = h2.reshape(n, d // LANES, LANES)
    tab_spec = pl.BlockSpec(tab_u.shape, lambda i: (0, 0), pipeline_mode=pl.Buffered(1))
    smem_blk = lambda tt: pl.BlockSpec((tt * npair,), lambda i: (i,), memory_space=pltpu.SMEM)
    slot_blk = lambda tt: pl.BlockSpec((npair, tt), lambda i: (0, i))
    sub_t = PEER_U_SUB if tn % PEER_U_SUB == 0 else tn
    ce, co = pl.pallas_call(
        functools.partial(_peer_u_kernel, tn=tn, sub_t=sub_t),
        grid=(n // tn,),
        in_specs=[smem_blk(tn), pl.BlockSpec((tn, d // LANES, LANES), lambda i: (i, 0, 0)), slot_blk(tn), slot_blk(tn), tab_spec],
        out_specs=[slot_blk(tn), slot_blk(tn)],
        out_shape=[jax.ShapeDtypeStruct((npair, n), F32)] * 2,
        scratch_shapes=[pltpu.VMEM((sub_t, npair, LANES), F32)] * 2,
        compiler_params=_cparams(("arbitrary",)),
        name="peer_u",
    )(toff.reshape(n * npair), x3, eid, gate, tab_u)
    tv = tn
    rep = jnp.asarray(np.arange(tv)[:, None] == np.arange(tv * LANES)[None, :] // LANES, BF16)
    y3 = pl.pallas_call(
        functools.partial(_peer_v_kernel, tn=tv, npair=npair),
        grid=(n // tv,),
        in_specs=[smem_blk(tv), slot_blk(tv), slot_blk(tv),
                  pl.BlockSpec(rep.shape, lambda i: (0, 0), pipeline_mode=pl.Buffered(1)), tab_spec],
        out_specs=pl.BlockSpec((tv, d // LANES, LANES), lambda i: (i, 0, 0)),
        out_shape=jax.ShapeDtypeStruct((n, d // LANES, LANES), F32),
        scratch_shapes=[pltpu.VMEM((tv, npair, LANES), I32)],
        compiler_params=_cparams(("arbitrary",)),
        name="peer_v",
    )(toff.reshape(n * npair), ce, co, rep, tab_v)
    return y3.reshape(n, d)


def _ple_kernel(x1_ref, yp_ref, p_ref, gple_ref, wg_ref, wp_ref, o_ref):
    x2 = x1_ref[...] + yp_ref[...]
    hn = x2 * lax.rsqrt(jnp.mean(x2 * x2, axis=-1, keepdims=True) + EPS) * gple_ref[...]
    gate = jax.nn.sigmoid(_dot(hn.astype(BF16), wg_ref[...]))
    o_ref[...] = x2 + gate * _dot(p_ref[...].astype(BF16), wp_ref[...])


def _ple(x1, yp, p2d, g_ple, w_ple_gate, w_ple, tm):
    n = x1.shape[0]
    row = lambda i: (i, 0)
    fixed = lambda i: (0, 0)
    return pl.pallas_call(
        _ple_kernel,
        grid=(n // tm,),
        in_specs=[pl.BlockSpec((tm, D_MODEL), row), pl.BlockSpec((tm, D_MODEL), row), pl.BlockSpec((tm, PLE_DIM), row),
                  pl.BlockSpec((1, D_MODEL), fixed), pl.BlockSpec((D_MODEL, D_MODEL), fixed),
                  pl.BlockSpec((PLE_DIM, D_MODEL), fixed)],
        out_specs=pl.BlockSpec((tm, D_MODEL), row),
        out_shape=jax.ShapeDtypeStruct((n, D_MODEL), F32),
        compiler_params=_cparams(("parallel",)),
        name="ple_out",
    )(x1, yp, p2d, g_ple.reshape(1, D_MODEL).astype(F32), w_ple_gate.astype(BF16), w_ple.astype(BF16))


PAGES_PER_STEP = 8
SAMPLE_SOFTMAX_STATES = 4


def _sample_scores_kernel(pt_ref, qh_ref, wrep_ref, kinew_ref, *refs, npg):
    pages = refs[:npg]
    sc_ref, scnew_ref = refs[npg:]
    j = pl.program_id(1)
    qh = qh_ref[0]
    wrep = wrep_ref[0]

    def raw(keys_t):
        return _dot(qh, keys_t.astype(BF16))

    def score(s):
        s = jnp.maximum(s, 0.0) * wrep
        tot = s[0:SUBLANES]
        for h in range(1, IDX_HEADS):
            tot = tot + s[h * SUBLANES:(h + 1) * SUBLANES]
        return tot

    dots = [raw(pages[p][0]) for p in range(npg)]
    for p in range(npg):
        sc_ref[0, :, p * PAGE_SIZE:(p + 1) * PAGE_SIZE] = score(dots[p])

    @pl.when(j == 0)
    def _():
        sn = score(raw(kinew_ref[0]))
        causal = _lane_iota((SUBLANES, PAGE_SIZE)) <= _row_iota((SUBLANES, PAGE_SIZE))
        scnew_ref[0] = jnp.where(causal, sn, -jnp.inf)


def _sample_thr_kernel(sc_ref, thr_ref, jcut_ref, keys_ref, cand_ref, *, k_sel, tq, tk, t):
    nkb = keys_ref.shape[0]
    for kb in range(nkb):
        keys_ref[kb] = _sort_key(sc_ref[:, :, kb * tk:(kb + 1) * tk].reshape(tq, tk))
    _select_rows(keys_ref, thr_ref, jcut_ref, cand_ref, nkb, k_sel, tq, tk, min(SEARCH_ROWS, tq), live_rows=t)


def _sample_attn_kernel(pt_ref, q_ref, sc_ref, scnew_ref, thr_ref, jcut_ref, knew_ref, vnew_ref, *refs, npg, past):
    kpages = refs[:npg]
    vpages = refs[npg:2 * npg]
    o_ref, m_ref, l_ref, acc_ref = refs[2 * npg:]
    j = pl.program_id(1)

    @pl.when(j == 0)
    def _():
        m_ref[...] = jnp.full(m_ref.shape, NEG_BIG, F32)
        l_ref[...] = jnp.zeros(l_ref.shape, F32)
        acc_ref[...] = jnp.zeros(acc_ref.shape, F32)

    q = q_ref[0]
    thr = thr_ref[...]
    jcut = jcut_ref[...]
    lane = _lane_iota((SUBLANES, PAGE_SIZE))
    kvd = A_KV_HEADS * A_HEAD_DIM

    wide = lambda a: jnp.concatenate([a] * (kvd // LANES), axis=1)

    def update(groups):
        ss = []
        for _, kts, _, scs, pos0s, causal in groups:
            biases = []
            for sc8, pos0 in zip(scs, pos0s):
                key = _sort_key(sc8)
                pos = lane + pos0
                sel = jnp.where(key > thr, 0.0, jnp.where(key == thr, jnp.where(pos <= jcut, 0.0, NEG_BIG), NEG_BIG))
                if causal is not None:
                    sel = jnp.where(causal, sel, NEG_BIG)
                biases.append(sel)
            bias = jnp.concatenate([jnp.concatenate(biases, axis=1)] * A_HEADS, axis=0)
            ss.append(jnp.concatenate([_dot(q, kt[...].reshape(kvd, PAGE_SIZE).astype(BF16)) for kt in kts], axis=1) + bias)
        m_olds = [m_ref[g[0]] for g in groups]
        m_news = [jnp.maximum(mo, jnp.max(s, axis=1, keepdims=True)) for mo, s in zip(m_olds, ss)]
        alphas = [jnp.exp2(mo - mn) for mo, mn in zip(m_olds, m_news)]
        ps = [jnp.exp2(s - mn[:, 0:1]) for s, mn in zip(ss, m_news)]
        sums = [jnp.sum(p, axis=1, keepdims=True) for p in ps]
        pvs = []
        for (_, _, vts, _, _, _), p in zip(groups, ps):
            pb = p.astype(BF16)
            parts = [_dot_nt(pb[:, i * PAGE_SIZE:(i + 1) * PAGE_SIZE], vt[...].reshape(kvd, PAGE_SIZE).astype(BF16))
                     for i, vt in enumerate(vts)]
            pv = parts[0]
            for part in parts[1:]:
                pv = pv + part
            pvs.append(pv)
        for g, mn, al, sm, pv in zip(groups, m_news, alphas, sums, pvs):
            ci = g[0]
            l_ref[ci] = al * l_ref[ci] + sm
            acc_ref[ci] = wide(al) * acc_ref[ci] + pv
            m_ref[ci] = mn

    nst = m_ref.shape[0]
    update([(ci, [kpages[p].at[0] for p in range(ci, npg, nst)], [vpages[p].at[0] for p in range(ci, npg, nst)],
             [sc_ref[0, :, p * PAGE_SIZE:(p + 1) * PAGE_SIZE] for p in range(ci, npg, nst)],
             [(j * npg + p) * PAGE_SIZE for p in range(ci, npg, nst)], None) for ci in range(nst)])

    @pl.when(j == pl.num_programs(1) - 1)
    def _():
        update([(0, [knew_ref.at[0]], [vnew_ref.at[0]], [scnew_ref[0]], [past], lane <= _row_iota((SUBLANES, PAGE_SIZE)))])
        m_all = m_ref[0]
        for ci in range(1, nst):
            m_all = jnp.maximum(m_all, m_ref[ci])
        l_all = jnp.zeros(m_all.shape, F32)
        acc_all = jnp.zeros(acc_ref.shape[1:], F32)
        for ci in range(nst):
            w = jnp.exp2(m_ref[ci] - m_all)
            l_all = l_all + w * l_ref[ci]
            acc_all = acc_all + wide(w) * acc_ref[ci]
        o_ref[0] = acc_all / wide(l_all)


def _head_major(a, bd, t, nh, hd):
    a = a.reshape(bd, t, nh, hd).transpose(0, 2, 1, 3)
    a = jnp.pad(a, ((0, 0), (0, 0), (0, SUBLANES - t), (0, 0)))
    return a.reshape(bd, nh * SUBLANES, hd)


def _attn_sample(q, k, v, qi, misc, cache_k, cache_v, cache_idx_k, page_table, bd, t):
    assert t <= SUBLANES
    n_pages = page_table.shape[1]
    past = n_pages * PAGE_SIZE
    npg = PAGES_PER_STEP if n_pages % PAGES_PER_STEP == 0 else 1
    nj = n_pages // npg
    k_sel = min(TOPK_MAX, (past + t) // 4)
    rows = A_HEADS * SUBLANES

    qh = _head_major(qi, bd, t, IDX_HEADS, IDX_DIM)
    wi = misc[:, MISC_WI:MISC_WI + IDX_HEADS].reshape(bd, t, IDX_HEADS).transpose(0, 2, 1)
    wrep = jnp.broadcast_to(jnp.pad(wi, ((0, 0), (0, 0), (0, SUBLANES - t))).reshape(bd, rows, 1), (bd, rows, LANES))
    pad_page = lambda a, w: jnp.pad(a.reshape(bd, t, w), ((0, 0), (0, PAGE_SIZE - t), (0, 0)))
    kinew = pad_page(misc[:, :IDX_DIM], IDX_DIM).transpose(0, 2, 1)
    cik = cache_idx_k.transpose(0, 2, 1)

    per_b = lambda b, j, pt: (b, 0, 0)
    per_b4 = lambda b, j, pt: (b, 0, 0, 0)
    page_spec = lambda shape, p: pl.BlockSpec(shape, lambda b, j, pt: (pt[b, j * npg + p],) + (0,) * (len(shape) - 1))
    sc, scnew = pl.pallas_call(
        functools.partial(_sample_scores_kernel, npg=npg),
        grid_spec=pltpu.PrefetchScalarGridSpec(
            num_scalar_prefetch=1, grid=(bd, nj),
            in_specs=[pl.BlockSpec((1, rows, IDX_DIM), per_b), pl.BlockSpec((1, rows, LANES), per_b),
                      pl.BlockSpec((1, IDX_DIM, PAGE_SIZE), per_b)]
                     + [page_spec((1, IDX_DIM, PAGE_SIZE), p) for p in range(npg)],
            out_specs=[pl.BlockSpec((1, SUBLANES, npg * PAGE_SIZE), lambda b, j, pt: (b, 0, j)),
                       pl.BlockSpec((1, SUBLANES, PAGE_SIZE), per_b)]),
        out_shape=[jax.ShapeDtypeStruct((bd, SUBLANES, past), F32), jax.ShapeDtypeStruct((bd, SUBLANES, PAGE_SIZE), F32)],
        compiler_params=_cparams(("parallel", "arbitrary")),
        name="sample_scores",
    )(page_table, qh, wrep, kinew, *([cik] * npg))

    length = past + PAGE_SIZE
    sc_all = jnp.concatenate([sc, scnew], axis=2)
    tb = SUBLANES if bd % SUBLANES == 0 else 1
    tq = tb * SUBLANES
    tk = 5 * LANES if length % (5 * LANES) == 0 else LANES
    thr, jcut = pl.pallas_call(
        functools.partial(_sample_thr_kernel, k_sel=k_sel, tq=tq, tk=tk, t=t),
        grid=(bd // tb,),
        in_specs=[pl.BlockSpec((tb, SUBLANES, length), lambda i: (i, 0, 0))],
        out_specs=[pl.BlockSpec((tq, LANES), lambda i: (i, 0))] * 2,
        out_shape=[jax.ShapeDtypeStruct((bd * SUBLANES, LANES), I32)] * 2,
        scratch_shapes=[pltpu.VMEM((length // tk, tq, tk), I32), pltpu.VMEM((tq, LANES), I32)],
        compiler_params=_cparams(("parallel",)),
        name="sample_thr",
    )(sc_all)

    qa = _head_major(q, bd, t, A_HEADS, A_HEAD_DIM)
    kv_of_row = (np.arange(rows) // SUBLANES) // A_REP
    own = jnp.asarray(kv_of_row[:, None] == np.arange(A_KV_HEADS)[None, :])
    qa = jnp.where(own[None, :, :, None], qa[:, :, None, :], jnp.zeros((), qa.dtype)).reshape(bd, rows, A_KV)
    to_page_t = lambda a: pad_page(a, A_KV).reshape(bd, PAGE_SIZE, A_KV_HEADS, A_HEAD_DIM).transpose(0, 2, 3, 1)
    knew, vnew = to_page_t(k), to_page_t(v)
    ck = cache_k.transpose(0, 2, 3, 1)
    cv = cache_v.transpose(0, 2, 3, 1)
    kv_blk = (1, A_KV_HEADS, A_HEAD_DIM, PAGE_SIZE)
    nst = SAMPLE_SOFTMAX_STATES if npg % SAMPLE_SOFTMAX_STATES == 0 else 1
    o = pl.pallas_call(
        functools.partial(_sample_attn_kernel, npg=npg, past=past),
        grid_spec=pltpu.PrefetchScalarGridSpec(
            num_scalar_prefetch=1, grid=(bd, nj),
            in_specs=[pl.BlockSpec((1, rows, A_KV), per_b),
                      pl.BlockSpec((1, SUBLANES, npg * PAGE_SIZE), lambda b, j, pt: (b, 0, j)),
                      pl.BlockSpec((1, SUBLANES, PAGE_SIZE), per_b),
                      pl.BlockSpec((SUBLANES, LANES), lambda b, j, pt: (b, 0)),
                      pl.BlockSpec((SUBLANES, LANES), lambda b, j, pt: (b, 0)),
                      pl.BlockSpec(kv_blk, per_b4), pl.BlockSpec(kv_blk, per_b4)]
                     + [page_spec(kv_blk, p) for p in range(npg)] * 2,
            out_specs=pl.BlockSpec((1, rows, A_KV), per_b),
            scratch_shapes=[pltpu.VMEM((nst, rows, LANES), F32), pltpu.VMEM((nst, rows, LANES), F32),
                            pltpu.VMEM((nst, rows, A_KV), F32)]),
        out_shape=jax.ShapeDtypeStruct((bd, rows, A_KV), F32),
        compiler_params=_cparams(("parallel", "arbitrary")),
        name="sample_attn",
    )(page_table, qa, sc, scnew, thr, jcut, knew, vnew, *([ck] * npg), *([cv] * npg))
    o = jnp.sum(jnp.where(own[None, :, :, None], o.reshape(bd, rows, A_KV_HEADS, A_HEAD_DIM), 0.0), axis=2)
    o = o.reshape(bd, A_HEADS, SUBLANES, A_HEAD_DIM)[:, :, :t].transpose(0, 2, 1, 3)
    return o.reshape(bd * t, A_Q)


def _tile(n, pref):
    return pref if n % pref == 0 else n


def _layer(x, p_emb, conv_state, delta_state, attn_fn, prm):
    b, t, _ = x.shape
    n = b * t
    x2d = x.reshape(n, D_MODEL)
    tm = _tile(n, ROW_TILE)
    (q, k, kdup, v, vb, qi, misc, kidup, dqkv, dz, ga, gb) = _proj(
        x2d, prm['g_mix'], prm['w_perm'], prm['g_q'], prm['g_k'], prm['g_idx_k'], tm)

    oa = attn_fn(q, k, kdup, v, vb, qi, misc, kidup)

    tp = -(-t // SUBLANES) * SUBLANES
    c = min(DN_CHUNK, tp)
    pad_t = lambda a: jnp.pad(a.reshape(b, t, a.shape[-1]), ((0, 0), (0, tp - t), (0, 0)))
    dqkv3 = dqkv.reshape(b, t, DN_CONV_CH)
    halo = jnp.zeros((b, SUBLANES, DN_CONV_CH), F32)
    if conv_state is not None:
        halo = halo.at[:, SUBLANES - (CONV_W - 1):].set(conv_state.astype(F32))
        hist = jnp.concatenate([conv_state.astype(F32), dqkv3], axis=1)
    else:
        hist = jnp.concatenate([jnp.zeros((b, CONV_W - 1, DN_CONV_CH), F32), dqkv3], axis=1)
    new_conv = hist[:, -(CONV_W - 1):]
    qn, kn, vv = _conv(pad_t(dqkv), halo, prm['conv_w'], _tile(tp, CONV_TT))
    s0 = jnp.zeros((b, DN_HEADS, DN_DK, DN_DV), F32) if delta_state is None else delta_state.astype(F32)
    ob, new_delta = _delta(qn, kn, vv, pad_t(dz), pad_t(misc), prm['a_log'], prm['dt_bias'], prm['g_dn_out'], s0, c, t)
    ob = ob[:, :t].reshape(n, DN_V)

    x1, h2, qt = _merge(x2d, oa, ob, ga, gb, prm['w_oa'], prm['w_ob'], prm['w_out'], prm['g_ffn'], prm['w_pq'], tm)
    eid, gate, toff = _route(qt, prm['sub_keys'], _tile(n, TOKEN_TILE))
    npair = PEER_TOPK * PEER_HEADS
    yp = _peer_experts(eid.reshape(npair, n), gate.reshape(npair, n), toff, h2, prm['tab_u'], prm['tab_v'], _tile(n, TOKEN_TILE))
    y = _ple(x1, yp, p_emb.reshape(n, PLE_DIM), prm['g_ple'], prm['w_ple_gate'], prm['w_ple'], tm)

    return (y.reshape(b, t, D_MODEL), k.reshape(b, t, A_KV_HEADS, A_HEAD_DIM), v.reshape(b, t, A_KV_HEADS, A_HEAD_DIM),
            misc[:, :IDX_DIM].reshape(b, t, IDX_DIM), new_conv, new_delta)


def kernel(x_prompt, x_sample, cache_k, cache_v, cache_idx_k, state_conv, state_delta, page_table, p_prompt, p_sample,
           g_mix, w_in, g_q, g_k, g_idx_k, conv_w, a_log, dt_bias, g_dn_out, w_oa, w_ob, w_out, g_ffn, w_pq, sub_keys,
           peer_u, peer_v, g_ple, w_ple_gate, w_ple):
    depth = w_in.shape[0]
    xp, xs = x_prompt, x_sample
    outs = [[] for _ in range(10)]
    for i in range(depth):
        prm = dict(g_mix=g_mix[i], w_perm=_permute_w_in(w_in[i]), g_q=g_q[i], g_k=g_k[i], g_idx_k=g_idx_k[i],
                   conv_w=conv_w[i].astype(F32), a_log=a_log[i], dt_bias=dt_bias[i], g_dn_out=g_dn_out[i],
                   w_oa=w_oa[i], w_ob=w_ob[i], w_out=w_out[i], g_ffn=g_ffn[i], w_pq=w_pq[i], sub_keys=sub_keys[i],
                   tab_u=_pack_table(peer_u[i]), tab_v=_pack_table(peer_v[i]), g_ple=g_ple[i],
                   w_ple_gate=w_ple_gate[i], w_ple=w_ple[i])

        bp, s, _ = xp.shape

        def attn_prompt(q, k, kdup, v, vb, qi, misc, kidup):
            k_sel = min(TOPK_MAX, s // 4)
            tq, tk = _tile(s, ATTN_TQ), _tile(s, ATTN_TK)
            r3 = lambda a: a.reshape(bp, s, a.shape[-1])
            mask = _idx_mask(r3(qi), r3(misc), r3(kidup), k_sel, tq, tk)
            return _attn(r3(q), r3(kdup), r3(vb), mask, tq, tk).reshape(bp * s, A_Q)

        res = _layer(xp, p_prompt[i], None, None, attn_prompt, prm)
        xp = res[0]
        for lst, val in zip(outs[:5], res[1:]):
            lst.append(val)

        bd, t, _ = xs.shape

        def attn_sample(q, k, kdup, v, vb, qi, misc, kidup):
            return _attn_sample(q, k, v, qi, misc, cache_k[i], cache_v[i], cache_idx_k[i], page_table, bd, t)

        res = _layer(xs, p_sample[i], state_conv[i], state_delta[i], attn_sample, prm)
        xs = res[0]
        for lst, val in zip(outs[5:], res[1:]):
            lst.append(val)

    cast = [cache_k.dtype, cache_v.dtype, cache_idx_k.dtype, state_conv.dtype, state_delta.dtype] * 2
    stacked = [jnp.stack(lst).astype(dt) for lst, dt in zip(outs, cast)]
    return (xp, xs, *stacked)
```

```python
import functools
import math

import jax
import jax.numpy as jnp
import numpy as np
from jax import lax
from jax.experimental import pallas as pl
from jax.experimental.pallas import tpu as pltpu

F32 = jnp.float32
BF16 = jnp.bfloat16
I32 = jnp.int32

D_MODEL = 1024
PAGE_SIZE = 128
A_HEADS = 8
A_KV_HEADS = 4
A_REP = A_HEADS // A_KV_HEADS
A_HEAD_DIM = 64
A_Q = A_HEADS * A_HEAD_DIM
A_KV = A_KV_HEADS * A_HEAD_DIM
IDX_HEADS = 8
IDX_DIM = 64
TOPK_MAX = 256
DN_HEADS = 4
DN_DK = 128
DN_DV = 128
DN_QK = DN_HEADS * DN_DK
DN_V = DN_HEADS * DN_DV
DN_CONV_CH = 2 * DN_QK + DN_V
CONV_W = 4
DN_CHUNK = 64
PEER_HEADS = 8
PEER_KEYS = 128
PEER_DKEY = 256
PEER_TOPK = 16
PEER_HALF = PEER_KEYS * PEER_KEYS // 2
PLE_DIM = 256
EPS = 1e-6
IN_SPLITS = (A_Q, A_KV, A_KV, IDX_HEADS * IDX_DIM, IDX_DIM, IDX_HEADS, DN_CONV_CH, DN_V, DN_HEADS, DN_HEADS, D_MODEL, D_MODEL)

LANES = 128
SUBLANES = 8
VMEM_LIMIT = 56 * 1024 * 1024

ROW_TILE = 256
ATTN_TQ = 256
ATTN_TK = 512
CONV_TT = 512
TOKEN_TILE = LANES
SEARCH_ROWS = 64
NO_TIE_CUT = 2 ** 30
NEG_INF_HI_KEY = (0xFF80 ^ 0x7FFF) - 2 ** 16
MIN_NORMAL = 2.0 ** -126
NEG_BIG = -1e30
Q_SCALE = A_HEAD_DIM ** -0.5 * math.log2(math.e)
INT_MIN = -(2 ** 31)

MISC_WI = IDX_DIM
MISC_DB = MISC_WI + IDX_HEADS
MISC_DA = MISC_DB + DN_HEADS

SEG_Q = 0
SEG_K = SEG_Q + A_Q
SEG_V = SEG_K + A_KV
SEG_QI = SEG_V + A_KV
SEG_MISC = SEG_QI + IDX_HEADS * IDX_DIM
SEG_DQKV = SEG_MISC + LANES
SEG_DZ = SEG_DQKV + DN_CONV_CH
SEG_GA = SEG_DZ + DN_V
SEG_GB = SEG_GA + D_MODEL
SEG_END = SEG_GB + D_MODEL


def _cparams(sem):
    return pltpu.CompilerParams(dimension_semantics=sem, vmem_limit_bytes=VMEM_LIMIT)


def _dot(a, b):
    return jnp.dot(a, b, preferred_element_type=F32)


def _dot_nt(a, b):
    return lax.dot_general(a, b, (((1,), (1,)), ((), ())), preferred_element_type=F32)


def _split(a):
    hi = a.astype(BF16)
    lo = (a - hi.astype(F32)).astype(BF16)
    return hi, lo


def _dot3(a, b, nt=False):
    d = _dot_nt if nt else _dot
    ah, al = _split(a)
    bh, bl = _split(b)
    return d(ah, bh) + (d(ah, bl) + d(al, bh))


def _lane_iota(shape):
    return lax.broadcasted_iota(I32, shape, len(shape) - 1)


def _row_iota(shape):
    return lax.broadcasted_iota(I32, shape, len(shape) - 2)


def _half_norm(blk, gain):
    lane = _lane_iota(blk.shape)
    lo = lane < A_HEAD_DIM
    sq = blk * blk
    s_lo = jnp.sum(jnp.where(lo, sq, 0.0), axis=-1, keepdims=True)
    s_hi = jnp.sum(jnp.where(lo, 0.0, sq), axis=-1, keepdims=True)
    r_lo = lax.rsqrt(s_lo * (1.0 / A_HEAD_DIM) + EPS)
    r_hi = lax.rsqrt(s_hi * (1.0 / A_HEAD_DIM) + EPS)
    return blk * jnp.where(lo, r_lo, r_hi) * gain


def _proj_kernel(x_ref, gmix_ref, w_ref, gq_ref, gk_ref, gik_ref,
                 q_ref, k_ref, kdup_ref, v_ref, vb_ref, qi_ref, misc_ref, kidup_ref,
                 dqkv_ref, dz_ref, ga_ref, gb_ref):
    x = x_ref[...]
    h = x * lax.rsqrt(jnp.mean(x * x, axis=-1, keepdims=True) + EPS) * gmix_ref[...]
    hb = h.astype(BF16)

    def seg(a, b):
        return _dot(hb, w_ref[:, a:b])

    lane = _lane_iota((x.shape[0], LANES))
    lo = lane < A_HEAD_DIM

    zq = seg(SEG_Q, SEG_K)
    for c in range(A_Q // LANES):
        blk = _half_norm(zq[:, c * LANES:(c + 1) * LANES], gq_ref[...])
        q_ref[:, c * LANES:(c + 1) * LANES] = (blk * Q_SCALE).astype(BF16)

    zk = seg(SEG_K, SEG_V)
    for c in range(A_KV // LANES):
        blk = _half_norm(zk[:, c * LANES:(c + 1) * LANES], gk_ref[...])
        k_ref[:, c * LANES:(c + 1) * LANES] = blk
        rolled = pltpu.roll(blk, A_HEAD_DIM, axis=1)
        kdup_ref[:, (2 * c) * LANES:(2 * c + 1) * LANES] = jnp.where(lo, blk, rolled).astype(BF16)
        kdup_ref[:, (2 * c + 1) * LANES:(2 * c + 2) * LANES] = jnp.where(lo, rolled, blk).astype(BF16)

    zv = seg(SEG_V, SEG_QI)
    v_ref[...] = zv
    vb_ref[...] = zv.astype(BF16)

    qi_ref[...] = seg(SEG_QI, SEG_MISC).astype(BF16)

    zm = seg(SEG_MISC, SEG_DQKV)
    s_ik = jnp.sum(jnp.where(lo, zm * zm, 0.0), axis=-1, keepdims=True)
    kin = zm * lax.rsqrt(s_ik * (1.0 / IDX_DIM) + EPS) * gik_ref[...]
    wscale = (IDX_HEADS * IDX_DIM) ** -0.5
    misc_ref[...] = jnp.where(lo, kin, jnp.where(lane < MISC_DB, zm * wscale, zm))
    kin0 = jnp.where(lo, kin, 0.0)
    kidup_ref[...] = (kin0 + pltpu.roll(kin0, IDX_DIM, axis=1)).astype(BF16)

    dqkv_ref[...] = seg(SEG_DQKV, SEG_DZ)
    dz_ref[...] = seg(SEG_DZ, SEG_GA)
    ga_ref[...] = seg(SEG_GA, SEG_GB)
    gb_ref[...] = seg(SEG_GB, SEG_END)


def _permute_w_in(w_in):
    cuts = np.cumsum((0,) + IN_SPLITS)
    parts = [w_in[:, cuts[i]:cuts[i + 1]] for i in range(len(IN_SPLITS))]
    aq, ak, av, iq, ik, iw, dqkv, dz, db, da, ga, gb = parts
    pad = jnp.zeros((w_in.shape[0], LANES - IDX_DIM - IDX_HEADS - 2 * DN_HEADS), w_in.dtype)
    return jnp.concatenate([aq, ak, av, iq, ik, iw, db, da, pad, dqkv, dz, ga, gb], axis=1).astype(BF16)


def _tile2(g):
    return jnp.concatenate([g, g]).reshape(1, LANES).astype(F32)


def _proj(x2d, g_mix, w_perm, g_q, g_k, g_idx_k, tm):
    n = x2d.shape[0]
    assert n % tm == 0
    widths = [(A_Q, BF16), (A_KV, F32), (2 * A_KV, BF16), (A_KV, F32), (A_KV, BF16), (IDX_HEADS * IDX_DIM, BF16),
              (LANES, F32), (LANES, BF16), (DN_CONV_CH, F32), (DN_V, F32), (D_MODEL, F32), (D_MODEL, F32)]
    row = lambda i: (i, 0)
    fixed = lambda i: (0, 0)
    return pl.pallas_call(
        _proj_kernel,
        grid=(n // tm,),
        in_specs=[pl.BlockSpec((tm, D_MODEL), row), pl.BlockSpec((1, D_MODEL), fixed),
                  pl.BlockSpec((D_MODEL, SEG_END), fixed), pl.BlockSpec((1, LANES), fixed),
                  pl.BlockSpec((1, LANES), fixed), pl.BlockSpec((1, LANES), fixed)],
        out_specs=[pl.BlockSpec((tm, w), row) for w, _ in widths],
        out_shape=[jax.ShapeDtypeStruct((n, w), dt) for w, dt in widths],
        compiler_params=_cparams(("parallel",)),
        name="proj_in",
    )(x2d, g_mix.reshape(1, D_MODEL), w_perm, _tile2(g_q), _tile2(g_k), _tile2(g_idx_k))


def _sort_key(score):
    score = jnp.where(score == 0.0, 0.0, score)
    bits = pltpu.bitcast(score, I32)
    return jnp.where(bits < 0, bits ^ jnp.int32(0x7FFFFFFF), bits)


def _index_scores(qi, wi_tile, kblk):
    lane = _lane_iota((qi.shape[0], LANES))
    lo = lane < IDX_DIM
    sc = None
    for c in range(IDX_HEADS // 2):
        q128 = qi[:, c * LANES:(c + 1) * LANES]
        zero = jnp.zeros_like(q128)
        for half in range(2):
            qm = jnp.where(lo, q128, zero) if half == 0 else jnp.where(lo, zero, q128)
            s = _dot_nt(qm, kblk)
            hidx = MISC_WI + 2 * c + half
            term = jnp.maximum(s, 0.0) * wi_tile[:, hidx:hidx + 1]
            sc = term if sc is None else sc + term
    return sc


def _select_rows(keys_ref, thr_ref, jcut_ref, cand_ref, nvalid, k_sel, tq, tk, rb, live_rows=None, khi_ref=None,
                 candh_ref=None):
    nchunk = tk // LANES
    nrb = tq // rb
    nbits_idx = int(math.ceil(math.log2(keys_ref.shape[0] * tk))) + 1
    assert keys_ref.shape[0] * nchunk <= 256
    lane = _lane_iota((rb, LANES))
    ones = jnp.ones((LANES, LANES), BF16)
    rows = lambda r: slice(r * rb, (r + 1) * rb)

    def count(pred, src_ref=keys_ref, cnd_ref=cand_ref, dtype=F32, blocks=tuple(range(nrb))):
        def body(kb, cnts):
            out = []
            for cnt, r in zip(cnts, blocks):
                blk = src_ref[kb, rows(r), :]
                cand_b = cnd_ref[rows(r), :]
                for c in range(nchunk):
                    cnt = cnt + pred(blk[:, c * LANES:(c + 1) * LANES], cand_b, kb * tk + c * LANES, r)
                out.append(cnt)
            return tuple(out)

        zeros = tuple(jnp.zeros((rb, LANES), dtype) for _ in blocks)
        cnts = lax.fori_loop(0, nvalid, body, zeros)
        return [_dot(cnt.astype(BF16), ones) for cnt in cnts]

    def search_bits(nbits, count_ge, to_cand):
        def step(i, carry):
            cand = thr_ref[...] + lax.shift_left(jnp.int32(1), nbits - 1 - i)
            to_cand(cand)
            tots = count_ge()
            for r in range(nrb):
                thr_ref[rows(r), :] = jnp.where(tots[r] >= k_sel, cand[rows(r)], thr_ref[rows(r), :])
            return carry
        lax.fori_loop(0, nbits, step, 0)

    def store_cand(cand):
        cand_ref[...] = cand

    ge_i32 = lambda: count(lambda kv, cb, base, r: jnp.where(kv >= cb, 1.0, 0.0))
    if khi_ref is None:
        thr_ref[...] = jnp.full((tq, LANES), INT_MIN, I32)
        search_bits(32, ge_i32, store_cand)
    else:
        half = 16

        def store_cand_hi(cand):
            bits = jnp.left_shift(jnp.where(cand < 0, cand ^ jnp.int32(0x7FFF), cand), half)
            val = pltpu.bitcast(bits, F32)
            val = jnp.where(cand < NEG_INF_HI_KEY, -jnp.inf, val)
            val = jnp.where(cand > 0, jnp.maximum(val, MIN_NORMAL), val)
            candh_ref[...] = val.astype(BF16)

        one, zero = jnp.ones((), BF16), jnp.zeros((), BF16)
        ge_bf16 = lambda: count(lambda kv, cb, base, r: jnp.where(kv >= cb, one, zero), khi_ref, candh_ref, BF16)
        thr_ref[...] = jnp.full((tq, LANES), -(2 ** (half - 1)), I32)
        search_bits(half, ge_bf16, store_cand_hi)
        thr_ref[...] = jnp.left_shift(thr_ref[...], half)
        search_bits(half, ge_i32, store_cand)

    cand_ref[...] = thr_ref[...]
    c_gt = count(lambda kv, cb, base, r: jnp.where(kv > cb, 1.0, 0.0))
    c_ge = count(lambda kv, cb, base, r: jnp.where(kv >= cb, 1.0, 0.0))
    need = [k_sel - c for c in c_gt]
    jcut_ref[...] = jnp.full((tq, LANES), NO_TIE_CUT, I32)

    if live_rows is not None:
        c_ge = [jnp.where(_row_iota((rb, LANES)) % SUBLANES < live_rows, c, 0.0) for c in c_ge]
    for r in range(nrb):
        @pl.when(jnp.max(c_ge[r]) > k_sel)
        def _(r=r):
            jcut_ref[rows(r), :] = jnp.zeros((rb, LANES), I32)

            def jstep(i, carry):
                candj = jcut_ref[rows(r), :] + lax.shift_left(jnp.int32(1), nbits_idx - 1 - i)
                cand_ref[rows(r), :] = candj
                f = count(lambda kv, cb, base, rr: jnp.where(kv == thr_ref[rows(rr), :],
                                                             jnp.where(lane + base < cb, 1.0, 0.0), 0.0), blocks=(r,))[0]
                jcut_ref[rows(r), :] = jnp.where(f < need[r], candj, jcut_ref[rows(r), :])
                return carry

            lax.fori_loop(0, nbits_idx, jstep, 0)


def _idx_mask_kernel(qi_ref, wi_ref, kidup_ref, mask_ref, keys_ref, thr_ref, jcut_ref, cand_ref, khi_ref, candh_ref, *,
                     k_sel, tq, tk, rb):
    i = pl.program_id(1)
    nkb = keys_ref.shape[0]
    nvalid = ((i + 1) * tq + tk - 1) // tk
    qi = qi_ref[0]
    wi_tile = wi_ref[0]
    q_pos = i * tq + _row_iota((tq, tk))
    lane = _lane_iota((tq, tk))

    def fill(kb, carry):
        kblk = kidup_ref[0, pl.ds(pl.multiple_of(kb * tk, tk), tk), :]
        sc = _index_scores(qi, wi_tile, kblk)
        sc = jnp.where(lane + kb * tk <= q_pos, sc, -jnp.inf)
        sc = jnp.where(sc == 0.0, 0.0, sc)
        keys_ref[kb] = _sort_key(sc)
        khi_ref[kb] = pltpu.bitcast(pltpu.bitcast(sc, I32) & HI_HALF, F32).astype(BF16)
        return carry

    lax.fori_loop(0, nvalid, fill, 0)
    _select_rows(keys_ref, thr_ref, jcut_ref, cand_ref, nvalid, k_sel, tq, tk, rb, khi_ref=khi_ref, candh_ref=candh_ref)

    for kb in range(nkb):
        @pl.when(kb < nvalid)
        def _():
            thr = jnp.broadcast_to(thr_ref[:, 0:1], (tq, tk))
            jcut = jnp.broadcast_to(jcut_ref[:, 0:1], (tq, tk))
            key = keys_ref[kb]
            pos = lane + kb * tk
            sel = jnp.where(key > thr, 1, jnp.where(key == thr, jnp.where(pos <= jcut, 1, 0), 0))
            sel = jnp.where(pos <= q_pos, sel, 0)
            mask_ref[0, :, kb * tk:(kb + 1) * tk] = sel.astype(jnp.int8)

        @pl.when(kb >= nvalid)
        def _():
            mask_ref[0, :, kb * tk:(kb + 1) * tk] = jnp.zeros((tq, tk), jnp.int8)


def _idx_mask(qi, misc, kidup, k_sel, tq, tk):
    b, s, _ = qi.shape
    assert s % tq == 0 and s % tk == 0 and tq % 32 == 0
    rb = min(SEARCH_ROWS, tq)
    kern = functools.partial(_idx_mask_kernel, k_sel=k_sel, tq=tq, tk=tk, rb=rb)
    return pl.pallas_call(
        kern,
        grid=(b, s // tq),
        in_specs=[pl.BlockSpec((1, tq, IDX_HEADS * IDX_DIM), lambda bi, i: (bi, i, 0)),
                  pl.BlockSpec((1, tq, LANES), lambda bi, i: (bi, i, 0)),
                  pl.BlockSpec((1, s, LANES), lambda bi, i: (bi, 0, 0))],
        out_specs=pl.BlockSpec((1, tq, s), lambda bi, i: (bi, i, 0)),
        out_shape=jax.ShapeDtypeStruct((b, s, s), jnp.int8),
        scratch_shapes=[pltpu.VMEM((s // tk, tq, tk), I32)] + [pltpu.VMEM((tq, LANES), I32)] * 3
                       + [pltpu.VMEM((s // tk, tq, tk), BF16), pltpu.VMEM((tq, LANES), BF16)],
        compiler_params=_cparams(("parallel", "parallel")),
        name="idx_mask",
    )(qi, misc, kidup)


def _attn_kernel(q_ref, kdup_ref, vb_ref, mask_ref, o_ref, m_ref, l_ref, acc_ref, *, tq, tk):
    i = pl.program_id(1)
    kb = pl.program_id(2)
    nk = pl.num_programs(2)
    last_needed = ((i + 1) * tq - 1) // tk

    @pl.when(kb == 0)
    def _():
        m_ref[...] = jnp.full(m_ref.shape, NEG_BIG, F32)
        l_ref[...] = jnp.zeros(l_ref.shape, F32)
        acc_ref[...] = jnp.zeros(acc_ref.shape, F32)

    @pl.when(kb <= last_needed)
    def _():
        bias = jnp.where(mask_ref[0].astype(I32) != 0, 0.0, NEG_BIG).astype(F32)
        bias2 = jnp.concatenate([bias, bias], axis=0)
        lane = _lane_iota((tq, LANES))
        lo = lane < A_HEAD_DIM

        def qk(g):
            q128 = q_ref[0, :, g * LANES:(g + 1) * LANES]
            zero = jnp.zeros_like(q128)
            q2 = jnp.concatenate([jnp.where(lo, q128, zero), jnp.where(lo, zero, q128)], axis=0)
            return _dot_nt(q2, kdup_ref[0, :, g * LANES:(g + 1) * LANES])

        s_next = qk(0)
        for g in range(A_KV_HEADS):
            s = s_next + bias2
            if g + 1 < A_KV_HEADS:
                s_next = qk(g + 1)
            m_old = m_ref[g]
            m_new = jnp.maximum(m_old, jnp.max(s, axis=1, keepdims=True))
            alpha = jnp.exp2(m_old - m_new)
            p = jnp.exp2(s - m_new[:, 0:1])
            l_ref[g] = alpha * l_ref[g] + jnp.sum(p, axis=1, keepdims=True)
            v128 = vb_ref[0, :, (g // 2) * LANES:(g // 2 + 1) * LANES]
            acc_ref[g] = alpha * acc_ref[g] + _dot(p.astype(BF16), v128)
            m_ref[g] = m_new

    @pl.when(kb == nk - 1)
    def _():
        lane = _lane_iota((tq, LANES))
        lo = lane < A_HEAD_DIM
        for g in range(A_KV_HEADS):
            a = acc_ref[g] / l_ref[g]
            top, bot = a[:tq], a[tq:]
            if g % 2 == 0:
                o128 = jnp.where(lo, top, pltpu.roll(bot, A_HEAD_DIM, axis=1))
            else:
                o128 = jnp.where(lo, pltpu.roll(top, A_HEAD_DIM, axis=1), bot)
            o_ref[0, :, g * LANES:(g + 1) * LANES] = o128


def _attn(q, kdup, vb, mask, tq, tk):
    b, s, _ = q.shape
    nk = s // tk

    def kv_idx(bi, i, kb):
        return (bi, jnp.minimum(kb, ((i + 1) * tq - 1) // tk), 0)

    def mask_idx(bi, i, kb):
        return (bi, i, jnp.minimum(kb, ((i + 1) * tq - 1) // tk))

    kern = functools.partial(_attn_kernel, tq=tq, tk=tk)
    return pl.pallas_call(
        kern,
        grid=(b, s // tq, nk),
        in_specs=[pl.BlockSpec((1, tq, A_Q), lambda bi, i, kb: (bi, i, 0)),
                  pl.BlockSpec((1, tk, 2 * A_KV), kv_idx),
                  pl.BlockSpec((1, tk, A_KV), kv_idx),
                  pl.BlockSpec((1, tq, tk), mask_idx)],
        out_specs=pl.BlockSpec((1, tq, A_Q), lambda bi, i, kb: (bi, i, 0)),
        out_shape=jax.ShapeDtypeStruct((b, s, A_Q), F32),
        scratch_shapes=[pltpu.VMEM((A_KV_HEADS, 2 * tq, LANES), F32)] * 3,
        compiler_params=_cparams(("parallel", "parallel", "arbitrary")),
        name="sel_attn",
    )(q, kdup, vb, mask)


def _conv_kernel(x_ref, xprev_ref, halo0_ref, w_ref, q_ref, k_ref, v_ref, *, tt):
    i = pl.program_id(1)
    x = x_ref[0]
    halo = jnp.where(i == 0, halo0_ref[0], xprev_ref[0])
    w = w_ref[...]

    def post(y, rows):
        y = y * jax.nn.sigmoid(y)
        for h in range(DN_HEADS):
            for j, ref in enumerate((q_ref, k_ref)):
                blk = y[:, j * DN_QK + h * DN_DK: j * DN_QK + (h + 1) * DN_DK]
                blk = blk * lax.rsqrt(jnp.sum(blk * blk, axis=-1, keepdims=True) + EPS)
                ref[0, 0:rows, h * DN_DK:(h + 1) * DN_DK] = blk
        v_ref[0, 0:rows, :] = y[:, 2 * DN_QK:]

    y = x * w[CONV_W - 1:CONV_W, :]
    for j in range(1, CONV_W):
        y = y + pltpu.roll(x, j, axis=0) * w[CONV_W - 1 - j:CONV_W - j, :]
    post(y, tt)

    x8 = x[0:SUBLANES]
    row = _row_iota(x8.shape)
    y8 = x8 * w[CONV_W - 1:CONV_W, :]
    for j in range(1, CONV_W):
        xs = jnp.where(row < j, pltpu.roll(halo, j, axis=0), pltpu.roll(x8, j, axis=0))
        y8 = y8 + xs * w[CONV_W - 1 - j:CONV_W - j, :]
    post(y8, SUBLANES)


def _conv(dqkv, halo0, conv_w, tt):
    b, t, c = dqkv.shape
    assert t % tt == 0 and tt % SUBLANES == 0
    r = tt // SUBLANES
    kern = functools.partial(_conv_kernel, tt=tt)
    out = jax.ShapeDtypeStruct((b, t, DN_QK), F32)
    return pl.pallas_call(
        kern,
        grid=(b, t // tt),
        in_specs=[pl.BlockSpec((1, tt, c), lambda bi, i: (bi, i, 0)),
                  pl.BlockSpec((1, SUBLANES, c), lambda bi, i: (bi, jnp.maximum(i * r - 1, 0), 0)),
                  pl.BlockSpec((1, SUBLANES, c), lambda bi, i: (bi, 0, 0)),
                  pl.BlockSpec((CONV_W, c), lambda bi, i: (0, 0))],
        out_specs=[pl.BlockSpec((1, tt, DN_QK), lambda bi, i: (bi, i, 0))] * 3,
        out_shape=[out, out, out],
        compiler_params=_cparams(("parallel", "parallel")),
        name="dn_conv",
    )(dqkv, dqkv, halo0, conv_w)


def _softplus(x):
    return jnp.maximum(x, 0.0) + jnp.log(1.0 + jnp.exp(-jnp.abs(x)))


def _pad_rows(a, rows):
    if a.shape[0] == rows:
        return a
    return jnp.concatenate([a, jnp.zeros((rows - a.shape[0], a.shape[1]), a.dtype)], axis=0)


DELTA_SEQS_PER_STEP = 4


def _delta_kernel(q_ref, k_ref, v_ref, z_ref, misc_ref, nega_ref, dtb_ref, gout_ref, s0_ref, o_ref, st_ref, *, c, t_valid, bb):
    n = pl.program_id(1)

    @pl.when(n == 0)
    def _():
        st_ref[...] = s0_ref[...]

    row = _row_iota((c, LANES))
    live = row < t_valid
    ri = lax.broadcasted_iota(I32, (c, c), 0)
    ci = lax.broadcasted_iota(I32, (c, c), 1)
    incl = ri >= ci
    strict = ri > ci
    eye = jnp.where(ri == ci, 1.0, 0.0).astype(F32)
    scale = DN_DK ** -0.5

    chains = [(bi, h) for bi in range(bb) for h in range(DN_HEADS)]
    nch = range(len(chains))
    beta_t, gc_t, gc_tr = [], [], []
    for bi in range(bb):
        mt = misc_ref[bi]
        beta_t.append(jnp.where(live, jax.nn.sigmoid(mt), 0.0))
        g = jnp.where(live, nega_ref[...] * _softplus(mt + dtb_ref[...]), 0.0)
        sft = 1
        while sft < c:
            g = g + jnp.where(row >= sft, pltpu.roll(g, sft, axis=0), 0.0)
            sft *= 2
        gc_t.append(g)
        gc_tr.append(_pad_rows(g, LANES).T)

    sl = lambda h: slice(h * DN_DK, (h + 1) * DN_DK)
    q = [jnp.where(live, q_ref[bi, :, sl(h)] * scale, 0.0) for bi, h in chains]
    k = [jnp.where(live, k_ref[bi, :, sl(h)], 0.0) for bi, h in chains]
    v = [jnp.where(live, v_ref[bi, :, sl(h)], 0.0) for bi, h in chains]
    beta = [beta_t[bi][:, MISC_DB + h:MISC_DB + h + 1] for bi, h in chains]
    gcc = [gc_t[bi][:, MISC_DA + h:MISC_DA + h + 1] for bi, h in chains]
    gcr = [gc_tr[bi][MISC_DA + h:MISC_DA + h + 1, 0:c] for bi, h in chains]
    decay = [jnp.where(incl, jnp.exp(jnp.where(incl, gcc[i] - gcr[i], 0.0)), 0.0) for i in nch]
    kb = [k[i] * beta[i] for i in nch]
    pw = [-jnp.where(strict, _dot3(kb[i], k[i], nt=True) * decay[i], 0.0) for i in nch]
    r = [eye + pw[i] for i in nch]
    nn = 2
    while nn < c:
        pw = [_dot3(pw[i], pw[i]) for i in nch]
        r = [_dot3(r[i], eye + pw[i]) for i in nch]
        nn *= 2
    egc = [jnp.exp(gcc[i]) for i in nch]
    u = [_dot3(r[i], v[i] * beta[i]) for i in nch]
    w = [_dot3(r[i], kb[i] * egc[i]) for i in nch]
    a_qk = [_dot3(q[i], k[i], nt=True) * decay[i] for i in nch]
    s = [st_ref[bi, h] for bi, h in chains]
    v_new = [u[i] - _dot3(w[i], s[i]) for i in nch]
    o = [_dot3(q[i] * egc[i], s[i]) + _dot3(a_qk[i], v_new[i]) for i in nch]
    glast = [gcc[i][c - 1:c, :] for i in nch]
    kw_t = [_pad_rows(k[i] * jnp.exp(glast[i] - gcc[i]), LANES).T for i in nch]
    s_new = [s[i] * jnp.exp(glast[i]) + _dot3(kw_t[i], _pad_rows(v_new[i], LANES)) for i in nch]
    for i, (bi, h) in enumerate(chains):
        st_ref[bi, h] = s_new[i]
        on = o[i] * lax.rsqrt(jnp.mean(o[i] * o[i], axis=-1, keepdims=True) + EPS) * gout_ref[...]
        z = z_ref[bi, :, sl(h)]
        o_ref[bi, :, sl(h)] = on * (z * jax.nn.sigmoid(z))


def _delta(qn, kn, v, dz, misc, a_log, dt_bias, g_dn_out, s0, c, t_valid):
    b, t, _ = qn.shape
    assert t % c == 0 and (t_valid == t or t == c)
    nega = jnp.zeros((1, LANES), F32).at[0, MISC_DA:MISC_DA + DN_HEADS].set(-jnp.exp(a_log.astype(F32)))
    dtb = jnp.zeros((1, LANES), F32).at[0, MISC_DA:MISC_DA + DN_HEADS].set(dt_bias.astype(F32))
    tok = lambda bi, n: (bi, n, 0)
    fixed = lambda bi, n: (0, 0)
    st = lambda bi, n: (bi, 0, 0, 0)
    bb = DELTA_SEQS_PER_STEP if b % DELTA_SEQS_PER_STEP == 0 else 1
    kern = functools.partial(_delta_kernel, c=c, t_valid=t_valid, bb=bb)
    return pl.pallas_call(
        kern,
        grid=(b // bb, t // c),
        in_specs=[pl.BlockSpec((bb, c, DN_QK), tok), pl.BlockSpec((bb, c, DN_QK), tok), pl.BlockSpec((bb, c, DN_V), tok),
                  pl.BlockSpec((bb, c, DN_V), tok), pl.BlockSpec((bb, c, LANES), tok),
                  pl.BlockSpec((1, LANES), fixed), pl.BlockSpec((1, LANES), fixed), pl.BlockSpec((1, DN_DV), fixed),
                  pl.BlockSpec((bb, DN_HEADS, DN_DK, DN_DV), st)],
        out_specs=[pl.BlockSpec((bb, c, DN_V), tok), pl.BlockSpec((bb, DN_HEADS, DN_DK, DN_DV), st)],
        out_shape=[jax.ShapeDtypeStruct((b, t, DN_V), F32), jax.ShapeDtypeStruct((b, DN_HEADS, DN_DK, DN_DV), F32)],
        compiler_params=_cparams(("parallel", "arbitrary")),
        name="delta_rule",
    )(qn, kn, v, dz, misc, nega, dtb, g_dn_out.reshape(1, DN_DV).astype(F32), s0)


def _merge_kernel(x_ref, oa_ref, ob_ref, ga_ref, gb_ref, woa_ref, wob_ref, wout_ref, gffn_ref, wpqt_ref,
                  x1_ref, h2_ref, qt_ref):
    ma = jax.nn.sigmoid(ga_ref[...]) * _dot(oa_ref[...].astype(BF16), woa_ref[...])
    mb = jax.nn.sigmoid(gb_ref[...]) * _dot(ob_ref[...].astype(BF16), wob_ref[...])
    x1 = x_ref[...] + _dot((ma + mb).astype(BF16), wout_ref[...])
    x1_ref[...] = x1
    h2 = x1 * lax.rsqrt(jnp.mean(x1 * x1, axis=-1, keepdims=True) + EPS) * gffn_ref[...]
    h2_ref[...] = h2
    qt_ref[...] = _dot_nt(wpqt_ref[...], h2.astype(BF16))


def _merge(x2d, oa, ob, ga, gb, w_oa, w_ob, w_out, g_ffn, w_pq, tm):
    n = x2d.shape[0]
    nq = w_pq.shape[1]
    row = lambda i: (i, 0)
    fixed = lambda i: (0, 0)
    return pl.pallas_call(
        _merge_kernel,
        grid=(n // tm,),
        in_specs=[pl.BlockSpec((tm, D_MODEL), row), pl.BlockSpec((tm, A_Q), row), pl.BlockSpec((tm, DN_V), row),
                  pl.BlockSpec((tm, D_MODEL), row), pl.BlockSpec((tm, D_MODEL), row),
                  pl.BlockSpec((A_Q, D_MODEL), fixed), pl.BlockSpec((DN_V, D_MODEL), fixed),
                  pl.BlockSpec((D_MODEL, D_MODEL), fixed), pl.BlockSpec((1, D_MODEL), fixed),
                  pl.BlockSpec((nq, D_MODEL), fixed)],
        out_specs=[pl.BlockSpec((tm, D_MODEL), row), pl.BlockSpec((tm, D_MODEL), row), pl.BlockSpec((nq, tm), lambda i: (0, i))],
        out_shape=[jax.ShapeDtypeStruct((n, D_MODEL), F32), jax.ShapeDtypeStruct((n, D_MODEL), F32),
                   jax.ShapeDtypeStruct((nq, n), F32)],
        compiler_params=_cparams(("parallel",)),
        name="merge_out",
    )(x2d, oa, ob, ga, gb, w_oa.astype(BF16), w_ob.astype(BF16), w_out.astype(BF16),
      g_ffn.reshape(1, D_MODEL).astype(F32), w_pq.T.astype(BF16))


def _peer_cands():
    cands = [(i, j) for i in range(PEER_TOPK) for j in range(PEER_TOPK) if (i + 1) * (j + 1) <= PEER_TOPK]
    return sorted(cands, key=lambda ij: ij[0] * PEER_TOPK + ij[1])


def _route_kernel(qt_ref, sk_ref, eid_ref, gate_ref, toff_ref, val_ref, idx_ref, *, tn):
    half = PEER_DKEY // 2
    key_iota = lax.broadcasted_iota(I32, (PEER_KEYS, tn), 0)

    heads_per_iter = 4

    def per_group(hg, carry):
        tabs = [(hh, c) for hh in range(heads_per_iter) for c in range(2)]
        ss = []
        for hh, c in tabs:
            t = 2 * (hg * heads_per_iter + hh) + c
            qblk = qt_ref[pl.ds(pl.multiple_of(t * half, half), half), :]
            ss.append(_dot3(sk_ref[t], qblk))
        for r in range(PEER_TOPK):
            ms = [jnp.max(s, axis=0, keepdims=True) for s in ss]
            ams = [jnp.min(jnp.where(s == m, key_iota, PEER_KEYS), axis=0, keepdims=True) for s, m in zip(ss, ms)]
            for i, (hh, c) in enumerate(tabs):
                h = hg * heads_per_iter + hh
                val_ref[c, r, pl.ds(h, 1), :] = ms[i]
                idx_ref[c, r, pl.ds(h, 1), :] = ams[i]
            ss = [jnp.where(key_iota == am, -jnp.inf, s) for s, am in zip(ss, ams)]
        return carry

    lax.fori_loop(0, PEER_HEADS // heads_per_iter, per_group, 0)

    cands = _peer_cands()
    cv = [val_ref[0, i] + val_ref[1, j] for i, j in cands]
    ce = [idx_ref[0, i] * PEER_KEYS + idx_ref[1, j] for i, j in cands]
    nc = len(cands)
    rank = []
    for a in range(nc):
        rk = jnp.zeros(cv[a].shape, I32)
        for b in range(nc):
            if b < a:
                rk = rk + jnp.where(cv[b] >= cv[a], 1, 0)
            elif b > a:
                rk = rk + jnp.where(cv[b] > cv[a], 1, 0)
        rank.append(rk)
    mx = cv[0]
    ex = [jnp.where(rank[a] < PEER_TOPK, jnp.exp(cv[a] - mx), 0.0) for a in range(nc)]
    den = ex[0]
    for a in range(1, nc):
        den = den + ex[a]
    inv = 1.0 / den
    e_slots = []
    for slot in range(PEER_TOPK):
        e = jnp.zeros(cv[0].shape, I32)
        g = jnp.zeros(cv[0].shape, F32)
        for a in range(nc):
            hit = rank[a] == slot
            e = jnp.where(hit, ce[a], e)
            g = jnp.where(hit, ex[a], g)
        eid_ref[slot] = e
        gate_ref[slot] = g * inv
        e_slots.append(e)
    toff_ref[...] = ((jnp.concatenate(e_slots, axis=0) & (PEER_HALF - 1)) * SUBLANES).T


def _route(qt, sub_keys, tn):
    nq, n = qt.shape
    sk = sub_keys.reshape(2 * PEER_HEADS, PEER_KEYS, PEER_DKEY // 2).astype(F32)
    kern = functools.partial(_route_kernel, tn=tn)
    blk = pl.BlockSpec((PEER_TOPK, PEER_HEADS, tn), lambda i: (0, 0, i))
    npair = PEER_TOPK * PEER_HEADS
    return pl.pallas_call(
        kern,
        grid=(n // tn,),
        in_specs=[pl.BlockSpec((nq, tn), lambda i: (0, i)),
                  pl.BlockSpec((2 * PEER_HEADS, PEER_KEYS, PEER_DKEY // 2), lambda i: (0, 0, 0))],
        out_specs=[blk, blk, pl.BlockSpec((tn, npair), lambda i: (i, 0))],
        out_shape=[jax.ShapeDtypeStruct((PEER_TOPK, PEER_HEADS, n), I32), jax.ShapeDtypeStruct((PEER_TOPK, PEER_HEADS, n), F32),
                   jax.ShapeDtypeStruct((n, npair), I32)],
        scratch_shapes=[pltpu.VMEM((2, PEER_TOPK, PEER_HEADS, tn), F32), pltpu.VMEM((2, PEER_TOPK, PEER_HEADS, tn), I32)],
        compiler_params=_cparams(("parallel",)),
        name="peer_route",
    )(qt, sk)


def _pack_table(tab):
    e, d = tab.shape
    assert d == SUBLANES * LANES
    bits = lax.bitcast_convert_type(tab.astype(BF16), jnp.uint16).astype(jnp.uint32)
    assert e == 2 * PEER_HALF
    words = (bits[:PEER_HALF] << 16) | bits[PEER_HALF:]
    return lax.bitcast_convert_type(words, I32).reshape(e // 2 * SUBLANES, LANES)


HI_HALF = -65536


def _bf16_words(x):
    return pltpu.bitcast(x.astype(BF16).astype(F32), I32) & HI_HALF


def _shr16(w):
    return lax.shift_right_logical(w, jnp.full(w.shape, 16, I32))


def _packed_mul(a_words, b_words):
    return pltpu.bitcast(pltpu.bitcast(a_words, BF16) * pltpu.bitcast(b_words, BF16), I32)


def _packed_add(a_words, b_words):
    return pltpu.bitcast(pltpu.bitcast(a_words, BF16) + pltpu.bitcast(b_words, BF16), I32)


def _hi_f32(w):
    return pltpu.bitcast(w & HI_HALF, F32)


def _lo_f32(w):
    return pltpu.bitcast(jnp.left_shift(w, 16), F32)


_BITREV8 = (0, 4, 2, 6, 1, 5, 3, 7)


PEER_U_SUB = 16


def _peer_u_kernel(toff_ref, x_ref, eid_ref, gate_ref, tab_ref, ce_ref, co_ref, rhi_ref, rlo_ref, *, tn, sub_t):
    npair = gate_ref.shape[0]
    lane = _lane_iota((npair, tn))
    sub = lax.broadcasted_iota(I32, (SUBLANES, LANES), 0)
    keep = {step: (sub & step) == 0 for step in (4, 2, 1)}

    def per_token(tt, t0):
        t = t0 + tt
        xw = _bf16_words(x_ref[t])
        xw = xw | _shr16(xw)
        for g in range(npair // SUBLANES):
            prods = []
            for p in _BITREV8:
                off = pl.multiple_of(toff_ref[t * npair + (g * SUBLANES + p)], SUBLANES)
                prods.append(_packed_mul(tab_ref[pl.ds(off, SUBLANES), :], xw))
            step = SUBLANES // 2
            while step >= 1:
                nxt = []
                for a in range(0, len(prods), 2):
                    lo_t, hi_t = prods[a], prods[a + 1]
                    left = jnp.where(keep[step], lo_t, pltpu.roll(hi_t, step, axis=0))
                    right = jnp.where(keep[step], pltpu.roll(lo_t, SUBLANES - step, axis=0), hi_t)
                    nxt.append(_packed_add(left, right))
                prods = nxt
                step //= 2
            rhi_ref[tt, g * SUBLANES:(g + 1) * SUBLANES, :] = _hi_f32(prods[0])
            rlo_ref[tt, g * SUBLANES:(g + 1) * SUBLANES, :] = _lo_f32(prods[0])
        return t0

    def per_sub(sb, accs):
        acc_hi, acc_lo = accs
        t0 = sb * sub_t
        lax.fori_loop(0, sub_t, per_token, t0)
        ones = jnp.ones((LANES, LANES), BF16)
        for tt in range(sub_t):
            hit = lane == t0 + tt
            acc_hi = jnp.where(hit, jnp.sum(rhi_ref[tt], axis=1, keepdims=True), acc_hi)
            acc_lo = jnp.where(hit, _dot(rlo_ref[tt].astype(BF16), ones)[:, 0:tn], acc_lo)
        return acc_hi, acc_lo

    zero = jnp.zeros((npair, tn), F32)
    acc_hi, acc_lo = lax.fori_loop(0, tn // sub_t, per_sub, (zero, zero))
    even = eid_ref[...] < PEER_HALF
    coef = gate_ref[...] * jax.nn.gelu(jnp.where(even, acc_hi, acc_lo))
    ce_ref[...] = jnp.where(even, coef, 0.0)
    co_ref[...] = jnp.where(even, 0.0, coef)


def _peer_v_kernel(toff_ref, ce_ref, co_ref, rep_ref, tab_ref, y_ref, bw_ref, *, tn, npair):
    ce = ce_ref[...].astype(BF16)
    co = co_ref[...].astype(BF16)
    chunk = min(16, tn)
    for c in range(tn // chunk):
        rep = rep_ref[:, c * chunk * LANES:(c + 1) * chunk * LANES]
        be = _dot(ce, rep)
        bo = _dot(co, rep)
        words = (pltpu.bitcast(be, I32) & HI_HALF) | _shr16(pltpu.bitcast(bo, I32))
        for t in range(chunk):
            bw_ref[c * chunk + t] = words[:, t * LANES:(t + 1) * LANES]

    nacc = 4

    def per_token(t, carry):
        zero = jnp.zeros((SUBLANES, LANES), F32)
        acc_hi = [zero] * nacc
        acc_lo = [zero] * nacc
        base = t * npair
        for k in range(npair):
            off = pl.multiple_of(toff_ref[base + k], SUBLANES)
            cw = jnp.broadcast_to(bw_ref[t, k:k + 1, :], (SUBLANES, LANES))
            prod = _packed_mul(tab_ref[pl.ds(off, SUBLANES), :], cw)
            acc_hi[k % nacc] = acc_hi[k % nacc] + _hi_f32(prod)
            acc_lo[k % nacc] = acc_lo[k % nacc] + _lo_f32(prod)
        y_ref[t] = ((acc_hi[0] + acc_hi[1]) + (acc_hi[2] + acc_hi[3])) + ((acc_lo[0] + acc_lo[1]) + (acc_lo[2] + acc_lo[3]))
        return carry

    lax.fori_loop(0, tn, per_token, 0)


def _peer_experts(eid, gate, toff, h2, tab_u, tab_v, tn):
    npair, n = eid.shape
    d = h2.shape[1]
    x3 = h2.reshape(n, d // LANES, LANES)
    tab_spec = pl.BlockSpec(tab_u.shape, lambda i: (0, 0), pipeline_mode=pl.Buffered(1))
    smem_blk = lambda tt: pl.BlockSpec((tt * npair,), lambda i: (i,), memory_space=pltpu.SMEM)
    slot_blk = lambda tt: pl.BlockSpec((npair, tt), lambda i: (0, i))
    sub_t = PEER_U_SUB if tn % PEER_U_SUB == 0 else tn
    ce, co = pl.pallas_call(
        functools.partial(_peer_u_kernel, tn=tn, sub_t=sub_t),
        grid=(n // tn,),
        in_specs=[smem_blk(tn), pl.BlockSpec((tn, d // LANES, LANES), lambda i: (i, 0, 0)), slot_blk(tn), slot_blk(tn), tab_spec],
        out_specs=[slot_blk(tn), slot_blk(tn)],
        out_shape=[jax.ShapeDtypeStruct((npair, n), F32)] * 2,
        scratch_shapes=[pltpu.VMEM((sub_t, npair, LANES), F32)] * 2,
        compiler_params=_cparams(("arbitrary",)),
        name="peer_u",
    )(toff.reshape(n * npair), x3, eid, gate, tab_u)
    tv = tn
    rep = jnp.asarray(np.arange(tv)[:, None] == np.arange(tv * LANES)[None, :] // LANES, BF16)
    y3 = pl.pallas_call(
        functools.partial(_peer_v_kernel, tn=tv, npair=npair),
        grid=(n // tv,),
        in_specs=[smem_blk(tv), slot_blk(tv), slot_blk(tv),
                  pl.BlockSpec(rep.shape, lambda i: (0, 0), pipeline_mode=pl.Buffered(1)), tab_spec],
        out_specs=pl.BlockSpec((tv, d // LANES, LANES), lambda i: (i, 0, 0)),
        out_shape=jax.ShapeDtypeStruct((n, d // LANES, LANES), F32),
        scratch_shapes=[pltpu.VMEM((tv, npair, LANES), I32)],
        compiler_params=_cparams(("arbitrary",)),
        name="peer_v",
    )(toff.reshape(n * npair), ce, co, rep, tab_v)
    return y3.reshape(n, d)


def _ple_kernel(x1_ref, yp_ref, p_ref, gple_ref, wg_ref, wp_ref, o_ref):
    x2 = x1_ref[...] + yp_ref[...]
    hn = x2 * lax.rsqrt(jnp.mean(x2 * x2, axis=-1, keepdims=True) + EPS) * gple_ref[...]
    gate = jax.nn.sigmoid(_dot(hn.astype(BF16), wg_ref[...]))
    o_ref[...] = x2 + gate * _dot(p_ref[...].astype(BF16), wp_ref[...])


def _ple(x1, yp, p2d, g_ple, w_ple_gate, w_ple, tm):
    n = x1.shape[0]
    row = lambda i: (i, 0)
    fixed = lambda i: (0, 0)
    return pl.pallas_call(
        _ple_kernel,
        grid=(n // tm,),
        in_specs=[pl.BlockSpec((tm, D_MODEL), row), pl.BlockSpec((tm, D_MODEL), row), pl.BlockSpec((tm, PLE_DIM), row),
                  pl.BlockSpec((1, D_MODEL), fixed), pl.BlockSpec((D_MODEL, D_MODEL), fixed),
                  pl.BlockSpec((PLE_DIM, D_MODEL), fixed)],
        out_specs=pl.BlockSpec((tm, D_MODEL), row),
        out_shape=jax.ShapeDtypeStruct((n, D_MODEL), F32),
        compiler_params=_cparams(("parallel",)),
        name="ple_out",
    )(x1, yp, p2d, g_ple.reshape(1, D_MODEL).astype(F32), w_ple_gate.astype(BF16), w_ple.astype(BF16))


PAGES_PER_STEP = 8
SAMPLE_SOFTMAX_STATES = 4


def _sample_scores_kernel(pt_ref, qh_ref, wrep_ref, kinew_ref, *refs, npg):
    pages = refs[:npg]
    sc_ref, scnew_ref = refs[npg:]
    j = pl.program_id(1)
    qh = qh_ref[0]
    wrep = wrep_ref[0]

    def raw(keys_t):
        return _dot(qh, keys_t.astype(BF16))

    def score(s):
        s = jnp.maximum(s, 0.0) * wrep
        tot = s[0:SUBLANES]
        for h in range(1, IDX_HEADS):
            tot = tot + s[h * SUBLANES:(h + 1) * SUBLANES]
        return tot

    dots = [raw(pages[p][0]) for p in range(npg)]
    for p in range(npg):
        sc_ref[0, :, p * PAGE_SIZE:(p + 1) * PAGE_SIZE] = score(dots[p])

    @pl.when(j == 0)
    def _():
        sn = score(raw(kinew_ref[0]))
        causal = _lane_iota((SUBLANES, PAGE_SIZE)) <= _row_iota((SUBLANES, PAGE_SIZE))
        scnew_ref[0] = jnp.where(causal, sn, -jnp.inf)


def _sample_thr_kernel(sc_ref, thr_ref, jcut_ref, keys_ref, cand_ref, *, k_sel, tq, tk, t):
    nkb = keys_ref.shape[0]
    for kb in range(nkb):
        keys_ref[kb] = _sort_key(sc_ref[:, :, kb * tk:(kb + 1) * tk].reshape(tq, tk))
    _select_rows(keys_ref, thr_ref, jcut_ref, cand_ref, nkb, k_sel, tq, tk, min(SEARCH_ROWS, tq), live_rows=t)


def _sample_attn_kernel(pt_ref, q_ref, sc_ref, scnew_ref, thr_ref, jcut_ref, knew_ref, vnew_ref, *refs, npg, past):
    kpages = refs[:npg]
    vpages = refs[npg:2 * npg]
    o_ref, m_ref, l_ref, acc_ref = refs[2 * npg:]
    j = pl.program_id(1)

    @pl.when(j == 0)
    def _():
        m_ref[...] = jnp.full(m_ref.shape, NEG_BIG, F32)
        l_ref[...] = jnp.zeros(l_ref.shape, F32)
        acc_ref[...] = jnp.zeros(acc_ref.shape, F32)

    q = q_ref[0]
    thr = thr_ref[...]
    jcut = jcut_ref[...]
    lane = _lane_iota((SUBLANES, PAGE_SIZE))
    kvd = A_KV_HEADS * A_HEAD_DIM

    wide = lambda a: jnp.concatenate([a] * (kvd // LANES), axis=1)

    def update(groups):
        ss = []
        for _, kts, _, scs, pos0s, causal in groups:
            biases = []
            for sc8, pos0 in zip(scs, pos0s):
                key = _sort_key(sc8)
                pos = lane + pos0
                sel = jnp.where(key > thr, 0.0, jnp.where(key == thr, jnp.where(pos <= jcut, 0.0, NEG_BIG), NEG_BIG))
                if causal is not None:
                    sel = jnp.where(causal, sel, NEG_BIG)
                biases.append(sel)
            bias = jnp.concatenate([jnp.concatenate(biases, axis=1)] * A_HEADS, axis=0)
            ss.append(jnp.concatenate([_dot(q, kt[...].reshape(kvd, PAGE_SIZE).astype(BF16)) for kt in kts], axis=1) + bias)
        m_olds = [m_ref[g[0]] for g in groups]
        m_news = [jnp.maximum(mo, jnp.max(s, axis=1, keepdims=True)) for mo, s in zip(m_olds, ss)]
        alphas = [jnp.exp2(mo - mn) for mo, mn in zip(m_olds, m_news)]
        ps = [jnp.exp2(s - mn[:, 0:1]) for s, mn in zip(ss, m_news)]
        sums = [jnp.sum(p, axis=1, keepdims=True) for p in ps]
        pvs = []
        for (_, _, vts, _, _, _), p in zip(groups, ps):
            pb = p.astype(BF16)
            parts = [_dot_nt(pb[:, i * PAGE_SIZE:(i + 1) * PAGE_SIZE], vt[...].reshape(kvd, PAGE_SIZE).astype(BF16))
                     for i, vt in enumerate(vts)]
            pv = parts[0]
            for part in parts[1:]:
                pv = pv + part
            pvs.append(pv)
        for g, mn, al, sm, pv in zip(groups, m_news, alphas, sums, pvs):
            ci = g[0]
            l_ref[ci] = al * l_ref[ci] + sm
            acc_ref[ci] = wide(al) * acc_ref[ci] + pv
            m_ref[ci] = mn

    nst = m_ref.shape[0]
    update([(ci, [kpages[p].at[0] for p in range(ci, npg, nst)], [vpages[p].at[0] for p in range(ci, npg, nst)],
             [sc_ref[0, :, p * PAGE_SIZE:(p + 1) * PAGE_SIZE] for p in range(ci, npg, nst)],
             [(j * npg + p) * PAGE_SIZE for p in range(ci, npg, nst)], None) for ci in range(nst)])

    @pl.when(j == pl.num_programs(1) - 1)
    def _():
        update([(0, [knew_ref.at[0]], [vnew_ref.at[0]], [scnew_ref[0]], [past], lane <= _row_iota((SUBLANES, PAGE_SIZE)))])
        m_all = m_ref[0]
        for ci in range(1, nst):
            m_all = jnp.maximum(m_all, m_ref[ci])
        l_all = jnp.zeros(m_all.shape, F32)
        acc_all = jnp.zeros(acc_ref.shape[1:], F32)
        for ci in range(nst):
            w = jnp.exp2(m_ref[ci] - m_all)
            l_all = l_all + w * l_ref[ci]
            acc_all = acc_all + wide(w) * acc_ref[ci]
        o_ref[0] = acc_all / wide(l_all)


def _head_major(a, bd, t, nh, hd):
    a = a.reshape(bd, t, nh, hd).transpose(0, 2, 1, 3)
    a = jnp.pad(a, ((0, 0), (0, 0), (0, SUBLANES - t), (0, 0)))
    return a.reshape(bd, nh * SUBLANES, hd)


def _attn_sample(q, k, v, qi, misc, cache_k, cache_v, cache_idx_k, page_table, bd, t):
    assert t <= SUBLANES
    n_pages = page_table.shape[1]
    past = n_pages * PAGE_SIZE
    npg = PAGES_PER_STEP if n_pages % PAGES_PER_STEP == 0 else 1
    nj = n_pages // npg
    k_sel = min(TOPK_MAX, (past + t) // 4)
    rows = A_HEADS * SUBLANES

    qh = _head_major(qi, bd, t, IDX_HEADS, IDX_DIM)
    wi = misc[:, MISC_WI:MISC_WI + IDX_HEADS].reshape(bd, t, IDX_HEADS).transpose(0, 2, 1)
    wrep = jnp.broadcast_to(jnp.pad(wi, ((0, 0), (0, 0), (0, SUBLANES - t))).reshape(bd, rows, 1), (bd, rows, LANES))
    pad_page = lambda a, w: jnp.pad(a.reshape(bd, t, w), ((0, 0), (0, PAGE_SIZE - t), (0, 0)))
    kinew = pad_page(misc[:, :IDX_DIM], IDX_DIM).transpose(0, 2, 1)
    cik = cache_idx_k.transpose(0, 2, 1)

    per_b = lambda b, j, pt: (b, 0, 0)
    per_b4 = lambda b, j, pt: (b, 0, 0, 0)
    page_spec = lambda shape, p: pl.BlockSpec(shape, lambda b, j, pt: (pt[b, j * npg + p],) + (0,) * (len(shape) - 1))
    sc, scnew = pl.pallas_call(
        functools.partial(_sample_scores_kernel, npg=npg),
        grid_spec=pltpu.PrefetchScalarGridSpec(
            num_scalar_prefetch=1, grid=(bd, nj),
            in_specs=[pl.BlockSpec((1, rows, IDX_DIM), per_b), pl.BlockSpec((1, rows, LANES), per_b),
                      pl.BlockSpec((1, IDX_DIM, PAGE_SIZE), per_b)]
                     + [page_spec((1, IDX_DIM, PAGE_SIZE), p) for p in range(npg)],
            out_specs=[pl.BlockSpec((1, SUBLANES, npg * PAGE_SIZE), lambda b, j, pt: (b, 0, j)),
                       pl.BlockSpec((1, SUBLANES, PAGE_SIZE), per_b)]),
        out_shape=[jax.ShapeDtypeStruct((bd, SUBLANES, past), F32), jax.ShapeDtypeStruct((bd, SUBLANES, PAGE_SIZE), F32)],
        compiler_params=_cparams(("parallel", "arbitrary")),
        name="sample_scores",
    )(page_table, qh, wrep, kinew, *([cik] * npg))

    length = past + PAGE_SIZE
    sc_all = jnp.concatenate([sc, scnew], axis=2)
    tb = SUBLANES if bd % SUBLANES == 0 else 1
    tq = tb * SUBLANES
    tk = 5 * LANES if length % (5 * LANES) == 0 else LANES
    thr, jcut = pl.pallas_call(
        functools.partial(_sample_thr_kernel, k_sel=k_sel, tq=tq, tk=tk, t=t),
        grid=(bd // tb,),
        in_specs=[pl.BlockSpec((tb, SUBLANES, length), lambda i: (i, 0, 0))],
        out_specs=[pl.BlockSpec((tq, LANES), lambda i: (i, 0))] * 2,
        out_shape=[jax.ShapeDtypeStruct((bd * SUBLANES, LANES), I32)] * 2,
        scratch_shapes=[pltpu.VMEM((length // tk, tq, tk), I32), pltpu.VMEM((tq, LANES), I32)],
        compiler_params=_cparams(("parallel",)),
        name="sample_thr",
    )(sc_all)

    qa = _head_major(q, bd, t, A_HEADS, A_HEAD_DIM)
    kv_of_row = (np.arange(rows) // SUBLANES) // A_REP
    own = jnp.asarray(kv_of_row[:, None] == np.arange(A_KV_HEADS)[None, :])
    qa = jnp.where(own[None, :, :, None], qa[:, :, None, :], jnp.zeros((), qa.dtype)).reshape(bd, rows, A_KV)
    to_page_t = lambda a: pad_page(a, A_KV).reshape(bd, PAGE_SIZE, A_KV_HEADS, A_HEAD_DIM).transpose(0, 2, 3, 1)
    knew, vnew = to_page_t(k), to_page_t(v)
    ck = cache_k.transpose(0, 2, 3, 1)
    cv = cache_v.transpose(0, 2, 3, 1)
    kv_blk = (1, A_KV_HEADS, A_HEAD_DIM, PAGE_SIZE)
    nst = SAMPLE_SOFTMAX_STATES if npg % SAMPLE_SOFTMAX_STATES == 0 else 1
    o = pl.pallas_call(
        functools.partial(_sample_attn_kernel, npg=npg, past=past),
        grid_spec=pltpu.PrefetchScalarGridSpec(
            num_scalar_prefetch=1, grid=(bd, nj),
            in_specs=[pl.BlockSpec((1, rows, A_KV), per_b),
                      pl.BlockSpec((1, SUBLANES, npg * PAGE_SIZE), lambda b, j, pt: (b, 0, j)),
                      pl.BlockSpec((1, SUBLANES, PAGE_SIZE), per_b),
                      pl.BlockSpec((SUBLANES, LANES), lambda b, j, pt: (b, 0)),
                      pl.BlockSpec((SUBLANES, LANES), lambda b, j, pt: (b, 0)),
                      pl.BlockSpec(kv_blk, per_b4), pl.BlockSpec(kv_blk, per_b4)]
                     + [page_spec(kv_blk, p) for p in range(npg)] * 2,
            out_specs=pl.BlockSpec((1, rows, A_KV), per_b),
            scratch_shapes=[pltpu.VMEM((nst, rows, LANES), F32), pltpu.VMEM((nst, rows, LANES), F32),
                            pltpu.VMEM((nst, rows, A_KV), F32)]),
        out_shape=jax.ShapeDtypeStruct((bd, rows, A_KV), F32),
        compiler_params=_cparams(("parallel", "arbitrary")),
        name="sample_attn",
    )(page_table, qa, sc, scnew, thr, jcut, knew, vnew, *([ck] * npg), *([cv] * npg))
    o = jnp.sum(jnp.where(own[None, :, :, None], o.reshape(bd, rows, A_KV_HEADS, A_HEAD_DIM), 0.0), axis=2)
    o = o.reshape(bd, A_HEADS, SUBLANES, A_HEAD_DIM)[:, :, :t].transpose(0, 2, 1, 3)
    return o.reshape(bd * t, A_Q)


def _tile(n, pref):
    return pref if n % pref == 0 else n


def _layer(x, p_emb, conv_state, delta_state, attn_fn, prm):
    b, t, _ = x.shape
    n = b * t
    x2d = x.reshape(n, D_MODEL)
    tm = _tile(n, ROW_TILE)
    (q, k, kdup, v, vb, qi, misc, kidup, dqkv, dz, ga, gb) = _proj(
        x2d, prm['g_mix'], prm['w_perm'], prm['g_q'], prm['g_k'], prm['g_idx_k'], tm)

    oa = attn_fn(q, k, kdup, v, vb, qi, misc, kidup)

    tp = -(-t // SUBLANES) * SUBLANES
    c = min(DN_CHUNK, tp)
    pad_t = lambda a: jnp.pad(a.reshape(b, t, a.shape[-1]), ((0, 0), (0, tp - t), (0, 0)))
    dqkv3 = dqkv.reshape(b, t, DN_CONV_CH)
    halo = jnp.zeros((b, SUBLANES, DN_CONV_CH), F32)
    if conv_state is not None:
        halo = halo.at[:, SUBLANES - (CONV_W - 1):].set(conv_state.astype(F32))
        hist = jnp.concatenate([conv_state.astype(F32), dqkv3], axis=1)
    else:
        hist = jnp.concatenate([jnp.zeros((b, CONV_W - 1, DN_CONV_CH), F32), dqkv3], axis=1)
    new_conv = hist[:, -(CONV_W - 1):]
    qn, kn, vv = _conv(pad_t(dqkv), halo, prm['conv_w'], _tile(tp, CONV_TT))
    s0 = jnp.zeros((b, DN_HEADS, DN_DK, DN_DV), F32) if delta_state is None else delta_state.astype(F32)
    ob, new_delta = _delta(qn, kn, vv, pad_t(dz), pad_t(misc), prm['a_log'], prm['dt_bias'], prm['g_dn_out'], s0, c, t)
    ob = ob[:, :t].reshape(n, DN_V)

    x1, h2, qt = _merge(x2d, oa, ob, ga, gb, prm['w_oa'], prm['w_ob'], prm['w_out'], prm['g_ffn'], prm['w_pq'], tm)
    eid, gate, toff = _route(qt, prm['sub_keys'], _tile(n, TOKEN_TILE))
    npair = PEER_TOPK * PEER_HEADS
    yp = _peer_experts(eid.reshape(npair, n), gate.reshape(npair, n), toff, h2, prm['tab_u'], prm['tab_v'], _tile(n, TOKEN_TILE))
    y = _ple(x1, yp, p_emb.reshape(n, PLE_DIM), prm['g_ple'], prm['w_ple_gate'], prm['w_ple'], tm)

    return (y.reshape(b, t, D_MODEL), k.reshape(b, t, A_KV_HEADS, A_HEAD_DIM), v.reshape(b, t, A_KV_HEADS, A_HEAD_DIM),
            misc[:, :IDX_DIM].reshape(b, t, IDX_DIM), new_conv, new_delta)


def kernel(x_prompt, x_sample, cache_k, cache_v, cache_idx_k, state_conv, state_delta, page_table, p_prompt, p_sample,
           g_mix, w_in, g_q, g_k, g_idx_k, conv_w, a_log, dt_bias, g_dn_out, w_oa, w_ob, w_out, g_ffn, w_pq, sub_keys,
           peer_u, peer_v, g_ple, w_ple_gate, w_ple):
    depth = w_in.shape[0]
    xp, xs = x_prompt, x_sample
    outs = [[] for _ in range(10)]
    for i in range(depth):
        prm = dict(g_mix=g_mix[i], w_perm=_permute_w_in(w_in[i]), g_q=g_q[i], g_k=g_k[i], g_idx_k=g_idx_k[i],
                   conv_w=conv_w[i].astype(F32), a_log=a_log[i], dt_bias=dt_bias[i], g_dn_out=g_dn_out[i],
                   w_oa=w_oa[i], w_ob=w_ob[i], w_out=w_out[i], g_ffn=g_ffn[i], w_pq=w_pq[i], sub_keys=sub_keys[i],
                   tab_u=_pack_table(peer_u[i]), tab_v=_pack_table(peer_v[i]), g_ple=g_ple[i],
                   w_ple_gate=w_ple_gate[i], w_ple=w_ple[i])

        bp, s, _ = xp.shape

        def attn_prompt(q, k, kdup, v, vb, qi, misc, kidup):
            k_sel = min(TOPK_MAX, s // 4)
            tq, tk = _tile(s, ATTN_TQ), _tile(s, ATTN_TK)
            r3 = lambda a: a.reshape(bp, s, a.shape[-1])
            mask = _idx_mask(r3(qi), r3(misc), r3(kidup), k_sel, tq, tk)
            return _attn(r3(q), r3(kdup), r3(vb), mask, tq, tk).reshape(bp * s, A_Q)

        res = _layer(xp, p_prompt[i], None, None, attn_prompt, prm)
        xp = res[0]
        for lst, val in zip(outs[:5], res[1:]):
            lst.append(val)

        bd, t, _ = xs.shape

        def attn_sample(q, k, kdup, v, vb, qi, misc, kidup):
            return _attn_sample(q, k, v, qi, misc, cache_k[i], cache_v[i], cache_idx_k[i], page_table, bd, t)

        res = _layer(xs, p_sample[i], state_conv[i], state_delta[i], attn_sample, prm)
        xs = res[0]
        for lst, val in zip(outs[5:], res[1:]):
            lst.append(val)

    cast = [cache_k.dtype, cache_v.dtype, cache_idx_k.dtype, state_conv.dtype, state_delta.dtype] * 2
    stacked = [jnp.stack(lst).astype(dt) for lst, dt in zip(outs, cast)]
    return (xp, xs, *stacked)
```

```python
import functools
import math

import jax
import jax.numpy as jnp
import numpy as np
from jax import lax
from jax.experimental import pallas as pl
from jax.experimental.pallas import tpu as pltpu

F32 = jnp.float32
BF16 = jnp.bfloat16
I32 = jnp.int32

D_MODEL = 1024
PAGE_SIZE = 128
A_HEADS = 8
A_KV_HEADS = 4
A_REP = A_HEADS // A_KV_HEADS
A_HEAD_DIM = 64
A_Q = A_HEADS * A_HEAD_DIM
A_KV = A_KV_HEADS * A_HEAD_DIM
IDX_HEADS = 8
IDX_DIM = 64
TOPK_MAX = 256
DN_HEADS = 4
DN_DK = 128
DN_DV = 128
DN_QK = DN_HEADS * DN_DK
DN_V = DN_HEADS * DN_DV
DN_CONV_CH = 2 * DN_QK + DN_V
CONV_W = 4
DN_CHUNK = 64
PEER_HEADS = 8
PEER_KEYS = 128
PEER_DKEY = 256
PEER_TOPK = 16
PEER_HALF = PEER_KEYS * PEER_KEYS // 2
PLE_DIM = 256
EPS = 1e-6
IN_SPLITS = (A_Q, A_KV, A_KV, IDX_HEADS * IDX_DIM, IDX_DIM, IDX_HEADS, DN_CONV_CH, DN_V, DN_HEADS, DN_HEADS, D_MODEL, D_MODEL)

LANES = 128
SUBLANES = 8
VMEM_LIMIT = 56 * 1024 * 1024

ROW_TILE = 256
ATTN_TQ = 256
ATTN_TK = 512
CONV_TT = 512
TOKEN_TILE = LANES
SEARCH_ROWS = 64
NO_TIE_CUT = 2 ** 30
NEG_INF_HI_KEY = (0xFF80 ^ 0x7FFF) - 2 ** 16
MIN_NORMAL = 2.0 ** -126
NEG_BIG = -1e30
Q_SCALE = A_HEAD_DIM ** -0.5 * math.log2(math.e)
INT_MIN = -(2 ** 31)

MISC_WI = IDX_DIM
MISC_DB = MISC_WI + IDX_HEADS
MISC_DA = MISC_DB + DN_HEADS

SEG_Q = 0
SEG_K = SEG_Q + A_Q
SEG_V = SEG_K + A_KV
SEG_QI = SEG_V + A_KV
SEG_MISC = SEG_QI + IDX_HEADS * IDX_DIM
SEG_DQKV = SEG_MISC + LANES
SEG_DZ = SEG_DQKV + DN_CONV_CH
SEG_GA = SEG_DZ + DN_V
SEG_GB = SEG_GA + D_MODEL
SEG_END = SEG_GB + D_MODEL


def _cparams(sem):
    return pltpu.CompilerParams(dimension_semantics=sem, vmem_limit_bytes=VMEM_LIMIT)


def _dot(a, b):
    return jnp.dot(a, b, preferred_element_type=F32)


def _dot_nt(a, b):
    return lax.dot_general(a, b, (((1,), (1,)), ((), ())), preferred_element_type=F32)


def _split(a):
    hi = a.astype(BF16)
    lo = (a - hi.astype(F32)).astype(BF16)
    return hi, lo


def _dot3(a, b, nt=False):
    d = _dot_nt if nt else _dot
    ah, al = _split(a)
    bh, bl = _split(b)
    return d(ah, bh) + (d(ah, bl) + d(al, bh))


def _lane_iota(shape):
    return lax.broadcasted_iota(I32, shape, len(shape) - 1)


def _row_iota(shape):
    return lax.broadcasted_iota(I32, shape, len(shape) - 2)


def _half_norm(blk, gain):
    lane = _lane_iota(blk.shape)
    lo = lane < A_HEAD_DIM
    sq = blk * blk
    s_lo = jnp.sum(jnp.where(lo, sq, 0.0), axis=-1, keepdims=True)
    s_hi = jnp.sum(jnp.where(lo, 0.0, sq), axis=-1, keepdims=True)
    r_lo = lax.rsqrt(s_lo * (1.0 / A_HEAD_DIM) + EPS)
    r_hi = lax.rsqrt(s_hi * (1.0 / A_HEAD_DIM) + EPS)
    return blk * jnp.where(lo, r_lo, r_hi) * gain


def _proj_kernel(x_ref, gmix_ref, w_ref, gq_ref, gk_ref, gik_ref,
                 q_ref, k_ref, kdup_ref, v_ref, vb_ref, qi_ref, misc_ref, kidup_ref,
                 dqkv_ref, dz_ref, ga_ref, gb_ref):
    x = x_ref[...]
    h = x * lax.rsqrt(jnp.mean(x * x, axis=-1, keepdims=True) + EPS) * gmix_ref[...]
    hb = h.astype(BF16)

    def seg(a, b):
        return _dot(hb, w_ref[:, a:b])

    lane = _lane_iota((x.shape[0], LANES))
    lo = lane < A_HEAD_DIM

    zq = seg(SEG_Q, SEG_K)
    for c in range(A_Q // LANES):
        blk = _half_norm(zq[:, c * LANES:(c + 1) * LANES], gq_ref[...])
        q_ref[:, c * LANES:(c + 1) * LANES] = (blk * Q_SCALE).astype(BF16)

    zk = seg(SEG_K, SEG_V)
    for c in range(A_KV // LANES):
        blk = _half_norm(zk[:, c * LANES:(c + 1) * LANES], gk_ref[...])
        k_ref[:, c * LANES:(c + 1) * LANES] = blk
        rolled = pltpu.roll(blk, A_HEAD_DIM, axis=1)
        kdup_ref[:, (2 * c) * LANES:(2 * c + 1) * LANES] = jnp.where(lo, blk, rolled).astype(BF16)
        kdup_ref[:, (2 * c + 1) * LANES:(2 * c + 2) * LANES] = jnp.where(lo, rolled, blk).astype(BF16)

    zv = seg(SEG_V, SEG_QI)
    v_ref[...] = zv
    vb_ref[...] = zv.astype(BF16)

    qi_ref[...] = seg(SEG_QI, SEG_MISC).astype(BF16)

    zm = seg(SEG_MISC, SEG_DQKV)
    s_ik = jnp.sum(jnp.where(lo, zm * zm, 0.0), axis=-1, keepdims=True)
    kin = zm * lax.rsqrt(s_ik * (1.0 / IDX_DIM) + EPS) * gik_ref[...]
    wscale = (IDX_HEADS * IDX_DIM) ** -0.5
    misc_ref[...] = jnp.where(lo, kin, jnp.where(lane < MISC_DB, zm * wscale, zm))
    kin0 = jnp.where(lo, kin, 0.0)
    kidup_ref[...] = (kin0 + pltpu.roll(kin0, IDX_DIM, axis=1)).astype(BF16)

    dqkv_ref[...] = seg(SEG_DQKV, SEG_DZ)
    dz_ref[...] = seg(SEG_DZ, SEG_GA)
    ga_ref[...] = seg(SEG_GA, SEG_GB)
    gb_ref[...] = seg(SEG_GB, SEG_END)


def _permute_w_in(w_in):
    cuts = np.cumsum((0,) + IN_SPLITS)
    parts = [w_in[:, cuts[i]:cuts[i + 1]] for i in range(len(IN_SPLITS))]
    aq, ak, av, iq, ik, iw, dqkv, dz, db, da, ga, gb = parts
    pad = jnp.zeros((w_in.shape[0], LANES - IDX_DIM - IDX_HEADS - 2 * DN_HEADS), w_in.dtype)
    return jnp.concatenate([aq, ak, av, iq, ik, iw, db, da, pad, dqkv, dz, ga, gb], axis=1).astype(BF16)


def _tile2(g):
    return jnp.concatenate([g, g]).reshape(1, LANES).astype(F32)


def _proj(x2d, g_mix, w_perm, g_q, g_k, g_idx_k, tm):
    n = x2d.shape[0]
    assert n % tm == 0
    widths = [(A_Q, BF16), (A_KV, F32), (2 * A_KV, BF16), (A_KV, F32), (A_KV, BF16), (IDX_HEADS * IDX_DIM, BF16),
              (LANES, F32), (LANES, BF16), (DN_CONV_CH, F32), (DN_V, F32), (D_MODEL, F32), (D_MODEL, F32)]
    row = lambda i: (i, 0)
    fixed = lambda i: (0, 0)
    return pl.pallas_call(
        _proj_kernel,
        grid=(n // tm,),
        in_specs=[pl.BlockSpec((tm, D_MODEL), row), pl.BlockSpec((1, D_MODEL), fixed),
                  pl.BlockSpec((D_MODEL, SEG_END), fixed), pl.BlockSpec((1, LANES), fixed),
                  pl.BlockSpec((1, LANES), fixed), pl.BlockSpec((1, LANES), fixed)],
        out_specs=[pl.BlockSpec((tm, w), row) for w, _ in widths],
        out_shape=[jax.ShapeDtypeStruct((n, w), dt) for w, dt in widths],
        compiler_params=_cparams(("parallel",)),
        name="proj_in",
    )(x2d, g_mix.reshape(1, D_MODEL), w_perm, _tile2(g_q), _tile2(g_k), _tile2(g_idx_k))


def _sort_key(score):
    score = jnp.where(score == 0.0, 0.0, score)
    bits = pltpu.bitcast(score, I32)
    return jnp.where(bits < 0, bits ^ jnp.int32(0x7FFFFFFF), bits)


def _index_scores(qi, wi_tile, kblk):
    lane = _lane_iota((qi.shape[0], LANES))
    lo = lane < IDX_DIM
    sc = None
    for c in range(IDX_HEADS // 2):
        q128 = qi[:, c * LANES:(c + 1) * LANES]
        zero = jnp.zeros_like(q128)
        for half in range(2):
            qm = jnp.where(lo, q128, zero) if half == 0 else jnp.where(lo, zero, q128)
            s = _dot_nt(qm, kblk)
            hidx = MISC_WI + 2 * c + half
            term = jnp.maximum(s, 0.0) * wi_tile[:, hidx:hidx + 1]
            sc = term if sc is None else sc + term
    return sc


def _select_rows(keys_ref, thr_ref, jcut_ref, cand_ref, nvalid, k_sel, tq, tk, rb, live_rows=None, khi_ref=None,
                 candh_ref=None):
    nchunk = tk // LANES
    nrb = tq // rb
    nbits_idx = int(math.ceil(math.log2(keys_ref.shape[0] * tk))) + 1
    assert keys_ref.shape[0] * nchunk <= 256
    lane = _lane_iota((rb, LANES))
    ones = jnp.ones((LANES, LANES), BF16)
    rows = lambda r: slice(r * rb, (r + 1) * rb)

    def count(pred, src_ref=keys_ref, cnd_ref=cand_ref, dtype=F32, blocks=tuple(range(nrb))):
        def body(kb, cnts):
            out = []
            for cnt, r in zip(cnts, blocks):
                blk = src_ref[kb, rows(r), :]
                cand_b = cnd_ref[rows(r), :]
                for c in range(nchunk):
                    cnt = cnt + pred(blk[:, c * LANES:(c + 1) * LANES], cand_b, kb * tk + c * LANES, r)
                out.append(cnt)
            return tuple(out)

        zeros = tuple(jnp.zeros((rb, LANES), dtype) for _ in blocks)
        cnts = lax.fori_loop(0, nvalid, body, zeros)
        return [_dot(cnt.astype(BF16), ones) for cnt in cnts]

    def search_bits(nbits, count_ge, to_cand):
        def step(i, carry):
            cand = thr_ref[...] + lax.shift_left(jnp.int32(1), nbits - 1 - i)
            to_cand(cand)
            tots = count_ge()
            for r in range(nrb):
                thr_ref[rows(r), :] = jnp.where(tots[r] >= k_sel, cand[rows(r)], thr_ref[rows(r), :])
            return carry
        lax.fori_loop(0, nbits, step, 0)

    def store_cand(cand):
        cand_ref[...] = cand

    ge_i32 = lambda: count(lambda kv, cb, base, r: jnp.where(kv >= cb, 1.0, 0.0))
    if khi_ref is None:
        thr_ref[...] = jnp.full((tq, LANES), INT_MIN, I32)
        search_bits(32, ge_i32, store_cand)
    else:
        half = 16

        def store_cand_hi(cand):
            bits = jnp.left_shift(jnp.where(cand < 0, cand ^ jnp.int32(0x7FFF), cand), half)
            val = pltpu.bitcast(bits, F32)
            val = jnp.where(cand < NEG_INF_HI_KEY, -jnp.inf, val)
            val = jnp.where(cand > 0, jnp.maximum(val, MIN_NORMAL), val)
            candh_ref[...] = val.astype(BF16)

        one, zero = jnp.ones((), BF16), jnp.zeros((), BF16)
        ge_bf16 = lambda: count(lambda kv, cb, base, r: jnp.where(kv >= cb, one, zero), khi_ref, candh_ref, BF16)
        thr_ref[...] = jnp.full((tq, LANES), -(2 ** (half - 1)), I32)
        search_bits(half, ge_bf16, store_cand_hi)
        thr_ref[...] = jnp.left_shift(thr_ref[...], half)
        search_bits(half, ge_i32, store_cand)

    cand_ref[...] = thr_ref[...]
    c_gt = count(lambda kv, cb, base, r: jnp.where(kv > cb, 1.0, 0.0))
    c_ge = count(lambda kv, cb, base, r: jnp.where(kv >= cb, 1.0, 0.0))
    need = [k_sel - c for c in c_gt]
    jcut_ref[...] = jnp.full((tq, LANES), NO_TIE_CUT, I32)

    if live_rows is not None:
        c_ge = [jnp.where(_row_iota((rb, LANES)) % SUBLANES < live_rows, c, 0.0) for c in c_ge]
    for r in range(nrb):
        @pl.when(jnp.max(c_ge[r]) > k_sel)
        def _(r=r):
            jcut_ref[rows(r), :] = jnp.zeros((rb, LANES), I32)

            def jstep(i, carry):
                candj = jcut_ref[rows(r), :] + lax.shift_left(jnp.int32(1), nbits_idx - 1 - i)
                cand_ref[rows(r), :] = candj
                f = count(lambda kv, cb, base, rr: jnp.where(kv == thr_ref[rows(rr), :],
                                                             jnp.where(lane + base < cb, 1.0, 0.0), 0.0), blocks=(r,))[0]
                jcut_ref[rows(r), :] = jnp.where(f < need[r], candj, jcut_ref[rows(r), :])
                return carry

            lax.fori_loop(0, nbits_idx, jstep, 0)


def _idx_mask_kernel(qi_ref, wi_ref, kidup_ref, mask_ref, keys_ref, thr_ref, jcut_ref, cand_ref, khi_ref, candh_ref, *,
                     k_sel, tq, tk, rb):
    i = pl.program_id(1)
    nkb = keys_ref.shape[0]
    nvalid = ((i + 1) * tq + tk - 1) // tk
    qi = qi_ref[0]
    wi_tile = wi_ref[0]
    q_pos = i * tq + _row_iota((tq, tk))
    lane = _lane_iota((tq, tk))

    def fill(kb, carry):
        kblk = kidup_ref[0, pl.ds(pl.multiple_of(kb * tk, tk), tk), :]
        sc = _index_scores(qi, wi_tile, kblk)
        sc = jnp.where(lane + kb * tk <= q_pos, sc, -jnp.inf)
        sc = jnp.where(sc == 0.0, 0.0, sc)
        keys_ref[kb] = _sort_key(sc)
        khi_ref[kb] = pltpu.bitcast(pltpu.bitcast(sc, I32) & HI_HALF, F32).astype(BF16)
        return carry

    lax.fori_loop(0, nvalid, fill, 0)
    _select_rows(keys_ref, thr_ref, jcut_ref, cand_ref, nvalid, k_sel, tq, tk, rb, khi_ref=khi_ref, candh_ref=candh_ref)

    for kb in range(nkb):
        @pl.when(kb < nvalid)
        def _():
            thr = jnp.broadcast_to(thr_ref[:, 0:1], (tq, tk))
            jcut = jnp.broadcast_to(jcut_ref[:, 0:1], (tq, tk))
            key = keys_ref[kb]
            pos = lane + kb * tk
            sel = jnp.where(key > thr, 1, jnp.where(key == thr, jnp.where(pos <= jcut, 1, 0), 0))
            sel = jnp.where(pos <= q_pos, sel, 0)
            mask_ref[0, :, kb * tk:(kb + 1) * tk] = sel.astype(jnp.int8)

        @pl.when(kb >= nvalid)
        def _():
            mask_ref[0, :, kb * tk:(kb + 1) * tk] = jnp.zeros((tq, tk), jnp.int8)


def _idx_mask(qi, misc, kidup, k_sel, tq, tk):
    b, s, _ = qi.shape
    assert s % tq == 0 and s % tk == 0 and tq % 32 == 0
    rb = min(SEARCH_ROWS, tq)
    kern = functools.partial(_idx_mask_kernel, k_sel=k_sel, tq=tq, tk=tk, rb=rb)
    return pl.pallas_call(
        kern,
        grid=(b, s // tq),
        in_specs=[pl.BlockSpec((1, tq, IDX_HEADS * IDX_DIM), lambda bi, i: (bi, i, 0)),
                  pl.BlockSpec((1, tq, LANES), lambda bi, i: (bi, i, 0)),
                  pl.BlockSpec((1, s, LANES), lambda bi, i: (bi, 0, 0))],
        out_specs=pl.BlockSpec((1, tq, s), lambda bi, i: (bi, i, 0)),
        out_shape=jax.ShapeDtypeStruct((b, s, s), jnp.int8),
        scratch_shapes=[pltpu.VMEM((s // tk, tq, tk), I32)] + [pltpu.VMEM((tq, LANES), I32)] * 3
                       + [pltpu.VMEM((s // tk, tq, tk), BF16), pltpu.VMEM((tq, LANES), BF16)],
        compiler_params=_cparams(("parallel", "parallel")),
        name="idx_mask",
    )(qi, misc, kidup)


def _attn_kernel(q_ref, kdup_ref, vb_ref, mask_ref, o_ref, m_ref, l_ref, acc_ref, *, tq, tk):
    i = pl.program_id(1)
    kb = pl.program_id(2)
    nk = pl.num_programs(2)
    last_needed = ((i + 1) * tq - 1) // tk

    @pl.when(kb == 0)
    def _():
        m_ref[...] = jnp.full(m_ref.shape, NEG_BIG, F32)
        l_ref[...] = jnp.zeros(l_ref.shape, F32)
        acc_ref[...] = jnp.zeros(acc_ref.shape, F32)

    @pl.when(kb <= last_needed)
    def _():
        bias = jnp.where(mask_ref[0].astype(I32) != 0, 0.0, NEG_BIG).astype(F32)
        bias2 = jnp.concatenate([bias, bias], axis=0)
        lane = _lane_iota((tq, LANES))
        lo = lane < A_HEAD_DIM

        def qk(g):
            q128 = q_ref[0, :, g * LANES:(g + 1) * LANES]
            zero = jnp.zeros_like(q128)
            q2 = jnp.concatenate([jnp.where(lo, q128, zero), jnp.where(lo, zero, q128)], axis=0)
            return _dot_nt(q2, kdup_ref[0, :, g * LANES:(g + 1) * LANES])

        s_next = qk(0)
        for g in range(A_KV_HEADS):
            s = s_next + bias2
            if g + 1 < A_KV_HEADS:
                s_next = qk(g + 1)
            m_old = m_ref[g]
            m_new = jnp.maximum(m_old, jnp.max(s, axis=1, keepdims=True))
            alpha = jnp.exp2(m_old - m_new)
            p = jnp.exp2(s - m_new[:, 0:1])
            l_ref[g] = alpha * l_ref[g] + jnp.sum(p, axis=1, keepdims=True)
            v128 = vb_ref[0, :, (g // 2) * LANES:(g // 2 + 1) * LANES]
            acc_ref[g] = alpha * acc_ref[g] + _dot(p.astype(BF16), v128)
            m_ref[g] = m_new

    @pl.when(kb == nk - 1)
    def _():
        lane = _lane_iota((tq, LANES))
        lo = lane < A_HEAD_DIM
        for g in range(A_KV_HEADS):
            a = acc_ref[g] / l_ref[g]
            top, bot = a[:tq], a[tq:]
            if g % 2 == 0:
                o128 = jnp.where(lo, top, pltpu.roll(bot, A_HEAD_DIM, axis=1))
            else:
                o128 = jnp.where(lo, pltpu.roll(top, A_HEAD_DIM, axis=1), bot)
            o_ref[0, :, g * LANES:(g + 1) * LANES] = o128


def _attn(q, kdup, vb, mask, tq, tk):
    b, s, _ = q.shape
    nk = s // tk

    def kv_idx(bi, i, kb):
        return (bi, jnp.minimum(kb, ((i + 1) * tq - 1) // tk), 0)

    def mask_idx(bi, i, kb):
        return (bi, i, jnp.minimum(kb, ((i + 1) * tq - 1) // tk))

    kern = functools.partial(_attn_kernel, tq=tq, tk=tk)
    return pl.pallas_call(
        kern,
        grid=(b, s // tq, nk),
        in_specs=[pl.BlockSpec((1, tq, A_Q), lambda bi, i, kb: (bi, i, 0)),
                  pl.BlockSpec((1, tk, 2 * A_KV), kv_idx),
                  pl.BlockSpec((1, tk, A_KV), kv_idx),
                  pl.BlockSpec((1, tq, tk), mask_idx)],
        out_specs=pl.BlockSpec((1, tq, A_Q), lambda bi, i, kb: (bi, i, 0)),
        out_shape=jax.ShapeDtypeStruct((b, s, A_Q), F32),
        scratch_shapes=[pltpu.VMEM((A_KV_HEADS, 2 * tq, LANES), F32)] * 3,
        compiler_params=_cparams(("parallel", "parallel", "arbitrary")),
        name="sel_attn",
    )(q, kdup, vb, mask)


def _conv_kernel(x_ref, xprev_ref, halo0_ref, w_ref, q_ref, k_ref, v_ref, *, tt):
    i = pl.program_id(1)
    x = x_ref[0]
    halo = jnp.where(i == 0, halo0_ref[0], xprev_ref[0])
    w = w_ref[...]

    def post(y, rows):
        y = y * jax.nn.sigmoid(y)
        for h in range(DN_HEADS):
            for j, ref in enumerate((q_ref, k_ref)):
                blk = y[:, j * DN_QK + h * DN_DK: j * DN_QK + (h + 1) * DN_DK]
                blk = blk * lax.rsqrt(jnp.sum(blk * blk, axis=-1, keepdims=True) + EPS)
                ref[0, 0:rows, h * DN_DK:(h + 1) * DN_DK] = blk
        v_ref[0, 0:rows, :] = y[:, 2 * DN_QK:]

    y = x * w[CONV_W - 1:CONV_W, :]
    for j in range(1, CONV_W):
        y = y + pltpu.roll(x, j, axis=0) * w[CONV_W - 1 - j:CONV_W - j, :]
    post(y, tt)

    x8 = x[0:SUBLANES]
    row = _row_iota(x8.shape)
    y8 = x8 * w[CONV_W - 1:CONV_W, :]
    for j in range(1, CONV_W):
        xs = jnp.where(row < j, pltpu.roll(halo, j, axis=0), pltpu.roll(x8, j, axis=0))
        y8 = y8 + xs * w[CONV_W - 1 - j:CONV_W - j, :]
    post(y8, SUBLANES)


def _conv(dqkv, halo0, conv_w, tt):
    b, t, c = dqkv.shape
    assert t % tt == 0 and tt % SUBLANES == 0
    r = tt // SUBLANES
    kern = functools.partial(_conv_kernel, tt=tt)
    out = jax.ShapeDtypeStruct((b, t, DN_QK), F32)
    return pl.pallas_call(
        kern,
        grid=(b, t // tt),
        in_specs=[pl.BlockSpec((1, tt, c), lambda bi, i: (bi, i, 0)),
                  pl.BlockSpec((1, SUBLANES, c), lambda bi, i: (bi, jnp.maximum(i * r - 1, 0), 0)),
                  pl.BlockSpec((1, SUBLANES, c), lambda bi, i: (bi, 0, 0)),
                  pl.BlockSpec((CONV_W, c), lambda bi, i: (0, 0))],
        out_specs=[pl.BlockSpec((1, tt, DN_QK), lambda bi, i: (bi, i, 0))] * 3,
        out_shape=[out, out, out],
        compiler_params=_cparams(("parallel", "parallel")),
        name="dn_conv",
    )(dqkv, dqkv, halo0, conv_w)


def _softplus(x):
    return jnp.maximum(x, 0.0) + jnp.log(1.0 + jnp.exp(-jnp.abs(x)))


def _pad_rows(a, rows):
    if a.shape[0] == rows:
        return a
    return jnp.concatenate([a, jnp.zeros((rows - a.shape[0], a.shape[1]), a.dtype)], axis=0)


DELTA_SEQS_PER_STEP = 4


def _delta_kernel(q_ref, k_ref, v_ref, z_ref, misc_ref, nega_ref, dtb_ref, gout_ref, s0_ref, o_ref, st_ref, *, c, t_valid, bb):
    n = pl.program_id(1)

    @pl.when(n == 0)
    def _():
        st_ref[...] = s0_ref[...]

    row = _row_iota((c, LANES))
    live = row < t_valid
    ri = lax.broadcasted_iota(I32, (c, c), 0)
    ci = lax.broadcasted_iota(I32, (c, c), 1)
    incl = ri >= ci
    strict = ri > ci
    eye = jnp.where(ri == ci, 1.0, 0.0).astype(F32)
    scale = DN_DK ** -0.5

    chains = [(bi, h) for bi in range(bb) for h in range(DN_HEADS)]
    nch = range(len(chains))
    beta_t, gc_t, gc_tr = [], [], []
    for bi in range(bb):
        mt = misc_ref[bi]
        beta_t.append(jnp.where(live, jax.nn.sigmoid(mt), 0.0))
        g = jnp.where(live, nega_ref[...] * _softplus(mt + dtb_ref[...]), 0.0)
        sft = 1
        while sft < c:
            g = g + jnp.where(row >= sft, pltpu.roll(g, sft, axis=0), 0.0)
            sft *= 2
        gc_t.append(g)
        gc_tr.append(_pad_rows(g, LANES).T)

    sl = lambda h: slice(h * DN_DK, (h + 1) * DN_DK)
    q = [jnp.where(live, q_ref[bi, :, sl(h)] * scale, 0.0) for bi, h in chains]
    k = [jnp.where(live, k_ref[bi, :, sl(h)], 0.0) for bi, h in chains]
    v = [jnp.where(live, v_ref[bi, :, sl(h)], 0.0) for bi, h in chains]
    beta = [beta_t[bi][:, MISC_DB + h:MISC_DB + h + 1] for bi, h in chains]
    gcc = [gc_t[bi][:, MISC_DA + h:MISC_DA + h + 1] for bi, h in chains]
    gcr = [gc_tr[bi][MISC_DA + h:MISC_DA + h + 1, 0:c] for bi, h in chains]
    decay = [jnp.where(incl, jnp.exp(jnp.where(incl, gcc[i] - gcr[i], 0.0)), 0.0) for i in nch]
    kb = [k[i] * beta[i] for i in nch]
    pw = [-jnp.where(strict, _dot3(kb[i], k[i], nt=True) * decay[i], 0.0) for i in nch]
    r = [eye + pw[i] for i in nch]
    nn = 2
    while nn < c:
        pw = [_dot3(pw[i], pw[i]) for i in nch]
        r = [_dot3(r[i], eye + pw[i]) for i in nch]
        nn *= 2
    egc = [jnp.exp(gcc[i]) for i in nch]
    u = [_dot3(r[i], v[i] * beta[i]) for i in nch]
    w = [_dot3(r[i], kb[i] * egc[i]) for i in nch]
    a_qk = [_dot3(q[i], k[i], nt=True) * decay[i] for i in nch]
    s = [st_ref[bi, h] for bi, h in chains]
    v_new = [u[i] - _dot3(w[i], s[i]) for i in nch]
    o = [_dot3(q[i] * egc[i], s[i]) + _dot3(a_qk[i], v_new[i]) for i in nch]
    glast = [gcc[i][c - 1:c, :] for i in nch]
    kw_t = [_pad_rows(k[i] * jnp.exp(glast[i] - gcc[i]), LANES).T for i in nch]
    s_new = [s[i] * jnp.exp(glast[i]) + _dot3(kw_t[i], _pad_rows(v_new[i], LANES)) for i in nch]
    for i, (bi, h) in enumerate(chains):
        st_ref[bi, h] = s_new[i]
        on = o[i] * lax.rsqrt(jnp.mean(o[i] * o[i], axis=-1, keepdims=True) + EPS) * gout_ref[...]
        z = z_ref[bi, :, sl(h)]
        o_ref[bi, :, sl(h)] = on * (z * jax.nn.sigmoid(z))


def _delta(qn, kn, v, dz, misc, a_log, dt_bias, g_dn_out, s0, c, t_valid):
    b, t, _ = qn.shape
    assert t % c == 0 and (t_valid == t or t == c)
    nega = jnp.zeros((1, LANES), F32).at[0, MISC_DA:MISC_DA + DN_HEADS].set(-jnp.exp(a_log.astype(F32)))
    dtb = jnp.zeros((1, LANES), F32).at[0, MISC_DA:MISC_DA + DN_HEADS].set(dt_bias.astype(F32))
    tok = lambda bi, n: (bi, n, 0)
    fixed = lambda bi, n: (0, 0)
    st = lambda bi, n: (bi, 0, 0, 0)
    bb = DELTA_SEQS_PER_STEP if b % DELTA_SEQS_PER_STEP == 0 else 1
    kern = functools.partial(_delta_kernel, c=c, t_valid=t_valid, bb=bb)
    return pl.pallas_call(
        kern,
        grid=(b // bb, t // c),
        in_specs=[pl.BlockSpec((bb, c, DN_QK), tok), pl.BlockSpec((bb, c, DN_QK), tok), pl.BlockSpec((bb, c, DN_V), tok),
                  pl.BlockSpec((bb, c, DN_V), tok), pl.BlockSpec((bb, c, LANES), tok),
                  pl.BlockSpec((1, LANES), fixed), pl.BlockSpec((1, LANES), fixed), pl.BlockSpec((1, DN_DV), fixed),
                  pl.BlockSpec((bb, DN_HEADS, DN_DK, DN_DV), st)],
        out_specs=[pl.BlockSpec((bb, c, DN_V), tok), pl.BlockSpec((bb, DN_HEADS, DN_DK, DN_DV), st)],
        out_shape=[jax.ShapeDtypeStruct((b, t, DN_V), F32), jax.ShapeDtypeStruct((b, DN_HEADS, DN_DK, DN_DV), F32)],
        compiler_params=_cparams(("parallel", "arbitrary")),
        name="delta_rule",
    )(qn, kn, v, dz, misc, nega, dtb, g_dn_out.reshape(1, DN_DV).astype(F32), s0)


def _merge_kernel(x_ref, oa_ref, ob_ref, ga_ref, gb_ref, woa_ref, wob_ref, wout_ref, gffn_ref, wpqt_ref,
                  x1_ref, h2_ref, qt_ref):
    ma = jax.nn.sigmoid(ga_ref[...]) * _dot(oa_ref[...].astype(BF16), woa_ref[...])
    mb = jax.nn.sigmoid(gb_ref[...]) * _dot(ob_ref[...].astype(BF16), wob_ref[...])
    x1 = x_ref[...] + _dot((ma + mb).astype(BF16), wout_ref[...])
    x1_ref[...] = x1
    h2 = x1 * lax.rsqrt(jnp.mean(x1 * x1, axis=-1, keepdims=True) + EPS) * gffn_ref[...]
    h2_ref[...] = h2
    qt_ref[...] = _dot_nt(wpqt_ref[...], h2.astype(BF16))


def _merge(x2d, oa, ob, ga, gb, w_oa, w_ob, w_out, g_ffn, w_pq, tm):
    n = x2d.shape[0]
    nq = w_pq.shape[1]
    row = lambda i: (i, 0)
    fixed = lambda i: (0, 0)
    return pl.pallas_call(
        _merge_kernel,
        grid=(n // tm,),
        in_specs=[pl.BlockSpec((tm, D_MODEL), row), pl.BlockSpec((tm, A_Q), row), pl.BlockSpec((tm, DN_V), row),
                  pl.BlockSpec((tm, D_MODEL), row), pl.BlockSpec((tm, D_MODEL), row),
                  pl.BlockSpec((A_Q, D_MODEL), fixed), pl.BlockSpec((DN_V, D_MODEL), fixed),
                  pl.BlockSpec((D_MODEL, D_MODEL), fixed), pl.BlockSpec((1, D_MODEL), fixed),
                  pl.BlockSpec((nq, D_MODEL), fixed)],
        out_specs=[pl.BlockSpec((tm, D_MODEL), row), pl.BlockSpec((tm, D_MODEL), row), pl.BlockSpec((nq, tm), lambda i: (0, i))],
        out_shape=[jax.ShapeDtypeStruct((n, D_MODEL), F32), jax.ShapeDtypeStruct((n, D_MODEL), F32),
                   jax.ShapeDtypeStruct((nq, n), F32)],
        compiler_params=_cparams(("parallel",)),
        name="merge_out",
    )(x2d, oa, ob, ga, gb, w_oa.astype(BF16), w_ob.astype(BF16), w_out.astype(BF16),
      g_ffn.reshape(1, D_MODEL).astype(F32), w_pq.T.astype(BF16))


def _peer_cands():
    cands = [(i, j) for i in range(PEER_TOPK) for j in range(PEER_TOPK) if (i + 1) * (j + 1) <= PEER_TOPK]
    return sorted(cands, key=lambda ij: ij[0] * PEER_TOPK + ij[1])


def _route_kernel(qt_ref, sk_ref, eid_ref, gate_ref, toff_ref, val_ref, idx_ref, *, tn):
    half = PEER_DKEY // 2
    key_iota = lax.broadcasted_iota(I32, (PEER_KEYS, tn), 0)

    heads_per_iter = 4

    def per_group(hg, carry):
        tabs = [(hh, c) for hh in range(heads_per_iter) for c in range(2)]
        ss = []
        for hh, c in tabs:
            t = 2 * (hg * heads_per_iter + hh) + c
            qblk = qt_ref[pl.ds(pl.multiple_of(t * half, half), half), :]
            ss.append(_dot3(sk_ref[t], qblk))
        for r in range(PEER_TOPK):
            ms = [jnp.max(s, axis=0, keepdims=True) for s in ss]
            ams = [jnp.min(jnp.where(s == m, key_iota, PEER_KEYS), axis=0, keepdims=True) for s, m in zip(ss, ms)]
            for i, (hh, c) in enumerate(tabs):
                h = hg * heads_per_iter + hh
                val_ref[c, r, pl.ds(h, 1), :] = ms[i]
                idx_ref[c, r, pl.ds(h, 1), :] = ams[i]
            ss = [jnp.where(key_iota == am, -jnp.inf, s) for s, am in zip(ss, ams)]
        return carry

    lax.fori_loop(0, PEER_HEADS // heads_per_iter, per_group, 0)

    cands = _peer_cands()
    cv = [val_ref[0, i] + val_ref[1, j] for i, j in cands]
    ce = [idx_ref[0, i] * PEER_KEYS + idx_ref[1, j] for i, j in cands]
    nc = len(cands)
    rank = []
    for a in range(nc):
        rk = jnp.zeros(cv[a].shape, I32)
        for b in range(nc):
            if b < a:
                rk = rk + jnp.where(cv[b] >= cv[a], 1, 0)
            elif b > a:
                rk = rk + jnp.where(cv[b] > cv[a], 1, 0)
        rank.append(rk)
    mx = cv[0]
    ex = [jnp.where(rank[a] < PEER_TOPK, jnp.exp(cv[a] - mx), 0.0) for a in range(nc)]
    den = ex[0]
    for a in range(1, nc):
        den = den + ex[a]
    inv = 1.0 / den
    e_slots = []
    for slot in range(PEER_TOPK):
        e = jnp.zeros(cv[0].shape, I32)
        g = jnp.zeros(cv[0].shape, F32)
        for a in range(nc):
            hit = rank[a] == slot
            e = jnp.where(hit, ce[a], e)
            g = jnp.where(hit, ex[a], g)
        eid_ref[slot] = e
        gate_ref[slot] = g * inv
        e_slots.append(e)
    toff_ref[...] = ((jnp.concatenate(e_slots, axis=0) & (PEER_HALF - 1)) * SUBLANES).T


def _route(qt, sub_keys, tn):
    nq, n = qt.shape
    sk = sub_keys.reshape(2 * PEER_HEADS, PEER_KEYS, PEER_DKEY // 2).astype(F32)
    kern = functools.partial(_route_kernel, tn=tn)
    blk = pl.BlockSpec((PEER_TOPK, PEER_HEADS, tn), lambda i: (0, 0, i))
    npair = PEER_TOPK * PEER_HEADS
    return pl.pallas_call(
        kern,
        grid=(n // tn,),
        in_specs=[pl.BlockSpec((nq, tn), lambda i: (0, i)),
                  pl.BlockSpec((2 * PEER_HEADS, PEER_KEYS, PEER_DKEY // 2), lambda i: (0, 0, 0))],
        out_specs=[blk, blk, pl.BlockSpec((tn, npair), lambda i: (i, 0))],
        out_shape=[jax.ShapeDtypeStruct((PEER_TOPK, PEER_HEADS, n), I32), jax.ShapeDtypeStruct((PEER_TOPK, PEER_HEADS, n), F32),
                   jax.ShapeDtypeStruct((n, npair), I32)],
        scratch_shapes=[pltpu.VMEM((2, PEER_TOPK, PEER_HEADS, tn), F32), pltpu.VMEM((2, PEER_TOPK, PEER_HEADS, tn), I32)],
        compiler_params=_cparams(("parallel",)),
        name="peer_route",
    )(qt, sk)


def _pack_table(tab):
    e, d = tab.shape
    assert d == SUBLANES * LANES
    bits = lax.bitcast_convert_type(tab.astype(BF16), jnp.uint16).astype(jnp.uint32)
    assert e == 2 * PEER_HALF
    words = (bits[:PEER_HALF] << 16) | bits[PEER_HALF:]
    return lax.bitcast_convert_type(words, I32).reshape(e // 2 * SUBLANES, LANES)


HI_HALF = -65536


def _bf16_words(x):
    return pltpu.bitcast(x.astype(BF16).astype(F32), I32) & HI_HALF


def _shr16(w):
    return lax.shift_right_logical(w, jnp.full(w.shape, 16, I32))


def _packed_mul(a_words, b_words):
    return pltpu.bitcast(pltpu.bitcast(a_words, BF16) * pltpu.bitcast(b_words, BF16), I32)


def _packed_add(a_words, b_words):
    return pltpu.bitcast(pltpu.bitcast(a_words, BF16) + pltpu.bitcast(b_words, BF16), I32)


def _hi_f32(w):
    return pltpu.bitcast(w & HI_HALF, F32)


def _lo_f32(w):
    return pltpu.bitcast(jnp.left_shift(w, 16), F32)


_BITREV8 = (0, 4, 2, 6, 1, 5, 3, 7)


PEER_U_SUB = 16


def _peer_u_kernel(toff_ref, x_ref, eid_ref, gate_ref, tab_ref, ce_ref, co_ref, rhi_ref, rlo_ref, *, tn, sub_t):
    npair = gate_ref.shape[0]
    lane = _lane_iota((npair, tn))
    sub = lax.broadcasted_iota(I32, (SUBLANES, LANES), 0)
    keep = {step: (sub & step) == 0 for step in (4, 2, 1)}

    def per_token(tt, t0):
        t = t0 + tt
        xw = _bf16_words(x_ref[t])
        xw = xw | _shr16(xw)
        for g in range(npair // SUBLANES):
            prods = []
            for p in _BITREV8:
                off = pl.multiple_of(toff_ref[t * npair + (g * SUBLANES + p)], SUBLANES)
                prods.append(_packed_mul(tab_ref[pl.ds(off, SUBLANES), :], xw))
            step = SUBLANES // 2
            while step >= 1:
                nxt = []
                for a in range(0, len(prods), 2):
                    lo_t, hi_t = prods[a], prods[a + 1]
                    left = jnp.where(keep[step], lo_t, pltpu.roll(hi_t, step, axis=0))
                    right = jnp.where(keep[step], pltpu.roll(lo_t, SUBLANES - step, axis=0), hi_t)
                    nxt.append(_packed_add(left, right))
                prods = nxt
                step //= 2
            rhi_ref[tt, g * SUBLANES:(g + 1) * SUBLANES, :] = _hi_f32(prods[0])
            rlo_ref[tt, g * SUBLANES:(g + 1) * SUBLANES, :] = _lo_f32(prods[0])
        return t0

    def per_sub(sb, accs):
        acc_hi, acc_lo = accs
        t0 = sb * sub_t
        lax.fori_loop(0, sub_t, per_token, t0)
        ones = jnp.ones((LANES, LANES), BF16)
        for tt in range(sub_t):
            hit = lane == t0 + tt
            acc_hi = jnp.where(hit, jnp.sum(rhi_ref[tt], axis=1, keepdims=True), acc_hi)
            acc_lo = jnp.where(hit, _dot(rlo_ref[tt].astype(BF16), ones)[:, 0:tn], acc_lo)
        return acc_hi, acc_lo

    zero = jnp.zeros((npair, tn), F32)
    acc_hi, acc_lo = lax.fori_loop(0, tn // sub_t, per_sub, (zero, zero))
    even = eid_ref[...] < PEER_HALF
    coef = gate_ref[...] * jax.nn.gelu(jnp.where(even, acc_hi, acc_lo))
    ce_ref[...] = jnp.where(even, coef, 0.0)
    co_ref[...] = jnp.where(even, 0.0, coef)


def _peer_v_kernel(toff_ref, ce_ref, co_ref, rep_ref, tab_ref, y_ref, bw_ref, *, tn, npair):
    ce = ce_ref[...].astype(BF16)
    co = co_ref[...].astype(BF16)
    chunk = min(16, tn)
    for c in range(tn // chunk):
        rep = rep_ref[:, c * chunk * LANES:(c + 1) * chunk * LANES]
        be = _dot(ce, rep)
        bo = _dot(co, rep)
        words = (pltpu.bitcast(be, I32) & HI_HALF) | _shr16(pltpu.bitcast(bo, I32))
        for t in range(chunk):
            bw_ref[c * chunk + t] = words[:, t * LANES:(t + 1) * LANES]

    nacc = 4

    def per_token(t, carry):
        zero = jnp.zeros((SUBLANES, LANES), F32)
        acc_hi = [zero] * nacc
        acc_lo = [zero] * nacc
        base = t * npair
        for k in range(npair):
            off = pl.multiple_of(toff_ref[base + k], SUBLANES)
            cw = jnp.broadcast_to(bw_ref[t, k:k + 1, :], (SUBLANES, LANES))
            prod = _packed_mul(tab_ref[pl.ds(off, SUBLANES), :], cw)
            acc_hi[k % nacc] = acc_hi[k % nacc] + _hi_f32(prod)
            acc_lo[k % nacc] = acc_lo[k % nacc] + _lo_f32(prod)
        y_ref[t] = ((acc_hi[0] + acc_hi[1]) + (acc_hi[2] + acc_hi[3])) + ((acc_lo[0] + acc_lo[1]) + (acc_lo[2] + acc_lo[3]))
        return carry

    lax.fori_loop(0, tn, per_token, 0)


def _peer_experts(eid, gate, toff, h2, tab_u, tab_v, tn):
    npair, n = eid.shape
    d = h2.shape[1]
    x3 = h2.reshape(n, d // LANES, LANES)
    tab_spec = pl.BlockSpec(tab_u.shape, lambda i: (0, 0), pipeline_mode=pl.Buffered(1))
    smem_blk = lambda tt: pl.BlockSpec((tt * npair,), lambda i: (i,), memory_space=pltpu.SMEM)
    slot_blk = lambda tt: pl.BlockSpec((npair, tt), lambda i: (0, i))
    sub_t = PEER_U_SUB if tn % PEER_U_SUB == 0 else tn
    ce, co = pl.pallas_call(
        functools.partial(_peer_u_kernel, tn=tn, sub_t=sub_t),
        grid=(n // tn,),
        in_specs=[smem_blk(tn), pl.BlockSpec((tn, d // LANES, LANES), lambda i: (i, 0, 0)), slot_blk(tn), slot_blk(tn), tab_spec],
        out_specs=[slot_blk(tn), slot_blk(tn)],
        out_shape=[jax.ShapeDtypeStruct((npair, n), F32)] * 2,
        scratch_shapes=[pltpu.VMEM((sub_t, npair, LANES), F32)] * 2,
        compiler_params=_cparams(("arbitrary",)),
        name="peer_u",
    )(toff.reshape(n * npair), x3, eid, gate, tab_u)
    tv = tn
    rep = jnp.asarray(np.arange(tv)[:, None] == np.arange(tv * LANES)[None, :] // LANES, BF16)
    y3 = pl.pallas_call(
        functools.partial(_peer_v_kernel, tn=tv, npair=npair),
        grid=(n // tv,),
        in_specs=[smem_blk(tv), slot_blk(tv), slot_blk(tv),
                  pl.BlockSpec(rep.shape, lambda i: (0, 0), pipeline_mode=pl.Buffered(1)), tab_spec],
        out_specs=pl.BlockSpec((tv, d // LANES, LANES), lambda i: (i, 0, 0)),
        out_shape=jax.ShapeDtypeStruct((n, d // LANES, LANES), F32),
        scratch_shapes=[pltpu.VMEM((tv, npair, LANES), I32)],
        compiler_params=_cparams(("arbitrary",)),
        name="peer_v",
    )(toff.reshape(n * npair), ce, co, rep, tab_v)
    return y3.reshape(n, d)


def _ple_kernel(x1_ref, yp_ref, p_ref, gple_ref, wg_ref, wp_ref, o_ref):
    x2 = x1_ref[...] + yp_ref[...]
    hn = x2 * lax.rsqrt(jnp.mean(x2 * x2, axis=-1, keepdims=True) + EPS) * gple_ref[...]
    gate = jax.nn.sigmoid(_dot(hn.astype(BF16), wg_ref[...]))
    o_ref[...] = x2 + gate * _dot(p_ref[...].astype(BF16), wp_ref[...])


def _ple(x1, yp, p2d, g_ple, w_ple_gate, w_ple, tm):
    n = x1.shape[0]
    row = lambda i: (i, 0)
    fixed = lambda i: (0, 0)
    return pl.pallas_call(
        _ple_kernel,
        grid=(n // tm,),
        in_specs=[pl.BlockSpec((tm, D_MODEL), row), pl.BlockSpec((tm, D_MODEL), row), pl.BlockSpec((tm, PLE_DIM), row),
                  pl.BlockSpec((1, D_MODEL), fixed), pl.BlockSpec((D_MODEL, D_MODEL), fixed),
                  pl.BlockSpec((PLE_DIM, D_MODEL), fixed)],
        out_specs=pl.BlockSpec((tm, D_MODEL), row),
        out_shape=jax.ShapeDtypeStruct((n, D_MODEL), F32),
        compiler_params=_cparams(("parallel",)),
        name="ple_out",
    )(x1, yp, p2d, g_ple.reshape(1, D_MODEL).astype(F32), w_ple_gate.astype(BF16), w_ple.astype(BF16))


PAGES_PER_STEP = 32
SAMPLE_SOFTMAX_STATES = 4


def _sample_scores_kernel(pt_ref, qh_ref, wrep_ref, kinew_ref, *refs, npg):
    pages = refs[:npg]
    sc_ref, scnew_ref = refs[npg:]
    j = pl.program_id(1)
    qh = qh_ref[0]
    wrep = wrep_ref[0]

    def raw(keys_t):
        return _dot(qh, keys_t.astype(BF16))

    def score(s):
        s = jnp.maximum(s, 0.0) * wrep
        tot = s[0:SUBLANES]
        for h in range(1, IDX_HEADS):
            tot = tot + s[h * SUBLANES:(h + 1) * SUBLANES]
        return tot

    dots = [raw(pages[p][0]) for p in range(npg)]
    for p in range(npg):
        sc_ref[0, :, p * PAGE_SIZE:(p + 1) * PAGE_SIZE] = score(dots[p])

    @pl.when(j == 0)
    def _():
        sn = score(raw(kinew_ref[0]))
        causal = _lane_iota((SUBLANES, PAGE_SIZE)) <= _row_iota((SUBLANES, PAGE_SIZE))
        scnew_ref[0] = jnp.where(causal, sn, -jnp.inf)


def _sample_thr_kernel(sc_ref, thr_ref, jcut_ref, keys_ref, cand_ref, *, k_sel, tq, tk, t):
    nkb = keys_ref.shape[0]
    for kb in range(nkb):
        keys_ref[kb] = _sort_key(sc_ref[:, :, kb * tk:(kb + 1) * tk].reshape(tq, tk))
    _select_rows(keys_ref, thr_ref, jcut_ref, cand_ref, nkb, k_sel, tq, tk, min(SEARCH_ROWS, tq), live_rows=t)


def _sample_attn_kernel(pt_ref, q_ref, sc_ref, scnew_ref, thr_ref, jcut_ref, knew_ref, vnew_ref, *refs, npg, past):
    kpages = refs[:npg]
    vpages = refs[npg:2 * npg]
    o_ref, m_ref, l_ref, acc_ref = refs[2 * npg:]
    j = pl.program_id(1)

    @pl.when(j == 0)
    def _():
        m_ref[...] = jnp.full(m_ref.shape, NEG_BIG, F32)
        l_ref[...] = jnp.zeros(l_ref.shape, F32)
        acc_ref[...] = jnp.zeros(acc_ref.shape, F32)

    q = q_ref[0]
    thr = thr_ref[...]
    jcut = jcut_ref[...]
    lane = _lane_iota((SUBLANES, PAGE_SIZE))
    kvd = A_KV_HEADS * A_HEAD_DIM

    wide = lambda a: jnp.concatenate([a] * (kvd // LANES), axis=1)

    def update(groups):
        ss = []
        for _, kts, _, scs, pos0s, causal in groups:
            biases = []
            for sc8, pos0 in zip(scs, pos0s):
                key = _sort_key(sc8)
                pos = lane + pos0
                sel = jnp.where(key > thr, 0.0, jnp.where(key == thr, jnp.where(pos <= jcut, 0.0, NEG_BIG), NEG_BIG))
                if causal is not None:
                    sel = jnp.where(causal, sel, NEG_BIG)
                biases.append(sel)
            bias = jnp.concatenate([jnp.concatenate(biases, axis=1)] * A_HEADS, axis=0)
            ss.append(jnp.concatenate([_dot(q, kt[...].reshape(kvd, PAGE_SIZE).astype(BF16)) for kt in kts], axis=1) + bias)
        m_olds = [m_ref[g[0]] for g in groups]
        m_news = [jnp.maximum(mo, jnp.max(s, axis=1, keepdims=True)) for mo, s in zip(m_olds, ss)]
        alphas = [jnp.exp2(mo - mn) for mo, mn in zip(m_olds, m_news)]
        ps = [jnp.exp2(s - mn[:, 0:1]) for s, mn in zip(ss, m_news)]
        sums = [jnp.sum(p, axis=1, keepdims=True) for p in ps]
        pvs = []
        for (_, _, vts, _, _, _), p in zip(groups, ps):
            pb = p.astype(BF16)
            parts = [_dot_nt(pb[:, i * PAGE_SIZE:(i + 1) * PAGE_SIZE], vt[...].reshape(kvd, PAGE_SIZE).astype(BF16))
                     for i, vt in enumerate(vts)]
            pv = parts[0]
            for part in parts[1:]:
                pv = pv + part
            pvs.append(pv)
        for g, mn, al, sm, pv in zip(groups, m_news, alphas, sums, pvs):
            ci = g[0]
            l_ref[ci] = al * l_ref[ci] + sm
            acc_ref[ci] = wide(al) * acc_ref[ci] + pv
            m_ref[ci] = mn

    nst = m_ref.shape[0]
    update([(ci, [kpages[p].at[0] for p in range(ci, npg, nst)], [vpages[p].at[0] for p in range(ci, npg, nst)],
             [sc_ref[0, :, p * PAGE_SIZE:(p + 1) * PAGE_SIZE] for p in range(ci, npg, nst)],
             [(j * npg + p) * PAGE_SIZE for p in range(ci, npg, nst)], None) for ci in range(nst)])

    @pl.when(j == pl.num_programs(1) - 1)
    def _():
        update([(0, [knew_ref.at[0]], [vnew_ref.at[0]], [scnew_ref[0]], [past], lane <= _row_iota((SUBLANES, PAGE_SIZE)))])
        m_all = m_ref[0]
        for ci in range(1, nst):
            m_all = jnp.maximum(m_all, m_ref[ci])
        l_all = jnp.zeros(m_all.shape, F32)
        acc_all = jnp.zeros(acc_ref.shape[1:], F32)
        for ci in range(nst):
            w = jnp.exp2(m_ref[ci] - m_all)
            l_all = l_all + w * l_ref[ci]
            acc_all = acc_all + wide(w) * acc_ref[ci]
        o_ref[0] = acc_all / wide(l_all)


def _head_major(a, bd, t, nh, hd):
    a = a.reshape(bd, t, nh, hd).transpose(0, 2, 1, 3)
    a = jnp.pad(a, ((0, 0), (0, 0), (0, SUBLANES - t), (0, 0)))
    return a.reshape(bd, nh * SUBLANES, hd)


def _attn_sample(q, k, v, qi, misc, cache_k, cache_v, cache_idx_k, page_table, bd, t):
    assert t <= SUBLANES
    n_pages = page_table.shape[1]
    past = n_pages * PAGE_SIZE
    npg = PAGES_PER_STEP if n_pages % PAGES_PER_STEP == 0 else 1
    nj = n_pages // npg
    k_sel = min(TOPK_MAX, (past + t) // 4)
    rows = A_HEADS * SUBLANES

    qh = _head_major(qi, bd, t, IDX_HEADS, IDX_DIM)
    wi = misc[:, MISC_WI:MISC_WI + IDX_HEADS].reshape(bd, t, IDX_HEADS).transpose(0, 2, 1)
    wrep = jnp.broadcast_to(jnp.pad(wi, ((0, 0), (0, 0), (0, SUBLANES - t))).reshape(bd, rows, 1), (bd, rows, LANES))
    pad_page = lambda a, w: jnp.pad(a.reshape(bd, t, w), ((0, 0), (0, PAGE_SIZE - t), (0, 0)))
    kinew = pad_page(misc[:, :IDX_DIM], IDX_DIM).transpose(0, 2, 1)
    cik = cache_idx_k.transpose(0, 2, 1)

    per_b = lambda b, j, pt: (b, 0, 0)
    per_b4 = lambda b, j, pt: (b, 0, 0, 0)
    page_spec = lambda shape, p: pl.BlockSpec(shape, lambda b, j, pt: (pt[b, j * npg + p],) + (0,) * (len(shape) - 1))
    sc, scnew = pl.pallas_call(
        functools.partial(_sample_scores_kernel, npg=npg),
        grid_spec=pltpu.PrefetchScalarGridSpec(
            num_scalar_prefetch=1, grid=(bd, nj),
            in_specs=[pl.BlockSpec((1, rows, IDX_DIM), per_b), pl.BlockSpec((1, rows, LANES), per_b),
                      pl.BlockSpec((1, IDX_DIM, PAGE_SIZE), per_b)]
                     + [page_spec((1, IDX_DIM, PAGE_SIZE), p) for p in range(npg)],
            out_specs=[pl.BlockSpec((1, SUBLANES, npg * PAGE_SIZE), lambda b, j, pt: (b, 0, j)),
                       pl.BlockSpec((1, SUBLANES, PAGE_SIZE), per_b)]),
        out_shape=[jax.ShapeDtypeStruct((bd, SUBLANES, past), F32), jax.ShapeDtypeStruct((bd, SUBLANES, PAGE_SIZE), F32)],
        compiler_params=_cparams(("parallel", "arbitrary")),
        name="sample_scores",
    )(page_table, qh, wrep, kinew, *([cik] * npg))

    length = past + PAGE_SIZE
    sc_all = jnp.concatenate([sc, scnew], axis=2)
    tb = SUBLANES if bd % SUBLANES == 0 else 1
    tq = tb * SUBLANES
    tk = 5 * LANES if length % (5 * LANES) == 0 else LANES
    thr, jcut = pl.pallas_call(
        functools.partial(_sample_thr_kernel, k_sel=k_sel, tq=tq, tk=tk, t=t),
        grid=(bd // tb,),
        in_specs=[pl.BlockSpec((tb, SUBLANES, length), lambda i: (i, 0, 0))],
        out_specs=[pl.BlockSpec((tq, LANES), lambda i: (i, 0))] * 2,
        out_shape=[jax.ShapeDtypeStruct((bd * SUBLANES, LANES), I32)] * 2,
        scratch_shapes=[pltpu.VMEM((length // tk, tq, tk), I32), pltpu.VMEM((tq, LANES), I32)],
        compiler_params=_cparams(("parallel",)),
        name="sample_thr",
    )(sc_all)

    qa = _head_major(q, bd, t, A_HEADS, A_HEAD_DIM)
    kv_of_row = (np.arange(rows) // SUBLANES) // A_REP
    own = jnp.asarray(kv_of_row[:, None] == np.arange(A_KV_HEADS)[None, :])
    qa = jnp.where(own[None, :, :, None], qa[:, :, None, :], jnp.zeros((), qa.dtype)).reshape(bd, rows, A_KV)
    to_page_t = lambda a: pad_page(a, A_KV).reshape(bd, PAGE_SIZE, A_KV_HEADS, A_HEAD_DIM).transpose(0, 2, 3, 1)
    knew, vnew = to_page_t(k), to_page_t(v)
    ck = cache_k.transpose(0, 2, 3, 1)
    cv = cache_v.transpose(0, 2, 3, 1)
    kv_blk = (1, A_KV_HEADS, A_HEAD_DIM, PAGE_SIZE)
    nst = SAMPLE_SOFTMAX_STATES if npg % SAMPLE_SOFTMAX_STATES == 0 else 1
    o = pl.pallas_call(
        functools.partial(_sample_attn_kernel, npg=npg, past=past),
        grid_spec=pltpu.PrefetchScalarGridSpec(
            num_scalar_prefetch=1, grid=(bd, nj),
            in_specs=[pl.BlockSpec((1, rows, A_KV), per_b),
                      pl.BlockSpec((1, SUBLANES, npg * PAGE_SIZE), lambda b, j, pt: (b, 0, j)),
                      pl.BlockSpec((1, SUBLANES, PAGE_SIZE), per_b),
                      pl.BlockSpec((SUBLANES, LANES), lambda b, j, pt: (b, 0)),
                      pl.BlockSpec((SUBLANES, LANES), lambda b, j, pt: (b, 0)),
                      pl.BlockSpec(kv_blk, per_b4), pl.BlockSpec(kv_blk, per_b4)]
                     + [page_spec(kv_blk, p) for p in range(npg)] * 2,
            out_specs=pl.BlockSpec((1, rows, A_KV), per_b),
            scratch_shapes=[pltpu.VMEM((nst, rows, LANES), F32), pltpu.VMEM((nst, rows, LANES), F32),
                            pltpu.VMEM((nst, rows, A_KV), F32)]),
        out_shape=jax.ShapeDtypeStruct((bd, rows, A_KV), F32),
        compiler_params=_cparams(("parallel", "arbitrary")),
        name="sample_attn",
    )(page_table, qa, sc, scnew, thr, jcut, knew, vnew, *([ck] * npg), *([cv] * npg))
    o = jnp.sum(jnp.where(own[None, :, :, None], o.reshape(bd, rows, A_KV_HEADS, A_HEAD_DIM), 0.0), axis=2)
    o = o.reshape(bd, A_HEADS, SUBLANES, A_HEAD_DIM)[:, :, :t].transpose(0, 2, 1, 3)
    return o.reshape(bd * t, A_Q)


def _tile(n, pref):
    return pref if n % pref == 0 else n


def _layer(x, p_emb, conv_state, delta_state, attn_fn, prm):
    b, t, _ = x.shape
    n = b * t
    x2d = x.reshape(n, D_MODEL)
    tm = _tile(n, ROW_TILE)
    (q, k, kdup, v, vb, qi, misc, kidup, dqkv, dz, ga, gb) = _proj(
        x2d, prm['g_mix'], prm['w_perm'], prm['g_q'], prm['g_k'], prm['g_idx_k'], tm)

    oa = attn_fn(q, k, kdup, v, vb, qi, misc, kidup)

    tp = -(-t // SUBLANES) * SUBLANES
    c = min(DN_CHUNK, tp)
    pad_t = lambda a: jnp.pad(a.reshape(b, t, a.shape[-1]), ((0, 0), (0, tp - t), (0, 0)))
    dqkv3 = dqkv.reshape(b, t, DN_CONV_CH)
    halo = jnp.zeros((b, SUBLANES, DN_CONV_CH), F32)
    if conv_state is not None:
        halo = halo.at[:, SUBLANES - (CONV_W - 1):].set(conv_state.astype(F32))
        hist = jnp.concatenate([conv_state.astype(F32), dqkv3], axis=1)
    else:
        hist = jnp.concatenate([jnp.zeros((b, CONV_W - 1, DN_CONV_CH), F32), dqkv3], axis=1)
    new_conv = hist[:, -(CONV_W - 1):]
    qn, kn, vv = _conv(pad_t(dqkv), halo, prm['conv_w'], _tile(tp, CONV_TT))
    s0 = jnp.zeros((b, DN_HEADS, DN_DK, DN_DV), F32) if delta_state is None else delta_state.astype(F32)
    ob, new_delta = _delta(qn, kn, vv, pad_t(dz), pad_t(misc), prm['a_log'], prm['dt_bias'], prm['g_dn_out'], s0, c, t)
    ob = ob[:, :t].reshape(n, DN_V)

    x1, h2, qt = _merge(x2d, oa, ob, ga, gb, prm['w_oa'], prm['w_ob'], prm['w_out'], prm['g_ffn'], prm['w_pq'], tm)
    eid, gate, toff = _route(qt, prm['sub_keys'], _tile(n, TOKEN_TILE))
    npair = PEER_TOPK * PEER_HEADS
    yp = _peer_experts(eid.reshape(npair, n), gate.reshape(npair, n), toff, h2, prm['tab_u'], prm['tab_v'], _tile(n, TOKEN_TILE))
    y = _ple(x1, yp, p_emb.reshape(n, PLE_DIM), prm['g_ple'], prm['w_ple_gate'], prm['w_ple'], tm)

    return (y.reshape(b, t, D_MODEL), k.reshape(b, t, A_KV_HEADS, A_HEAD_DIM), v.reshape(b, t, A_KV_HEADS, A_HEAD_DIM),
            misc[:, :IDX_DIM].reshape(b, t, IDX_DIM), new_conv, new_delta)


def kernel(x_prompt, x_sample, cache_k, cache_v, cache_idx_k, state_conv, state_delta, page_table, p_prompt, p_sample,
           g_mix, w_in, g_q, g_k, g_idx_k, conv_w, a_log, dt_bias, g_dn_out, w_oa, w_ob, w_out, g_ffn, w_pq, sub_keys,
           peer_u, peer_v, g_ple, w_ple_gate, w_ple):
    depth = w_in.shape[0]
    xp, xs = x_prompt, x_sample
    outs = [[] for _ in range(10)]
    for i in range(depth):
        prm = dict(g_mix=g_mix[i], w_perm=_permute_w_in(w_in[i]), g_q=g_q[i], g_k=g_k[i], g_idx_k=g_idx_k[i],
                   conv_w=conv_w[i].astype(F32), a_log=a_log[i], dt_bias=dt_bias[i], g_dn_out=g_dn_out[i],
                   w_oa=w_oa[i], w_ob=w_ob[i], w_out=w_out[i], g_ffn=g_ffn[i], w_pq=w_pq[i], sub_keys=sub_keys[i],
                   tab_u=_pack_table(peer_u[i]), tab_v=_pack_table(peer_v[i]), g_ple=g_ple[i],
                   w_ple_gate=w_ple_gate[i], w_ple=w_ple[i])

        bp, s, _ = xp.shape

        def attn_prompt(q, k, kdup, v, vb, qi, misc, kidup):
            k_sel = min(TOPK_MAX, s // 4)
            tq, tk = _tile(s, ATTN_TQ), _tile(s, ATTN_TK)
            r3 = lambda a: a.reshape(bp, s, a.shape[-1])
            mask = _idx_mask(r3(qi), r3(misc), r3(kidup), k_sel, tq, tk)
            return _attn(r3(q), r3(kdup), r3(vb), mask, tq, tk).reshape(bp * s, A_Q)

        res = _layer(xp, p_prompt[i], None, None, attn_prompt, prm)
        xp = res[0]
        for lst, val in zip(outs[:5], res[1:]):
            lst.append(val)

        bd, t, _ = xs.shape

        def attn_sample(q, k, kdup, v, vb, qi, misc, kidup):
            return _attn_sample(q, k, v, qi, misc, cache_k[i], cache_v[i], cache_idx_k[i], page_table, bd, t)

        res = _layer(xs, p_sample[i], state_conv[i], state_delta[i], attn_sample, prm)
        xs = res[0]
        for lst, val in zip(outs[5:], res[1:]):
            lst.append(val)

    cast = [cache_k.dtype, cache_v.dtype, cache_idx_k.dtype, state_conv.dtype, state_delta.dtype] * 2
    stacked = [jnp.stack(lst).astype(dt) for lst, dt in zip(outs, cast)]
    return (xp, xs, *stacked)
```

```python
import functools
import math

import jax
import jax.numpy as jnp
import numpy as np
from jax import lax
from jax.experimental import pallas as pl
from jax.experimental.pallas import tpu as pltpu

F32 = jnp.float32
BF16 = jnp.bfloat16
I32 = jnp.int32

D_MODEL = 1024
PAGE_SIZE = 128
A_HEADS = 8
A_KV_HEADS = 4
A_REP = A_HEADS // A_KV_HEADS
A_HEAD_DIM = 64
A_Q = A_HEADS * A_HEAD_DIM
A_KV = A_KV_HEADS * A_HEAD_DIM
IDX_HEADS = 8
IDX_DIM = 64
TOPK_MAX = 256
DN_HEADS = 4
DN_DK = 128
DN_DV = 128
DN_QK = DN_HEADS * DN_DK
DN_V = DN_HEADS * DN_DV
DN_CONV_CH = 2 * DN_QK + DN_V
CONV_W = 4
DN_CHUNK = 64
PEER_HEADS = 8
PEER_KEYS = 128
PEER_DKEY = 256
PEER_TOPK = 16
PEER_HALF = PEER_KEYS * PEER_KEYS // 2
PLE_DIM = 256
EPS = 1e-6
IN_SPLITS = (A_Q, A_KV, A_KV, IDX_HEADS * IDX_DIM, IDX_DIM, IDX_HEADS, DN_CONV_CH, DN_V, DN_HEADS, DN_HEADS, D_MODEL, D_MODEL)

LANES = 128
SUBLANES = 8
VMEM_LIMIT = 56 * 1024 * 1024

ROW_TILE = 256
ATTN_TQ = 256
ATTN_TK = 512
ATTN_KEY_BLOCKS_PER_STEP = 2
CONV_TT = 512
TOKEN_TILE = LANES
SEARCH_ROWS = 64
NO_TIE_CUT = 2 ** 30
NEG_INF_HI_KEY = (0xFF80 ^ 0x7FFF) - 2 ** 16
MIN_NORMAL = 2.0 ** -126
NEG_BIG = -1e30
Q_SCALE = A_HEAD_DIM ** -0.5 * math.log2(math.e)
INT_MIN = -(2 ** 31)

MISC_WI = IDX_DIM
MISC_DB = MISC_WI + IDX_HEADS
MISC_DA = MISC_DB + DN_HEADS

SEG_Q = 0
SEG_K = SEG_Q + A_Q
SEG_V = SEG_K + A_KV
SEG_QI = SEG_V + A_KV
SEG_MISC = SEG_QI + IDX_HEADS * IDX_DIM
SEG_DQKV = SEG_MISC + LANES
SEG_DZ = SEG_DQKV + DN_CONV_CH
SEG_GA = SEG_DZ + DN_V
SEG_GB = SEG_GA + D_MODEL
SEG_END = SEG_GB + D_MODEL


def _cparams(sem):
    return pltpu.CompilerParams(dimension_semantics=sem, vmem_limit_bytes=VMEM_LIMIT)


def _dot(a, b):
    return jnp.dot(a, b, preferred_element_type=F32)


def _dot_nt(a, b):
    return lax.dot_general(a, b, (((1,), (1,)), ((), ())), preferred_element_type=F32)


def _split(a):
    hi = a.astype(BF16)
    lo = (a - hi.astype(F32)).astype(BF16)
    return hi, lo


def _dot3(a, b, nt=False):
    d = _dot_nt if nt else _dot
    ah, al = _split(a)
    bh, bl = _split(b)
    return d(ah, bh) + (d(ah, bl) + d(al, bh))


def _lane_iota(shape):
    return lax.broadcasted_iota(I32, shape, len(shape) - 1)


def _row_iota(shape):
    return lax.broadcasted_iota(I32, shape, len(shape) - 2)


def _half_norm(blk, gain):
    lane = _lane_iota(blk.shape)
    lo = lane < A_HEAD_DIM
    sq = blk * blk
    s_lo = jnp.sum(jnp.where(lo, sq, 0.0), axis=-1, keepdims=True)
    s_hi = jnp.sum(jnp.where(lo, 0.0, sq), axis=-1, keepdims=True)
    r_lo = lax.rsqrt(s_lo * (1.0 / A_HEAD_DIM) + EPS)
    r_hi = lax.rsqrt(s_hi * (1.0 / A_HEAD_DIM) + EPS)
    return blk * jnp.where(lo, r_lo, r_hi) * gain


def _proj_kernel(x_ref, gmix_ref, w_ref, gq_ref, gk_ref, gik_ref,
                 q_ref, k_ref, kdup_ref, v_ref, vb_ref, qi_ref, misc_ref, kidup_ref,
                 dqkv_ref, dz_ref, ga_ref, gb_ref):
    x = x_ref[...]
    h = x * lax.rsqrt(jnp.mean(x * x, axis=-1, keepdims=True) + EPS) * gmix_ref[...]
    hb = h.astype(BF16)

    def seg(a, b):
        return _dot(hb, w_ref[:, a:b])

    lane = _lane_iota((x.shape[0], LANES))
    lo = lane < A_HEAD_DIM

    zq = seg(SEG_Q, SEG_K)
    for c in range(A_Q // LANES):
        blk = _half_norm(zq[:, c * LANES:(c + 1) * LANES], gq_ref[...])
        q_ref[:, c * LANES:(c + 1) * LANES] = (blk * Q_SCALE).astype(BF16)

    zk = seg(SEG_K, SEG_V)
    for c in range(A_KV // LANES):
        blk = _half_norm(zk[:, c * LANES:(c + 1) * LANES], gk_ref[...])
        k_ref[:, c * LANES:(c + 1) * LANES] = blk
        rolled = pltpu.roll(blk, A_HEAD_DIM, axis=1)
        kdup_ref[:, (2 * c) * LANES:(2 * c + 1) * LANES] = jnp.where(lo, blk, rolled).astype(BF16)
        kdup_ref[:, (2 * c + 1) * LANES:(2 * c + 2) * LANES] = jnp.where(lo, rolled, blk).astype(BF16)

    zv = seg(SEG_V, SEG_QI)
    v_ref[...] = zv
    vb_ref[...] = zv.astype(BF16)

    qi_ref[...] = seg(SEG_QI, SEG_MISC).astype(BF16)

    zm = seg(SEG_MISC, SEG_DQKV)
    s_ik = jnp.sum(jnp.where(lo, zm * zm, 0.0), axis=-1, keepdims=True)
    kin = zm * lax.rsqrt(s_ik * (1.0 / IDX_DIM) + EPS) * gik_ref[...]
    wscale = (IDX_HEADS * IDX_DIM) ** -0.5
    misc_ref[...] = jnp.where(lo, kin, jnp.where(lane < MISC_DB, zm * wscale, zm))
    kin0 = jnp.where(lo, kin, 0.0)
    kidup_ref[...] = (kin0 + pltpu.roll(kin0, IDX_DIM, axis=1)).astype(BF16)

    dqkv_ref[...] = seg(SEG_DQKV, SEG_DZ)
    dz_ref[...] = seg(SEG_DZ, SEG_GA)
    ga_ref[...] = seg(SEG_GA, SEG_GB)
    gb_ref[...] = seg(SEG_GB, SEG_END)


def _permute_w_in(w_in):
    cuts = np.cumsum((0,) + IN_SPLITS)
    parts = [w_in[:, cuts[i]:cuts[i + 1]] for i in range(len(IN_SPLITS))]
    aq, ak, av, iq, ik, iw, dqkv, dz, db, da, ga, gb = parts
    pad = jnp.zeros((w_in.shape[0], LANES - IDX_DIM - IDX_HEADS - 2 * DN_HEADS), w_in.dtype)
    return jnp.concatenate([aq, ak, av, iq, ik, iw, db, da, pad, dqkv, dz, ga, gb], axis=1).astype(BF16)


def _tile2(g):
    return jnp.concatenate([g, g]).reshape(1, LANES).astype(F32)


def _proj(x2d, g_mix, w_perm, g_q, g_k, g_idx_k, tm):
    n = x2d.shape[0]
    assert n % tm == 0
    widths = [(A_Q, BF16), (A_KV, F32), (2 * A_KV, BF16), (A_KV, F32), (A_KV, BF16), (IDX_HEADS * IDX_DIM, BF16),
              (LANES, F32), (LANES, BF16), (DN_CONV_CH, F32), (DN_V, F32), (D_MODEL, F32), (D_MODEL, F32)]
    row = lambda i: (i, 0)
    fixed = lambda i: (0, 0)
    return pl.pallas_call(
        _proj_kernel,
        grid=(n // tm,),
        in_specs=[pl.BlockSpec((tm, D_MODEL), row), pl.BlockSpec((1, D_MODEL), fixed),
                  pl.BlockSpec((D_MODEL, SEG_END), fixed), pl.BlockSpec((1, LANES), fixed),
                  pl.BlockSpec((1, LANES), fixed), pl.BlockSpec((1, LANES), fixed)],
        out_specs=[pl.BlockSpec((tm, w), row) for w, _ in widths],
        out_shape=[jax.ShapeDtypeStruct((n, w), dt) for w, dt in widths],
        compiler_params=_cparams(("parallel",)),
        name="proj_in",
    )(x2d, g_mix.reshape(1, D_MODEL), w_perm, _tile2(g_q), _tile2(g_k), _tile2(g_idx_k))


def _sort_key(score):
    score = jnp.where(score == 0.0, 0.0, score)
    bits = pltpu.bitcast(score, I32)
    return jnp.where(bits < 0, bits ^ jnp.int32(0x7FFFFFFF), bits)


def _index_scores(qi, wi_tile, kblk):
    lane = _lane_iota((qi.shape[0], LANES))
    lo = lane < IDX_DIM
    sc = None
    for c in range(IDX_HEADS // 2):
        q128 = qi[:, c * LANES:(c + 1) * LANES]
        zero = jnp.zeros_like(q128)
        for half in range(2):
            qm = jnp.where(lo, q128, zero) if half == 0 else jnp.where(lo, zero, q128)
            s = _dot_nt(qm, kblk)
            hidx = MISC_WI + 2 * c + half
            term = jnp.maximum(s, 0.0) * wi_tile[:, hidx:hidx + 1]
            sc = term if sc is None else sc + term
    return sc


def _select_rows(keys_ref, thr_ref, jcut_ref, cand_ref, nvalid, k_sel, tq, tk, rb, live_rows=None, khi_ref=None,
                 candh_ref=None):
    nchunk = tk // LANES
    nrb = tq // rb
    nbits_idx = int(math.ceil(math.log2(keys_ref.shape[0] * tk))) + 1
    assert keys_ref.shape[0] * nchunk <= 256
    lane = _lane_iota((rb, LANES))
    ones = jnp.ones((LANES, LANES), BF16)
    rows = lambda r: slice(r * rb, (r + 1) * rb)

    def count(pred, src_ref=keys_ref, cnd_ref=cand_ref, dtype=F32, blocks=tuple(range(nrb))):
        def body(kb, cnts):
            out = []
            for cnt, r in zip(cnts, blocks):
                blk = src_ref[kb, rows(r), :]
                cand_b = cnd_ref[rows(r), :]
                for c in range(nchunk):
                    cnt = cnt + pred(blk[:, c * LANES:(c + 1) * LANES], cand_b, kb * tk + c * LANES, r)
                out.append(cnt)
            return tuple(out)

        zeros = tuple(jnp.zeros((rb, LANES), dtype) for _ in blocks)
        cnts = lax.fori_loop(0, nvalid, body, zeros)
        return [_dot(cnt.astype(BF16), ones) for cnt in cnts]

    def search_bits(nbits, count_ge, to_cand):
        def step(i, carry):
            cand = thr_ref[...] + lax.shift_left(jnp.int32(1), nbits - 1 - i)
            to_cand(cand)
            tots = count_ge()
            for r in range(nrb):
                thr_ref[rows(r), :] = jnp.where(tots[r] >= k_sel, cand[rows(r)], thr_ref[rows(r), :])
            return carry
        lax.fori_loop(0, nbits, step, 0)

    def store_cand(cand):
        cand_ref[...] = cand

    ge_i32 = lambda: count(lambda kv, cb, base, r: jnp.where(kv >= cb, 1.0, 0.0))
    if khi_ref is None:
        thr_ref[...] = jnp.full((tq, LANES), INT_MIN, I32)
        search_bits(32, ge_i32, store_cand)
    else:
        half = 16

        def store_cand_hi(cand):
            bits = jnp.left_shift(jnp.where(cand < 0, cand ^ jnp.int32(0x7FFF), cand), half)
            val = pltpu.bitcast(bits, F32)
            val = jnp.where(cand < NEG_INF_HI_KEY, -jnp.inf, val)
            val = jnp.where(cand > 0, jnp.maximum(val, MIN_NORMAL), val)
            candh_ref[...] = val.astype(BF16)

        one, zero = jnp.ones((), BF16), jnp.zeros((), BF16)
        ge_bf16 = lambda: count(lambda kv, cb, base, r: jnp.where(kv >= cb, one, zero), khi_ref, candh_ref, BF16)
        thr_ref[...] = jnp.full((tq, LANES), -(2 ** (half - 1)), I32)
        search_bits(half, ge_bf16, store_cand_hi)
        thr_ref[...] = jnp.left_shift(thr_ref[...], half)
        search_bits(half, ge_i32, store_cand)

    cand_ref[...] = thr_ref[...]
    c_gt = count(lambda kv, cb, base, r: jnp.where(kv > cb, 1.0, 0.0))
    c_ge = count(lambda kv, cb, base, r: jnp.where(kv >= cb, 1.0, 0.0))
    need = [k_sel - c for c in c_gt]
    jcut_ref[...] = jnp.full((tq, LANES), NO_TIE_CUT, I32)

    if live_rows is not None:
        c_ge = [jnp.where(_row_iota((rb, LANES)) % SUBLANES < live_rows, c, 0.0) for c in c_ge]
    for r in range(nrb):
        @pl.when(jnp.max(c_ge[r]) > k_sel)
        def _(r=r):
            jcut_ref[rows(r), :] = jnp.zeros((rb, LANES), I32)

            def jstep(i, carry):
                candj = jcut_ref[rows(r), :] + lax.shift_left(jnp.int32(1), nbits_idx - 1 - i)
                cand_ref[rows(r), :] = candj
                f = count(lambda kv, cb, base, rr: jnp.where(kv == thr_ref[rows(rr), :],
                                                             jnp.where(lane + base < cb, 1.0, 0.0), 0.0), blocks=(r,))[0]
                jcut_ref[rows(r), :] = jnp.where(f < need[r], candj, jcut_ref[rows(r), :])
                return carry

            lax.fori_loop(0, nbits_idx, jstep, 0)


def _idx_mask_kernel(qi_ref, wi_ref, kidup_ref, mask_ref, keys_ref, thr_ref, jcut_ref, cand_ref, khi_ref, candh_ref, *,
                     k_sel, tq, tk, rb):
    i = pl.program_id(1)
    nkb = keys_ref.shape[0]
    nvalid = ((i + 1) * tq + tk - 1) // tk
    qi = qi_ref[0]
    wi_tile = wi_ref[0]
    q_pos = i * tq + _row_iota((tq, tk))
    lane = _lane_iota((tq, tk))

    def fill(kb, carry):
        kblk = kidup_ref[0, pl.ds(pl.multiple_of(kb * tk, tk), tk), :]
        sc = _index_scores(qi, wi_tile, kblk)
        sc = jnp.where(lane + kb * tk <= q_pos, sc, -jnp.inf)
        sc = jnp.where(sc == 0.0, 0.0, sc)
        keys_ref[kb] = _sort_key(sc)
        khi_ref[kb] = pltpu.bitcast(pltpu.bitcast(sc, I32) & HI_HALF, F32).astype(BF16)
        return carry

    lax.fori_loop(0, nvalid, fill, 0)
    _select_rows(keys_ref, thr_ref, jcut_ref, cand_ref, nvalid, k_sel, tq, tk, rb, khi_ref=khi_ref, candh_ref=candh_ref)

    for kb in range(nkb):
        @pl.when(kb < nvalid)
        def _():
            thr = jnp.broadcast_to(thr_ref[:, 0:1], (tq, tk))
            jcut = jnp.broadcast_to(jcut_ref[:, 0:1], (tq, tk))
            key = keys_ref[kb]
            pos = lane + kb * tk
            sel = jnp.where(key > thr, 1, jnp.where(key == thr, jnp.where(pos <= jcut, 1, 0), 0))
            sel = jnp.where(pos <= q_pos, sel, 0)
            mask_ref[0, :, kb * tk:(kb + 1) * tk] = sel.astype(jnp.int8)

        @pl.when(kb >= nvalid)
        def _():
            mask_ref[0, :, kb * tk:(kb + 1) * tk] = jnp.zeros((tq, tk), jnp.int8)


def _idx_mask(qi, misc, kidup, k_sel, tq, tk):
    b, s, _ = qi.shape
    assert s % tq == 0 and s % tk == 0 and tq % 32 == 0
    rb = min(SEARCH_ROWS, tq)
    kern = functools.partial(_idx_mask_kernel, k_sel=k_sel, tq=tq, tk=tk, rb=rb)
    return pl.pallas_call(
        kern,
        grid=(b, s // tq),
        in_specs=[pl.BlockSpec((1, tq, IDX_HEADS * IDX_DIM), lambda bi, i: (bi, i, 0)),
                  pl.BlockSpec((1, tq, LANES), lambda bi, i: (bi, i, 0)),
                  pl.BlockSpec((1, s, LANES), lambda bi, i: (bi, 0, 0))],
        out_specs=pl.BlockSpec((1, tq, s), lambda bi, i: (bi, i, 0)),
        out_shape=jax.ShapeDtypeStruct((b, s, s), jnp.int8),
        scratch_shapes=[pltpu.VMEM((s // tk, tq, tk), I32)] + [pltpu.VMEM((tq, LANES), I32)] * 3
                       + [pltpu.VMEM((s // tk, tq, tk), BF16), pltpu.VMEM((tq, LANES), BF16)],
        compiler_params=_cparams(("parallel", "parallel")),
        name="idx_mask",
    )(qi, misc, kidup)


def _attn_kernel(q_ref, kdup_ref, vb_ref, mask_ref, o_ref, m_ref, l_ref, acc_ref, *, tq, tk, nsub):
    i = pl.program_id(1)
    kb = pl.program_id(2)
    nk = pl.num_programs(2)
    last_needed = ((i + 1) * tq - 1) // tk

    @pl.when(kb == 0)
    def _():
        m_ref[...] = jnp.full(m_ref.shape, NEG_BIG, F32)
        l_ref[...] = jnp.zeros(l_ref.shape, F32)
        acc_ref[...] = jnp.zeros(acc_ref.shape, F32)

    def key_block(sb):
        ks = slice(sb * tk, (sb + 1) * tk)
        bias = jnp.where(mask_ref[0, :, ks].astype(I32) != 0, 0.0, NEG_BIG).astype(F32)
        bias2 = jnp.concatenate([bias, bias], axis=0)
        lane = _lane_iota((tq, LANES))
        lo = lane < A_HEAD_DIM

        def qk(g):
            q128 = q_ref[0, :, g * LANES:(g + 1) * LANES]
            zero = jnp.zeros_like(q128)
            q2 = jnp.concatenate([jnp.where(lo, q128, zero), jnp.where(lo, zero, q128)], axis=0)
            return _dot_nt(q2, kdup_ref[0, ks, g * LANES:(g + 1) * LANES])

        s_next = qk(0)
        for g in range(A_KV_HEADS):
            s = s_next + bias2
            if g + 1 < A_KV_HEADS:
                s_next = qk(g + 1)
            m_old = m_ref[g]
            m_new = jnp.maximum(m_old, jnp.max(s, axis=1, keepdims=True))
            alpha = jnp.exp2(m_old - m_new)
            p = jnp.exp2(s - m_new[:, 0:1])
            l_ref[g] = alpha * l_ref[g] + jnp.sum(p, axis=1, keepdims=True)
            v128 = vb_ref[0, ks, (g // 2) * LANES:(g // 2 + 1) * LANES]
            acc_ref[g] = alpha * acc_ref[g] + _dot(p.astype(BF16), v128)
            m_ref[g] = m_new

    for sb in range(nsub):
        pl.when(kb * nsub + sb <= last_needed)(functools.partial(key_block, sb))

    @pl.when(kb == nk - 1)
    def _():
        lane = _lane_iota((tq, LANES))
        lo = lane < A_HEAD_DIM
        for g in range(A_KV_HEADS):
            a = acc_ref[g] / l_ref[g]
            top, bot = a[:tq], a[tq:]
            if g % 2 == 0:
                o128 = jnp.where(lo, top, pltpu.roll(bot, A_HEAD_DIM, axis=1))
            else:
                o128 = jnp.where(lo, pltpu.roll(top, A_HEAD_DIM, axis=1), bot)
            o_ref[0, :, g * LANES:(g + 1) * LANES] = o128


def _attn(q, kdup, vb, mask, tq, tk):
    b, s, _ = q.shape
    nsub = ATTN_KEY_BLOCKS_PER_STEP if s % (tk * ATTN_KEY_BLOCKS_PER_STEP) == 0 else 1
    tkb = tk * nsub
    nk = s // tkb

    def kv_idx(bi, i, kb):
        return (bi, jnp.minimum(kb, ((i + 1) * tq - 1) // tkb), 0)

    def mask_idx(bi, i, kb):
        return (bi, i, jnp.minimum(kb, ((i + 1) * tq - 1) // tkb))

    kern = functools.partial(_attn_kernel, tq=tq, tk=tk, nsub=nsub)
    return pl.pallas_call(
        kern,
        grid=(b, s // tq, nk),
        in_specs=[pl.BlockSpec((1, tq, A_Q), lambda bi, i, kb: (bi, i, 0)),
                  pl.BlockSpec((1, tkb, 2 * A_KV), kv_idx),
                  pl.BlockSpec((1, tkb, A_KV), kv_idx),
                  pl.BlockSpec((1, tq, tkb), mask_idx)],
        out_specs=pl.BlockSpec((1, tq, A_Q), lambda bi, i, kb: (bi, i, 0)),
        out_shape=jax.ShapeDtypeStruct((b, s, A_Q), F32),
        scratch_shapes=[pltpu.VMEM((A_KV_HEADS, 2 * tq, LANES), F32)] * 3,
        compiler_params=_cparams(("parallel", "parallel", "arbitrary")),
        name="sel_attn",
    )(q, kdup, vb, mask)


def _conv_kernel(x_ref, xprev_ref, halo0_ref, w_ref, q_ref, k_ref, v_ref, *, tt):
    i = pl.program_id(1)
    x = x_ref[0]
    halo = jnp.where(i == 0, halo0_ref[0], xprev_ref[0])
    w = w_ref[...]

    def post(y, rows):
        y = y * jax.nn.sigmoid(y)
        for h in range(DN_HEADS):
            for j, ref in enumerate((q_ref, k_ref)):
                blk = y[:, j * DN_QK + h * DN_DK: j * DN_QK + (h + 1) * DN_DK]
                blk = blk * lax.rsqrt(jnp.sum(blk * blk, axis=-1, keepdims=True) + EPS)
                ref[0, 0:rows, h * DN_DK:(h + 1) * DN_DK] = blk
        v_ref[0, 0:rows, :] = y[:, 2 * DN_QK:]

    y = x * w[CONV_W - 1:CONV_W, :]
    for j in range(1, CONV_W):
        y = y + pltpu.roll(x, j, axis=0) * w[CONV_W - 1 - j:CONV_W - j, :]
    post(y, tt)

    x8 = x[0:SUBLANES]
    row = _row_iota(x8.shape)
    y8 = x8 * w[CONV_W - 1:CONV_W, :]
    for j in range(1, CONV_W):
        xs = jnp.where(row < j, pltpu.roll(halo, j, axis=0), pltpu.roll(x8, j, axis=0))
        y8 = y8 + xs * w[CONV_W - 1 - j:CONV_W - j, :]
    post(y8, SUBLANES)


def _conv(dqkv, halo0, conv_w, tt):
    b, t, c = dqkv.shape
    assert t % tt == 0 and tt % SUBLANES == 0
    r = tt // SUBLANES
    kern = functools.partial(_conv_kernel, tt=tt)
    out = jax.ShapeDtypeStruct((b, t, DN_QK), F32)
    return pl.pallas_call(
        kern,
        grid=(b, t // tt),
        in_specs=[pl.BlockSpec((1, tt, c), lambda bi, i: (bi, i, 0)),
                  pl.BlockSpec((1, SUBLANES, c), lambda bi, i: (bi, jnp.maximum(i * r - 1, 0), 0)),
                  pl.BlockSpec((1, SUBLANES, c), lambda bi, i: (bi, 0, 0)),
                  pl.BlockSpec((CONV_W, c), lambda bi, i: (0, 0))],
        out_specs=[pl.BlockSpec((1, tt, DN_QK), lambda bi, i: (bi, i, 0))] * 3,
        out_shape=[out, out, out],
        compiler_params=_cparams(("parallel", "parallel")),
        name="dn_conv",
    )(dqkv, dqkv, halo0, conv_w)


def _softplus(x):
    return jnp.maximum(x, 0.0) + jnp.log(1.0 + jnp.exp(-jnp.abs(x)))


def _pad_rows(a, rows):
    if a.shape[0] == rows:
        return a
    return jnp.concatenate([a, jnp.zeros((rows - a.shape[0], a.shape[1]), a.dtype)], axis=0)


DELTA_SEQS_PER_STEP = 4


def _delta_kernel(q_ref, k_ref, v_ref, z_ref, misc_ref, nega_ref, dtb_ref, gout_ref, s0_ref, o_ref, st_ref, *, c, t_valid, bb):
    n = pl.program_id(1)

    @pl.when(n == 0)
    def _():
        st_ref[...] = s0_ref[...]

    row = _row_iota((c, LANES))
    live = row < t_valid
    ri = lax.broadcasted_iota(I32, (c, c), 0)
    ci = lax.broadcasted_iota(I32, (c, c), 1)
    incl = ri >= ci
    strict = ri > ci
    eye = jnp.where(ri == ci, 1.0, 0.0).astype(F32)
    scale = DN_DK ** -0.5

    chains = [(bi, h) for bi in range(bb) for h in range(DN_HEADS)]
    nch = range(len(chains))
    beta_t, gc_t, gc_tr = [], [], []
    for bi in range(bb):
        mt = misc_ref[bi]
        beta_t.append(jnp.where(live, jax.nn.sigmoid(mt), 0.0))
        g = jnp.where(live, nega_ref[...] * _softplus(mt + dtb_ref[...]), 0.0)
        sft = 1
        while sft < c:
            g = g + jnp.where(row >= sft, pltpu.roll(g, sft, axis=0), 0.0)
            sft *= 2
        gc_t.append(g)
        gc_tr.append(_pad_rows(g, LANES).T)

    sl = lambda h: slice(h * DN_DK, (h + 1) * DN_DK)
    q = [jnp.where(live, q_ref[bi, :, sl(h)] * scale, 0.0) for bi, h in chains]
    k = [jnp.where(live, k_ref[bi, :, sl(h)], 0.0) for bi, h in chains]
    v = [jnp.where(live, v_ref[bi, :, sl(h)], 0.0) for bi, h in chains]
    beta = [beta_t[bi][:, MISC_DB + h:MISC_DB + h + 1] for bi, h in chains]
    gcc = [gc_t[bi][:, MISC_DA + h:MISC_DA + h + 1] for bi, h in chains]
    gcr = [gc_tr[bi][MISC_DA + h:MISC_DA + h + 1, 0:c] for bi, h in chains]
    decay = [jnp.where(incl, jnp.exp(jnp.where(incl, gcc[i] - gcr[i], 0.0)), 0.0) for i in nch]
    kb = [k[i] * beta[i] for i in nch]
    pw = [-jnp.where(strict, _dot3(kb[i], k[i], nt=True) * decay[i], 0.0) for i in nch]
    r = [eye + pw[i] for i in nch]
    nn = 2
    while nn < c:
        pw = [_dot3(pw[i], pw[i]) for i in nch]
        r = [_dot3(r[i], eye + pw[i]) for i in nch]
        nn *= 2
    egc = [jnp.exp(gcc[i]) for i in nch]
    u = [_dot3(r[i], v[i] * beta[i]) for i in nch]
    w = [_dot3(r[i], kb[i] * egc[i]) for i in nch]
    a_qk = [_dot3(q[i], k[i], nt=True) * decay[i] for i in nch]
    s = [st_ref[bi, h] for bi, h in chains]
    v_new = [u[i] - _dot3(w[i], s[i]) for i in nch]
    o = [_dot3(q[i] * egc[i], s[i]) + _dot3(a_qk[i], v_new[i]) for i in nch]
    glast = [gcc[i][c - 1:c, :] for i in nch]
    kw_t = [_pad_rows(k[i] * jnp.exp(glast[i] - gcc[i]), LANES).T for i in nch]
    s_new = [s[i] * jnp.exp(glast[i]) + _dot3(kw_t[i], _pad_rows(v_new[i], LANES)) for i in nch]
    for i, (bi, h) in enumerate(chains):
        st_ref[bi, h] = s_new[i]
        on = o[i] * lax.rsqrt(jnp.mean(o[i] * o[i], axis=-1, keepdims=True) + EPS) * gout_ref[...]
        z = z_ref[bi, :, sl(h)]
        o_ref[bi, :, sl(h)] = on * (z * jax.nn.sigmoid(z))


def _delta(qn, kn, v, dz, misc, a_log, dt_bias, g_dn_out, s0, c, t_valid):
    b, t, _ = qn.shape
    assert t % c == 0 and (t_valid == t or t == c)
    nega = jnp.zeros((1, LANES), F32).at[0, MISC_DA:MISC_DA + DN_HEADS].set(-jnp.exp(a_log.astype(F32)))
    dtb = jnp.zeros((1, LANES), F32).at[0, MISC_DA:MISC_DA + DN_HEADS].set(dt_bias.astype(F32))
    tok = lambda bi, n: (bi, n, 0)
    fixed = lambda bi, n: (0, 0)
    st = lambda bi, n: (bi, 0, 0, 0)
    bb = DELTA_SEQS_PER_STEP if b % DELTA_SEQS_PER_STEP == 0 else 1
    kern = functools.partial(_delta_kernel, c=c, t_valid=t_valid, bb=bb)
    return pl.pallas_call(
        kern,
        grid=(b // bb, t // c),
        in_specs=[pl.BlockSpec((bb, c, DN_QK), tok), pl.BlockSpec((bb, c, DN_QK), tok), pl.BlockSpec((bb, c, DN_V), tok),
                  pl.BlockSpec((bb, c, DN_V), tok), pl.BlockSpec((bb, c, LANES), tok),
                  pl.BlockSpec((1, LANES), fixed), pl.BlockSpec((1, LANES), fixed), pl.BlockSpec((1, DN_DV), fixed),
                  pl.BlockSpec((bb, DN_HEADS, DN_DK, DN_DV), st)],
        out_specs=[pl.BlockSpec((bb, c, DN_V), tok), pl.BlockSpec((bb, DN_HEADS, DN_DK, DN_DV), st)],
        out_shape=[jax.ShapeDtypeStruct((b, t, DN_V), F32), jax.ShapeDtypeStruct((b, DN_HEADS, DN_DK, DN_DV), F32)],
        compiler_params=_cparams(("parallel", "arbitrary")),
        name="delta_rule",
    )(qn, kn, v, dz, misc, nega, dtb, g_dn_out.reshape(1, DN_DV).astype(F32), s0)


def _merge_kernel(x_ref, oa_ref, ob_ref, ga_ref, gb_ref, woa_ref, wob_ref, wout_ref, gffn_ref, wpqt_ref,
                  x1_ref, h2_ref, qt_ref):
    ma = jax.nn.sigmoid(ga_ref[...]) * _dot(oa_ref[...].astype(BF16), woa_ref[...])
    mb = jax.nn.sigmoid(gb_ref[...]) * _dot(ob_ref[...].astype(BF16), wob_ref[...])
    x1 = x_ref[...] + _dot((ma + mb).astype(BF16), wout_ref[...])
    x1_ref[...] = x1
    h2 = x1 * lax.rsqrt(jnp.mean(x1 * x1, axis=-1, keepdims=True) + EPS) * gffn_ref[...]
    h2_ref[...] = h2
    qt_ref[...] = _dot_nt(wpqt_ref[...], h2.astype(BF16))


def _merge(x2d, oa, ob, ga, gb, w_oa, w_ob, w_out, g_ffn, w_pq, tm):
    n = x2d.shape[0]
    nq = w_pq.shape[1]
    row = lambda i: (i, 0)
    fixed = lambda i: (0, 0)
    return pl.pallas_call(
        _merge_kernel,
        grid=(n // tm,),
        in_specs=[pl.BlockSpec((tm, D_MODEL), row), pl.BlockSpec((tm, A_Q), row), pl.BlockSpec((tm, DN_V), row),
                  pl.BlockSpec((tm, D_MODEL), row), pl.BlockSpec((tm, D_MODEL), row),
                  pl.BlockSpec((A_Q, D_MODEL), fixed), pl.BlockSpec((DN_V, D_MODEL), fixed),
                  pl.BlockSpec((D_MODEL, D_MODEL), fixed), pl.BlockSpec((1, D_MODEL), fixed),
                  pl.BlockSpec((nq, D_MODEL), fixed)],
        out_specs=[pl.BlockSpec((tm, D_MODEL), row), pl.BlockSpec((tm, D_MODEL), row), pl.BlockSpec((nq, tm), lambda i: (0, i))],
        out_shape=[jax.ShapeDtypeStruct((n, D_MODEL), F32), jax.ShapeDtypeStruct((n, D_MODEL), F32),
                   jax.ShapeDtypeStruct((nq, n), F32)],
        compiler_params=_cparams(("parallel",)),
        name="merge_out",
    )(x2d, oa, ob, ga, gb, w_oa.astype(BF16), w_ob.astype(BF16), w_out.astype(BF16),
      g_ffn.reshape(1, D_MODEL).astype(F32), w_pq.T.astype(BF16))


def _peer_cands():
    cands = [(i, j) for i in range(PEER_TOPK) for j in range(PEER_TOPK) if (i + 1) * (j + 1) <= PEER_TOPK]
    return sorted(cands, key=lambda ij: ij[0] * PEER_TOPK + ij[1])


def _route_kernel(qt_ref, sk_ref, eid_ref, gate_ref, toff_ref, val_ref, idx_ref, *, tn):
    half = PEER_DKEY // 2
    key_iota = lax.broadcasted_iota(I32, (PEER_KEYS, tn), 0)

    heads_per_iter = 4

    def per_group(hg, carry):
        tabs = [(hh, c) for hh in range(heads_per_iter) for c in range(2)]
        ss = []
        for hh, c in tabs:
            t = 2 * (hg * heads_per_iter + hh) + c
            qblk = qt_ref[pl.ds(pl.multiple_of(t * half, half), half), :]
            ss.append(_dot3(sk_ref[t], qblk))
        for r in range(PEER_TOPK):
            ms = [jnp.max(s, axis=0, keepdims=True) for s in ss]
            ams = [jnp.min(jnp.where(s == m, key_iota, PEER_KEYS), axis=0, keepdims=True) for s, m in zip(ss, ms)]
            for i, (hh, c) in enumerate(tabs):
                h = hg * heads_per_iter + hh
                val_ref[c, r, pl.ds(h, 1), :] = ms[i]
                idx_ref[c, r, pl.ds(h, 1), :] = ams[i]
            ss = [jnp.where(key_iota == am, -jnp.inf, s) for s, am in zip(ss, ams)]
        return carry

    lax.fori_loop(0, PEER_HEADS // heads_per_iter, per_group, 0)

    cands = _peer_cands()
    cv = [val_ref[0, i] + val_ref[1, j] for i, j in cands]
    ce = [idx_ref[0, i] * PEER_KEYS + idx_ref[1, j] for i, j in cands]
    nc = len(cands)
    rank = []
    for a in range(nc):
        rk = jnp.zeros(cv[a].shape, I32)
        for b in range(nc):
            if b < a:
                rk = rk + jnp.where(cv[b] >= cv[a], 1, 0)
            elif b > a:
                rk = rk + jnp.where(cv[b] > cv[a], 1, 0)
        rank.append(rk)
    mx = cv[0]
    ex = [jnp.where(rank[a] < PEER_TOPK, jnp.exp(cv[a] - mx), 0.0) for a in range(nc)]
    den = ex[0]
    for a in range(1, nc):
        den = den + ex[a]
    inv = 1.0 / den
    e_slots = []
    for slot in range(PEER_TOPK):
        e = jnp.zeros(cv[0].shape, I32)
        g = jnp.zeros(cv[0].shape, F32)
        for a in range(nc):
            hit = rank[a] == slot
            e = jnp.where(hit, ce[a], e)
            g = jnp.where(hit, ex[a], g)
        eid_ref[slot] = e
        gate_ref[slot] = g * inv
        e_slots.append(e)
    toff_ref[...] = ((jnp.concatenate(e_slots, axis=0) & (PEER_HALF - 1)) * SUBLANES).T


def _route(qt, sub_keys, tn):
    nq, n = qt.shape
    sk = sub_keys.reshape(2 * PEER_HEADS, PEER_KEYS, PEER_DKEY // 2).astype(F32)
    kern = functools.partial(_route_kernel, tn=tn)
    blk = pl.BlockSpec((PEER_TOPK, PEER_HEADS, tn), lambda i: (0, 0, i))
    npair = PEER_TOPK * PEER_HEADS
    return pl.pallas_call(
        kern,
        grid=(n // tn,),
        in_specs=[pl.BlockSpec((nq, tn), lambda i: (0, i)),
                  pl.BlockSpec((2 * PEER_HEADS, PEER_KEYS, PEER_DKEY // 2), lambda i: (0, 0, 0))],
        out_specs=[blk, blk, pl.BlockSpec((tn, npair), lambda i: (i, 0))],
        out_shape=[jax.ShapeDtypeStruct((PEER_TOPK, PEER_HEADS, n), I32), jax.ShapeDtypeStruct((PEER_TOPK, PEER_HEADS, n), F32),
                   jax.ShapeDtypeStruct((n, npair), I32)],
        scratch_shapes=[pltpu.VMEM((2, PEER_TOPK, PEER_HEADS, tn), F32), pltpu.VMEM((2, PEER_TOPK, PEER_HEADS, tn), I32)],
        compiler_params=_cparams(("parallel",)),
        name="peer_route",
    )(qt, sk)


def _pack_table(tab):
    e, d = tab.shape
    assert d == SUBLANES * LANES
    bits = lax.bitcast_convert_type(tab.astype(BF16), jnp.uint16).astype(jnp.uint32)
    assert e == 2 * PEER_HALF
    words = (bits[:PEER_HALF] << 16) | bits[PEER_HALF:]
    return lax.bitcast_convert_type(words, I32).reshape(e // 2 * SUBLANES, LANES)


HI_HALF = -65536


def _bf16_words(x):
    return pltpu.bitcast(x.astype(BF16).astype(F32), I32) & HI_HALF


def _shr16(w):
    return lax.shift_right_logical(w, jnp.full(w.shape, 16, I32))


def _packed_mul(a_words, b_words):
    return pltpu.bitcast(pltpu.bitcast(a_words, BF16) * pltpu.bitcast(b_words, BF16), I32)


def _packed_add(a_words, b_words):
    return pltpu.bitcast(pltpu.bitcast(a_words, BF16) + pltpu.bitcast(b_words, BF16), I32)


def _hi_f32(w):
    return pltpu.bitcast(w & HI_HALF, F32)


def _lo_f32(w):
    return pltpu.bitcast(jnp.left_shift(w, 16), F32)


_BITREV8 = (0, 4, 2, 6, 1, 5, 3, 7)


PEER_U_SUB = 16


def _peer_u_kernel(toff_ref, x_ref, eid_ref, gate_ref, tab_ref, ce_ref, co_ref, rhi_ref, rlo_ref, *, tn, sub_t):
    npair = gate_ref.shape[0]
    lane = _lane_iota((npair, tn))
    sub = lax.broadcasted_iota(I32, (SUBLANES, LANES), 0)
    keep = {step: (sub & step) == 0 for step in (4, 2, 1)}

    def per_token(tt, t0):
        t = t0 + tt
        xw = _bf16_words(x_ref[t])
        xw = xw | _shr16(xw)
        for g in range(npair // SUBLANES):
            prods = []
            for p in _BITREV8:
                off = pl.multiple_of(toff_ref[t * npair + (g * SUBLANES + p)], SUBLANES)
                prods.append(_packed_mul(tab_ref[pl.ds(off, SUBLANES), :], xw))
            step = SUBLANES // 2
            while step >= 1:
                nxt = []
                for a in range(0, len(prods), 2):
                    lo_t, hi_t = prods[a], prods[a + 1]
                    left = jnp.where(keep[step], lo_t, pltpu.roll(hi_t, step, axis=0))
                    right = jnp.where(keep[step], pltpu.roll(lo_t, SUBLANES - step, axis=0), hi_t)
                    nxt.append(_packed_add(left, right))
                prods = nxt
                step //= 2
            rhi_ref[tt, g * SUBLANES:(g + 1) * SUBLANES, :] = _hi_f32(prods[0])
            rlo_ref[tt, g * SUBLANES:(g + 1) * SUBLANES, :] = _lo_f32(prods[0])
        return t0

    def per_sub(sb, accs):
        acc_hi, acc_lo = accs
        t0 = sb * sub_t
        lax.fori_loop(0, sub_t, per_token, t0)
        ones = jnp.ones((LANES, LANES), BF16)
        for tt in range(sub_t):
            hit = lane == t0 + tt
            acc_hi = jnp.where(hit, jnp.sum(rhi_ref[tt], axis=1, keepdims=True), acc_hi)
            acc_lo = jnp.where(hit, _dot(rlo_ref[tt].astype(BF16), ones)[:, 0:tn], acc_lo)
        return acc_hi, acc_lo

    zero = jnp.zeros((npair, tn), F32)
    acc_hi, acc_lo = lax.fori_loop(0, tn // sub_t, per_sub, (zero, zero))
    even = eid_ref[...] < PEER_HALF
    coef = gate_ref[...] * jax.nn.gelu(jnp.where(even, acc_hi, acc_lo))
    ce_ref[...] = jnp.where(even, coef, 0.0)
    co_ref[...] = jnp.where(even, 0.0, coef)


def _peer_v_kernel(toff_ref, ce_ref, co_ref, rep_ref, tab_ref, y_ref, bw_ref, *, tn, npair):
    ce = ce_ref[...].astype(BF16)
    co = co_ref[...].astype(BF16)
    chunk = min(16, tn)
    for c in range(tn // chunk):
        rep = rep_ref[:, c * chunk * LANES:(c + 1) * chunk * LANES]
        be = _dot(ce, rep)
        bo = _dot(co, rep)
        words = (pltpu.bitcast(be, I32) & HI_HALF) | _shr16(pltpu.bitcast(bo, I32))
        for t in range(chunk):
            bw_ref[c * chunk + t] = words[:, t * LANES:(t + 1) * LANES]

    nacc = 4

    def per_token(t, carry):
        zero = jnp.zeros((SUBLANES, LANES), F32)
        acc_hi = [zero] * nacc
        acc_lo = [zero] * nacc
        base = t * npair
        for k in range(npair):
            off = pl.multiple_of(toff_ref[base + k], SUBLANES)
            cw = jnp.broadcast_to(bw_ref[t, k:k + 1, :], (SUBLANES, LANES))
            prod = _packed_mul(tab_ref[pl.ds(off, SUBLANES), :], cw)
            acc_hi[k % nacc] = acc_hi[k % nacc] + _hi_f32(prod)
            acc_lo[k % nacc] = acc_lo[k % nacc] + _lo_f32(prod)
        y_ref[t] = ((acc_hi[0] + acc_hi[1]) + (acc_hi[2] + acc_hi[3])) + ((acc_lo[0] + acc_lo[1]) + (acc_lo[2] + acc_lo[3]))
        return carry

    lax.fori_loop(0, tn, per_token, 0)


def _peer_experts(eid, gate, toff, h2, tab_u, tab_v, tn):
    npair, n = eid.shape
    d = h2.shape[1]
    x3 = h2.reshape(n, d // LANES, LANES)
    tab_spec = pl.BlockSpec(tab_u.shape, lambda i: (0, 0), pipeline_mode=pl.Buffered(1))
    smem_blk = lambda tt: pl.BlockSpec((tt * npair,), lambda i: (i,), memory_space=pltpu.SMEM)
    slot_blk = lambda tt: pl.BlockSpec((npair, tt), lambda i: (0, i))
    sub_t = PEER_U_SUB if tn % PEER_U_SUB == 0 else tn
    ce, co = pl.pallas_call(
        functools.partial(_peer_u_kernel, tn=tn, sub_t=sub_t),
        grid=(n // tn,),
        in_specs=[smem_blk(tn), pl.BlockSpec((tn, d // LANES, LANES), lambda i: (i, 0, 0)), slot_blk(tn), slot_blk(tn), tab_spec],
        out_specs=[slot_blk(tn), slot_blk(tn)],
        out_shape=[jax.ShapeDtypeStruct((npair, n), F32)] * 2,
        scratch_shapes=[pltpu.VMEM((sub_t, npair, LANES), F32)] * 2,
        compiler_params=_cparams(("arbitrary",)),
        name="peer_u",
    )(toff.reshape(n * npair), x3, eid, gate, tab_u)
    tv = tn
    rep = jnp.asarray(np.arange(tv)[:, None] == np.arange(tv * LANES)[None, :] // LANES, BF16)
    y3 = pl.pallas_call(
        functools.partial(_peer_v_kernel, tn=tv, npair=npair),
        grid=(n // tv,),
        in_specs=[smem_blk(tv), slot_blk(tv), slot_blk(tv),
                  pl.BlockSpec(rep.shape, lambda i: (0, 0), pipeline_mode=pl.Buffered(1)), tab_spec],
        out_specs=pl.BlockSpec((tv, d // LANES, LANES), lambda i: (i, 0, 0)),
        out_shape=jax.ShapeDtypeStruct((n, d // LANES, LANES), F32),
        scratch_shapes=[pltpu.VMEM((tv, npair, LANES), I32)],
        compiler_params=_cparams(("arbitrary",)),
        name="peer_v",
    )(toff.reshape(n * npair), ce, co, rep, tab_v)
    return y3.reshape(n, d)


def _ple_kernel(x1_ref, yp_ref, p_ref, gple_ref, wg_ref, wp_ref, o_ref):
    x2 = x1_ref[...] + yp_ref[...]
    hn = x2 * lax.rsqrt(jnp.mean(x2 * x2, axis=-1, keepdims=True) + EPS) * gple_ref[...]
    gate = jax.nn.sigmoid(_dot(hn.astype(BF16), wg_ref[...]))
    o_ref[...] = x2 + gate * _dot(p_ref[...].astype(BF16), wp_ref[...])


def _ple(x1, yp, p2d, g_ple, w_ple_gate, w_ple, tm):
    n = x1.shape[0]
    row = lambda i: (i, 0)
    fixed = lambda i: (0, 0)
    return pl.pallas_call(
        _ple_kernel,
        grid=(n // tm,),
        in_specs=[pl.BlockSpec((tm, D_MODEL), row), pl.BlockSpec((tm, D_MODEL), row), pl.BlockSpec((tm, PLE_DIM), row),
                  pl.BlockSpec((1, D_MODEL), fixed), pl.BlockSpec((D_MODEL, D_MODEL), fixed),
                  pl.BlockSpec((PLE_DIM, D_MODEL), fixed)],
        out_specs=pl.BlockSpec((tm, D_MODEL), row),
        out_shape=jax.ShapeDtypeStruct((n, D_MODEL), F32),
        compiler_params=_cparams(("parallel",)),
        name="ple_out",
    )(x1, yp, p2d, g_ple.reshape(1, D_MODEL).astype(F32), w_ple_gate.astype(BF16), w_ple.astype(BF16))


PAGES_PER_STEP = 32
SAMPLE_SOFTMAX_STATES = 4


def _sample_scores_kernel(pt_ref, qh_ref, wrep_ref, kinew_ref, *refs, npg):
    pages = refs[:npg]
    sc_ref, scnew_ref = refs[npg:]
    j = pl.program_id(1)
    qh = qh_ref[0]
    wrep = wrep_ref[0]

    def raw(keys_t):
        return _dot(qh, keys_t.astype(BF16))

    def score(s):
        s = jnp.maximum(s, 0.0) * wrep
        tot = s[0:SUBLANES]
        for h in range(1, IDX_HEADS):
            tot = tot + s[h * SUBLANES:(h + 1) * SUBLANES]
        return tot

    dots = [raw(pages[p][0]) for p in range(npg)]
    for p in range(npg):
        sc_ref[0, :, p * PAGE_SIZE:(p + 1) * PAGE_SIZE] = score(dots[p])

    @pl.when(j == 0)
    def _():
        sn = score(raw(kinew_ref[0]))
        causal = _lane_iota((SUBLANES, PAGE_SIZE)) <= _row_iota((SUBLANES, PAGE_SIZE))
        scnew_ref[0] = jnp.where(causal, sn, -jnp.inf)


def _sample_thr_kernel(sc_ref, thr_ref, jcut_ref, keys_ref, cand_ref, *, k_sel, tq, tk, t):
    nkb = keys_ref.shape[0]
    for kb in range(nkb):
        keys_ref[kb] = _sort_key(sc_ref[:, :, kb * tk:(kb + 1) * tk].reshape(tq, tk))
    _select_rows(keys_ref, thr_ref, jcut_ref, cand_ref, nkb, k_sel, tq, tk, min(SEARCH_ROWS, tq), live_rows=t)


def _sample_attn_kernel(pt_ref, q_ref, sc_ref, scnew_ref, thr_ref, jcut_ref, knew_ref, vnew_ref, *refs, npg, past):
    kpages = refs[:npg]
    vpages = refs[npg:2 * npg]
    o_ref, m_ref, l_ref, acc_ref = refs[2 * npg:]
    j = pl.program_id(1)

    @pl.when(j == 0)
    def _():
        m_ref[...] = jnp.full(m_ref.shape, NEG_BIG, F32)
        l_ref[...] = jnp.zeros(l_ref.shape, F32)
        acc_ref[...] = jnp.zeros(acc_ref.shape, F32)

    q = q_ref[0]
    thr = thr_ref[...]
    jcut = jcut_ref[...]
    lane = _lane_iota((SUBLANES, PAGE_SIZE))
    kvd = A_KV_HEADS * A_HEAD_DIM

    wide = lambda a: jnp.concatenate([a] * (kvd // LANES), axis=1)

    def update(groups):
        ss = []
        for _, kts, _, scs, pos0s, causal in groups:
            biases = []
            for sc8, pos0 in zip(scs, pos0s):
                key = _sort_key(sc8)
                pos = lane + pos0
                sel = jnp.where(key > thr, 0.0, jnp.where(key == thr, jnp.where(pos <= jcut, 0.0, NEG_BIG), NEG_BIG))
                if causal is not None:
                    sel = jnp.where(causal, sel, NEG_BIG)
                biases.append(sel)
            bias = jnp.concatenate([jnp.concatenate(biases, axis=1)] * A_HEADS, axis=0)
            ss.append(jnp.concatenate([_dot(q, kt[...].reshape(kvd, PAGE_SIZE).astype(BF16)) for kt in kts], axis=1) + bias)
        m_olds = [m_ref[g[0]] for g in groups]
        m_news = [jnp.maximum(mo, jnp.max(s, axis=1, keepdims=True)) for mo, s in zip(m_olds, ss)]
        alphas = [jnp.exp2(mo - mn) for mo, mn in zip(m_olds, m_news)]
        ps = [jnp.exp2(s - mn[:, 0:1]) for s, mn in zip(ss, m_news)]
        sums = [jnp.sum(p, axis=1, keepdims=True) for p in ps]
        pvs = []
        for (_, _, vts, _, _, _), p in zip(groups, ps):
            pb = p.astype(BF16)
            parts = [_dot_nt(pb[:, i * PAGE_SIZE:(i + 1) * PAGE_SIZE], vt[...].reshape(kvd, PAGE_SIZE).astype(BF16))
                     for i, vt in enumerate(vts)]
            pv = parts[0]
            for part in parts[1:]:
                pv = pv + part
            pvs.append(pv)
        for g, mn, al, sm, pv in zip(groups, m_news, alphas, sums, pvs):
            ci = g[0]
            l_ref[ci] = al * l_ref[ci] + sm
            acc_ref[ci] = wide(al) * acc_ref[ci] + pv
            m_ref[ci] = mn

    nst = m_ref.shape[0]
    update([(ci, [kpages[p].at[0] for p in range(ci, npg, nst)], [vpages[p].at[0] for p in range(ci, npg, nst)],
             [sc_ref[0, :, p * PAGE_SIZE:(p + 1) * PAGE_SIZE] for p in range(ci, npg, nst)],
             [(j * npg + p) * PAGE_SIZE for p in range(ci, npg, nst)], None) for ci in range(nst)])

    @pl.when(j == pl.num_programs(1) - 1)
    def _():
        update([(0, [knew_ref.at[0]], [vnew_ref.at[0]], [scnew_ref[0]], [past], lane <= _row_iota((SUBLANES, PAGE_SIZE)))])
        m_all = m_ref[0]
        for ci in range(1, nst):
            m_all = jnp.maximum(m_all, m_ref[ci])
        l_all = jnp.zeros(m_all.shape, F32)
        acc_all = jnp.zeros(acc_ref.shape[1:], F32)
        for ci in range(nst):
            w = jnp.exp2(m_ref[ci] - m_all)
            l_all = l_all + w * l_ref[ci]
            acc_all = acc_all + wide(w) * acc_ref[ci]
        o_ref[0] = acc_all / wide(l_all)


def _head_major(a, bd, t, nh, hd):
    a = a.reshape(bd, t, nh, hd).transpose(0, 2, 1, 3)
    a = jnp.pad(a, ((0, 0), (0, 0), (0, SUBLANES - t), (0, 0)))
    return a.reshape(bd, nh * SUBLANES, hd)


def _attn_sample(q, k, v, qi, misc, cache_k, cache_v, cache_idx_k, page_table, bd, t):
    assert t <= SUBLANES
    n_pages = page_table.shape[1]
    past = n_pages * PAGE_SIZE
    npg = PAGES_PER_STEP if n_pages % PAGES_PER_STEP == 0 else 1
    nj = n_pages // npg
    k_sel = min(TOPK_MAX, (past + t) // 4)
    rows = A_HEADS * SUBLANES

    qh = _head_major(qi, bd, t, IDX_HEADS, IDX_DIM)
    wi = misc[:, MISC_WI:MISC_WI + IDX_HEADS].reshape(bd, t, IDX_HEADS).transpose(0, 2, 1)
    wrep = jnp.broadcast_to(jnp.pad(wi, ((0, 0), (0, 0), (0, SUBLANES - t))).reshape(bd, rows, 1), (bd, rows, LANES))
    pad_page = lambda a, w: jnp.pad(a.reshape(bd, t, w), ((0, 0), (0, PAGE_SIZE - t), (0, 0)))
    kinew = pad_page(misc[:, :IDX_DIM], IDX_DIM).transpose(0, 2, 1)
    cik = cache_idx_k.transpose(0, 2, 1)

    per_b = lambda b, j, pt: (b, 0, 0)
    per_b4 = lambda b, j, pt: (b, 0, 0, 0)
    page_spec = lambda shape, p: pl.BlockSpec(shape, lambda b, j, pt: (pt[b, j * npg + p],) + (0,) * (len(shape) - 1))
    sc, scnew = pl.pallas_call(
        functools.partial(_sample_scores_kernel, npg=npg),
        grid_spec=pltpu.PrefetchScalarGridSpec(
            num_scalar_prefetch=1, grid=(bd, nj),
            in_specs=[pl.BlockSpec((1, rows, IDX_DIM), per_b), pl.BlockSpec((1, rows, LANES), per_b),
                      pl.BlockSpec((1, IDX_DIM, PAGE_SIZE), per_b)]
                     + [page_spec((1, IDX_DIM, PAGE_SIZE), p) for p in range(npg)],
            out_specs=[pl.BlockSpec((1, SUBLANES, npg * PAGE_SIZE), lambda b, j, pt: (b, 0, j)),
                       pl.BlockSpec((1, SUBLANES, PAGE_SIZE), per_b)]),
        out_shape=[jax.ShapeDtypeStruct((bd, SUBLANES, past), F32), jax.ShapeDtypeStruct((bd, SUBLANES, PAGE_SIZE), F32)],
        compiler_params=_cparams(("parallel", "arbitrary")),
        name="sample_scores",
    )(page_table, qh, wrep, kinew, *([cik] * npg))

    length = past + PAGE_SIZE
    sc_all = jnp.concatenate([sc, scnew], axis=2)
    tb = SUBLANES if bd % SUBLANES == 0 else 1
    tq = tb * SUBLANES
    tk = 5 * LANES if length % (5 * LANES) == 0 else LANES
    thr, jcut = pl.pallas_call(
        functools.partial(_sample_thr_kernel, k_sel=k_sel, tq=tq, tk=tk, t=t),
        grid=(bd // tb,),
        in_specs=[pl.BlockSpec((tb, SUBLANES, length), lambda i: (i, 0, 0))],
        out_specs=[pl.BlockSpec((tq, LANES), lambda i: (i, 0))] * 2,
        out_shape=[jax.ShapeDtypeStruct((bd * SUBLANES, LANES), I32)] * 2,
        scratch_shapes=[pltpu.VMEM((length // tk, tq, tk), I32), pltpu.VMEM((tq, LANES), I32)],
        compiler_params=_cparams(("parallel",)),
        name="sample_thr",
    )(sc_all)

    qa = _head_major(q, bd, t, A_HEADS, A_HEAD_DIM)
    kv_of_row = (np.arange(rows) // SUBLANES) // A_REP
    own = jnp.asarray(kv_of_row[:, None] == np.arange(A_KV_HEADS)[None, :])
    qa = jnp.where(own[None, :, :, None], qa[:, :, None, :], jnp.zeros((), qa.dtype)).reshape(bd, rows, A_KV)
    to_page_t = lambda a: pad_page(a, A_KV).reshape(bd, PAGE_SIZE, A_KV_HEADS, A_HEAD_DIM).transpose(0, 2, 3, 1)
    knew, vnew = to_page_t(k), to_page_t(v)
    ck = cache_k.transpose(0, 2, 3, 1)
    cv = cache_v.transpose(0, 2, 3, 1)
    kv_blk = (1, A_KV_HEADS, A_HEAD_DIM, PAGE_SIZE)
    nst = SAMPLE_SOFTMAX_STATES if npg % SAMPLE_SOFTMAX_STATES == 0 else 1
    o = pl.pallas_call(
        functools.partial(_sample_attn_kernel, npg=npg, past=past),
        grid_spec=pltpu.PrefetchScalarGridSpec(
            num_scalar_prefetch=1, grid=(bd, nj),
            in_specs=[pl.BlockSpec((1, rows, A_KV), per_b),
                      pl.BlockSpec((1, SUBLANES, npg * PAGE_SIZE), lambda b, j, pt: (b, 0, j)),
                      pl.BlockSpec((1, SUBLANES, PAGE_SIZE), per_b),
                      pl.BlockSpec((SUBLANES, LANES), lambda b, j, pt: (b, 0)),
                      pl.BlockSpec((SUBLANES, LANES), lambda b, j, pt: (b, 0)),
                      pl.BlockSpec(kv_blk, per_b4), pl.BlockSpec(kv_blk, per_b4)]
                     + [page_spec(kv_blk, p) for p in range(npg)] * 2,
            out_specs=pl.BlockSpec((1, rows, A_KV), per_b),
            scratch_shapes=[pltpu.VMEM((nst, rows, LANES), F32), pltpu.VMEM((nst, rows, LANES), F32),
                            pltpu.VMEM((nst, rows, A_KV), F32)]),
        out_shape=jax.ShapeDtypeStruct((bd, rows, A_KV), F32),
        compiler_params=_cparams(("parallel", "arbitrary")),
        name="sample_attn",
    )(page_table, qa, sc, scnew, thr, jcut, knew, vnew, *([ck] * npg), *([cv] * npg))
    o = jnp.sum(jnp.where(own[None, :, :, None], o.reshape(bd, rows, A_KV_HEADS, A_HEAD_DIM), 0.0), axis=2)
    o = o.reshape(bd, A_HEADS, SUBLANES, A_HEAD_DIM)[:, :, :t].transpose(0, 2, 1, 3)
    return o.reshape(bd * t, A_Q)


def _tile(n, pref):
    return pref if n % pref == 0 else n


def _layer(x, p_emb, conv_state, delta_state, attn_fn, prm):
    b, t, _ = x.shape
    n = b * t
    x2d = x.reshape(n, D_MODEL)
    tm = _tile(n, ROW_TILE)
    (q, k, kdup, v, vb, qi, misc, kidup, dqkv, dz, ga, gb) = _proj(
        x2d, prm['g_mix'], prm['w_perm'], prm['g_q'], prm['g_k'], prm['g_idx_k'], tm)

    oa = attn_fn(q, k, kdup, v, vb, qi, misc, kidup)

    tp = -(-t // SUBLANES) * SUBLANES
    c = min(DN_CHUNK, tp)
    pad_t = lambda a: jnp.pad(a.reshape(b, t, a.shape[-1]), ((0, 0), (0, tp - t), (0, 0)))
    dqkv3 = dqkv.reshape(b, t, DN_CONV_CH)
    halo = jnp.zeros((b, SUBLANES, DN_CONV_CH), F32)
    if conv_state is not None:
        halo = halo.at[:, SUBLANES - (CONV_W - 1):].set(conv_state.astype(F32))
        hist = jnp.concatenate([conv_state.astype(F32), dqkv3], axis=1)
    else:
        hist = jnp.concatenate([jnp.zeros((b, CONV_W - 1, DN_CONV_CH), F32), dqkv3], axis=1)
    new_conv = hist[:, -(CONV_W - 1):]
    qn, kn, vv = _conv(pad_t(dqkv), halo, prm['conv_w'], _tile(tp, CONV_TT))
    s0 = jnp.zeros((b, DN_HEADS, DN_DK, DN_DV), F32) if delta_state is None else delta_state.astype(F32)
    ob, new_delta = _delta(qn, kn, vv, pad_t(dz), pad_t(misc), prm['a_log'], prm['dt_bias'], prm['g_dn_out'], s0, c, t)
    ob = ob[:, :t].reshape(n, DN_V)

    x1, h2, qt = _merge(x2d, oa, ob, ga, gb, prm['w_oa'], prm['w_ob'], prm['w_out'], prm['g_ffn'], prm['w_pq'], tm)
    eid, gate, toff = _route(qt, prm['sub_keys'], _tile(n, TOKEN_TILE))
    npair = PEER_TOPK * PEER_HEADS
    yp = _peer_experts(eid.reshape(npair, n), gate.reshape(npair, n), toff, h2, prm['tab_u'], prm['tab_v'], _tile(n, TOKEN_TILE))
    y = _ple(x1, yp, p_emb.reshape(n, PLE_DIM), prm['g_ple'], prm['w_ple_gate'], prm['w_ple'], tm)

    return (y.reshape(b, t, D_MODEL), k.reshape(b, t, A_KV_HEADS, A_HEAD_DIM), v.reshape(b, t, A_KV_HEADS, A_HEAD_DIM),
            misc[:, :IDX_DIM].reshape(b, t, IDX_DIM), new_conv, new_delta)


def kernel(x_prompt, x_sample, cache_k, cache_v, cache_idx_k, state_conv, state_delta, page_table, p_prompt, p_sample,
           g_mix, w_in, g_q, g_k, g_idx_k, conv_w, a_log, dt_bias, g_dn_out, w_oa, w_ob, w_out, g_ffn, w_pq, sub_keys,
           peer_u, peer_v, g_ple, w_ple_gate, w_ple):
    depth = w_in.shape[0]
    xp, xs = x_prompt, x_sample
    outs = [[] for _ in range(10)]
    for i in range(depth):
        prm = dict(g_mix=g_mix[i], w_perm=_permute_w_in(w_in[i]), g_q=g_q[i], g_k=g_k[i], g_idx_k=g_idx_k[i],
                   conv_w=conv_w[i].astype(F32), a_log=a_log[i], dt_bias=dt_bias[i], g_dn_out=g_dn_out[i],
                   w_oa=w_oa[i], w_ob=w_ob[i], w_out=w_out[i], g_ffn=g_ffn[i], w_pq=w_pq[i], sub_keys=sub_keys[i],
                   tab_u=_pack_table(peer_u[i]), tab_v=_pack_table(peer_v[i]), g_ple=g_ple[i],
                   w_ple_gate=w_ple_gate[i], w_ple=w_ple[i])

        bp, s, _ = xp.shape

        def attn_prompt(q, k, kdup, v, vb, qi, misc, kidup):
            k_sel = min(TOPK_MAX, s // 4)
            tq, tk = _tile(s, ATTN_TQ), _tile(s, ATTN_TK)
            r3 = lambda a: a.reshape(bp, s, a.shape[-1])
            mask = _idx_mask(r3(qi), r3(misc), r3(kidup), k_sel, tq, tk)
            return _attn(r3(q), r3(kdup), r3(vb), mask, tq, tk).reshape(bp * s, A_Q)

        res = _layer(xp, p_prompt[i], None, None, attn_prompt, prm)
        xp = res[0]
        for lst, val in zip(outs[:5], res[1:]):
            lst.append(val)

        bd, t, _ = xs.shape

        def attn_sample(q, k, kdup, v, vb, qi, misc, kidup):
            return _attn_sample(q, k, v, qi, misc, cache_k[i], cache_v[i], cache_idx_k[i], page_table, bd, t)

        res = _layer(xs, p_sample[i], state_conv[i], state_delta[i], attn_sample, prm)
        xs = res[0]
        for lst, val in zip(outs[5:], res[1:]):
            lst.append(val)

    cast = [cache_k.dtype, cache_v.dtype, cache_idx_k.dtype, state_conv.dtype, state_delta.dtype] * 2
    stacked = [jnp.stack(lst).astype(dt) for lst, dt in zip(outs, cast)]
    return (xp, xs, *stacked)
```

```python
import functools
import math

import jax
import jax.numpy as jnp
import numpy as np
from jax import lax
from jax.experimental import pallas as pl
from jax.experimental.pallas import tpu as pltpu

F32 = jnp.float32
BF16 = jnp.bfloat16
I32 = jnp.int32

D_MODEL = 1024
PAGE_SIZE = 128
A_HEADS = 8
A_KV_HEADS = 4
A_REP = A_HEADS // A_KV_HEADS
A_HEAD_DIM = 64
A_Q = A_HEADS * A_HEAD_DIM
A_KV = A_KV_HEADS * A_HEAD_DIM
IDX_HEADS = 8
IDX_DIM = 64
TOPK_MAX = 256
DN_HEADS = 4
DN_DK = 128
DN_DV = 128
DN_QK = DN_HEADS * DN_DK
DN_V = DN_HEADS * DN_DV
DN_CONV_CH = 2 * DN_QK + DN_V
CONV_W = 4
DN_CHUNK = 64
PEER_HEADS = 8
PEER_KEYS = 128
PEER_DKEY = 256
PEER_TOPK = 16
PEER_HALF = PEER_KEYS * PEER_KEYS // 2
PLE_DIM = 256
EPS = 1e-6
IN_SPLITS = (A_Q, A_KV, A_KV, IDX_HEADS * IDX_DIM, IDX_DIM, IDX_HEADS, DN_CONV_CH, DN_V, DN_HEADS, DN_HEADS, D_MODEL, D_MODEL)

LANES = 128
SUBLANES = 8
VMEM_LIMIT = 56 * 1024 * 1024

ROW_TILE = 256
ATTN_TQ = 256
IDX_TQ = 512
ATTN_TK = 512
ATTN_KEY_BLOCKS_PER_STEP = 2
CONV_TT = 512
TOKEN_TILE = LANES
SEARCH_ROWS = 64
NO_TIE_CUT = 2 ** 30
NEG_INF_HI_KEY = (0xFF80 ^ 0x7FFF) - 2 ** 16
MIN_NORMAL = 2.0 ** -126
NEG_BIG = -1e30
Q_SCALE = A_HEAD_DIM ** -0.5 * math.log2(math.e)
INT_MIN = -(2 ** 31)

MISC_WI = IDX_DIM
MISC_DB = MISC_WI + IDX_HEADS
MISC_DA = MISC_DB + DN_HEADS

SEG_Q = 0
SEG_K = SEG_Q + A_Q
SEG_V = SEG_K + A_KV
SEG_QI = SEG_V + A_KV
SEG_MISC = SEG_QI + IDX_HEADS * IDX_DIM
SEG_DQKV = SEG_MISC + LANES
SEG_DZ = SEG_DQKV + DN_CONV_CH
SEG_GA = SEG_DZ + DN_V
SEG_GB = SEG_GA + D_MODEL
SEG_END = SEG_GB + D_MODEL


def _cparams(sem):
    return pltpu.CompilerParams(dimension_semantics=sem, vmem_limit_bytes=VMEM_LIMIT)


def _dot(a, b):
    return jnp.dot(a, b, preferred_element_type=F32)


def _dot_nt(a, b):
    return lax.dot_general(a, b, (((1,), (1,)), ((), ())), preferred_element_type=F32)


def _split(a):
    hi = a.astype(BF16)
    lo = (a - hi.astype(F32)).astype(BF16)
    return hi, lo


def _dot3(a, b, nt=False):
    d = _dot_nt if nt else _dot
    ah, al = _split(a)
    bh, bl = _split(b)
    return d(ah, bh) + (d(ah, bl) + d(al, bh))


def _lane_iota(shape):
    return lax.broadcasted_iota(I32, shape, len(shape) - 1)


def _row_iota(shape):
    return lax.broadcasted_iota(I32, shape, len(shape) - 2)


def _half_norm(blk, gain):
    lane = _lane_iota(blk.shape)
    lo = lane < A_HEAD_DIM
    sq = blk * blk
    s_lo = jnp.sum(jnp.where(lo, sq, 0.0), axis=-1, keepdims=True)
    s_hi = jnp.sum(jnp.where(lo, 0.0, sq), axis=-1, keepdims=True)
    r_lo = lax.rsqrt(s_lo * (1.0 / A_HEAD_DIM) + EPS)
    r_hi = lax.rsqrt(s_hi * (1.0 / A_HEAD_DIM) + EPS)
    return blk * jnp.where(lo, r_lo, r_hi) * gain


def _proj_kernel(x_ref, gmix_ref, w_ref, gq_ref, gk_ref, gik_ref,
                 q_ref, k_ref, kdup_ref, v_ref, vb_ref, qi_ref, misc_ref, kidup_ref,
                 dqkv_ref, dz_ref, ga_ref, gb_ref):
    x = x_ref[...]
    h = x * lax.rsqrt(jnp.mean(x * x, axis=-1, keepdims=True) + EPS) * gmix_ref[...]
    hb = h.astype(BF16)

    def seg(a, b):
        return _dot(hb, w_ref[:, a:b])

    lane = _lane_iota((x.shape[0], LANES))
    lo = lane < A_HEAD_DIM

    zq = seg(SEG_Q, SEG_K)
    for c in range(A_Q // LANES):
        blk = _half_norm(zq[:, c * LANES:(c + 1) * LANES], gq_ref[...])
        q_ref[:, c * LANES:(c + 1) * LANES] = (blk * Q_SCALE).astype(BF16)

    zk = seg(SEG_K, SEG_V)
    for c in range(A_KV // LANES):
        blk = _half_norm(zk[:, c * LANES:(c + 1) * LANES], gk_ref[...])
        k_ref[:, c * LANES:(c + 1) * LANES] = blk
        rolled = pltpu.roll(blk, A_HEAD_DIM, axis=1)
        kdup_ref[:, (2 * c) * LANES:(2 * c + 1) * LANES] = jnp.where(lo, blk, rolled).astype(BF16)
        kdup_ref[:, (2 * c + 1) * LANES:(2 * c + 2) * LANES] = jnp.where(lo, rolled, blk).astype(BF16)

    zv = seg(SEG_V, SEG_QI)
    v_ref[...] = zv
    vb_ref[...] = zv.astype(BF16)

    qi_ref[...] = seg(SEG_QI, SEG_MISC).astype(BF16)

    zm = seg(SEG_MISC, SEG_DQKV)
    s_ik = jnp.sum(jnp.where(lo, zm * zm, 0.0), axis=-1, keepdims=True)
    kin = zm * lax.rsqrt(s_ik * (1.0 / IDX_DIM) + EPS) * gik_ref[...]
    wscale = (IDX_HEADS * IDX_DIM) ** -0.5
    misc_ref[...] = jnp.where(lo, kin, jnp.where(lane < MISC_DB, zm * wscale, zm))
    kin0 = jnp.where(lo, kin, 0.0)
    kidup_ref[...] = (kin0 + pltpu.roll(kin0, IDX_DIM, axis=1)).astype(BF16)

    dqkv_ref[...] = seg(SEG_DQKV, SEG_DZ)
    dz_ref[...] = seg(SEG_DZ, SEG_GA)
    ga_ref[...] = seg(SEG_GA, SEG_GB)
    gb_ref[...] = seg(SEG_GB, SEG_END)


def _permute_w_in(w_in):
    cuts = np.cumsum((0,) + IN_SPLITS)
    parts = [w_in[:, cuts[i]:cuts[i + 1]] for i in range(len(IN_SPLITS))]
    aq, ak, av, iq, ik, iw, dqkv, dz, db, da, ga, gb = parts
    pad = jnp.zeros((w_in.shape[0], LANES - IDX_DIM - IDX_HEADS - 2 * DN_HEADS), w_in.dtype)
    return jnp.concatenate([aq, ak, av, iq, ik, iw, db, da, pad, dqkv, dz, ga, gb], axis=1).astype(BF16)


def _tile2(g):
    return jnp.concatenate([g, g]).reshape(1, LANES).astype(F32)


def _proj(x2d, g_mix, w_perm, g_q, g_k, g_idx_k, tm):
    n = x2d.shape[0]
    assert n % tm == 0
    widths = [(A_Q, BF16), (A_KV, F32), (2 * A_KV, BF16), (A_KV, F32), (A_KV, BF16), (IDX_HEADS * IDX_DIM, BF16),
              (LANES, F32), (LANES, BF16), (DN_CONV_CH, F32), (DN_V, F32), (D_MODEL, F32), (D_MODEL, F32)]
    row = lambda i: (i, 0)
    fixed = lambda i: (0, 0)
    return pl.pallas_call(
        _proj_kernel,
        grid=(n // tm,),
        in_specs=[pl.BlockSpec((tm, D_MODEL), row), pl.BlockSpec((1, D_MODEL), fixed),
                  pl.BlockSpec((D_MODEL, SEG_END), fixed), pl.BlockSpec((1, LANES), fixed),
                  pl.BlockSpec((1, LANES), fixed), pl.BlockSpec((1, LANES), fixed)],
        out_specs=[pl.BlockSpec((tm, w), row) for w, _ in widths],
        out_shape=[jax.ShapeDtypeStruct((n, w), dt) for w, dt in widths],
        compiler_params=_cparams(("parallel",)),
        name="proj_in",
    )(x2d, g_mix.reshape(1, D_MODEL), w_perm, _tile2(g_q), _tile2(g_k), _tile2(g_idx_k))


def _sort_key(score):
    score = jnp.where(score == 0.0, 0.0, score)
    bits = pltpu.bitcast(score, I32)
    return jnp.where(bits < 0, bits ^ jnp.int32(0x7FFFFFFF), bits)


def _index_scores(qi, wi_tile, kblk):
    lane = _lane_iota((qi.shape[0], LANES))
    lo = lane < IDX_DIM
    sc = None
    for c in range(IDX_HEADS // 2):
        q128 = qi[:, c * LANES:(c + 1) * LANES]
        zero = jnp.zeros_like(q128)
        for half in range(2):
            qm = jnp.where(lo, q128, zero) if half == 0 else jnp.where(lo, zero, q128)
            s = _dot_nt(qm, kblk)
            hidx = MISC_WI + 2 * c + half
            term = jnp.maximum(s, 0.0) * wi_tile[:, hidx:hidx + 1]
            sc = term if sc is None else sc + term
    return sc


def _select_rows(keys_ref, thr_ref, jcut_ref, cand_ref, nvalid, k_sel, tq, tk, rb, live_rows=None, khi_ref=None,
                 candh_ref=None):
    nchunk = tk // LANES
    nrb = tq // rb
    nbits_idx = int(math.ceil(math.log2(keys_ref.shape[0] * tk))) + 1
    assert keys_ref.shape[0] * nchunk <= 256
    lane = _lane_iota((rb, LANES))
    ones = jnp.ones((LANES, LANES), BF16)
    rows = lambda r: slice(r * rb, (r + 1) * rb)

    def count(pred, src_ref=keys_ref, cnd_ref=cand_ref, dtype=F32, blocks=tuple(range(nrb))):
        def body(kb, cnts):
            out = []
            for cnt, r in zip(cnts, blocks):
                blk = src_ref[kb, rows(r), :]
                cand_b = cnd_ref[rows(r), :]
                for c in range(nchunk):
                    cnt = cnt + pred(blk[:, c * LANES:(c + 1) * LANES], cand_b, kb * tk + c * LANES, r)
                out.append(cnt)
            return tuple(out)

        zeros = tuple(jnp.zeros((rb, LANES), dtype) for _ in blocks)
        cnts = lax.fori_loop(0, nvalid, body, zeros)
        return [_dot(cnt.astype(BF16), ones) for cnt in cnts]

    def search_bits(nbits, count_ge, to_cand):
        def step(i, carry):
            cand = thr_ref[...] + lax.shift_left(jnp.int32(1), nbits - 1 - i)
            to_cand(cand)
            tots = count_ge()
            for r in range(nrb):
                thr_ref[rows(r), :] = jnp.where(tots[r] >= k_sel, cand[rows(r)], thr_ref[rows(r), :])
            return carry
        lax.fori_loop(0, nbits, step, 0)

    def store_cand(cand):
        cand_ref[...] = cand

    ge_i32 = lambda: count(lambda kv, cb, base, r: jnp.where(kv >= cb, 1.0, 0.0))
    if khi_ref is None:
        thr_ref[...] = jnp.full((tq, LANES), INT_MIN, I32)
        search_bits(32, ge_i32, store_cand)
    else:
        half = 16

        def store_cand_hi(cand):
            bits = jnp.left_shift(jnp.where(cand < 0, cand ^ jnp.int32(0x7FFF), cand), half)
            val = pltpu.bitcast(bits, F32)
            val = jnp.where(cand < NEG_INF_HI_KEY, -jnp.inf, val)
            val = jnp.where(cand > 0, jnp.maximum(val, MIN_NORMAL), val)
            candh_ref[...] = val.astype(BF16)

        one, zero = jnp.ones((), BF16), jnp.zeros((), BF16)
        ge_bf16 = lambda: count(lambda kv, cb, base, r: jnp.where(kv >= cb, one, zero), khi_ref, candh_ref, BF16)
        thr_ref[...] = jnp.full((tq, LANES), -(2 ** (half - 1)), I32)
        search_bits(half, ge_bf16, store_cand_hi)
        thr_ref[...] = jnp.left_shift(thr_ref[...], half)
        search_bits(half, ge_i32, store_cand)

    cand_ref[...] = thr_ref[...]
    c_gt = count(lambda kv, cb, base, r: jnp.where(kv > cb, 1.0, 0.0))
    c_ge = count(lambda kv, cb, base, r: jnp.where(kv >= cb, 1.0, 0.0))
    need = [k_sel - c for c in c_gt]
    jcut_ref[...] = jnp.full((tq, LANES), NO_TIE_CUT, I32)

    if live_rows is not None:
        c_ge = [jnp.where(_row_iota((rb, LANES)) % SUBLANES < live_rows, c, 0.0) for c in c_ge]
    for r in range(nrb):
        @pl.when(jnp.max(c_ge[r]) > k_sel)
        def _(r=r):
            jcut_ref[rows(r), :] = jnp.zeros((rb, LANES), I32)

            def jstep(i, carry):
                candj = jcut_ref[rows(r), :] + lax.shift_left(jnp.int32(1), nbits_idx - 1 - i)
                cand_ref[rows(r), :] = candj
                f = count(lambda kv, cb, base, rr: jnp.where(kv == thr_ref[rows(rr), :],
                                                             jnp.where(lane + base < cb, 1.0, 0.0), 0.0), blocks=(r,))[0]
                jcut_ref[rows(r), :] = jnp.where(f < need[r], candj, jcut_ref[rows(r), :])
                return carry

            lax.fori_loop(0, nbits_idx, jstep, 0)


def _idx_mask_kernel(qi_ref, wi_ref, kidup_ref, mask_ref, keys_ref, thr_ref, jcut_ref, cand_ref, khi_ref, candh_ref, *,
                     k_sel, tq, tk, rb):
    i = pl.program_id(1)
    nkb = keys_ref.shape[0]
    nvalid = ((i + 1) * tq + tk - 1) // tk
    qi = qi_ref[0]
    wi_tile = wi_ref[0]
    q_pos = i * tq + _row_iota((tq, tk))
    lane = _lane_iota((tq, tk))

    def fill(kb, carry):
        kblk = kidup_ref[0, pl.ds(pl.multiple_of(kb * tk, tk), tk), :]
        sc = _index_scores(qi, wi_tile, kblk)
        sc = jnp.where(lane + kb * tk <= q_pos, sc, -jnp.inf)
        sc = jnp.where(sc == 0.0, 0.0, sc)
        keys_ref[kb] = _sort_key(sc)
        khi_ref[kb] = pltpu.bitcast(pltpu.bitcast(sc, I32) & HI_HALF, F32).astype(BF16)
        return carry

    lax.fori_loop(0, nvalid, fill, 0)
    _select_rows(keys_ref, thr_ref, jcut_ref, cand_ref, nvalid, k_sel, tq, tk, rb, khi_ref=khi_ref, candh_ref=candh_ref)

    for kb in range(nkb):
        @pl.when(kb < nvalid)
        def _():
            thr = jnp.broadcast_to(thr_ref[:, 0:1], (tq, tk))
            jcut = jnp.broadcast_to(jcut_ref[:, 0:1], (tq, tk))
            key = keys_ref[kb]
            pos = lane + kb * tk
            sel = jnp.where(key > thr, 1, jnp.where(key == thr, jnp.where(pos <= jcut, 1, 0), 0))
            sel = jnp.where(pos <= q_pos, sel, 0)
            mask_ref[0, :, kb * tk:(kb + 1) * tk] = sel.astype(jnp.int8)

        @pl.when(kb >= nvalid)
        def _():
            mask_ref[0, :, kb * tk:(kb + 1) * tk] = jnp.zeros((tq, tk), jnp.int8)


def _idx_mask(qi, misc, kidup, k_sel, tq, tk):
    b, s, _ = qi.shape
    assert s % tq == 0 and s % tk == 0 and tq % 32 == 0
    rb = min(SEARCH_ROWS, tq)
    kern = functools.partial(_idx_mask_kernel, k_sel=k_sel, tq=tq, tk=tk, rb=rb)
    return pl.pallas_call(
        kern,
        grid=(b, s // tq),
        in_specs=[pl.BlockSpec((1, tq, IDX_HEADS * IDX_DIM), lambda bi, i: (bi, i, 0)),
                  pl.BlockSpec((1, tq, LANES), lambda bi, i: (bi, i, 0)),
                  pl.BlockSpec((1, s, LANES), lambda bi, i: (bi, 0, 0))],
        out_specs=pl.BlockSpec((1, tq, s), lambda bi, i: (bi, i, 0)),
        out_shape=jax.ShapeDtypeStruct((b, s, s), jnp.int8),
        scratch_shapes=[pltpu.VMEM((s // tk, tq, tk), I32)] + [pltpu.VMEM((tq, LANES), I32)] * 3
                       + [pltpu.VMEM((s // tk, tq, tk), BF16), pltpu.VMEM((tq, LANES), BF16)],
        compiler_params=_cparams(("parallel", "parallel")),
        name="idx_mask",
    )(qi, misc, kidup)


def _attn_kernel(q_ref, kdup_ref, vb_ref, mask_ref, o_ref, m_ref, l_ref, acc_ref, *, tq, tk, nsub):
    i = pl.program_id(1)
    kb = pl.program_id(2)
    nk = pl.num_programs(2)
    last_needed = ((i + 1) * tq - 1) // tk

    @pl.when(kb == 0)
    def _():
        m_ref[...] = jnp.full(m_ref.shape, NEG_BIG, F32)
        l_ref[...] = jnp.zeros(l_ref.shape, F32)
        acc_ref[...] = jnp.zeros(acc_ref.shape, F32)

    def key_block(sb):
        ks = slice(sb * tk, (sb + 1) * tk)
        bias = jnp.where(mask_ref[0, :, ks].astype(I32) != 0, 0.0, NEG_BIG).astype(F32)
        bias2 = jnp.concatenate([bias, bias], axis=0)
        lane = _lane_iota((tq, LANES))
        lo = lane < A_HEAD_DIM

        def qk(g):
            q128 = q_ref[0, :, g * LANES:(g + 1) * LANES]
            zero = jnp.zeros_like(q128)
            q2 = jnp.concatenate([jnp.where(lo, q128, zero), jnp.where(lo, zero, q128)], axis=0)
            return _dot_nt(q2, kdup_ref[0, ks, g * LANES:(g + 1) * LANES])

        s_next = qk(0)
        for g in range(A_KV_HEADS):
            s = s_next + bias2
            if g + 1 < A_KV_HEADS:
                s_next = qk(g + 1)
            m_old = m_ref[g]
            m_new = jnp.maximum(m_old, jnp.max(s, axis=1, keepdims=True))
            alpha = jnp.exp2(m_old - m_new)
            p = jnp.exp2(s - m_new[:, 0:1])
            l_ref[g] = alpha * l_ref[g] + jnp.sum(p, axis=1, keepdims=True)
            v128 = vb_ref[0, ks, (g // 2) * LANES:(g // 2 + 1) * LANES]
            acc_ref[g] = alpha * acc_ref[g] + _dot(p.astype(BF16), v128)
            m_ref[g] = m_new

    for sb in range(nsub):
        pl.when(kb * nsub + sb <= last_needed)(functools.partial(key_block, sb))

    @pl.when(kb == nk - 1)
    def _():
        lane = _lane_iota((tq, LANES))
        lo = lane < A_HEAD_DIM
        for g in range(A_KV_HEADS):
            a = acc_ref[g] / l_ref[g]
            top, bot = a[:tq], a[tq:]
            if g % 2 == 0:
                o128 = jnp.where(lo, top, pltpu.roll(bot, A_HEAD_DIM, axis=1))
            else:
                o128 = jnp.where(lo, pltpu.roll(top, A_HEAD_DIM, axis=1), bot)
            o_ref[0, :, g * LANES:(g + 1) * LANES] = o128


def _attn(q, kdup, vb, mask, tq, tk):
    b, s, _ = q.shape
    nsub = ATTN_KEY_BLOCKS_PER_STEP if s % (tk * ATTN_KEY_BLOCKS_PER_STEP) == 0 else 1
    tkb = tk * nsub
    nk = s // tkb

    def kv_idx(bi, i, kb):
        return (bi, jnp.minimum(kb, ((i + 1) * tq - 1) // tkb), 0)

    def mask_idx(bi, i, kb):
        return (bi, i, jnp.minimum(kb, ((i + 1) * tq - 1) // tkb))

    kern = functools.partial(_attn_kernel, tq=tq, tk=tk, nsub=nsub)
    return pl.pallas_call(
        kern,
        grid=(b, s // tq, nk),
        in_specs=[pl.BlockSpec((1, tq, A_Q), lambda bi, i, kb: (bi, i, 0)),
                  pl.BlockSpec((1, tkb, 2 * A_KV), kv_idx),
                  pl.BlockSpec((1, tkb, A_KV), kv_idx),
                  pl.BlockSpec((1, tq, tkb), mask_idx)],
        out_specs=pl.BlockSpec((1, tq, A_Q), lambda bi, i, kb: (bi, i, 0)),
        out_shape=jax.ShapeDtypeStruct((b, s, A_Q), F32),
        scratch_shapes=[pltpu.VMEM((A_KV_HEADS, 2 * tq, LANES), F32)] * 3,
        compiler_params=_cparams(("parallel", "parallel", "arbitrary")),
        name="sel_attn",
    )(q, kdup, vb, mask)


def _conv_kernel(x_ref, xprev_ref, halo0_ref, w_ref, q_ref, k_ref, v_ref, *, tt):
    i = pl.program_id(1)
    x = x_ref[0]
    halo = jnp.where(i == 0, halo0_ref[0], xprev_ref[0])
    w = w_ref[...]

    def post(y, rows):
        y = y * jax.nn.sigmoid(y)
        for h in range(DN_HEADS):
            for j, ref in enumerate((q_ref, k_ref)):
                blk = y[:, j * DN_QK + h * DN_DK: j * DN_QK + (h + 1) * DN_DK]
                blk = blk * lax.rsqrt(jnp.sum(blk * blk, axis=-1, keepdims=True) + EPS)
                ref[0, 0:rows, h * DN_DK:(h + 1) * DN_DK] = blk
        v_ref[0, 0:rows, :] = y[:, 2 * DN_QK:]

    y = x * w[CONV_W - 1:CONV_W, :]
    for j in range(1, CONV_W):
        y = y + pltpu.roll(x, j, axis=0) * w[CONV_W - 1 - j:CONV_W - j, :]
    post(y, tt)

    x8 = x[0:SUBLANES]
    row = _row_iota(x8.shape)
    y8 = x8 * w[CONV_W - 1:CONV_W, :]
    for j in range(1, CONV_W):
        xs = jnp.where(row < j, pltpu.roll(halo, j, axis=0), pltpu.roll(x8, j, axis=0))
        y8 = y8 + xs * w[CONV_W - 1 - j:CONV_W - j, :]
    post(y8, SUBLANES)


def _conv(dqkv, halo0, conv_w, tt):
    b, t, c = dqkv.shape
    assert t % tt == 0 and tt % SUBLANES == 0
    r = tt // SUBLANES
    kern = functools.partial(_conv_kernel, tt=tt)
    out = jax.ShapeDtypeStruct((b, t, DN_QK), F32)
    return pl.pallas_call(
        kern,
        grid=(b, t // tt),
        in_specs=[pl.BlockSpec((1, tt, c), lambda bi, i: (bi, i, 0)),
                  pl.BlockSpec((1, SUBLANES, c), lambda bi, i: (bi, jnp.maximum(i * r - 1, 0), 0)),
                  pl.BlockSpec((1, SUBLANES, c), lambda bi, i: (bi, 0, 0)),
                  pl.BlockSpec((CONV_W, c), lambda bi, i: (0, 0))],
        out_specs=[pl.BlockSpec((1, tt, DN_QK), lambda bi, i: (bi, i, 0))] * 3,
        out_shape=[out, out, out],
        compiler_params=_cparams(("parallel", "parallel")),
        name="dn_conv",
    )(dqkv, dqkv, halo0, conv_w)


def _softplus(x):
    return jnp.maximum(x, 0.0) + jnp.log(1.0 + jnp.exp(-jnp.abs(x)))


def _pad_rows(a, rows):
    if a.shape[0] == rows:
        return a
    return jnp.concatenate([a, jnp.zeros((rows - a.shape[0], a.shape[1]), a.dtype)], axis=0)


DELTA_SEQS_PER_STEP = 4


def _delta_kernel(q_ref, k_ref, v_ref, z_ref, misc_ref, nega_ref, dtb_ref, gout_ref, s0_ref, o_ref, st_ref, *, c, t_valid, bb):
    n = pl.program_id(1)

    @pl.when(n == 0)
    def _():
        st_ref[...] = s0_ref[...]

    row = _row_iota((c, LANES))
    live = row < t_valid
    ri = lax.broadcasted_iota(I32, (c, c), 0)
    ci = lax.broadcasted_iota(I32, (c, c), 1)
    incl = ri >= ci
    strict = ri > ci
    eye = jnp.where(ri == ci, 1.0, 0.0).astype(F32)
    scale = DN_DK ** -0.5

    chains = [(bi, h) for bi in range(bb) for h in range(DN_HEADS)]
    nch = range(len(chains))
    beta_t, gc_t, gc_tr = [], [], []
    for bi in range(bb):
        mt = misc_ref[bi]
        beta_t.append(jnp.where(live, jax.nn.sigmoid(mt), 0.0))
        g = jnp.where(live, nega_ref[...] * _softplus(mt + dtb_ref[...]), 0.0)
        sft = 1
        while sft < c:
            g = g + jnp.where(row >= sft, pltpu.roll(g, sft, axis=0), 0.0)
            sft *= 2
        gc_t.append(g)
        gc_tr.append(_pad_rows(g, LANES).T)

    sl = lambda h: slice(h * DN_DK, (h + 1) * DN_DK)
    q = [jnp.where(live, q_ref[bi, :, sl(h)] * scale, 0.0) for bi, h in chains]
    k = [jnp.where(live, k_ref[bi, :, sl(h)], 0.0) for bi, h in chains]
    v = [jnp.where(live, v_ref[bi, :, sl(h)], 0.0) for bi, h in chains]
    beta = [beta_t[bi][:, MISC_DB + h:MISC_DB + h + 1] for bi, h in chains]
    gcc = [gc_t[bi][:, MISC_DA + h:MISC_DA + h + 1] for bi, h in chains]
    gcr = [gc_tr[bi][MISC_DA + h:MISC_DA + h + 1, 0:c] for bi, h in chains]
    decay = [jnp.where(incl, jnp.exp(jnp.where(incl, gcc[i] - gcr[i], 0.0)), 0.0) for i in nch]
    kb = [k[i] * beta[i] for i in nch]
    pw = [-jnp.where(strict, _dot3(kb[i], k[i], nt=True) * decay[i], 0.0) for i in nch]
    r = [eye + pw[i] for i in nch]
    nn = 2
    while nn < c:
        pw = [_dot3(pw[i], pw[i]) for i in nch]
        r = [_dot3(r[i], eye + pw[i]) for i in nch]
        nn *= 2
    egc = [jnp.exp(gcc[i]) for i in nch]
    u = [_dot3(r[i], v[i] * beta[i]) for i in nch]
    w = [_dot3(r[i], kb[i] * egc[i]) for i in nch]
    a_qk = [_dot3(q[i], k[i], nt=True) * decay[i] for i in nch]
    s = [st_ref[bi, h] for bi, h in chains]
    v_new = [u[i] - _dot3(w[i], s[i]) for i in nch]
    o = [_dot3(q[i] * egc[i], s[i]) + _dot3(a_qk[i], v_new[i]) for i in nch]
    glast = [gcc[i][c - 1:c, :] for i in nch]
    kw_t = [_pad_rows(k[i] * jnp.exp(glast[i] - gcc[i]), LANES).T for i in nch]
    s_new = [s[i] * jnp.exp(glast[i]) + _dot3(kw_t[i], _pad_rows(v_new[i], LANES)) for i in nch]
    for i, (bi, h) in enumerate(chains):
        st_ref[bi, h] = s_new[i]
        on = o[i] * lax.rsqrt(jnp.mean(o[i] * o[i], axis=-1, keepdims=True) + EPS) * gout_ref[...]
        z = z_ref[bi, :, sl(h)]
        o_ref[bi, :, sl(h)] = on * (z * jax.nn.sigmoid(z))


def _delta(qn, kn, v, dz, misc, a_log, dt_bias, g_dn_out, s0, c, t_valid):
    b, t, _ = qn.shape
    assert t % c == 0 and (t_valid == t or t == c)
    nega = jnp.zeros((1, LANES), F32).at[0, MISC_DA:MISC_DA + DN_HEADS].set(-jnp.exp(a_log.astype(F32)))
    dtb = jnp.zeros((1, LANES), F32).at[0, MISC_DA:MISC_DA + DN_HEADS].set(dt_bias.astype(F32))
    tok = lambda bi, n: (bi, n, 0)
    fixed = lambda bi, n: (0, 0)
    st = lambda bi, n: (bi, 0, 0, 0)
    bb = DELTA_SEQS_PER_STEP if b % DELTA_SEQS_PER_STEP == 0 else 1
    kern = functools.partial(_delta_kernel, c=c, t_valid=t_valid, bb=bb)
    return pl.pallas_call(
        kern,
        grid=(b // bb, t // c),
        in_specs=[pl.BlockSpec((bb, c, DN_QK), tok), pl.BlockSpec((bb, c, DN_QK), tok), pl.BlockSpec((bb, c, DN_V), tok),
                  pl.BlockSpec((bb, c, DN_V), tok), pl.BlockSpec((bb, c, LANES), tok),
                  pl.BlockSpec((1, LANES), fixed), pl.BlockSpec((1, LANES), fixed), pl.BlockSpec((1, DN_DV), fixed),
                  pl.BlockSpec((bb, DN_HEADS, DN_DK, DN_DV), st)],
        out_specs=[pl.BlockSpec((bb, c, DN_V), tok), pl.BlockSpec((bb, DN_HEADS, DN_DK, DN_DV), st)],
        out_shape=[jax.ShapeDtypeStruct((b, t, DN_V), F32), jax.ShapeDtypeStruct((b, DN_HEADS, DN_DK, DN_DV), F32)],
        compiler_params=_cparams(("parallel", "arbitrary")),
        name="delta_rule",
    )(qn, kn, v, dz, misc, nega, dtb, g_dn_out.reshape(1, DN_DV).astype(F32), s0)


def _merge_kernel(x_ref, oa_ref, ob_ref, ga_ref, gb_ref, woa_ref, wob_ref, wout_ref, gffn_ref, wpqt_ref,
                  x1_ref, h2_ref, qt_ref):
    ma = jax.nn.sigmoid(ga_ref[...]) * _dot(oa_ref[...].astype(BF16), woa_ref[...])
    mb = jax.nn.sigmoid(gb_ref[...]) * _dot(ob_ref[...].astype(BF16), wob_ref[...])
    x1 = x_ref[...] + _dot((ma + mb).astype(BF16), wout_ref[...])
    x1_ref[...] = x1
    h2 = x1 * lax.rsqrt(jnp.mean(x1 * x1, axis=-1, keepdims=True) + EPS) * gffn_ref[...]
    h2_ref[...] = h2
    qt_ref[...] = _dot_nt(wpqt_ref[...], h2.astype(BF16))


def _merge(x2d, oa, ob, ga, gb, w_oa, w_ob, w_out, g_ffn, w_pq, tm):
    n = x2d.shape[0]
    nq = w_pq.shape[1]
    row = lambda i: (i, 0)
    fixed = lambda i: (0, 0)
    return pl.pallas_call(
        _merge_kernel,
        grid=(n // tm,),
        in_specs=[pl.BlockSpec((tm, D_MODEL), row), pl.BlockSpec((tm, A_Q), row), pl.BlockSpec((tm, DN_V), row),
                  pl.BlockSpec((tm, D_MODEL), row), pl.BlockSpec((tm, D_MODEL), row),
                  pl.BlockSpec((A_Q, D_MODEL), fixed), pl.BlockSpec((DN_V, D_MODEL), fixed),
                  pl.BlockSpec((D_MODEL, D_MODEL), fixed), pl.BlockSpec((1, D_MODEL), fixed),
                  pl.BlockSpec((nq, D_MODEL), fixed)],
        out_specs=[pl.BlockSpec((tm, D_MODEL), row), pl.BlockSpec((tm, D_MODEL), row), pl.BlockSpec((nq, tm), lambda i: (0, i))],
        out_shape=[jax.ShapeDtypeStruct((n, D_MODEL), F32), jax.ShapeDtypeStruct((n, D_MODEL), F32),
                   jax.ShapeDtypeStruct((nq, n), F32)],
        compiler_params=_cparams(("parallel",)),
        name="merge_out",
    )(x2d, oa, ob, ga, gb, w_oa.astype(BF16), w_ob.astype(BF16), w_out.astype(BF16),
      g_ffn.reshape(1, D_MODEL).astype(F32), w_pq.T.astype(BF16))


def _peer_cands():
    cands = [(i, j) for i in range(PEER_TOPK) for j in range(PEER_TOPK) if (i + 1) * (j + 1) <= PEER_TOPK]
    return sorted(cands, key=lambda ij: ij[0] * PEER_TOPK + ij[1])


def _route_kernel(qt_ref, sk_ref, eid_ref, gate_ref, toff_ref, val_ref, idx_ref, *, tn):
    half = PEER_DKEY // 2
    key_iota = lax.broadcasted_iota(I32, (PEER_KEYS, tn), 0)

    heads_per_iter = 4

    def per_group(hg, carry):
        tabs = [(hh, c) for hh in range(heads_per_iter) for c in range(2)]
        ss = []
        for hh, c in tabs:
            t = 2 * (hg * heads_per_iter + hh) + c
            qblk = qt_ref[pl.ds(pl.multiple_of(t * half, half), half), :]
            ss.append(_dot3(sk_ref[t], qblk))
        for r in range(PEER_TOPK):
            ms = [jnp.max(s, axis=0, keepdims=True) for s in ss]
            ams = [jnp.min(jnp.where(s == m, key_iota, PEER_KEYS), axis=0, keepdims=True) for s, m in zip(ss, ms)]
            for i, (hh, c) in enumerate(tabs):
                h = hg * heads_per_iter + hh
                val_ref[c, r, pl.ds(h, 1), :] = ms[i]
                idx_ref[c, r, pl.ds(h, 1), :] = ams[i]
            ss = [jnp.where(key_iota == am, -jnp.inf, s) for s, am in zip(ss, ams)]
        return carry

    lax.fori_loop(0, PEER_HEADS // heads_per_iter, per_group, 0)

    cands = _peer_cands()
    cv = [val_ref[0, i] + val_ref[1, j] for i, j in cands]
    ce = [idx_ref[0, i] * PEER_KEYS + idx_ref[1, j] for i, j in cands]
    nc = len(cands)
    rank = []
    for a in range(nc):
        rk = jnp.zeros(cv[a].shape, I32)
        for b in range(nc):
            if b < a:
                rk = rk + jnp.where(cv[b] >= cv[a], 1, 0)
            elif b > a:
                rk = rk + jnp.where(cv[b] > cv[a], 1, 0)
        rank.append(rk)
    mx = cv[0]
    ex = [jnp.where(rank[a] < PEER_TOPK, jnp.exp(cv[a] - mx), 0.0) for a in range(nc)]
    den = ex[0]
    for a in range(1, nc):
        den = den + ex[a]
    inv = 1.0 / den
    e_slots = []
    for slot in range(PEER_TOPK):
        e = jnp.zeros(cv[0].shape, I32)
        g = jnp.zeros(cv[0].shape, F32)
        for a in range(nc):
            hit = rank[a] == slot
            e = jnp.where(hit, ce[a], e)
            g = jnp.where(hit, ex[a], g)
        eid_ref[slot] = e
        gate_ref[slot] = g * inv
        e_slots.append(e)
    toff_ref[...] = ((jnp.concatenate(e_slots, axis=0) & (PEER_HALF - 1)) * SUBLANES).T


def _route(qt, sub_keys, tn):
    nq, n = qt.shape
    sk = sub_keys.reshape(2 * PEER_HEADS, PEER_KEYS, PEER_DKEY // 2).astype(F32)
    kern = functools.partial(_route_kernel, tn=tn)
    blk = pl.BlockSpec((PEER_TOPK, PEER_HEADS, tn), lambda i: (0, 0, i))
    npair = PEER_TOPK * PEER_HEADS
    return pl.pallas_call(
        kern,
        grid=(n // tn,),
        in_specs=[pl.BlockSpec((nq, tn), lambda i: (0, i)),
                  pl.BlockSpec((2 * PEER_HEADS, PEER_KEYS, PEER_DKEY // 2), lambda i: (0, 0, 0))],
        out_specs=[blk, blk, pl.BlockSpec((tn, npair), lambda i: (i, 0))],
        out_shape=[jax.ShapeDtypeStruct((PEER_TOPK, PEER_HEADS, n), I32), jax.ShapeDtypeStruct((PEER_TOPK, PEER_HEADS, n), F32),
                   jax.ShapeDtypeStruct((n, npair), I32)],
        scratch_shapes=[pltpu.VMEM((2, PEER_TOPK, PEER_HEADS, tn), F32), pltpu.VMEM((2, PEER_TOPK, PEER_HEADS, tn), I32)],
        compiler_params=_cparams(("parallel",)),
        name="peer_route",
    )(qt, sk)


def _pack_table(tab):
    e, d = tab.shape
    assert d == SUBLANES * LANES
    bits = lax.bitcast_convert_type(tab.astype(BF16), jnp.uint16).astype(jnp.uint32)
    assert e == 2 * PEER_HALF
    words = (bits[:PEER_HALF] << 16) | bits[PEER_HALF:]
    return lax.bitcast_convert_type(words, I32).reshape(e // 2 * SUBLANES, LANES)


HI_HALF = -65536


def _bf16_words(x):
    return pltpu.bitcast(x.astype(BF16).astype(F32), I32) & HI_HALF


def _shr16(w):
    return lax.shift_right_logical(w, jnp.full(w.shape, 16, I32))


def _packed_mul(a_words, b_words):
    return pltpu.bitcast(pltpu.bitcast(a_words, BF16) * pltpu.bitcast(b_words, BF16), I32)


def _packed_add(a_words, b_words):
    return pltpu.bitcast(pltpu.bitcast(a_words, BF16) + pltpu.bitcast(b_words, BF16), I32)


def _hi_f32(w):
    return pltpu.bitcast(w & HI_HALF, F32)


def _lo_f32(w):
    return pltpu.bitcast(jnp.left_shift(w, 16), F32)


_BITREV8 = (0, 4, 2, 6, 1, 5, 3, 7)


PEER_U_SUB = 16


def _peer_u_kernel(toff_ref, x_ref, eid_ref, gate_ref, tab_ref, ce_ref, co_ref, rhi_ref, rlo_ref, *, tn, sub_t):
    npair = gate_ref.shape[0]
    lane = _lane_iota((npair, tn))
    sub = lax.broadcasted_iota(I32, (SUBLANES, LANES), 0)
    keep = {step: (sub & step) == 0 for step in (4, 2, 1)}

    def per_token(tt, t0):
        t = t0 + tt
        xw = _bf16_words(x_ref[t])
        xw = xw | _shr16(xw)
        for g in range(npair // SUBLANES):
            prods = []
            for p in _BITREV8:
                off = pl.multiple_of(toff_ref[t * npair + (g * SUBLANES + p)], SUBLANES)
                prods.append(_packed_mul(tab_ref[pl.ds(off, SUBLANES), :], xw))
            step = SUBLANES // 2
            while step >= 1:
                nxt = []
                for a in range(0, len(prods), 2):
                    lo_t, hi_t = prods[a], prods[a + 1]
                    left = jnp.where(keep[step], lo_t, pltpu.roll(hi_t, step, axis=0))
                    right = jnp.where(keep[step], pltpu.roll(lo_t, SUBLANES - step, axis=0), hi_t)
                    nxt.append(_packed_add(left, right))
                prods = nxt
                step //= 2
            rhi_ref[tt, g * SUBLANES:(g + 1) * SUBLANES, :] = _hi_f32(prods[0])
            rlo_ref[tt, g * SUBLANES:(g + 1) * SUBLANES, :] = _lo_f32(prods[0])
        return t0

    def per_sub(sb, accs):
        acc_hi, acc_lo = accs
        t0 = sb * sub_t
        lax.fori_loop(0, sub_t, per_token, t0)
        ones = jnp.ones((LANES, LANES), BF16)
        for tt in range(sub_t):
            hit = lane == t0 + tt
            acc_hi = jnp.where(hit, jnp.sum(rhi_ref[tt], axis=1, keepdims=True), acc_hi)
            acc_lo = jnp.where(hit, _dot(rlo_ref[tt].astype(BF16), ones)[:, 0:tn], acc_lo)
        return acc_hi, acc_lo

    zero = jnp.zeros((npair, tn), F32)
    acc_hi, acc_lo = lax.fori_loop(0, tn // sub_t, per_sub, (zero, zero))
    even = eid_ref[...] < PEER_HALF
    coef = gate_ref[...] * jax.nn.gelu(jnp.where(even, acc_hi, acc_lo))
    ce_ref[...] = jnp.where(even, coef, 0.0)
    co_ref[...] = jnp.where(even, 0.0, coef)


def _peer_v_kernel(toff_ref, ce_ref, co_ref, rep_ref, tab_ref, y_ref, bw_ref, *, tn, npair):
    ce = ce_ref[...].astype(BF16)
    co = co_ref[...].astype(BF16)
    chunk = min(16, tn)
    for c in range(tn // chunk):
        rep = rep_ref[:, c * chunk * LANES:(c + 1) * chunk * LANES]
        be = _dot(ce, rep)
        bo = _dot(co, rep)
        words = (pltpu.bitcast(be, I32) & HI_HALF) | _shr16(pltpu.bitcast(bo, I32))
        for t in range(chunk):
            bw_ref[c * chunk + t] = words[:, t * LANES:(t + 1) * LANES]

    nacc = 4

    def per_token(t, carry):
        zero = jnp.zeros((SUBLANES, LANES), F32)
        acc_hi = [zero] * nacc
        acc_lo = [zero] * nacc
        base = t * npair
        for k in range(npair):
            off = pl.multiple_of(toff_ref[base + k], SUBLANES)
            cw = jnp.broadcast_to(bw_ref[t, k:k + 1, :], (SUBLANES, LANES))
            prod = _packed_mul(tab_ref[pl.ds(off, SUBLANES), :], cw)
            acc_hi[k % nacc] = acc_hi[k % nacc] + _hi_f32(prod)
            acc_lo[k % nacc] = acc_lo[k % nacc] + _lo_f32(prod)
        y_ref[t] = ((acc_hi[0] + acc_hi[1]) + (acc_hi[2] + acc_hi[3])) + ((acc_lo[0] + acc_lo[1]) + (acc_lo[2] + acc_lo[3]))
        return carry

    lax.fori_loop(0, tn, per_token, 0)


def _peer_experts(eid, gate, toff, h2, tab_u, tab_v, tn):
    npair, n = eid.shape
    d = h2.shape[1]
    x3 = h2.reshape(n, d // LANES, LANES)
    tab_spec = pl.BlockSpec(tab_u.shape, lambda i: (0, 0), pipeline_mode=pl.Buffered(1))
    smem_blk = lambda tt: pl.BlockSpec((tt * npair,), lambda i: (i,), memory_space=pltpu.SMEM)
    slot_blk = lambda tt: pl.BlockSpec((npair, tt), lambda i: (0, i))
    sub_t = PEER_U_SUB if tn % PEER_U_SUB == 0 else tn
    ce, co = pl.pallas_call(
        functools.partial(_peer_u_kernel, tn=tn, sub_t=sub_t),
        grid=(n // tn,),
        in_specs=[smem_blk(tn), pl.BlockSpec((tn, d // LANES, LANES), lambda i: (i, 0, 0)), slot_blk(tn), slot_blk(tn), tab_spec],
        out_specs=[slot_blk(tn), slot_blk(tn)],
        out_shape=[jax.ShapeDtypeStruct((npair, n), F32)] * 2,
        scratch_shapes=[pltpu.VMEM((sub_t, npair, LANES), F32)] * 2,
        compiler_params=_cparams(("arbitrary",)),
        name="peer_u",
    )(toff.reshape(n * npair), x3, eid, gate, tab_u)
    tv = tn
    rep = jnp.asarray(np.arange(tv)[:, None] == np.arange(tv * LANES)[None, :] // LANES, BF16)
    y3 = pl.pallas_call(
        functools.partial(_peer_v_kernel, tn=tv, npair=npair),
        grid=(n // tv,),
        in_specs=[smem_blk(tv), slot_blk(tv), slot_blk(tv),
                  pl.BlockSpec(rep.shape, lambda i: (0, 0), pipeline_mode=pl.Buffered(1)), tab_spec],
        out_specs=pl.BlockSpec((tv, d // LANES, LANES), lambda i: (i, 0, 0)),
        out_shape=jax.ShapeDtypeStruct((n, d // LANES, LANES), F32),
        scratch_shapes=[pltpu.VMEM((tv, npair, LANES), I32)],
        compiler_params=_cparams(("arbitrary",)),
        name="peer_v",
    )(toff.reshape(n * npair), ce, co, rep, tab_v)
    return y3.reshape(n, d)


def _ple_kernel(x1_ref, yp_ref, p_ref, gple_ref, wg_ref, wp_ref, o_ref):
    x2 = x1_ref[...] + yp_ref[...]
    hn = x2 * lax.rsqrt(jnp.mean(x2 * x2, axis=-1, keepdims=True) + EPS) * gple_ref[...]
    gate = jax.nn.sigmoid(_dot(hn.astype(BF16), wg_ref[...]))
    o_ref[...] = x2 + gate * _dot(p_ref[...].astype(BF16), wp_ref[...])


def _ple(x1, yp, p2d, g_ple, w_ple_gate, w_ple, tm):
    n = x1.shape[0]
    row = lambda i: (i, 0)
    fixed = lambda i: (0, 0)
    return pl.pallas_call(
        _ple_kernel,
        grid=(n // tm,),
        in_specs=[pl.BlockSpec((tm, D_MODEL), row), pl.BlockSpec((tm, D_MODEL), row), pl.BlockSpec((tm, PLE_DIM), row),
                  pl.BlockSpec((1, D_MODEL), fixed), pl.BlockSpec((D_MODEL, D_MODEL), fixed),
                  pl.BlockSpec((PLE_DIM, D_MODEL), fixed)],
        out_specs=pl.BlockSpec((tm, D_MODEL), row),
        out_shape=jax.ShapeDtypeStruct((n, D_MODEL), F32),
        compiler_params=_cparams(("parallel",)),
        name="ple_out",
    )(x1, yp, p2d, g_ple.reshape(1, D_MODEL).astype(F32), w_ple_gate.astype(BF16), w_ple.astype(BF16))


PAGES_PER_STEP = 32
SAMPLE_SOFTMAX_STATES = 4


def _sample_scores_kernel(pt_ref, qh_ref, wrep_ref, kinew_ref, *refs, npg):
    pages = refs[:npg]
    sc_ref, scnew_ref = refs[npg:]
    j = pl.program_id(1)
    qh = qh_ref[0]
    wrep = wrep_ref[0]

    def raw(keys_t):
        return _dot(qh, keys_t.astype(BF16))

    def score(s):
        s = jnp.maximum(s, 0.0) * wrep
        tot = s[0:SUBLANES]
        for h in range(1, IDX_HEADS):
            tot = tot + s[h * SUBLANES:(h + 1) * SUBLANES]
        return tot

    dots = [raw(pages[p][0]) for p in range(npg)]
    for p in range(npg):
        sc_ref[0, :, p * PAGE_SIZE:(p + 1) * PAGE_SIZE] = score(dots[p])

    @pl.when(j == 0)
    def _():
        sn = score(raw(kinew_ref[0]))
        causal = _lane_iota((SUBLANES, PAGE_SIZE)) <= _row_iota((SUBLANES, PAGE_SIZE))
        scnew_ref[0] = jnp.where(causal, sn, -jnp.inf)


def _sample_thr_kernel(sc_ref, thr_ref, jcut_ref, keys_ref, cand_ref, *, k_sel, tq, tk, t):
    nkb = keys_ref.shape[0]
    for kb in range(nkb):
        keys_ref[kb] = _sort_key(sc_ref[:, :, kb * tk:(kb + 1) * tk].reshape(tq, tk))
    _select_rows(keys_ref, thr_ref, jcut_ref, cand_ref, nkb, k_sel, tq, tk, min(SEARCH_ROWS, tq), live_rows=t)


def _sample_attn_kernel(pt_ref, q_ref, sc_ref, scnew_ref, thr_ref, jcut_ref, knew_ref, vnew_ref, *refs, npg, past):
    kpages = refs[:npg]
    vpages = refs[npg:2 * npg]
    o_ref, m_ref, l_ref, acc_ref = refs[2 * npg:]
    j = pl.program_id(1)

    @pl.when(j == 0)
    def _():
        m_ref[...] = jnp.full(m_ref.shape, NEG_BIG, F32)
        l_ref[...] = jnp.zeros(l_ref.shape, F32)
        acc_ref[...] = jnp.zeros(acc_ref.shape, F32)

    q = q_ref[0]
    thr = thr_ref[...]
    jcut = jcut_ref[...]
    lane = _lane_iota((SUBLANES, PAGE_SIZE))
    kvd = A_KV_HEADS * A_HEAD_DIM

    wide = lambda a: jnp.concatenate([a] * (kvd // LANES), axis=1)

    def update(groups):
        ss = []
        for _, kts, _, scs, pos0s, causal in groups:
            biases = []
            for sc8, pos0 in zip(scs, pos0s):
                key = _sort_key(sc8)
                pos = lane + pos0
                sel = jnp.where(key > thr, 0.0, jnp.where(key == thr, jnp.where(pos <= jcut, 0.0, NEG_BIG), NEG_BIG))
                if causal is not None:
                    sel = jnp.where(causal, sel, NEG_BIG)
                biases.append(sel)
            bias = jnp.concatenate([jnp.concatenate(biases, axis=1)] * A_HEADS, axis=0)
            ss.append(jnp.concatenate([_dot(q, kt[...].reshape(kvd, PAGE_SIZE).astype(BF16)) for kt in kts], axis=1) + bias)
        m_olds = [m_ref[g[0]] for g in groups]
        m_news = [jnp.maximum(mo, jnp.max(s, axis=1, keepdims=True)) for mo, s in zip(m_olds, ss)]
        alphas = [jnp.exp2(mo - mn) for mo, mn in zip(m_olds, m_news)]
        ps = [jnp.exp2(s - mn[:, 0:1]) for s, mn in zip(ss, m_news)]
        sums = [jnp.sum(p, axis=1, keepdims=True) for p in ps]
        pvs = []
        for (_, _, vts, _, _, _), p in zip(groups, ps):
            pb = p.astype(BF16)
            parts = [_dot_nt(pb[:, i * PAGE_SIZE:(i + 1) * PAGE_SIZE], vt[...].reshape(kvd, PAGE_SIZE).astype(BF16))
                     for i, vt in enumerate(vts)]
            pv = parts[0]
            for part in parts[1:]:
                pv = pv + part
            pvs.append(pv)
        for g, mn, al, sm, pv in zip(groups, m_news, alphas, sums, pvs):
            ci = g[0]
            l_ref[ci] = al * l_ref[ci] + sm
            acc_ref[ci] = wide(al) * acc_ref[ci] + pv
            m_ref[ci] = mn

    nst = m_ref.shape[0]
    update([(ci, [kpages[p].at[0] for p in range(ci, npg, nst)], [vpages[p].at[0] for p in range(ci, npg, nst)],
             [sc_ref[0, :, p * PAGE_SIZE:(p + 1) * PAGE_SIZE] for p in range(ci, npg, nst)],
             [(j * npg + p) * PAGE_SIZE for p in range(ci, npg, nst)], None) for ci in range(nst)])

    @pl.when(j == pl.num_programs(1) - 1)
    def _():
        update([(0, [knew_ref.at[0]], [vnew_ref.at[0]], [scnew_ref[0]], [past], lane <= _row_iota((SUBLANES, PAGE_SIZE)))])
        m_all = m_ref[0]
        for ci in range(1, nst):
            m_all = jnp.maximum(m_all, m_ref[ci])
        l_all = jnp.zeros(m_all.shape, F32)
        acc_all = jnp.zeros(acc_ref.shape[1:], F32)
        for ci in range(nst):
            w = jnp.exp2(m_ref[ci] - m_all)
            l_all = l_all + w * l_ref[ci]
            acc_all = acc_all + wide(w) * acc_ref[ci]
        o_ref[0] = acc_all / wide(l_all)


def _head_major(a, bd, t, nh, hd):
    a = a.reshape(bd, t, nh, hd).transpose(0, 2, 1, 3)
    a = jnp.pad(a, ((0, 0), (0, 0), (0, SUBLANES - t), (0, 0)))
    return a.reshape(bd, nh * SUBLANES, hd)


def _attn_sample(q, k, v, qi, misc, cache_k, cache_v, cache_idx_k, page_table, bd, t):
    assert t <= SUBLANES
    n_pages = page_table.shape[1]
    past = n_pages * PAGE_SIZE
    npg = PAGES_PER_STEP if n_pages % PAGES_PER_STEP == 0 else 1
    nj = n_pages // npg
    k_sel = min(TOPK_MAX, (past + t) // 4)
    rows = A_HEADS * SUBLANES

    qh = _head_major(qi, bd, t, IDX_HEADS, IDX_DIM)
    wi = misc[:, MISC_WI:MISC_WI + IDX_HEADS].reshape(bd, t, IDX_HEADS).transpose(0, 2, 1)
    wrep = jnp.broadcast_to(jnp.pad(wi, ((0, 0), (0, 0), (0, SUBLANES - t))).reshape(bd, rows, 1), (bd, rows, LANES))
    pad_page = lambda a, w: jnp.pad(a.reshape(bd, t, w), ((0, 0), (0, PAGE_SIZE - t), (0, 0)))
    kinew = pad_page(misc[:, :IDX_DIM], IDX_DIM).transpose(0, 2, 1)
    cik = cache_idx_k.transpose(0, 2, 1)

    per_b = lambda b, j, pt: (b, 0, 0)
    per_b4 = lambda b, j, pt: (b, 0, 0, 0)
    page_spec = lambda shape, p: pl.BlockSpec(shape, lambda b, j, pt: (pt[b, j * npg + p],) + (0,) * (len(shape) - 1))
    sc, scnew = pl.pallas_call(
        functools.partial(_sample_scores_kernel, npg=npg),
        grid_spec=pltpu.PrefetchScalarGridSpec(
            num_scalar_prefetch=1, grid=(bd, nj),
            in_specs=[pl.BlockSpec((1, rows, IDX_DIM), per_b), pl.BlockSpec((1, rows, LANES), per_b),
                      pl.BlockSpec((1, IDX_DIM, PAGE_SIZE), per_b)]
                     + [page_spec((1, IDX_DIM, PAGE_SIZE), p) for p in range(npg)],
            out_specs=[pl.BlockSpec((1, SUBLANES, npg * PAGE_SIZE), lambda b, j, pt: (b, 0, j)),
                       pl.BlockSpec((1, SUBLANES, PAGE_SIZE), per_b)]),
        out_shape=[jax.ShapeDtypeStruct((bd, SUBLANES, past), F32), jax.ShapeDtypeStruct((bd, SUBLANES, PAGE_SIZE), F32)],
        compiler_params=_cparams(("parallel", "arbitrary")),
        name="sample_scores",
    )(page_table, qh, wrep, kinew, *([cik] * npg))

    length = past + PAGE_SIZE
    sc_all = jnp.concatenate([sc, scnew], axis=2)
    tb = SUBLANES if bd % SUBLANES == 0 else 1
    tq = tb * SUBLANES
    tk = 5 * LANES if length % (5 * LANES) == 0 else LANES
    thr, jcut = pl.pallas_call(
        functools.partial(_sample_thr_kernel, k_sel=k_sel, tq=tq, tk=tk, t=t),
        grid=(bd // tb,),
        in_specs=[pl.BlockSpec((tb, SUBLANES, length), lambda i: (i, 0, 0))],
        out_specs=[pl.BlockSpec((tq, LANES), lambda i: (i, 0))] * 2,
        out_shape=[jax.ShapeDtypeStruct((bd * SUBLANES, LANES), I32)] * 2,
        scratch_shapes=[pltpu.VMEM((length // tk, tq, tk), I32), pltpu.VMEM((tq, LANES), I32)],
        compiler_params=_cparams(("parallel",)),
        name="sample_thr",
    )(sc_all)

    qa = _head_major(q, bd, t, A_HEADS, A_HEAD_DIM)
    kv_of_row = (np.arange(rows) // SUBLANES) // A_REP
    own = jnp.asarray(kv_of_row[:, None] == np.arange(A_KV_HEADS)[None, :])
    qa = jnp.where(own[None, :, :, None], qa[:, :, None, :], jnp.zeros((), qa.dtype)).reshape(bd, rows, A_KV)
    to_page_t = lambda a: pad_page(a, A_KV).reshape(bd, PAGE_SIZE, A_KV_HEADS, A_HEAD_DIM).transpose(0, 2, 3, 1)
    knew, vnew = to_page_t(k), to_page_t(v)
    ck = cache_k.transpose(0, 2, 3, 1)
    cv = cache_v.transpose(0, 2, 3, 1)
    kv_blk = (1, A_KV_HEADS, A_HEAD_DIM, PAGE_SIZE)
    nst = SAMPLE_SOFTMAX_STATES if npg % SAMPLE_SOFTMAX_STATES == 0 else 1
    o = pl.pallas_call(
        functools.partial(_sample_attn_kernel, npg=npg, past=past),
        grid_spec=pltpu.PrefetchScalarGridSpec(
            num_scalar_prefetch=1, grid=(bd, nj),
            in_specs=[pl.BlockSpec((1, rows, A_KV), per_b),
                      pl.BlockSpec((1, SUBLANES, npg * PAGE_SIZE), lambda b, j, pt: (b, 0, j)),
                      pl.BlockSpec((1, SUBLANES, PAGE_SIZE), per_b),
                      pl.BlockSpec((SUBLANES, LANES), lambda b, j, pt: (b, 0)),
                      pl.BlockSpec((SUBLANES, LANES), lambda b, j, pt: (b, 0)),
                      pl.BlockSpec(kv_blk, per_b4), pl.BlockSpec(kv_blk, per_b4)]
                     + [page_spec(kv_blk, p) for p in range(npg)] * 2,
            out_specs=pl.BlockSpec((1, rows, A_KV), per_b),
            scratch_shapes=[pltpu.VMEM((nst, rows, LANES), F32), pltpu.VMEM((nst, rows, LANES), F32),
                            pltpu.VMEM((nst, rows, A_KV), F32)]),
        out_shape=jax.ShapeDtypeStruct((bd, rows, A_KV), F32),
        compiler_params=_cparams(("parallel", "arbitrary")),
        name="sample_attn",
    )(page_table, qa, sc, scnew, thr, jcut, knew, vnew, *([ck] * npg), *([cv] * npg))
    o = jnp.sum(jnp.where(own[None, :, :, None], o.reshape(bd, rows, A_KV_HEADS, A_HEAD_DIM), 0.0), axis=2)
    o = o.reshape(bd, A_HEADS, SUBLANES, A_HEAD_DIM)[:, :, :t].transpose(0, 2, 1, 3)
    return o.reshape(bd * t, A_Q)


def _tile(n, pref):
    return pref if n % pref == 0 else n


def _layer(x, p_emb, conv_state, delta_state, attn_fn, prm):
    b, t, _ = x.shape
    n = b * t
    x2d = x.reshape(n, D_MODEL)
    tm = _tile(n, ROW_TILE)
    (q, k, kdup, v, vb, qi, misc, kidup, dqkv, dz, ga, gb) = _proj(
        x2d, prm['g_mix'], prm['w_perm'], prm['g_q'], prm['g_k'], prm['g_idx_k'], tm)

    oa = attn_fn(q, k, kdup, v, vb, qi, misc, kidup)

    tp = -(-t // SUBLANES) * SUBLANES
    c = min(DN_CHUNK, tp)
    pad_t = lambda a: jnp.pad(a.reshape(b, t, a.shape[-1]), ((0, 0), (0, tp - t), (0, 0)))
    dqkv3 = dqkv.reshape(b, t, DN_CONV_CH)
    halo = jnp.zeros((b, SUBLANES, DN_CONV_CH), F32)
    if conv_state is not None:
        halo = halo.at[:, SUBLANES - (CONV_W - 1):].set(conv_state.astype(F32))
        hist = jnp.concatenate([conv_state.astype(F32), dqkv3], axis=1)
    else:
        hist = jnp.concatenate([jnp.zeros((b, CONV_W - 1, DN_CONV_CH), F32), dqkv3], axis=1)
    new_conv = hist[:, -(CONV_W - 1):]
    qn, kn, vv = _conv(pad_t(dqkv), halo, prm['conv_w'], _tile(tp, CONV_TT))
    s0 = jnp.zeros((b, DN_HEADS, DN_DK, DN_DV), F32) if delta_state is None else delta_state.astype(F32)
    ob, new_delta = _delta(qn, kn, vv, pad_t(dz), pad_t(misc), prm['a_log'], prm['dt_bias'], prm['g_dn_out'], s0, c, t)
    ob = ob[:, :t].reshape(n, DN_V)

    x1, h2, qt = _merge(x2d, oa, ob, ga, gb, prm['w_oa'], prm['w_ob'], prm['w_out'], prm['g_ffn'], prm['w_pq'], tm)
    eid, gate, toff = _route(qt, prm['sub_keys'], _tile(n, TOKEN_TILE))
    npair = PEER_TOPK * PEER_HEADS
    yp = _peer_experts(eid.reshape(npair, n), gate.reshape(npair, n), toff, h2, prm['tab_u'], prm['tab_v'], _tile(n, TOKEN_TILE))
    y = _ple(x1, yp, p_emb.reshape(n, PLE_DIM), prm['g_ple'], prm['w_ple_gate'], prm['w_ple'], tm)

    return (y.reshape(b, t, D_MODEL), k.reshape(b, t, A_KV_HEADS, A_HEAD_DIM), v.reshape(b, t, A_KV_HEADS, A_HEAD_DIM),
            misc[:, :IDX_DIM].reshape(b, t, IDX_DIM), new_conv, new_delta)


def kernel(x_prompt, x_sample, cache_k, cache_v, cache_idx_k, state_conv, state_delta, page_table, p_prompt, p_sample,
           g_mix, w_in, g_q, g_k, g_idx_k, conv_w, a_log, dt_bias, g_dn_out, w_oa, w_ob, w_out, g_ffn, w_pq, sub_keys,
           peer_u, peer_v, g_ple, w_ple_gate, w_ple):
    depth = w_in.shape[0]
    xp, xs = x_prompt, x_sample
    outs = [[] for _ in range(10)]
    for i in range(depth):
        prm = dict(g_mix=g_mix[i], w_perm=_permute_w_in(w_in[i]), g_q=g_q[i], g_k=g_k[i], g_idx_k=g_idx_k[i],
                   conv_w=conv_w[i].astype(F32), a_log=a_log[i], dt_bias=dt_bias[i], g_dn_out=g_dn_out[i],
                   w_oa=w_oa[i], w_ob=w_ob[i], w_out=w_out[i], g_ffn=g_ffn[i], w_pq=w_pq[i], sub_keys=sub_keys[i],
                   tab_u=_pack_table(peer_u[i]), tab_v=_pack_table(peer_v[i]), g_ple=g_ple[i],
                   w_ple_gate=w_ple_gate[i], w_ple=w_ple[i])

        bp, s, _ = xp.shape

        def attn_prompt(q, k, kdup, v, vb, qi, misc, kidup):
            k_sel = min(TOPK_MAX, s // 4)
            tq, tk = _tile(s, ATTN_TQ), _tile(s, ATTN_TK)
            r3 = lambda a: a.reshape(bp, s, a.shape[-1])
            mask = _idx_mask(r3(qi), r3(misc), r3(kidup), k_sel, _tile(s, IDX_TQ), tk)
            return _attn(r3(q), r3(kdup), r3(vb), mask, tq, tk).reshape(bp * s, A_Q)

        res = _layer(xp, p_prompt[i], None, None, attn_prompt, prm)
        xp = res[0]
        for lst, val in zip(outs[:5], res[1:]):
            lst.append(val)

        bd, t, _ = xs.shape

        def attn_sample(q, k, kdup, v, vb, qi, misc, kidup):
            return _attn_sample(q, k, v, qi, misc, cache_k[i], cache_v[i], cache_idx_k[i], page_table, bd, t)

        res = _layer(xs, p_sample[i], state_conv[i], state_delta[i], attn_sample, prm)
        xs = res[0]
        for lst, val in zip(outs[5:], res[1:]):
            lst.append(val)

    cast = [cache_k.dtype, cache_v.dtype, cache_idx_k.dtype, state_conv.dtype, state_delta.dtype] * 2
    stacked = [jnp.stack(lst).astype(dt) for lst, dt in zip(outs, cast)]
    return (xp, xs, *stacked)
```

```python
import functools
import math

import jax
import jax.numpy as jnp
import numpy as np
from jax import lax
from jax.experimental import pallas as pl
from jax.experimental.pallas import tpu as pltpu

F32 = jnp.float32
BF16 = jnp.bfloat16
I32 = jnp.int32

D_MODEL = 1024
PAGE_SIZE = 128
A_HEADS = 8
A_KV_HEADS = 4
A_REP = A_HEADS // A_KV_HEADS
A_HEAD_DIM = 64
A_Q = A_HEADS * A_HEAD_DIM
A_KV = A_KV_HEADS * A_HEAD_DIM
IDX_HEADS = 8
IDX_DIM = 64
TOPK_MAX = 256
DN_HEADS = 4
DN_DK = 128
DN_DV = 128
DN_QK = DN_HEADS * DN_DK
DN_V = DN_HEADS * DN_DV
DN_CONV_CH = 2 * DN_QK + DN_V
CONV_W = 4
DN_CHUNK = 64
PEER_HEADS = 8
PEER_KEYS = 128
PEER_DKEY = 256
PEER_TOPK = 16
PEER_HALF = PEER_KEYS * PEER_KEYS // 2
PLE_DIM = 256
EPS = 1e-6
IN_SPLITS = (A_Q, A_KV, A_KV, IDX_HEADS * IDX_DIM, IDX_DIM, IDX_HEADS, DN_CONV_CH, DN_V, DN_HEADS, DN_HEADS, D_MODEL, D_MODEL)

LANES = 128
SUBLANES = 8
VMEM_LIMIT = 56 * 1024 * 1024

ROW_TILE = 256
ATTN_TQ = 256
IDX_TQ = 512
ATTN_TK = 512
ATTN_KEY_BLOCKS_PER_STEP = 4
CONV_TT = 512
TOKEN_TILE = LANES
SEARCH_ROWS = 64
NO_TIE_CUT = 2 ** 30
NEG_INF_HI_KEY = (0xFF80 ^ 0x7FFF) - 2 ** 16
MIN_NORMAL = 2.0 ** -126
NEG_BIG = -1e30
Q_SCALE = A_HEAD_DIM ** -0.5 * math.log2(math.e)
INT_MIN = -(2 ** 31)

MISC_WI = IDX_DIM
MISC_DB = MISC_WI + IDX_HEADS
MISC_DA = MISC_DB + DN_HEADS

SEG_Q = 0
SEG_K = SEG_Q + A_Q
SEG_V = SEG_K + A_KV
SEG_QI = SEG_V + A_KV
SEG_MISC = SEG_QI + IDX_HEADS * IDX_DIM
SEG_DQKV = SEG_MISC + LANES
SEG_DZ = SEG_DQKV + DN_CONV_CH
SEG_GA = SEG_DZ + DN_V
SEG_GB = SEG_GA + D_MODEL
SEG_END = SEG_GB + D_MODEL


def _cparams(sem):
    return pltpu.CompilerParams(dimension_semantics=sem, vmem_limit_bytes=VMEM_LIMIT)


def _dot(a, b):
    return jnp.dot(a, b, preferred_element_type=F32)


def _dot_nt(a, b):
    return lax.dot_general(a, b, (((1,), (1,)), ((), ())), preferred_element_type=F32)


def _split(a):
    hi = a.astype(BF16)
    lo = (a - hi.astype(F32)).astype(BF16)
    return hi, lo


def _dot3(a, b, nt=False):
    d = _dot_nt if nt else _dot
    ah, al = _split(a)
    bh, bl = _split(b)
    return d(ah, bh) + (d(ah, bl) + d(al, bh))


def _lane_iota(shape):
    return lax.broadcasted_iota(I32, shape, len(shape) - 1)


def _row_iota(shape):
    return lax.broadcasted_iota(I32, shape, len(shape) - 2)


def _half_norm(blk, gain):
    lane = _lane_iota(blk.shape)
    lo = lane < A_HEAD_DIM
    sq = blk * blk
    s_lo = jnp.sum(jnp.where(lo, sq, 0.0), axis=-1, keepdims=True)
    s_hi = jnp.sum(jnp.where(lo, 0.0, sq), axis=-1, keepdims=True)
    r_lo = lax.rsqrt(s_lo * (1.0 / A_HEAD_DIM) + EPS)
    r_hi = lax.rsqrt(s_hi * (1.0 / A_HEAD_DIM) + EPS)
    return blk * jnp.where(lo, r_lo, r_hi) * gain


def _proj_kernel(x_ref, gmix_ref, w_ref, gq_ref, gk_ref, gik_ref,
                 q_ref, k_ref, kdup_ref, v_ref, vb_ref, qi_ref, misc_ref, kidup_ref,
                 dqkv_ref, dz_ref, ga_ref, gb_ref):
    x = x_ref[...]
    h = x * lax.rsqrt(jnp.mean(x * x, axis=-1, keepdims=True) + EPS) * gmix_ref[...]
    hb = h.astype(BF16)

    def seg(a, b):
        return _dot(hb, w_ref[:, a:b])

    lane = _lane_iota((x.shape[0], LANES))
    lo = lane < A_HEAD_DIM

    zq = seg(SEG_Q, SEG_K)
    for c in range(A_Q // LANES):
        blk = _half_norm(zq[:, c * LANES:(c + 1) * LANES], gq_ref[...])
        q_ref[:, c * LANES:(c + 1) * LANES] = (blk * Q_SCALE).astype(BF16)

    zk = seg(SEG_K, SEG_V)
    for c in range(A_KV // LANES):
        blk = _half_norm(zk[:, c * LANES:(c + 1) * LANES], gk_ref[...])
        k_ref[:, c * LANES:(c + 1) * LANES] = blk
        rolled = pltpu.roll(blk, A_HEAD_DIM, axis=1)
        kdup_ref[:, (2 * c) * LANES:(2 * c + 1) * LANES] = jnp.where(lo, blk, rolled).astype(BF16)
        kdup_ref[:, (2 * c + 1) * LANES:(2 * c + 2) * LANES] = jnp.where(lo, rolled, blk).astype(BF16)

    zv = seg(SEG_V, SEG_QI)
    v_ref[...] = zv
    vb_ref[...] = zv.astype(BF16)

    qi_ref[...] = seg(SEG_QI, SEG_MISC).astype(BF16)

    zm = seg(SEG_MISC, SEG_DQKV)
    s_ik = jnp.sum(jnp.where(lo, zm * zm, 0.0), axis=-1, keepdims=True)
    kin = zm * lax.rsqrt(s_ik * (1.0 / IDX_DIM) + EPS) * gik_ref[...]
    wscale = (IDX_HEADS * IDX_DIM) ** -0.5
    misc_ref[...] = jnp.where(lo, kin, jnp.where(lane < MISC_DB, zm * wscale, zm))
    kin0 = jnp.where(lo, kin, 0.0)
    kidup_ref[...] = (kin0 + pltpu.roll(kin0, IDX_DIM, axis=1)).astype(BF16)

    dqkv_ref[...] = seg(SEG_DQKV, SEG_DZ)
    dz_ref[...] = seg(SEG_DZ, SEG_GA)
    ga_ref[...] = seg(SEG_GA, SEG_GB)
    gb_ref[...] = seg(SEG_GB, SEG_END)


def _permute_w_in(w_in):
    cuts = np.cumsum((0,) + IN_SPLITS)
    parts = [w_in[:, cuts[i]:cuts[i + 1]] for i in range(len(IN_SPLITS))]
    aq, ak, av, iq, ik, iw, dqkv, dz, db, da, ga, gb = parts
    pad = jnp.zeros((w_in.shape[0], LANES - IDX_DIM - IDX_HEADS - 2 * DN_HEADS), w_in.dtype)
    return jnp.concatenate([aq, ak, av, iq, ik, iw, db, da, pad, dqkv, dz, ga, gb], axis=1).astype(BF16)


def _tile2(g):
    return jnp.concatenate([g, g]).reshape(1, LANES).astype(F32)


def _proj(x2d, g_mix, w_perm, g_q, g_k, g_idx_k, tm):
    n = x2d.shape[0]
    assert n % tm == 0
    widths = [(A_Q, BF16), (A_KV, F32), (2 * A_KV, BF16), (A_KV, F32), (A_KV, BF16), (IDX_HEADS * IDX_DIM, BF16),
              (LANES, F32), (LANES, BF16), (DN_CONV_CH, F32), (DN_V, F32), (D_MODEL, F32), (D_MODEL, F32)]
    row = lambda i: (i, 0)
    fixed = lambda i: (0, 0)
    return pl.pallas_call(
        _proj_kernel,
        grid=(n // tm,),
        in_specs=[pl.BlockSpec((tm, D_MODEL), row), pl.BlockSpec((1, D_MODEL), fixed),
                  pl.BlockSpec((D_MODEL, SEG_END), fixed), pl.BlockSpec((1, LANES), fixed),
                  pl.BlockSpec((1, LANES), fixed), pl.BlockSpec((1, LANES), fixed)],
        out_specs=[pl.BlockSpec((tm, w), row) for w, _ in widths],
        out_shape=[jax.ShapeDtypeStruct((n, w), dt) for w, dt in widths],
        compiler_params=_cparams(("parallel",)),
        name="proj_in",
    )(x2d, g_mix.reshape(1, D_MODEL), w_perm, _tile2(g_q), _tile2(g_k), _tile2(g_idx_k))


def _sort_key(score):
    score = jnp.where(score == 0.0, 0.0, score)
    bits = pltpu.bitcast(score, I32)
    return jnp.where(bits < 0, bits ^ jnp.int32(0x7FFFFFFF), bits)


def _index_scores(qi, wi_tile, kblk):
    lane = _lane_iota((qi.shape[0], LANES))
    lo = lane < IDX_DIM
    sc = None
    for c in range(IDX_HEADS // 2):
        q128 = qi[:, c * LANES:(c + 1) * LANES]
        zero = jnp.zeros_like(q128)
        for half in range(2):
            qm = jnp.where(lo, q128, zero) if half == 0 else jnp.where(lo, zero, q128)
            s = _dot_nt(qm, kblk)
            hidx = MISC_WI + 2 * c + half
            term = jnp.maximum(s, 0.0) * wi_tile[:, hidx:hidx + 1]
            sc = term if sc is None else sc + term
    return sc


def _select_rows(keys_ref, thr_ref, jcut_ref, cand_ref, nvalid, k_sel, tq, tk, rb, live_rows=None, khi_ref=None,
                 candh_ref=None):
    nchunk = tk // LANES
    nrb = tq // rb
    nbits_idx = int(math.ceil(math.log2(keys_ref.shape[0] * tk))) + 1
    assert keys_ref.shape[0] * nchunk <= 256
    lane = _lane_iota((rb, LANES))
    ones = jnp.ones((LANES, LANES), BF16)
    rows = lambda r: slice(r * rb, (r + 1) * rb)

    def count(pred, src_ref=keys_ref, cnd_ref=cand_ref, dtype=F32, blocks=tuple(range(nrb))):
        def body(kb, cnts):
            out = []
            for cnt, r in zip(cnts, blocks):
                blk = src_ref[kb, rows(r), :]
                cand_b = cnd_ref[rows(r), :]
                for c in range(nchunk):
                    cnt = cnt + pred(blk[:, c * LANES:(c + 1) * LANES], cand_b, kb * tk + c * LANES, r)
                out.append(cnt)
            return tuple(out)

        zeros = tuple(jnp.zeros((rb, LANES), dtype) for _ in blocks)
        cnts = lax.fori_loop(0, nvalid, body, zeros)
        return [_dot(cnt.astype(BF16), ones) for cnt in cnts]

    def search_bits(nbits, count_ge, to_cand):
        def step(i, carry):
            cand = thr_ref[...] + lax.shift_left(jnp.int32(1), nbits - 1 - i)
            to_cand(cand)
            tots = count_ge()
            for r in range(nrb):
                thr_ref[rows(r), :] = jnp.where(tots[r] >= k_sel, cand[rows(r)], thr_ref[rows(r), :])
            return carry
        lax.fori_loop(0, nbits, step, 0)

    def store_cand(cand):
        cand_ref[...] = cand

    ge_i32 = lambda: count(lambda kv, cb, base, r: jnp.where(kv >= cb, 1.0, 0.0))
    if khi_ref is None:
        thr_ref[...] = jnp.full((tq, LANES), INT_MIN, I32)
        search_bits(32, ge_i32, store_cand)
    else:
        half = 16

        def store_cand_hi(cand):
            bits = jnp.left_shift(jnp.where(cand < 0, cand ^ jnp.int32(0x7FFF), cand), half)
            val = pltpu.bitcast(bits, F32)
            val = jnp.where(cand < NEG_INF_HI_KEY, -jnp.inf, val)
            val = jnp.where(cand > 0, jnp.maximum(val, MIN_NORMAL), val)
            candh_ref[...] = val.astype(BF16)

        one, zero = jnp.ones((), BF16), jnp.zeros((), BF16)
        ge_bf16 = lambda: count(lambda kv, cb, base, r: jnp.where(kv >= cb, one, zero), khi_ref, candh_ref, BF16)
        thr_ref[...] = jnp.full((tq, LANES), -(2 ** (half - 1)), I32)
        search_bits(half, ge_bf16, store_cand_hi)
        thr_ref[...] = jnp.left_shift(thr_ref[...], half)
        search_bits(half, ge_i32, store_cand)

    cand_ref[...] = thr_ref[...]
    c_gt = count(lambda kv, cb, base, r: jnp.where(kv > cb, 1.0, 0.0))
    c_ge = count(lambda kv, cb, base, r: jnp.where(kv >= cb, 1.0, 0.0))
    need = [k_sel - c for c in c_gt]
    jcut_ref[...] = jnp.full((tq, LANES), NO_TIE_CUT, I32)

    if live_rows is not None:
        c_ge = [jnp.where(_row_iota((rb, LANES)) % SUBLANES < live_rows, c, 0.0) for c in c_ge]
    for r in range(nrb):
        @pl.when(jnp.max(c_ge[r]) > k_sel)
        def _(r=r):
            jcut_ref[rows(r), :] = jnp.zeros((rb, LANES), I32)

            def jstep(i, carry):
                candj = jcut_ref[rows(r), :] + lax.shift_left(jnp.int32(1), nbits_idx - 1 - i)
                cand_ref[rows(r), :] = candj
                f = count(lambda kv, cb, base, rr: jnp.where(kv == thr_ref[rows(rr), :],
                                                             jnp.where(lane + base < cb, 1.0, 0.0), 0.0), blocks=(r,))[0]
                jcut_ref[rows(r), :] = jnp.where(f < need[r], candj, jcut_ref[rows(r), :])
                return carry

            lax.fori_loop(0, nbits_idx, jstep, 0)


def _idx_mask_kernel(qi_ref, wi_ref, kidup_ref, mask_ref, keys_ref, thr_ref, jcut_ref, cand_ref, khi_ref, candh_ref, *,
                     k_sel, tq, tk, rb):
    i = pl.program_id(1)
    nkb = keys_ref.shape[0]
    nvalid = ((i + 1) * tq + tk - 1) // tk
    qi = qi_ref[0]
    wi_tile = wi_ref[0]
    q_pos = i * tq + _row_iota((tq, tk))
    lane = _lane_iota((tq, tk))

    def fill(kb, carry):
        kblk = kidup_ref[0, pl.ds(pl.multiple_of(kb * tk, tk), tk), :]
        sc = _index_scores(qi, wi_tile, kblk)
        sc = jnp.where(lane + kb * tk <= q_pos, sc, -jnp.inf)
        sc = jnp.where(sc == 0.0, 0.0, sc)
        keys_ref[kb] = _sort_key(sc)
        khi_ref[kb] = pltpu.bitcast(pltpu.bitcast(sc, I32) & HI_HALF, F32).astype(BF16)
        return carry

    lax.fori_loop(0, nvalid, fill, 0)
    _select_rows(keys_ref, thr_ref, jcut_ref, cand_ref, nvalid, k_sel, tq, tk, rb, khi_ref=khi_ref, candh_ref=candh_ref)

    for kb in range(nkb):
        @pl.when(kb < nvalid)
        def _():
            thr = jnp.broadcast_to(thr_ref[:, 0:1], (tq, tk))
            jcut = jnp.broadcast_to(jcut_ref[:, 0:1], (tq, tk))
            key = keys_ref[kb]
            pos = lane + kb * tk
            sel = jnp.where(key > thr, 1, jnp.where(key == thr, jnp.where(pos <= jcut, 1, 0), 0))
            sel = jnp.where(pos <= q_pos, sel, 0)
            mask_ref[0, :, kb * tk:(kb + 1) * tk] = sel.astype(jnp.int8)

        @pl.when(kb >= nvalid)
        def _():
            mask_ref[0, :, kb * tk:(kb + 1) * tk] = jnp.zeros((tq, tk), jnp.int8)


def _idx_mask(qi, misc, kidup, k_sel, tq, tk):
    b, s, _ = qi.shape
    assert s % tq == 0 and s % tk == 0 and tq % 32 == 0
    rb = min(SEARCH_ROWS, tq)
    kern = functools.partial(_idx_mask_kernel, k_sel=k_sel, tq=tq, tk=tk, rb=rb)
    return pl.pallas_call(
        kern,
        grid=(b, s // tq),
        in_specs=[pl.BlockSpec((1, tq, IDX_HEADS * IDX_DIM), lambda bi, i: (bi, i, 0)),
                  pl.BlockSpec((1, tq, LANES), lambda bi, i: (bi, i, 0)),
                  pl.BlockSpec((1, s, LANES), lambda bi, i: (bi, 0, 0))],
        out_specs=pl.BlockSpec((1, tq, s), lambda bi, i: (bi, i, 0)),
        out_shape=jax.ShapeDtypeStruct((b, s, s), jnp.int8),
        scratch_shapes=[pltpu.VMEM((s // tk, tq, tk), I32)] + [pltpu.VMEM((tq, LANES), I32)] * 3
                       + [pltpu.VMEM((s // tk, tq, tk), BF16), pltpu.VMEM((tq, LANES), BF16)],
        compiler_params=_cparams(("parallel", "parallel")),
        name="idx_mask",
    )(qi, misc, kidup)


def _attn_kernel(q_ref, kdup_ref, vb_ref, mask_ref, o_ref, m_ref, l_ref, acc_ref, *, tq, tk, nsub):
    i = pl.program_id(1)
    kb = pl.program_id(2)
    nk = pl.num_programs(2)
    last_needed = ((i + 1) * tq - 1) // tk

    @pl.when(kb == 0)
    def _():
        m_ref[...] = jnp.full(m_ref.shape, NEG_BIG, F32)
        l_ref[...] = jnp.zeros(l_ref.shape, F32)
        acc_ref[...] = jnp.zeros(acc_ref.shape, F32)

    def key_block(sb):
        ks = slice(sb * tk, (sb + 1) * tk)
        bias = jnp.where(mask_ref[0, :, ks].astype(I32) != 0, 0.0, NEG_BIG).astype(F32)
        bias2 = jnp.concatenate([bias, bias], axis=0)
        lane = _lane_iota((tq, LANES))
        lo = lane < A_HEAD_DIM

        def qk(g):
            q128 = q_ref[0, :, g * LANES:(g + 1) * LANES]
            zero = jnp.zeros_like(q128)
            q2 = jnp.concatenate([jnp.where(lo, q128, zero), jnp.where(lo, zero, q128)], axis=0)
            return _dot_nt(q2, kdup_ref[0, ks, g * LANES:(g + 1) * LANES])

        s_next = qk(0)
        for g in range(A_KV_HEADS):
            s = s_next + bias2
            if g + 1 < A_KV_HEADS:
                s_next = qk(g + 1)
            m_old = m_ref[g]
            m_new = jnp.maximum(m_old, jnp.max(s, axis=1, keepdims=True))
            alpha = jnp.exp2(m_old - m_new)
            p = jnp.exp2(s - m_new[:, 0:1])
            l_ref[g] = alpha * l_ref[g] + jnp.sum(p, axis=1, keepdims=True)
            v128 = vb_ref[0, ks, (g // 2) * LANES:(g // 2 + 1) * LANES]
            acc_ref[g] = alpha * acc_ref[g] + _dot(p.astype(BF16), v128)
            m_ref[g] = m_new

    for sb in range(nsub):
        pl.when(kb * nsub + sb <= last_needed)(functools.partial(key_block, sb))

    @pl.when(kb == nk - 1)
    def _():
        lane = _lane_iota((tq, LANES))
        lo = lane < A_HEAD_DIM
        for g in range(A_KV_HEADS):
            a = acc_ref[g] / l_ref[g]
            top, bot = a[:tq], a[tq:]
            if g % 2 == 0:
                o128 = jnp.where(lo, top, pltpu.roll(bot, A_HEAD_DIM, axis=1))
            else:
                o128 = jnp.where(lo, pltpu.roll(top, A_HEAD_DIM, axis=1), bot)
            o_ref[0, :, g * LANES:(g + 1) * LANES] = o128


def _attn(q, kdup, vb, mask, tq, tk):
    b, s, _ = q.shape
    nsub = ATTN_KEY_BLOCKS_PER_STEP if s % (tk * ATTN_KEY_BLOCKS_PER_STEP) == 0 else 1
    tkb = tk * nsub
    nk = s // tkb

    def kv_idx(bi, i, kb):
        return (bi, jnp.minimum(kb, ((i + 1) * tq - 1) // tkb), 0)

    def mask_idx(bi, i, kb):
        return (bi, i, jnp.minimum(kb, ((i + 1) * tq - 1) // tkb))

    kern = functools.partial(_attn_kernel, tq=tq, tk=tk, nsub=nsub)
    return pl.pallas_call(
        kern,
        grid=(b, s // tq, nk),
        in_specs=[pl.BlockSpec((1, tq, A_Q), lambda bi, i, kb: (bi, i, 0)),
                  pl.BlockSpec((1, tkb, 2 * A_KV), kv_idx),
                  pl.BlockSpec((1, tkb, A_KV), kv_idx),
                  pl.BlockSpec((1, tq, tkb), mask_idx)],
        out_specs=pl.BlockSpec((1, tq, A_Q), lambda bi, i, kb: (bi, i, 0)),
        out_shape=jax.ShapeDtypeStruct((b, s, A_Q), F32),
        scratch_shapes=[pltpu.VMEM((A_KV_HEADS, 2 * tq, LANES), F32)] * 3,
        compiler_params=_cparams(("parallel", "parallel", "arbitrary")),
        name="sel_attn",
    )(q, kdup, vb, mask)


def _conv_kernel(x_ref, xprev_ref, halo0_ref, w_ref, q_ref, k_ref, v_ref, *, tt):
    i = pl.program_id(1)
    x = x_ref[0]
    halo = jnp.where(i == 0, halo0_ref[0], xprev_ref[0])
    w = w_ref[...]

    def post(y, rows):
        y = y * jax.nn.sigmoid(y)
        for h in range(DN_HEADS):
            for j, ref in enumerate((q_ref, k_ref)):
                blk = y[:, j * DN_QK + h * DN_DK: j * DN_QK + (h + 1) * DN_DK]
                blk = blk * lax.rsqrt(jnp.sum(blk * blk, axis=-1, keepdims=True) + EPS)
                ref[0, 0:rows, h * DN_DK:(h + 1) * DN_DK] = blk
        v_ref[0, 0:rows, :] = y[:, 2 * DN_QK:]

    y = x * w[CONV_W - 1:CONV_W, :]
    for j in range(1, CONV_W):
        y = y + pltpu.roll(x, j, axis=0) * w[CONV_W - 1 - j:CONV_W - j, :]
    post(y, tt)

    x8 = x[0:SUBLANES]
    row = _row_iota(x8.shape)
    y8 = x8 * w[CONV_W - 1:CONV_W, :]
    for j in range(1, CONV_W):
        xs = jnp.where(row < j, pltpu.roll(halo, j, axis=0), pltpu.roll(x8, j, axis=0))
        y8 = y8 + xs * w[CONV_W - 1 - j:CONV_W - j, :]
    post(y8, SUBLANES)


def _conv(dqkv, halo0, conv_w, tt):
    b, t, c = dqkv.shape
    assert t % tt == 0 and tt % SUBLANES == 0
    r = tt // SUBLANES
    kern = functools.partial(_conv_kernel, tt=tt)
    out = jax.ShapeDtypeStruct((b, t, DN_QK), F32)
    return pl.pallas_call(
        kern,
        grid=(b, t // tt),
        in_specs=[pl.BlockSpec((1, tt, c), lambda bi, i: (bi, i, 0)),
                  pl.BlockSpec((1, SUBLANES, c), lambda bi, i: (bi, jnp.maximum(i * r - 1, 0), 0)),
                  pl.BlockSpec((1, SUBLANES, c), lambda bi, i: (bi, 0, 0)),
                  pl.BlockSpec((CONV_W, c), lambda bi, i: (0, 0))],
        out_specs=[pl.BlockSpec((1, tt, DN_QK), lambda bi, i: (bi, i, 0))] * 3,
        out_shape=[out, out, out],
        compiler_params=_cparams(("parallel", "parallel")),
        name="dn_conv",
    )(dqkv, dqkv, halo0, conv_w)


def _softplus(x):
    return jnp.maximum(x, 0.0) + jnp.log(1.0 + jnp.exp(-jnp.abs(x)))


def _pad_rows(a, rows):
    if a.shape[0] == rows:
        return a
    return jnp.concatenate([a, jnp.zeros((rows - a.shape[0], a.shape[1]), a.dtype)], axis=0)


DELTA_SEQS_PER_STEP = 4


def _delta_kernel(q_ref, k_ref, v_ref, z_ref, misc_ref, nega_ref, dtb_ref, gout_ref, s0_ref, o_ref, st_ref, *, c, t_valid, bb):
    n = pl.program_id(1)

    @pl.when(n == 0)
    def _():
        st_ref[...] = s0_ref[...]

    row = _row_iota((c, LANES))
    live = row < t_valid
    ri = lax.broadcasted_iota(I32, (c, c), 0)
    ci = lax.broadcasted_iota(I32, (c, c), 1)
    incl = ri >= ci
    strict = ri > ci
    eye = jnp.where(ri == ci, 1.0, 0.0).astype(F32)
    scale = DN_DK ** -0.5

    chains = [(bi, h) for bi in range(bb) for h in range(DN_HEADS)]
    nch = range(len(chains))
    beta_t, gc_t, gc_tr = [], [], []
    for bi in range(bb):
        mt = misc_ref[bi]
        beta_t.append(jnp.where(live, jax.nn.sigmoid(mt), 0.0))
        g = jnp.where(live, nega_ref[...] * _softplus(mt + dtb_ref[...]), 0.0)
        sft = 1
        while sft < c:
            g = g + jnp.where(row >= sft, pltpu.roll(g, sft, axis=0), 0.0)
            sft *= 2
        gc_t.append(g)
        gc_tr.append(_pad_rows(g, LANES).T)

    sl = lambda h: slice(h * DN_DK, (h + 1) * DN_DK)
    q = [jnp.where(live, q_ref[bi, :, sl(h)] * scale, 0.0) for bi, h in chains]
    k = [jnp.where(live, k_ref[bi, :, sl(h)], 0.0) for bi, h in chains]
    v = [jnp.where(live, v_ref[bi, :, sl(h)], 0.0) for bi, h in chains]
    beta = [beta_t[bi][:, MISC_DB + h:MISC_DB + h + 1] for bi, h in chains]
    gcc = [gc_t[bi][:, MISC_DA + h:MISC_DA + h + 1] for bi, h in chains]
    gcr = [gc_tr[bi][MISC_DA + h:MISC_DA + h + 1, 0:c] for bi, h in chains]
    decay = [jnp.where(incl, jnp.exp(jnp.where(incl, gcc[i] - gcr[i], 0.0)), 0.0) for i in nch]
    kb = [k[i] * beta[i] for i in nch]
    pw = [-jnp.where(strict, _dot3(kb[i], k[i], nt=True) * decay[i], 0.0) for i in nch]
    r = [eye + pw[i] for i in nch]
    nn = 2
    while nn < c:
        pw = [_dot3(pw[i], pw[i]) for i in nch]
        r = [_dot3(r[i], eye + pw[i]) for i in nch]
        nn *= 2
    egc = [jnp.exp(gcc[i]) for i in nch]
    u = [_dot3(r[i], v[i] * beta[i]) for i in nch]
    w = [_dot3(r[i], kb[i] * egc[i]) for i in nch]
    a_qk = [_dot3(q[i], k[i], nt=True) * decay[i] for i in nch]
    s = [st_ref[bi, h] for bi, h in chains]
    v_new = [u[i] - _dot3(w[i], s[i]) for i in nch]
    o = [_dot3(q[i] * egc[i], s[i]) + _dot3(a_qk[i], v_new[i]) for i in nch]
    glast = [gcc[i][c - 1:c, :] for i in nch]
    kw_t = [_pad_rows(k[i] * jnp.exp(glast[i] - gcc[i]), LANES).T for i in nch]
    s_new = [s[i] * jnp.exp(glast[i]) + _dot3(kw_t[i], _pad_rows(v_new[i], LANES)) for i in nch]
    for i, (bi, h) in enumerate(chains):
        st_ref[bi, h] = s_new[i]
        on = o[i] * lax.rsqrt(jnp.mean(o[i] * o[i], axis=-1, keepdims=True) + EPS) * gout_ref[...]
        z = z_ref[bi, :, sl(h)]
        o_ref[bi, :, sl(h)] = on * (z * jax.nn.sigmoid(z))


def _delta(qn, kn, v, dz, misc, a_log, dt_bias, g_dn_out, s0, c, t_valid):
    b, t, _ = qn.shape
    assert t % c == 0 and (t_valid == t or t == c)
    nega = jnp.zeros((1, LANES), F32).at[0, MISC_DA:MISC_DA + DN_HEADS].set(-jnp.exp(a_log.astype(F32)))
    dtb = jnp.zeros((1, LANES), F32).at[0, MISC_DA:MISC_DA + DN_HEADS].set(dt_bias.astype(F32))
    tok = lambda bi, n: (bi, n, 0)
    fixed = lambda bi, n: (0, 0)
    st = lambda bi, n: (bi, 0, 0, 0)
    bb = DELTA_SEQS_PER_STEP if b % DELTA_SEQS_PER_STEP == 0 else 1
    kern = functools.partial(_delta_kernel, c=c, t_valid=t_valid, bb=bb)
    return pl.pallas_call(
        kern,
        grid=(b // bb, t // c),
        in_specs=[pl.BlockSpec((bb, c, DN_QK), tok), pl.BlockSpec((bb, c, DN_QK), tok), pl.BlockSpec((bb, c, DN_V), tok),
                  pl.BlockSpec((bb, c, DN_V), tok), pl.BlockSpec((bb, c, LANES), tok),
                  pl.BlockSpec((1, LANES), fixed), pl.BlockSpec((1, LANES), fixed), pl.BlockSpec((1, DN_DV), fixed),
                  pl.BlockSpec((bb, DN_HEADS, DN_DK, DN_DV), st)],
        out_specs=[pl.BlockSpec((bb, c, DN_V), tok), pl.BlockSpec((bb, DN_HEADS, DN_DK, DN_DV), st)],
        out_shape=[jax.ShapeDtypeStruct((b, t, DN_V), F32), jax.ShapeDtypeStruct((b, DN_HEADS, DN_DK, DN_DV), F32)],
        compiler_params=_cparams(("parallel", "arbitrary")),
        name="delta_rule",
    )(qn, kn, v, dz, misc, nega, dtb, g_dn_out.reshape(1, DN_DV).astype(F32), s0)


def _merge_kernel(x_ref, oa_ref, ob_ref, ga_ref, gb_ref, woa_ref, wob_ref, wout_ref, gffn_ref, wpqt_ref,
                  x1_ref, h2_ref, qt_ref):
    ma = jax.nn.sigmoid(ga_ref[...]) * _dot(oa_ref[...].astype(BF16), woa_ref[...])
    mb = jax.nn.sigmoid(gb_ref[...]) * _dot(ob_ref[...].astype(BF16), wob_ref[...])
    x1 = x_ref[...] + _dot((ma + mb).astype(BF16), wout_ref[...])
    x1_ref[...] = x1
    h2 = x1 * lax.rsqrt(jnp.mean(x1 * x1, axis=-1, keepdims=True) + EPS) * gffn_ref[...]
    h2_ref[...] = h2
    qt_ref[...] = _dot_nt(wpqt_ref[...], h2.astype(BF16))


def _merge(x2d, oa, ob, ga, gb, w_oa, w_ob, w_out, g_ffn, w_pq, tm):
    n = x2d.shape[0]
    nq = w_pq.shape[1]
    row = lambda i: (i, 0)
    fixed = lambda i: (0, 0)
    return pl.pallas_call(
        _merge_kernel,
        grid=(n // tm,),
        in_specs=[pl.BlockSpec((tm, D_MODEL), row), pl.BlockSpec((tm, A_Q), row), pl.BlockSpec((tm, DN_V), row),
                  pl.BlockSpec((tm, D_MODEL), row), pl.BlockSpec((tm, D_MODEL), row),
                  pl.BlockSpec((A_Q, D_MODEL), fixed), pl.BlockSpec((DN_V, D_MODEL), fixed),
                  pl.BlockSpec((D_MODEL, D_MODEL), fixed), pl.BlockSpec((1, D_MODEL), fixed),
                  pl.BlockSpec((nq, D_MODEL), fixed)],
        out_specs=[pl.BlockSpec((tm, D_MODEL), row), pl.BlockSpec((tm, D_MODEL), row), pl.BlockSpec((nq, tm), lambda i: (0, i))],
        out_shape=[jax.ShapeDtypeStruct((n, D_MODEL), F32), jax.ShapeDtypeStruct((n, D_MODEL), F32),
                   jax.ShapeDtypeStruct((nq, n), F32)],
        compiler_params=_cparams(("parallel",)),
        name="merge_out",
    )(x2d, oa, ob, ga, gb, w_oa.astype(BF16), w_ob.astype(BF16), w_out.astype(BF16),
      g_ffn.reshape(1, D_MODEL).astype(F32), w_pq.T.astype(BF16))


def _peer_cands():
    cands = [(i, j) for i in range(PEER_TOPK) for j in range(PEER_TOPK) if (i + 1) * (j + 1) <= PEER_TOPK]
    return sorted(cands, key=lambda ij: ij[0] * PEER_TOPK + ij[1])


def _route_kernel(qt_ref, sk_ref, eid_ref, gate_ref, toff_ref, val_ref, idx_ref, *, tn):
    half = PEER_DKEY // 2
    key_iota = lax.broadcasted_iota(I32, (PEER_KEYS, tn), 0)

    heads_per_iter = 4

    def per_group(hg, carry):
        tabs = [(hh, c) for hh in range(heads_per_iter) for c in range(2)]
        ss = []
        for hh, c in tabs:
            t = 2 * (hg * heads_per_iter + hh) + c
            qblk = qt_ref[pl.ds(pl.multiple_of(t * half, half), half), :]
            ss.append(_dot3(sk_ref[t], qblk))
        for r in range(PEER_TOPK):
            ms = [jnp.max(s, axis=0, keepdims=True) for s in ss]
            ams = [jnp.min(jnp.where(s == m, key_iota, PEER_KEYS), axis=0, keepdims=True) for s, m in zip(ss, ms)]
            for i, (hh, c) in enumerate(tabs):
                h = hg * heads_per_iter + hh
                val_ref[c, r, pl.ds(h, 1), :] = ms[i]
                idx_ref[c, r, pl.ds(h, 1), :] = ams[i]
            ss = [jnp.where(key_iota == am, -jnp.inf, s) for s, am in zip(ss, ams)]
        return carry

    lax.fori_loop(0, PEER_HEADS // heads_per_iter, per_group, 0)

    cands = _peer_cands()
    cv = [val_ref[0, i] + val_ref[1, j] for i, j in cands]
    ce = [idx_ref[0, i] * PEER_KEYS + idx_ref[1, j] for i, j in cands]
    nc = len(cands)
    rank = []
    for a in range(nc):
        rk = jnp.zeros(cv[a].shape, I32)
        for b in range(nc):
            if b < a:
                rk = rk + jnp.where(cv[b] >= cv[a], 1, 0)
            elif b > a:
                rk = rk + jnp.where(cv[b] > cv[a], 1, 0)
        rank.append(rk)
    mx = cv[0]
    ex = [jnp.where(rank[a] < PEER_TOPK, jnp.exp(cv[a] - mx), 0.0) for a in range(nc)]
    den = ex[0]
    for a in range(1, nc):
        den = den + ex[a]
    inv = 1.0 / den
    e_slots = []
    for slot in range(PEER_TOPK):
        e = jnp.zeros(cv[0].shape, I32)
        g = jnp.zeros(cv[0].shape, F32)
        for a in range(nc):
            hit = rank[a] == slot
            e = jnp.where(hit, ce[a], e)
            g = jnp.where(hit, ex[a], g)
        eid_ref[slot] = e
        gate_ref[slot] = g * inv
        e_slots.append(e)
    toff_ref[...] = ((jnp.concatenate(e_slots, axis=0) & (PEER_HALF - 1)) * SUBLANES).T


def _route(qt, sub_keys, tn):
    nq, n = qt.shape
    sk = sub_keys.reshape(2 * PEER_HEADS, PEER_KEYS, PEER_DKEY // 2).astype(F32)
    kern = functools.partial(_route_kernel, tn=tn)
    blk = pl.BlockSpec((PEER_TOPK, PEER_HEADS, tn), lambda i: (0, 0, i))
    npair = PEER_TOPK * PEER_HEADS
    return pl.pallas_call(
        kern,
        grid=(n // tn,),
        in_specs=[pl.BlockSpec((nq, tn), lambda i: (0, i)),
                  pl.BlockSpec((2 * PEER_HEADS, PEER_KEYS, PEER_DKEY // 2), lambda i: (0, 0, 0))],
        out_specs=[blk, blk, pl.BlockSpec((tn, npair), lambda i: (i, 0))],
        out_shape=[jax.ShapeDtypeStruct((PEER_TOPK, PEER_HEADS, n), I32), jax.ShapeDtypeStruct((PEER_TOPK, PEER_HEADS, n), F32),
                   jax.ShapeDtypeStruct((n, npair), I32)],
        scratch_shapes=[pltpu.VMEM((2, PEER_TOPK, PEER_HEADS, tn), F32), pltpu.VMEM((2, PEER_TOPK, PEER_HEADS, tn), I32)],
        compiler_params=_cparams(("parallel",)),
        name="peer_route",
    )(qt, sk)


def _pack_table(tab):
    e, d = tab.shape
    assert d == SUBLANES * LANES
    bits = lax.bitcast_convert_type(tab.astype(BF16), jnp.uint16).astype(jnp.uint32)
    assert e == 2 * PEER_HALF
    words = (bits[:PEER_HALF] << 16) | bits[PEER_HALF:]
    return lax.bitcast_convert_type(words, I32).reshape(e // 2 * SUBLANES, LANES)


HI_HALF = -65536


def _bf16_words(x):
    return pltpu.bitcast(x.astype(BF16).astype(F32), I32) & HI_HALF


def _shr16(w):
    return lax.shift_right_logical(w, jnp.full(w.shape, 16, I32))


def _packed_mul(a_words, b_words):
    return pltpu.bitcast(pltpu.bitcast(a_words, BF16) * pltpu.bitcast(b_words, BF16), I32)


def _packed_add(a_words, b_words):
    return pltpu.bitcast(pltpu.bitcast(a_words, BF16) + pltpu.bitcast(b_words, BF16), I32)


def _hi_f32(w):
    return pltpu.bitcast(w & HI_HALF, F32)


def _lo_f32(w):
    return pltpu.bitcast(jnp.left_shift(w, 16), F32)


_BITREV8 = (0, 4, 2, 6, 1, 5, 3, 7)


PEER_U_SUB = 16


def _peer_u_kernel(toff_ref, x_ref, eid_ref, gate_ref, tab_ref, ce_ref, co_ref, rhi_ref, rlo_ref, *, tn, sub_t):
    npair = gate_ref.shape[0]
    lane = _lane_iota((npair, tn))
    sub = lax.broadcasted_iota(I32, (SUBLANES, LANES), 0)
    keep = {step: (sub & step) == 0 for step in (4, 2, 1)}

    def per_token(tt, t0):
        t = t0 + tt
        xw = _bf16_words(x_ref[t])
        xw = xw | _shr16(xw)
        for g in range(npair // SUBLANES):
            prods = []
            for p in _BITREV8:
                off = pl.multiple_of(toff_ref[t * npair + (g * SUBLANES + p)], SUBLANES)
                prods.append(_packed_mul(tab_ref[pl.ds(off, SUBLANES), :], xw))
            step = SUBLANES // 2
            while step >= 1:
                nxt = []
                for a in range(0, len(prods), 2):
                    lo_t, hi_t = prods[a], prods[a + 1]
                    left = jnp.where(keep[step], lo_t, pltpu.roll(hi_t, step, axis=0))
                    right = jnp.where(keep[step], pltpu.roll(lo_t, SUBLANES - step, axis=0), hi_t)
                    nxt.append(_packed_add(left, right))
                prods = nxt
                step //= 2
            rhi_ref[tt, g * SUBLANES:(g + 1) * SUBLANES, :] = _hi_f32(prods[0])
            rlo_ref[tt, g * SUBLANES:(g + 1) * SUBLANES, :] = _lo_f32(prods[0])
        return t0

    def per_sub(sb, accs):
        acc_hi, acc_lo = accs
        t0 = sb * sub_t
        lax.fori_loop(0, sub_t, per_token, t0)
        ones = jnp.ones((LANES, LANES), BF16)
        for tt in range(sub_t):
            hit = lane == t0 + tt
            acc_hi = jnp.where(hit, jnp.sum(rhi_ref[tt], axis=1, keepdims=True), acc_hi)
            acc_lo = jnp.where(hit, _dot(rlo_ref[tt].astype(BF16), ones)[:, 0:tn], acc_lo)
        return acc_hi, acc_lo

    zero = jnp.zeros((npair, tn), F32)
    acc_hi, acc_lo = lax.fori_loop(0, tn // sub_t, per_sub, (zero, zero))
    even = eid_ref[...] < PEER_HALF
    coef = gate_ref[...] * jax.nn.gelu(jnp.where(even, acc_hi, acc_lo))
    ce_ref[...] = jnp.where(even, coef, 0.0)
    co_ref[...] = jnp.where(even, 0.0, coef)


def _peer_v_kernel(toff_ref, ce_ref, co_ref, rep_ref, tab_ref, y_ref, bw_ref, *, tn, npair):
    ce = ce_ref[...].astype(BF16)
    co = co_ref[...].astype(BF16)
    chunk = min(16, tn)
    for c in range(tn // chunk):
        rep = rep_ref[:, c * chunk * LANES:(c + 1) * chunk * LANES]
        be = _dot(ce, rep)
        bo = _dot(co, rep)
        words = (pltpu.bitcast(be, I32) & HI_HALF) | _shr16(pltpu.bitcast(bo, I32))
        for t in range(chunk):
            bw_ref[c * chunk + t] = words[:, t * LANES:(t + 1) * LANES]

    nacc = 4

    def per_token(t, carry):
        zero = jnp.zeros((SUBLANES, LANES), F32)
        acc_hi = [zero] * nacc
        acc_lo = [zero] * nacc
        base = t * npair
        for k in range(npair):
            off = pl.multiple_of(toff_ref[base + k], SUBLANES)
            cw = jnp.broadcast_to(bw_ref[t, k:k + 1, :], (SUBLANES, LANES))
            prod = _packed_mul(tab_ref[pl.ds(off, SUBLANES), :], cw)
            acc_hi[k % nacc] = acc_hi[k % nacc] + _hi_f32(prod)
            acc_lo[k % nacc] = acc_lo[k % nacc] + _lo_f32(prod)
        y_ref[t] = ((acc_hi[0] + acc_hi[1]) + (acc_hi[2] + acc_hi[3])) + ((acc_lo[0] + acc_lo[1]) + (acc_lo[2] + acc_lo[3]))
        return carry

    lax.fori_loop(0, tn, per_token, 0)


def _peer_experts(eid, gate, toff, h2, tab_u, tab_v, tn):
    npair, n = eid.shape
    d = h2.shape[1]
    x3 = h2.reshape(n, d // LANES, LANES)
    tab_spec = pl.BlockSpec(tab_u.shape, lambda i: (0, 0), pipeline_mode=pl.Buffered(1))
    smem_blk = lambda tt: pl.BlockSpec((tt * npair,), lambda i: (i,), memory_space=pltpu.SMEM)
    slot_blk = lambda tt: pl.BlockSpec((npair, tt), lambda i: (0, i))
    sub_t = PEER_U_SUB if tn % PEER_U_SUB == 0 else tn
    ce, co = pl.pallas_call(
        functools.partial(_peer_u_kernel, tn=tn, sub_t=sub_t),
        grid=(n // tn,),
        in_specs=[smem_blk(tn), pl.BlockSpec((tn, d // LANES, LANES), lambda i: (i, 0, 0)), slot_blk(tn), slot_blk(tn), tab_spec],
        out_specs=[slot_blk(tn), slot_blk(tn)],
        out_shape=[jax.ShapeDtypeStruct((npair, n), F32)] * 2,
        scratch_shapes=[pltpu.VMEM((sub_t, npair, LANES), F32)] * 2,
        compiler_params=_cparams(("arbitrary",)),
        name="peer_u",
    )(toff.reshape(n * npair), x3, eid, gate, tab_u)
    tv = tn
    rep = jnp.asarray(np.arange(tv)[:, None] == np.arange(tv * LANES)[None, :] // LANES, BF16)
    y3 = pl.pallas_call(
        functools.partial(_peer_v_kernel, tn=tv, npair=npair),
        grid=(n // tv,),
        in_specs=[smem_blk(tv), slot_blk(tv), slot_blk(tv),
                  pl.BlockSpec(rep.shape, lambda i: (0, 0), pipeline_mode=pl.Buffered(1)), tab_spec],
        out_specs=pl.BlockSpec((tv, d // LANES, LANES), lambda i: (i, 0, 0)),
        out_shape=jax.ShapeDtypeStruct((n, d // LANES, LANES), F32),
        scratch_shapes=[pltpu.VMEM((tv, npair, LANES), I32)],
        compiler_params=_cparams(("arbitrary",)),
        name="peer_v",
    )(toff.reshape(n * npair), ce, co, rep, tab_v)
    return y3.reshape(n, d)


def _ple_kernel(x1_ref, yp_ref, p_ref, gple_ref, wg_ref, wp_ref, o_ref):
    x2 = x1_ref[...] + yp_ref[...]
    hn = x2 * lax.rsqrt(jnp.mean(x2 * x2, axis=-1, keepdims=True) + EPS) * gple_ref[...]
    gate = jax.nn.sigmoid(_dot(hn.astype(BF16), wg_ref[...]))
    o_ref[...] = x2 + gate * _dot(p_ref[...].astype(BF16), wp_ref[...])


def _ple(x1, yp, p2d, g_ple, w_ple_gate, w_ple, tm):
    n = x1.shape[0]
    row = lambda i: (i, 0)
    fixed = lambda i: (0, 0)
    return pl.pallas_call(
        _ple_kernel,
        grid=(n // tm,),
        in_specs=[pl.BlockSpec((tm, D_MODEL), row), pl.BlockSpec((tm, D_MODEL), row), pl.BlockSpec((tm, PLE_DIM), row),
                  pl.BlockSpec((1, D_MODEL), fixed), pl.BlockSpec((D_MODEL, D_MODEL), fixed),
                  pl.BlockSpec((PLE_DIM, D_MODEL), fixed)],
        out_specs=pl.BlockSpec((tm, D_MODEL), row),
        out_shape=jax.ShapeDtypeStruct((n, D_MODEL), F32),
        compiler_params=_cparams(("parallel",)),
        name="ple_out",
    )(x1, yp, p2d, g_ple.reshape(1, D_MODEL).astype(F32), w_ple_gate.astype(BF16), w_ple.astype(BF16))


PAGES_PER_STEP = 32
SAMPLE_SOFTMAX_STATES = 4


def _sample_scores_kernel(pt_ref, qh_ref, wrep_ref, kinew_ref, *refs, npg):
    pages = refs[:npg]
    sc_ref, scnew_ref = refs[npg:]
    j = pl.program_id(1)
    qh = qh_ref[0]
    wrep = wrep_ref[0]

    def raw(keys_t):
        return _dot(qh, keys_t.astype(BF16))

    def score(s):
        s = jnp.maximum(s, 0.0) * wrep
        tot = s[0:SUBLANES]
        for h in range(1, IDX_HEADS):
            tot = tot + s[h * SUBLANES:(h + 1) * SUBLANES]
        return tot

    dots = [raw(pages[p][0]) for p in range(npg)]
    for p in range(npg):
        sc_ref[0, :, p * PAGE_SIZE:(p + 1) * PAGE_SIZE] = score(dots[p])

    @pl.when(j == 0)
    def _():
        sn = score(raw(kinew_ref[0]))
        causal = _lane_iota((SUBLANES, PAGE_SIZE)) <= _row_iota((SUBLANES, PAGE_SIZE))
        scnew_ref[0] = jnp.where(causal, sn, -jnp.inf)


def _sample_thr_kernel(sc_ref, thr_ref, jcut_ref, keys_ref, cand_ref, *, k_sel, tq, tk, t):
    nkb = keys_ref.shape[0]
    for kb in range(nkb):
        keys_ref[kb] = _sort_key(sc_ref[:, :, kb * tk:(kb + 1) * tk].reshape(tq, tk))
    _select_rows(keys_ref, thr_ref, jcut_ref, cand_ref, nkb, k_sel, tq, tk, min(SEARCH_ROWS, tq), live_rows=t)


def _sample_attn_kernel(pt_ref, q_ref, sc_ref, scnew_ref, thr_ref, jcut_ref, knew_ref, vnew_ref, *refs, npg, past):
    kpages = refs[:npg]
    vpages = refs[npg:2 * npg]
    o_ref, m_ref, l_ref, acc_ref = refs[2 * npg:]
    j = pl.program_id(1)

    @pl.when(j == 0)
    def _():
        m_ref[...] = jnp.full(m_ref.shape, NEG_BIG, F32)
        l_ref[...] = jnp.zeros(l_ref.shape, F32)
        acc_ref[...] = jnp.zeros(acc_ref.shape, F32)

    q = q_ref[0]
    thr = thr_ref[...]
    jcut = jcut_ref[...]
    lane = _lane_iota((SUBLANES, PAGE_SIZE))
    kvd = A_KV_HEADS * A_HEAD_DIM

    wide = lambda a: jnp.concatenate([a] * (kvd // LANES), axis=1)

    def update(groups):
        ss = []
        for _, kts, _, scs, pos0s, causal in groups:
            biases = []
            for sc8, pos0 in zip(scs, pos0s):
                key = _sort_key(sc8)
                pos = lane + pos0
                sel = jnp.where(key > thr, 0.0, jnp.where(key == thr, jnp.where(pos <= jcut, 0.0, NEG_BIG), NEG_BIG))
                if causal is not None:
                    sel = jnp.where(causal, sel, NEG_BIG)
                biases.append(sel)
            bias = jnp.concatenate([jnp.concatenate(biases, axis=1)] * A_HEADS, axis=0)
            ss.append(jnp.concatenate([_dot(q, kt[...].reshape(kvd, PAGE_SIZE).astype(BF16)) for kt in kts], axis=1) + bias)
        m_olds = [m_ref[g[0]] for g in groups]
        m_news = [jnp.maximum(mo, jnp.max(s, axis=1, keepdims=True)) for mo, s in zip(m_olds, ss)]
        alphas = [jnp.exp2(mo - mn) for mo, mn in zip(m_olds, m_news)]
        ps = [jnp.exp2(s - mn[:, 0:1]) for s, mn in zip(ss, m_news)]
        sums = [jnp.sum(p, axis=1, keepdims=True) for p in ps]
        pvs = []
        for (_, _, vts, _, _, _), p in zip(groups, ps):
            pb = p.astype(BF16)
            parts = [_dot_nt(pb[:, i * PAGE_SIZE:(i + 1) * PAGE_SIZE], vt[...].reshape(kvd, PAGE_SIZE).astype(BF16))
                     for i, vt in enumerate(vts)]
            pv = parts[0]
            for part in parts[1:]:
                pv = pv + part
            pvs.append(pv)
        for g, mn, al, sm, pv in zip(groups, m_news, alphas, sums, pvs):
            ci = g[0]
            l_ref[ci] = al * l_ref[ci] + sm
            acc_ref[ci] = wide(al) * acc_ref[ci] + pv
            m_ref[ci] = mn

    nst = m_ref.shape[0]
    update([(ci, [kpages[p].at[0] for p in range(ci, npg, nst)], [vpages[p].at[0] for p in range(ci, npg, nst)],
             [sc_ref[0, :, p * PAGE_SIZE:(p + 1) * PAGE_SIZE] for p in range(ci, npg, nst)],
             [(j * npg + p) * PAGE_SIZE for p in range(ci, npg, nst)], None) for ci in range(nst)])

    @pl.when(j == pl.num_programs(1) - 1)
    def _():
        update([(0, [knew_ref.at[0]], [vnew_ref.at[0]], [scnew_ref[0]], [past], lane <= _row_iota((SUBLANES, PAGE_SIZE)))])
        m_all = m_ref[0]
        for ci in range(1, nst):
            m_all = jnp.maximum(m_all, m_ref[ci])
        l_all = jnp.zeros(m_all.shape, F32)
        acc_all = jnp.zeros(acc_ref.shape[1:], F32)
        for ci in range(nst):
            w = jnp.exp2(m_ref[ci] - m_all)
            l_all = l_all + w * l_ref[ci]
            acc_all = acc_all + wide(w) * acc_ref[ci]
        o_ref[0] = acc_all / wide(l_all)


def _head_major(a, bd, t, nh, hd):
    a = a.reshape(bd, t, nh, hd).transpose(0, 2, 1, 3)
    a = jnp.pad(a, ((0, 0), (0, 0), (0, SUBLANES - t), (0, 0)))
    return a.reshape(bd, nh * SUBLANES, hd)


def _attn_sample(q, k, v, qi, misc, cache_k, cache_v, cache_idx_k, page_table, bd, t):
    assert t <= SUBLANES
    n_pages = page_table.shape[1]
    past = n_pages * PAGE_SIZE
    npg = PAGES_PER_STEP if n_pages % PAGES_PER_STEP == 0 else 1
    nj = n_pages // npg
    k_sel = min(TOPK_MAX, (past + t) // 4)
    rows = A_HEADS * SUBLANES

    qh = _head_major(qi, bd, t, IDX_HEADS, IDX_DIM)
    wi = misc[:, MISC_WI:MISC_WI + IDX_HEADS].reshape(bd, t, IDX_HEADS).transpose(0, 2, 1)
    wrep = jnp.broadcast_to(jnp.pad(wi, ((0, 0), (0, 0), (0, SUBLANES - t))).reshape(bd, rows, 1), (bd, rows, LANES))
    pad_page = lambda a, w: jnp.pad(a.reshape(bd, t, w), ((0, 0), (0, PAGE_SIZE - t), (0, 0)))
    kinew = pad_page(misc[:, :IDX_DIM], IDX_DIM).transpose(0, 2, 1)
    cik = cache_idx_k.transpose(0, 2, 1)

    per_b = lambda b, j, pt: (b, 0, 0)
    per_b4 = lambda b, j, pt: (b, 0, 0, 0)
    page_spec = lambda shape, p: pl.BlockSpec(shape, lambda b, j, pt: (pt[b, j * npg + p],) + (0,) * (len(shape) - 1))
    sc, scnew = pl.pallas_call(
        functools.partial(_sample_scores_kernel, npg=npg),
        grid_spec=pltpu.PrefetchScalarGridSpec(
            num_scalar_prefetch=1, grid=(bd, nj),
            in_specs=[pl.BlockSpec((1, rows, IDX_DIM), per_b), pl.BlockSpec((1, rows, LANES), per_b),
                      pl.BlockSpec((1, IDX_DIM, PAGE_SIZE), per_b)]
                     + [page_spec((1, IDX_DIM, PAGE_SIZE), p) for p in range(npg)],
            out_specs=[pl.BlockSpec((1, SUBLANES, npg * PAGE_SIZE), lambda b, j, pt: (b, 0, j)),
                       pl.BlockSpec((1, SUBLANES, PAGE_SIZE), per_b)]),
        out_shape=[jax.ShapeDtypeStruct((bd, SUBLANES, past), F32), jax.ShapeDtypeStruct((bd, SUBLANES, PAGE_SIZE), F32)],
        compiler_params=_cparams(("parallel", "arbitrary")),
        name="sample_scores",
    )(page_table, qh, wrep, kinew, *([cik] * npg))

    length = past + PAGE_SIZE
    sc_all = jnp.concatenate([sc, scnew], axis=2)
    tb = SUBLANES if bd % SUBLANES == 0 else 1
    tq = tb * SUBLANES
    tk = 5 * LANES if length % (5 * LANES) == 0 else LANES
    thr, jcut = pl.pallas_call(
        functools.partial(_sample_thr_kernel, k_sel=k_sel, tq=tq, tk=tk, t=t),
        grid=(bd // tb,),
        in_specs=[pl.BlockSpec((tb, SUBLANES, length), lambda i: (i, 0, 0))],
        out_specs=[pl.BlockSpec((tq, LANES), lambda i: (i, 0))] * 2,
        out_shape=[jax.ShapeDtypeStruct((bd * SUBLANES, LANES), I32)] * 2,
        scratch_shapes=[pltpu.VMEM((length // tk, tq, tk), I32), pltpu.VMEM((tq, LANES), I32)],
        compiler_params=_cparams(("parallel",)),
        name="sample_thr",
    )(sc_all)

    qa = _head_major(q, bd, t, A_HEADS, A_HEAD_DIM)
    kv_of_row = (np.arange(rows) // SUBLANES) // A_REP
    own = jnp.asarray(kv_of_row[:, None] == np.arange(A_KV_HEADS)[None, :])
    qa = jnp.where(own[None, :, :, None], qa[:, :, None, :], jnp.zeros((), qa.dtype)).reshape(bd, rows, A_KV)
    to_page_t = lambda a: pad_page(a, A_KV).reshape(bd, PAGE_SIZE, A_KV_HEADS, A_HEAD_DIM).transpose(0, 2, 3, 1)
    knew, vnew = to_page_t(k), to_page_t(v)
    ck = cache_k.transpose(0, 2, 3, 1)
    cv = cache_v.transpose(0, 2, 3, 1)
    kv_blk = (1, A_KV_HEADS, A_HEAD_DIM, PAGE_SIZE)
    nst = SAMPLE_SOFTMAX_STATES if npg % SAMPLE_SOFTMAX_STATES == 0 else 1
    o = pl.pallas_call(
        functools.partial(_sample_attn_kernel, npg=npg, past=past),
        grid_spec=pltpu.PrefetchScalarGridSpec(
            num_scalar_prefetch=1, grid=(bd, nj),
            in_specs=[pl.BlockSpec((1, rows, A_KV), per_b),
                      pl.BlockSpec((1, SUBLANES, npg * PAGE_SIZE), lambda b, j, pt: (b, 0, j)),
                      pl.BlockSpec((1, SUBLANES, PAGE_SIZE), per_b),
                      pl.BlockSpec((SUBLANES, LANES), lambda b, j, pt: (b, 0)),
                      pl.BlockSpec((SUBLANES, LANES), lambda b, j, pt: (b, 0)),
                      pl.BlockSpec(kv_blk, per_b4), pl.BlockSpec(kv_blk, per_b4)]
                     + [page_spec(kv_blk, p) for p in range(npg)] * 2,
            out_specs=pl.BlockSpec((1, rows, A_KV), per_b),
            scratch_shapes=[pltpu.VMEM((nst, rows, LANES), F32), pltpu.VMEM((nst, rows, LANES), F32),
                            pltpu.VMEM((nst, rows, A_KV), F32)]),
        out_shape=jax.ShapeDtypeStruct((bd, rows, A_KV), F32),
        compiler_params=_cparams(("parallel", "arbitrary")),
        name="sample_attn",
    )(page_table, qa, sc, scnew, thr, jcut, knew, vnew, *([ck] * npg), *([cv] * npg))
    o = jnp.sum(jnp.where(own[None, :, :, None], o.reshape(bd, rows, A_KV_HEADS, A_HEAD_DIM), 0.0), axis=2)
    o = o.reshape(bd, A_HEADS, SUBLANES, A_HEAD_DIM)[:, :, :t].transpose(0, 2, 1, 3)
    return o.reshape(bd * t, A_Q)


def _tile(n, pref):
    return pref if n % pref == 0 else n


def _layer(x, p_emb, conv_state, delta_state, attn_fn, prm):
    b, t, _ = x.shape
    n = b * t
    x2d = x.reshape(n, D_MODEL)
    tm = _tile(n, ROW_TILE)
    (q, k, kdup, v, vb, qi, misc, kidup, dqkv, dz, ga, gb) = _proj(
        x2d, prm['g_mix'], prm['w_perm'], prm['g_q'], prm['g_k'], prm['g_idx_k'], tm)

    oa = attn_fn(q, k, kdup, v, vb, qi, misc, kidup)

    tp = -(-t // SUBLANES) * SUBLANES
    c = min(DN_CHUNK, tp)
    pad_t = lambda a: jnp.pad(a.reshape(b, t, a.shape[-1]), ((0, 0), (0, tp - t), (0, 0)))
    dqkv3 = dqkv.reshape(b, t, DN_CONV_CH)
    halo = jnp.zeros((b, SUBLANES, DN_CONV_CH), F32)
    if conv_state is not None:
        halo = halo.at[:, SUBLANES - (CONV_W - 1):].set(conv_state.astype(F32))
        hist = jnp.concatenate([conv_state.astype(F32), dqkv3], axis=1)
    else:
        hist = jnp.concatenate([jnp.zeros((b, CONV_W - 1, DN_CONV_CH), F32), dqkv3], axis=1)
    new_conv = hist[:, -(CONV_W - 1):]
    qn, kn, vv = _conv(pad_t(dqkv), halo, prm['conv_w'], _tile(tp, CONV_TT))
    s0 = jnp.zeros((b, DN_HEADS, DN_DK, DN_DV), F32) if delta_state is None else delta_state.astype(F32)
    ob, new_delta = _delta(qn, kn, vv, pad_t(dz), pad_t(misc), prm['a_log'], prm['dt_bias'], prm['g_dn_out'], s0, c, t)
    ob = ob[:, :t].reshape(n, DN_V)

    x1, h2, qt = _merge(x2d, oa, ob, ga, gb, prm['w_oa'], prm['w_ob'], prm['w_out'], prm['g_ffn'], prm['w_pq'], tm)
    eid, gate, toff = _route(qt, prm['sub_keys'], _tile(n, TOKEN_TILE))
    npair = PEER_TOPK * PEER_HEADS
    yp = _peer_experts(eid.reshape(npair, n), gate.reshape(npair, n), toff, h2, prm['tab_u'], prm['tab_v'], _tile(n, TOKEN_TILE))
    y = _ple(x1, yp, p_emb.reshape(n, PLE_DIM), prm['g_ple'], prm['w_ple_gate'], prm['w_ple'], tm)

    return (y.reshape(b, t, D_MODEL), k.reshape(b, t, A_KV_HEADS, A_HEAD_DIM), v.reshape(b, t, A_KV_HEADS, A_HEAD_DIM),
            misc[:, :IDX_DIM].reshape(b, t, IDX_DIM), new_conv, new_delta)


def kernel(x_prompt, x_sample, cache_k, cache_v, cache_idx_k, state_conv, state_delta, page_table, p_prompt, p_sample,
           g_mix, w_in, g_q, g_k, g_idx_k, conv_w, a_log, dt_bias, g_dn_out, w_oa, w_ob, w_out, g_ffn, w_pq, sub_keys,
           peer_u, peer_v, g_ple, w_ple_gate, w_ple):
    depth = w_in.shape[0]
    xp, xs = x_prompt, x_sample
    outs = [[] for _ in range(10)]
    for i in range(depth):
        prm = dict(g_mix=g_mix[i], w_perm=_permute_w_in(w_in[i]), g_q=g_q[i], g_k=g_k[i], g_idx_k=g_idx_k[i],
                   conv_w=conv_w[i].astype(F32), a_log=a_log[i], dt_bias=dt_bias[i], g_dn_out=g_dn_out[i],
                   w_oa=w_oa[i], w_ob=w_ob[i], w_out=w_out[i], g_ffn=g_ffn[i], w_pq=w_pq[i], sub_keys=sub_keys[i],
                   tab_u=_pack_table(peer_u[i]), tab_v=_pack_table(peer_v[i]), g_ple=g_ple[i],
                   w_ple_gate=w_ple_gate[i], w_ple=w_ple[i])

        bp, s, _ = xp.shape

        def attn_prompt(q, k, kdup, v, vb, qi, misc, kidup):
            k_sel = min(TOPK_MAX, s // 4)
            tq, tk = _tile(s, ATTN_TQ), _tile(s, ATTN_TK)
            r3 = lambda a: a.reshape(bp, s, a.shape[-1])
            mask = _idx_mask(r3(qi), r3(misc), r3(kidup), k_sel, _tile(s, IDX_TQ), tk)
            return _attn(r3(q), r3(kdup), r3(vb), mask, tq, tk).reshape(bp * s, A_Q)

        res = _layer(xp, p_prompt[i], None, None, attn_prompt, prm)
        xp = res[0]
        for lst, val in zip(outs[:5], res[1:]):
            lst.append(val)

        bd, t, _ = xs.shape

        def attn_sample(q, k, kdup, v, vb, qi, misc, kidup):
            return _attn_sample(q, k, v, qi, misc, cache_k[i], cache_v[i], cache_idx_k[i], page_table, bd, t)

        res = _layer(xs, p_sample[i], state_conv[i], state_delta[i], attn_sample, prm)
        xs = res[0]
        for lst, val in zip(outs[5:], res[1:]):
            lst.append(val)

    cast = [cache_k.dtype, cache_v.dtype, cache_idx_k.dtype, state_conv.dtype, state_delta.dtype] * 2
    stacked = [jnp.stack(lst).astype(dt) for lst, dt in zip(outs, cast)]
    return (xp, xs, *stacked)
```
